```python
import math
import jax, jax.numpy as jnp
from jax import lax
import numpy as np

D_MODEL = 1024
BATCH = 16
SEQ = 2048
DEPTH = 1

FOURIER_GROUPS = 4
FOURIER_CH = 128
FOURIER_WIDTH = FOURIER_GROUPS * FOURIER_CH
HEAD_DIM = 64
N_Q_HEADS = 8
N_KV_HEADS = 2
GQA_GROUP = N_Q_HEADS // N_KV_HEADS
ATTN_WIDTH = N_Q_HEADS * HEAD_DIM
KV_WIDTH = N_KV_HEADS * HEAD_DIM
MIX_WIDTH = FOURIER_WIDTH + ATTN_WIDTH
IN_WIDTH = FOURIER_WIDTH + ATTN_WIDTH + 2 * KV_WIDTH
WINDOW = 128
Q_BLOCK = 128
N_BUCKETS = 32
MAX_DISTANCE = 128
N_EXPERTS = 32
TOP_K = 4
D_EXPERT = D_MODEL
SWIGLU_ALPHA = 1.702
SWIGLU_LIMIT = 7.0
EXPERT_BLOCK = 256
NORM_EPS = 1e-5
QK_EPS = 1e-6

kernel_name = "hybrid_fnet_swa_moe_encoder"


def rmsnorm(x, g, eps=NORM_EPS):
    xf = x.astype(jnp.float32)
    y = xf * lax.rsqrt(jnp.mean(xf * xf, axis=-1, keepdims=True) + eps)
    return (y * g.astype(jnp.float32)).astype(x.dtype)


def t5_bucket(rel):
    nb = N_BUCKETS // 2
    max_exact = nb // 2
    ret = (rel > 0).astype(jnp.int32) * nb
    n = jnp.abs(rel)
    nf = jnp.maximum(n, 1).astype(jnp.float32)
    large = max_exact + (jnp.log(nf / max_exact) / math.log(MAX_DISTANCE / max_exact)
                         * (nb - max_exact)).astype(jnp.int32)
    large = jnp.minimum(large, nb - 1)
    return ret + jnp.where(n < max_exact, n, large)


def fourier_mixer(u, w_mix):
    b, s, _ = u.shape
    ug = u.reshape(b, s, FOURIER_GROUPS, FOURIER_CH).astype(jnp.float32)
    f = jnp.fft.fftn(ug, axes=(1, 3), norm="ortho").real
    y = jnp.einsum("bsgc,gcd->bsgd", f.astype(u.dtype), w_mix)
    return y.reshape(b, s, FOURIER_WIDTH)


def windowed_gqa(q, k, v, sinks, rel_bias):
    b, s = q.shape[0], q.shape[1]
    nb = s // Q_BLOCK
    qb = q.reshape(b, nb, Q_BLOCK, N_KV_HEADS, GQA_GROUP, HEAD_DIM)
    pad = ((0, 0), (Q_BLOCK, Q_BLOCK), (0, 0), (0, 0))

    def band(t):
        tp = jnp.pad(t, pad).reshape(b, nb + 2, Q_BLOCK, N_KV_HEADS, HEAD_DIM)
        return jnp.concatenate([tp[:, :-2], tp[:, 1:-1], tp[:, 2:]], axis=2)

    kb, vb = band(k), band(v)
    scores = jnp.einsum("bnqhgd,bnkhd->bnhgqk", qb, kb).astype(jnp.float32) * (HEAD_DIM ** -0.5)

    qi = jnp.arange(Q_BLOCK, dtype=jnp.int32)[:, None]
    kj = jnp.arange(3 * Q_BLOCK, dtype=jnp.int32)[None, :]
    rel = kj - Q_BLOCK - qi
    bias = rel_bias.astype(jnp.float32)[t5_bucket(rel)]
    bias = jnp.transpose(bias, (2, 0, 1)).reshape(N_KV_HEADS, GQA_GROUP, Q_BLOCK, 3 * Q_BLOCK)
    kpos = jnp.arange(nb, dtype=jnp.int32)[:, None, None] * Q_BLOCK - Q_BLOCK + kj[None]
    valid = (jnp.abs(rel)[None] <= WINDOW) & (kpos >= 0) & (kpos < s)
    scores = jnp.where(valid[None, :, None, None], scores + bias[None, None], jnp.float32(-1e30))

    sink = sinks.astype(jnp.float32).reshape(N_KV_HEADS, GQA_GROUP)[None, None, :, :, None, None]
    m = jnp.maximum(jnp.max(scores, axis=-1, keepdims=True), sink)
    p = jnp.exp(scores - m)
    denom = jnp.sum(p, axis=-1, keepdims=True) + jnp.exp(sink - m)
    o = jnp.einsum("bnhgqk,bnkhd->bnqhgd", (p / denom).astype(v.dtype), vb)
    return o.reshape(b, s, ATTN_WIDTH)


def moe_ffn(h, w_router, b_router, w_gate_up, b_gate_up, w_down, b_down):
    t, d = h.shape
    logits = (h @ w_router).astype(jnp.float32) + b_router.astype(jnp.float32)
    top_vals, top_idx = lax.top_k(logits, TOP_K)
    gates = jax.nn.softmax(top_vals, axis=-1)

    n_assign = t * TOP_K
    e_flat = top_idx.reshape(n_assign).astype(jnp.int32)
    tok_flat = jnp.arange(n_assign, dtype=jnp.int32) // TOP_K
    g_flat = gates.reshape(n_assign)
    order = jnp.argsort(e_flat)
    e_sorted = e_flat[order]
    counts = jnp.zeros((N_EXPERTS,), jnp.int32).at[e_flat].add(1)
    starts = jnp.cumsum(counts) - counts
    padded = (counts + EXPERT_BLOCK - 1) // EXPERT_BLOCK * EXPERT_BLOCK
    pends = jnp.cumsum(padded)
    pstarts = pends - padded
    dest = pstarts[e_sorted] + (jnp.arange(n_assign, dtype=jnp.int32) - starts[e_sorted])

    n_rows = n_assign + N_EXPERTS * EXPERT_BLOCK
    n_blocks = n_rows // EXPERT_BLOCK
    buf_tok = jnp.full((n_rows,), t, jnp.int32).at[dest].set(tok_flat[order])
    buf_gate = jnp.zeros((n_rows,), jnp.float32).at[dest].set(g_flat[order])
    blk_start = jnp.arange(n_blocks, dtype=jnp.int32) * EXPERT_BLOCK
    blk_e = jnp.minimum(jnp.searchsorted(pends, blk_start, side="right"), N_EXPERTS - 1).astype(jnp.int32)

    h_pad = jnp.concatenate([h, jnp.zeros((1, d), h.dtype)], axis=0)

    def expert_block(args):
        tok, e = args
        xb = h_pad[tok]
        gu = xb @ w_gate_up[e] + b_gate_up[e]
        g = jnp.minimum(gu[:, ::2], SWIGLU_LIMIT)
        up = jnp.clip(gu[:, 1::2], -SWIGLU_LIMIT, SWIGLU_LIMIT)
        act = g * jax.nn.sigmoid(g * SWIGLU_ALPHA) * (up + 1.0)
        return act @ w_down[e] + b_down[e]

    ys = lax.map(expert_block, (buf_tok.reshape(n_blocks, EXPERT_BLOCK), blk_e))
    ys = ys.reshape(n_rows, d) * buf_gate[:, None].astype(ys.dtype)
    return jnp.zeros((t + 1, d), ys.dtype).at[buf_tok].add(ys)[:t]


def setup_inputs(seed: int = 0) -> dict:
    key = jax.random.key(seed)
    ks = jax.random.split(key, 20)
    f32 = jnp.float32
    nrm = lambda k, shape, scale: jax.random.normal(k, shape, f32) * scale
    return {
        "x": nrm(ks[0], (BATCH, SEQ, D_MODEL), 1.0),
        "norm1": 1.0 + nrm(ks[1], (DEPTH, D_MODEL), 0.02),
        "w_in": nrm(ks[2], (DEPTH, D_MODEL, IN_WIDTH), D_MODEL ** -0.5),
        "q_norm": 1.0 + nrm(ks[3], (DEPTH, HEAD_DIM), 0.02),
        "k_norm": 1.0 + nrm(ks[4], (DEPTH, HEAD_DIM), 0.02),
        "sinks": nrm(ks[5], (DEPTH, N_Q_HEADS), 0.5),
        "rel_bias": nrm(ks[6], (N_BUCKETS, N_Q_HEADS), 0.3),
        "w_fourier": nrm(ks[7], (DEPTH, FOURIER_GROUPS, FOURIER_CH, FOURIER_CH), FOURIER_CH ** -0.5),
        "g_fourier_out": 1.0 + nrm(ks[8], (DEPTH, FOURIER_WIDTH), 0.02),
        "g_attn_out": 1.0 + nrm(ks[9], (DEPTH, ATTN_WIDTH), 0.02),
        "w_out": nrm(ks[10], (DEPTH, MIX_WIDTH, D_MODEL), MIX_WIDTH ** -0.5),
        "norm2": 1.0 + nrm(ks[11], (DEPTH, D_MODEL), 0.02),
        "w_router": nrm(ks[12], (DEPTH, D_MODEL, N_EXPERTS), D_MODEL ** -0.5),
        "b_router": nrm(ks[13], (DEPTH, N_EXPERTS), 0.01),
        "w_gate_up": nrm(ks[14], (DEPTH, N_EXPERTS, D_MODEL, 2 * D_EXPERT), D_MODEL ** -0.5),
        "b_gate_up": nrm(ks[15], (DEPTH, N_EXPERTS, 2 * D_EXPERT), 0.01),
        "w_down": nrm(ks[16], (DEPTH, N_EXPERTS, D_EXPERT, D_MODEL), D_EXPERT ** -0.5),
        "b_down": nrm(ks[17], (DEPTH, N_EXPERTS, D_MODEL), 0.01),
    }


def reference(x, norm1, w_in, q_norm, k_norm, sinks, rel_bias, w_fourier, g_fourier_out,
              g_attn_out, w_out, norm2, w_router, b_router, w_gate_up, b_gate_up, w_down, b_down):
    b, s, d = x.shape
    for l in range(DEPTH):
        h = rmsnorm(x, norm1[l])
        z = h @ w_in[l]
        u = z[..., :FOURIER_WIDTH]
        q = z[..., FOURIER_WIDTH:FOURIER_WIDTH + ATTN_WIDTH].reshape(b, s, N_Q_HEADS, HEAD_DIM)
        k = z[..., FOURIER_WIDTH + ATTN_WIDTH:FOURIER_WIDTH + ATTN_WIDTH + KV_WIDTH].reshape(b, s, N_KV_HEADS, HEAD_DIM)
        v = z[..., FOURIER_WIDTH + ATTN_WIDTH + KV_WIDTH:].reshape(b, s, N_KV_HEADS, HEAD_DIM)
        q = rmsnorm(q, q_norm[l], QK_EPS)
        k = rmsnorm(k, k_norm[l], QK_EPS)

        y_fourier = fourier_mixer(u, w_fourier[l])
        y_attn = windowed_gqa(q, k, v, sinks[l], rel_bias)
        mix = jnp.concatenate([rmsnorm(y_fourier, g_fourier_out[l]),
                               rmsnorm(y_attn, g_attn_out[l])], axis=-1)
        x = x + mix @ w_out[l]

        h2 = rmsnorm(x, norm2[l]).reshape(b * s, d)
        x = x + moe_ffn(h2, w_router[l], b_router[l], w_gate_up[l], b_gate_up[l],
                        w_down[l], b_down[l]).reshape(b, s, d)
    return x
```

```python
import functools
import math

import jax
import jax.numpy as jnp
import numpy as np
from jax import lax
from jax.experimental import pallas as pl
from jax.experimental.pallas import tpu as pltpu

F32 = jnp.float32
BF16 = jnp.bfloat16
I32 = jnp.int32

NORM_EPS = 1e-5
QK_EPS = 1e-6
HEAD_DIM = 64
N_Q_HEADS = 8
N_KV_HEADS = 2
FOURIER_GROUPS = 4
FOURIER_CH = 128
FOURIER_WIDTH = FOURIER_GROUPS * FOURIER_CH
ATTN_WIDTH = N_Q_HEADS * HEAD_DIM
KV_WIDTH = N_KV_HEADS * HEAD_DIM
WINDOW = 128
Q_BLOCK = 128
N_BUCKETS = 32
MAX_DISTANCE = 128
TOP_K = 4
SWIGLU_ALPHA = 1.702
SWIGLU_LIMIT = 7.0
MASK_VALUE = -1e30

LANES = 128
TOKEN_TILE = 512
MOVE_TILE = 256
EXPERT_ROWS = 512
VMEM_LIMIT = 56 * 1024 * 1024


def _params(n_axes, vmem=None):
    return pltpu.CompilerParams(
        dimension_semantics=("arbitrary",) * n_axes, vmem_limit_bytes=vmem)


def _pair_head_norm(xc, gain, lo):
    x2 = xc * xc
    s_lo = jnp.sum(jnp.where(lo, x2, 0.0), axis=-1, keepdims=True)
    s_hi = jnp.sum(jnp.where(lo, 0.0, x2), axis=-1, keepdims=True)
    inv = jnp.where(lo, lax.rsqrt(s_lo * (1.0 / HEAD_DIM) + QK_EPS),
                    lax.rsqrt(s_hi * (1.0 / HEAD_DIM) + QK_EPS))
    return xc * inv * gain


def _inproj_kernel(x_ref, g1_ref, w_ref, qg_ref, kg_ref, u_ref, q_ref, kv_ref):
    x = x_ref[...]
    ms = jnp.mean(x * x, axis=-1, keepdims=True)
    h = (x * lax.rsqrt(ms + NORM_EPS) * g1_ref[...]).astype(BF16)
    z = jnp.dot(h, w_ref[...], preferred_element_type=F32)
    u_ref[...] = z[:, :FOURIER_WIDTH].astype(BF16)
    rows = x.shape[0]
    lo = lax.broadcasted_iota(I32, (rows, LANES), 1) < HEAD_DIM
    q0 = FOURIER_WIDTH
    for c in range(ATTN_WIDTH // LANES):
        qc = _pair_head_norm(z[:, q0 + c * LANES:q0 + (c + 1) * LANES], qg_ref[...], lo)
        q_ref[:, c * LANES:(c + 1) * LANES] = (qc * (HEAD_DIM ** -0.5)).astype(BF16)
    k0 = q0 + ATTN_WIDTH
    kc = _pair_head_norm(z[:, k0:k0 + KV_WIDTH], kg_ref[...], lo)
    vc = z[:, k0 + KV_WIDTH:k0 + 2 * KV_WIDTH]
    kv_ref[:, 0:LANES] = kc.astype(BF16)
    kv_ref[:, LANES:2 * LANES] = pltpu.roll(kc, HEAD_DIM, 1).astype(BF16)
    kv_ref[:, 2 * LANES:3 * LANES] = vc.astype(BF16)
    kv_ref[:, 3 * LANES:4 * LANES] = pltpu.roll(vc, HEAD_DIM, 1).astype(BF16)


def _inproj(x2d, norm1, w_in, q_norm, k_norm):
    t, d = x2d.shape
    tm = min(TOKEN_TILE, t)
    n_in = w_in.shape[1]
    qg = jnp.tile(q_norm, LANES // HEAD_DIM).reshape(1, LANES)
    kg = jnp.tile(k_norm, LANES // HEAD_DIM).reshape(1, LANES)
    full = lambda i: (0, 0)
    return pl.pallas_call(
        _inproj_kernel,
        grid=(t // tm,),
        in_specs=[
            pl.BlockSpec((tm, d), lambda i: (i, 0)),
            pl.BlockSpec((1, d), full),
            pl.BlockSpec((d, n_in), full),
            pl.BlockSpec((1, LANES), full),
            pl.BlockSpec((1, LANES), full),
        ],
        out_specs=[
            pl.BlockSpec((tm, FOURIER_WIDTH), lambda i: (i, 0)),
            pl.BlockSpec((tm, ATTN_WIDTH), lambda i: (i, 0)),
            pl.BlockSpec((tm, 4 * LANES), lambda i: (i, 0)),
        ],
        out_shape=[
            jax.ShapeDtypeStruct((t, FOURIER_WIDTH), BF16),
            jax.ShapeDtypeStruct((t, ATTN_WIDTH), BF16),
            jax.ShapeDtypeStruct((t, 4 * LANES), BF16),
        ],
        compiler_params=_params(1, VMEM_LIMIT),
        name="inproj",
    )(x2d, norm1.reshape(1, d), w_in.astype(BF16), qg, kg)


def _fourier_kernel(u_ref, cs_ref, ss_ref, cc_ref, sc_ref, wf_ref, g_ref, o_ref, p_scr, q_scr,
                    *, scale, row_block):
    for g in range(FOURIER_GROUPS):
        sl = slice(g * FOURIER_CH, (g + 1) * FOURIER_CH)
        w = wf_ref[g].astype(BF16)
        a = (jnp.dot(cc_ref[...], w, preferred_element_type=F32) * scale).astype(BF16)
        b = (jnp.dot(sc_ref[...], w, preferred_element_type=F32) * scale).astype(BF16)
        ug = u_ref[:, sl]
        p_scr[:, sl] = jnp.dot(ug, a, preferred_element_type=F32).astype(BF16)
        q_scr[:, sl] = jnp.dot(ug, b, preferred_element_type=F32).astype(BF16)
    s = u_ref.shape[0]
    for r in range(s // row_block):
        rs = slice(r * row_block, (r + 1) * row_block)
        y = (jnp.dot(cs_ref[rs, :], p_scr[...], preferred_element_type=F32)
             + jnp.dot(ss_ref[rs, :], q_scr[...], preferred_element_type=F32))
        ms = jnp.mean(y * y, axis=-1, keepdims=True)
        o_ref[rs, :] = (y * lax.rsqrt(ms + NORM_EPS) * g_ref[...]).astype(BF16)


def _dft_tables(n):
    k = np.arange(n, dtype=np.int64)
    ang = 2.0 * np.pi * ((k[:, None] * k[None, :]) % n).astype(np.float64) / n
    return np.cos(ang), np.sin(ang)


def _fourier(u, w_fourier, g_out, batch, seq):
    cs, ss = _dft_tables(seq)
    cc, sc = _dft_tables(FOURIER_CH)
    scale = 1.0 / math.sqrt(seq * FOURIER_CH)
    row_block = min(512, seq)
    full2 = lambda b: (0, 0)
    return pl.pallas_call(
        functools.partial(_fourier_kernel, scale=scale, row_block=row_block),
        grid=(batch,),
        in_specs=[
            pl.BlockSpec((seq, FOURIER_WIDTH), lambda b: (b, 0)),
            pl.BlockSpec((seq, seq), full2),
            pl.BlockSpec((seq, seq), full2),
            pl.BlockSpec((FOURIER_CH, FOURIER_CH), full2),
            pl.BlockSpec((FOURIER_CH, FOURIER_CH), full2),
            pl.BlockSpec((FOURIER_GROUPS, FOURIER_CH, FOURIER_CH), lambda b: (0, 0, 0)),
            pl.BlockSpec((1, FOURIER_WIDTH), full2),
        ],
        out_specs=pl.BlockSpec((seq, FOURIER_WIDTH), lambda b: (b, 0)),
        out_shape=jax.ShapeDtypeStruct((batch * seq, FOURIER_WIDTH), BF16),
        scratch_shapes=[pltpu.VMEM((seq, FOURIER_WIDTH), BF16),
                        pltpu.VMEM((seq, FOURIER_WIDTH), BF16)],
        compiler_params=_params(1, VMEM_LIMIT),
        name="fourier",
    )(u, jnp.asarray(cs, BF16), jnp.asarray(ss, BF16), jnp.asarray(cc, BF16),
      jnp.asarray(-sc, BF16), w_fourier, g_out.reshape(1, FOURIER_WIDTH))


def _attn_kernel(sink_ref, q_ref, kvp_ref, kvo_ref, kvn_ref, bias_ref, g_ref, o_ref, acc_ref):
    i = pl.program_id(1)
    nb = pl.num_programs(1)
    kv = jnp.concatenate([kvp_ref[...], kvo_ref[...], kvn_ref[...]], axis=0)
    nk = kv.shape[0]
    lo = lax.broadcasted_iota(I32, (nk, LANES), 1) < HEAD_DIM
    k_a, k_b = kv[:, 0:LANES], kv[:, LANES:2 * LANES]
    v_a, v_b = kv[:, 2 * LANES:3 * LANES], kv[:, 3 * LANES:4 * LANES]
    zero = jnp.zeros_like(k_a)
    k_lo = (jnp.where(lo, k_a, zero), jnp.where(lo, k_b, zero))
    k_hi = (jnp.where(lo, zero, k_b), jnp.where(lo, zero, k_a))
    v_lo = (jnp.where(lo, v_a, zero), jnp.where(lo, v_b, zero))
    v_hi = (jnp.where(lo, zero, v_b), jnp.where(lo, zero, v_a))

    row = lax.broadcasted_iota(I32, (Q_BLOCK, nk), 0)
    col = lax.broadcasted_iota(I32, (Q_BLOCK, nk), 1)
    rel = col - Q_BLOCK - row
    valid = ((jnp.abs(rel) <= WINDOW)
             & ((col >= Q_BLOCK) | (i > 0))
             & ((col < 2 * Q_BLOCK) | (i < nb - 1)))
    nt = (((1,), (1,)), ((), ()))
    for h in range(N_KV_HEADS):
        qs = jnp.concatenate([q_ref[:, (2 * h) * LANES:(2 * h + 1) * LANES],
                              q_ref[:, (2 * h + 1) * LANES:(2 * h + 2) * LANES]], axis=0)
        s_par = (lax.dot_general(qs, k_lo[h], nt, preferred_element_type=F32),
                 lax.dot_general(qs, k_hi[h], nt, preferred_element_type=F32))
        vcat = jnp.concatenate([v_lo[h], v_hi[h]], axis=0)
        for c in range(2):
            probs = []
            for par in range(2):
                hq = 4 * h + 2 * c + par
                s = s_par[par][c * Q_BLOCK:(c + 1) * Q_BLOCK, :] + bias_ref[hq]
                s = jnp.where(valid, s, MASK_VALUE)
                sink = sink_ref[hq]
                m = jnp.maximum(jnp.max(s, axis=-1, keepdims=True), sink)
                p = jnp.exp(s - m)
                denom = jnp.sum(p, axis=-1, keepdims=True) + jnp.exp(sink - m)
                probs.append((p * (1.0 / denom)).astype(BF16))
            pcat = jnp.concatenate(probs, axis=1)
            chunk = 2 * h + c
            acc_ref[:, chunk * LANES:(chunk + 1) * LANES] = jnp.dot(
                pcat, vcat, preferred_element_type=F32)
    y = acc_ref[...]
    ms = jnp.mean(y * y, axis=-1, keepdims=True)
    o_ref[...] = (y * lax.rsqrt(ms + NORM_EPS) * g_ref[...]).astype(BF16)


def _t5_bucket(rel):
    nb = N_BUCKETS // 2
    max_exact = nb // 2
    ret = (rel > 0).astype(jnp.int32) * nb
    n = jnp.abs(rel)
    nf = jnp.maximum(n, 1).astype(jnp.float32)
    large = max_exact + (jnp.log(nf / max_exact) / math.log(MAX_DISTANCE / max_exact)
                         * (nb - max_exact)).astype(jnp.int32)
    large = jnp.minimum(large, nb - 1)
    return ret + jnp.where(n < max_exact, n, large)


def _attention(q, kv, sinks, rel_bias, g_out, batch, seq):
    nb = seq // Q_BLOCK
    qi = jnp.arange(Q_BLOCK, dtype=jnp.int32)[:, None]
    kj = jnp.arange(3 * Q_BLOCK, dtype=jnp.int32)[None, :]
    bias = rel_bias.astype(F32)[_t5_bucket(kj - Q_BLOCK - qi)]
    bias = jnp.transpose(bias, (2, 0, 1))
    grid_spec = pltpu.PrefetchScalarGridSpec(
        num_scalar_prefetch=1,
        grid=(batch, nb),
        in_specs=[
            pl.BlockSpec((Q_BLOCK, ATTN_WIDTH), lambda b, i, s: (b * nb + i, 0)),
            pl.BlockSpec((Q_BLOCK, 4 * LANES), lambda b, i, s: (b * nb + jnp.maximum(i - 1, 0), 0)),
            pl.BlockSpec((Q_BLOCK, 4 * LANES), lambda b, i, s: (b * nb + i, 0)),
            pl.BlockSpec((Q_BLOCK, 4 * LANES),
                         lambda b, i, s: (b * nb + jnp.minimum(i + 1, nb - 1), 0)),
            pl.BlockSpec((N_Q_HEADS, Q_BLOCK, 3 * Q_BLOCK), lambda b, i, s: (0, 0, 0)),
            pl.BlockSpec((1, ATTN_WIDTH), lambda b, i, s: (0, 0)),
        ],
        out_specs=pl.BlockSpec((Q_BLOCK, ATTN_WIDTH), lambda b, i, s: (b * nb + i, 0)),
        scratch_shapes=[pltpu.VMEM((Q_BLOCK, ATTN_WIDTH), F32)],
    )
    return pl.pallas_call(
        _attn_kernel,
        grid_spec=grid_spec,
        out_shape=jax.ShapeDtypeStruct((batch * seq, ATTN_WIDTH), BF16),
        compiler_params=_params(2, VMEM_LIMIT),
        name="attention",
    )(sinks.astype(F32), q, kv, kv, kv, bias, g_out.reshape(1, ATTN_WIDTH))


def _outproj_kernel(yf_ref, ya_ref, x_ref, wo_ref, g2_ref, wr_ref, br_ref, tri_ref,
                    x1_ref, h2_ref, idx_ref, gate_ref, rank_ref, cnt_ref, *, n_experts):
    half = yf_ref.shape[1]
    mix = (jnp.dot(yf_ref[...], wo_ref[:half, :], preferred_element_type=F32)
           + jnp.dot(ya_ref[...], wo_ref[half:, :], preferred_element_type=F32))
    x1 = x_ref[...] + mix
    x1_ref[...] = x1
    ms = jnp.mean(x1 * x1, axis=-1, keepdims=True)
    h2 = x1 * lax.rsqrt(ms + NORM_EPS) * g2_ref[...]
    h2_ref[...] = h2
    logits = jnp.dot(h2, wr_ref[...], preferred_element_type=F32,
                     precision=lax.Precision.HIGHEST) + br_ref[...]
    rows = logits.shape[0]
    lane_e = lax.broadcasted_iota(I32, (rows, n_experts), 1).astype(F32)
    work = logits
    vals, idxs = [], []
    for _ in range(TOP_K):
        m = jnp.max(work, axis=-1, keepdims=True)
        ik = jnp.min(jnp.where(work == m, lane_e, float(n_experts)), axis=-1, keepdims=True)
        work = jnp.where(lane_e == ik, -jnp.inf, work)
        vals.append(m)
        idxs.append(ik)
    exps = [jnp.exp(v - vals[0]) for v in vals]
    inv = 1.0 / (exps[0] + exps[1] + exps[2] + exps[3])
    lane_k = lax.broadcasted_iota(I32, (rows, TOP_K), 1)
    gate = jnp.zeros((rows, TOP_K), F32)
    idx = jnp.zeros((rows, TOP_K), F32)
    for k in range(TOP_K):
        gate = jnp.where(lane_k == k, exps[k] * inv, gate)
        idx = jnp.where(lane_k == k, idxs[k], idx)
    gate_ref[...] = gate
    idx_ref[...] = idx.astype(I32)

    lane = lax.broadcasted_iota(I32, (rows, LANES), 1).astype(F32)
    onehot = jnp.zeros((rows, LANES), F32)
    for k in range(TOP_K):
        onehot = onehot + jnp.where(lane == idxs[k] + float(k * n_experts), 1.0, 0.0)
    before = jnp.dot(tri_ref[...], onehot.astype(BF16), preferred_element_type=F32)
    colsum = jnp.sum(onehot, axis=0, keepdims=True)
    lane1 = lax.broadcasted_iota(I32, (1, LANES), 1)
    prefix = jnp.zeros((1, LANES), F32)
    total = colsum
    for k in range(1, TOP_K):
        rolled = pltpu.roll(colsum, k * n_experts, 1)
        prefix = prefix + jnp.where(lane1 >= k * n_experts, rolled, 0.0)
        total = total + rolled
    ranked = (before + prefix) * onehot
    lane_i = lax.broadcasted_iota(I32, (rows, LANES), 1)
    rank = jnp.zeros((rows, TOP_K), F32)
    for k in range(TOP_K):
        seg = (lane_i >= k * n_experts) & (lane_i < (k + 1) * n_experts)
        rk = jnp.sum(jnp.where(seg, ranked, 0.0), axis=-1, keepdims=True)
        rank = jnp.where(lane_k == k, rk, rank)
    rank_ref[...] = rank.astype(I32)
    cnt_ref[0] = total.astype(I32)


def _outproj(yf, ya, x2d, w_out, norm2, w_router, b_router):
    t, d = x2d.shape
    tm = min(TOKEN_TILE, t)
    n_tiles = t // tm
    n_experts = w_router.shape[1]
    assert TOP_K * n_experts == LANES
    tri = np.tril(np.ones((tm, tm), np.float32), -1)
    full = lambda i: (0, 0)
    row = lambda i: (i, 0)
    return pl.pallas_call(
        functools.partial(_outproj_kernel, n_experts=n_experts),
        grid=(n_tiles,),
        in_specs=[
            pl.BlockSpec((tm, yf.shape[1]), row),
            pl.BlockSpec((tm, ya.shape[1]), row),
            pl.BlockSpec((tm, d), row),
            pl.BlockSpec((w_out.shape[0], d), full),
            pl.BlockSpec((1, d), full),
            pl.BlockSpec((d, n_experts), full),
            pl.BlockSpec((1, n_experts), full),
            pl.BlockSpec((tm, tm), full),
        ],
        out_specs=[
            pl.BlockSpec((tm, d), row),
            pl.BlockSpec((tm, d), row),
            pl.BlockSpec((tm, TOP_K), row),
            pl.BlockSpec((tm, TOP_K), row),
            pl.BlockSpec((tm, TOP_K), row),
            pl.BlockSpec((1, 1, LANES), lambda i: (i, 0, 0)),
        ],
        out_shape=[
            jax.ShapeDtypeStruct((t, d), F32),
            jax.ShapeDtypeStruct((t, d), F32),
            jax.ShapeDtypeStruct((t, TOP_K), I32),
            jax.ShapeDtypeStruct((t, TOP_K), F32),
            jax.ShapeDtypeStruct((t, TOP_K), I32),
            jax.ShapeDtypeStruct((n_tiles, 1, LANES), I32),
        ],
        compiler_params=_params(1, VMEM_LIMIT),
        name="outproj_router",
    )(yf, ya, x2d, w_out.astype(BF16), norm2.reshape(1, d), w_router,
      b_router.reshape(1, n_experts), jnp.asarray(tri, BF16))


def _dispatch_kernel(dest_ref, h2_ref, xs_in_ref, xs_ref, sem):
    del xs_in_ref
    rows = h2_ref.shape[0]

    def issue(t, carry):
        for k in range(TOP_K):
            d = dest_ref[0, 0, TOP_K * t + k]
            pltpu.make_async_copy(h2_ref.at[pl.ds(t, 1), :], xs_ref.at[pl.ds(d, 1), :], sem).start()
        return carry

    lax.fori_loop(0, rows, issue, 0)
    for k in range(TOP_K):
        pltpu.make_async_copy(h2_ref, xs_ref.at[pl.ds(0, rows), :], sem).wait()


def _dispatch(dest, h2, n_rows):
    t, d = h2.shape
    tm = min(MOVE_TILE, t)
    dest3 = dest.reshape(t // tm, 1, TOP_K * tm)
    return pl.pallas_call(
        _dispatch_kernel,
        grid=(t // tm,),
        in_specs=[
            pl.BlockSpec((1, 1, TOP_K * tm), lambda i: (i, 0, 0), memory_space=pltpu.SMEM),
            pl.BlockSpec((tm, d), lambda i: (i, 0)),
            pl.BlockSpec(memory_space=pl.ANY),
        ],
        out_specs=pl.BlockSpec(memory_space=pl.ANY),
        out_shape=jax.ShapeDtypeStruct((n_rows, d), F32),
        scratch_shapes=[pltpu.SemaphoreType.DMA(())],
        input_output_aliases={2: 0},
        compiler_params=_params(1),
        name="dispatch",
    )(dest3, h2, jnp.zeros((n_rows, d), F32))


def _combine_kernel(dest_ref, x1_ref, gate_ref, ys_ref, o_ref, buf, sem):
    rows = x1_ref.shape[0]

    def issue(t, carry):
        for k in range(TOP_K):
            d = dest_ref[0, 0, TOP_K * t + k]
            pltpu.make_async_copy(ys_ref.at[pl.ds(d, 1), :], buf.at[k, pl.ds(t, 1), :], sem).start()
        return carry

    lax.fori_loop(0, rows, issue, 0)
    for k in range(TOP_K):
        pltpu.make_async_copy(ys_ref.at[pl.ds(0, rows), :], buf.at[k], sem).wait()
    acc = x1_ref[...]
    gate = gate_ref[...]
    for k in range(TOP_K):
        acc = acc + gate[:, k:k + 1] * buf[k]
    o_ref[...] = acc


def _combine(dest, x1, gate, ys):
    t, d = x1.shape
    tm = min(MOVE_TILE, t)
    dest3 = dest.reshape(t // tm, 1, TOP_K * tm)
    return pl.pallas_call(
        _combine_kernel,
        grid=(t // tm,),
        in_specs=[
            pl.BlockSpec((1, 1, TOP_K * tm), lambda i: (i, 0, 0), memory_space=pltpu.SMEM),
            pl.BlockSpec((tm, d), lambda i: (i, 0)),
            pl.BlockSpec((tm, TOP_K), lambda i: (i, 0)),
            pl.BlockSpec(memory_space=pl.ANY),
        ],
        out_specs=pl.BlockSpec((tm, d), lambda i: (i, 0)),
        out_shape=jax.ShapeDtypeStruct((t, d), F32),
        scratch_shapes=[pltpu.VMEM((TOP_K, tm, d), F32), pltpu.SemaphoreType.DMA(())],
        compiler_params=_params(1, VMEM_LIMIT),
        name="combine",
    )(dest3, x1, gate, ys)


def _expert_kernel(be_ref, meta_ref, xs_ref, wgu_ref, bg_ref, bu_ref, wd_ref, bd_ref, perm_ref,
                   ys_ref, wg_s, wu_s, wd_s):
    i = pl.program_id(0)
    n_used = meta_ref[0]
    active = i < n_used
    new_expert = (i == 0) | (be_ref[i] != be_ref[jnp.maximum(i - 1, 0)])

    @pl.when(active & new_expert)
    def _():
        width = perm_ref.shape[0]
        for c in range(wgu_ref.shape[2] // width):
            wc = wgu_ref[0, :, c * width:(c + 1) * width].astype(BF16)
            r = jnp.dot(wc, perm_ref[...], preferred_element_type=F32)
            wg_s[:, c * LANES:(c + 1) * LANES] = r[:, :LANES].astype(BF16)
            wu_s[:, c * LANES:(c + 1) * LANES] = r[:, LANES:].astype(BF16)
        wd_s[...] = wd_ref[0].astype(BF16)

    @pl.when(active)
    def _():
        xb = xs_ref[...].astype(BF16)
        g = jnp.dot(xb, wg_s[...], preferred_element_type=F32) + bg_ref[0]
        up = jnp.dot(xb, wu_s[...], preferred_element_type=F32) + bu_ref[0]
        g = jnp.minimum(g, SWIGLU_LIMIT)
        up = jnp.clip(up, -SWIGLU_LIMIT, SWIGLU_LIMIT)
        act = g * (1.0 / (1.0 + jnp.exp(-SWIGLU_ALPHA * g))) * (up + 1.0)
        ys_ref[...] = jnp.dot(act.astype(BF16), wd_s[...], preferred_element_type=F32) + bd_ref[0]

    @pl.when(jnp.logical_not(active))
    def _():
        ys_ref[...] = jnp.zeros_like(ys_ref)


def _experts(blk_e, meta, xs, w_gate_up, b_gate_up, w_down, b_down):
    n_rows, d = xs.shape
    n_experts, _, f2 = w_gate_up.shape
    f = f2 // 2
    bm = EXPERT_ROWS
    n_blocks = n_rows // bm
    width = 2 * LANES
    perm = np.zeros((width, width), np.float32)
    perm[2 * np.arange(LANES), np.arange(LANES)] = 1.0
    perm[2 * np.arange(LANES) + 1, LANES + np.arange(LANES)] = 1.0
    bg = b_gate_up[:, 0::2].reshape(n_experts, 1, f)
    bu = b_gate_up[:, 1::2].reshape(n_experts, 1, f)
    rows = lambda i, be, meta: (jnp.minimum(i, meta[0] - 1), 0)
    per_e = lambda i, be, meta: (be[i], 0, 0)
    grid_spec = pltpu.PrefetchScalarGridSpec(
        num_scalar_prefetch=2,
        grid=(n_blocks,),
        in_specs=[
            pl.BlockSpec((bm, d), rows),
            pl.BlockSpec((1, d, f2), per_e),
            pl.BlockSpec((1, 1, f), per_e),
            pl.BlockSpec((1, 1, f), per_e),
            pl.BlockSpec((1, f, d), per_e),
            pl.BlockSpec((1, 1, d), per_e),
            pl.BlockSpec((width, width), lambda i, be, meta: (0, 0)),
        ],
        out_specs=pl.BlockSpec((bm, d), lambda i, be, meta: (i, 0)),
        scratch_shapes=[pltpu.VMEM((d, f), BF16), pltpu.VMEM((d, f), BF16),
                        pltpu.VMEM((f, d), BF16)],
    )
    return pl.pallas_call(
        _expert_kernel,
        grid_spec=grid_spec,
        out_shape=jax.ShapeDtypeStruct((n_rows, d), F32),
        compiler_params=_params(1, VMEM_LIMIT),
        name="experts",
    )(blk_e, meta, xs, w_gate_up, bg, bu, w_down, b_down.reshape(n_experts, 1, d),
      jnp.asarray(perm, BF16))


def _routing_plan(idx, rank, counts, n_experts, tm, bm, n_blocks):
    t = idx.shape[0]
    n_tiles = t // tm
    cnt = counts[:, 0, :n_experts]
    per_expert = jnp.sum(cnt, axis=0)
    padded = (per_expert + bm - 1) // bm * bm
    pend = jnp.cumsum(padded)
    pstart = pend - padded
    base = pstart[None, :] + jnp.cumsum(cnt, axis=0) - cnt
    onehot = idx.reshape(n_tiles, tm, TOP_K, 1) == jnp.arange(n_experts, dtype=I32)
    dest = rank.reshape(n_tiles, tm, TOP_K) + jnp.sum(
        jnp.where(onehot, base[:, None, None, :], 0), axis=-1)
    n_used = pend[-1] // bm
    starts = jnp.arange(n_blocks, dtype=I32) * bm
    blk = jnp.sum((starts[:, None] >= pend[None, :]).astype(I32), axis=1)
    blk = jnp.minimum(blk, n_experts - 1)
    last = jnp.sum((((n_used - 1) * bm) >= pend).astype(I32))
    blk_e = jnp.where(jnp.arange(n_blocks) < n_used, blk, jnp.minimum(last, n_experts - 1))
    return dest.reshape(t, TOP_K).astype(I32), blk_e.astype(I32), n_used.astype(I32).reshape(1)


def _layer(x2d, batch, seq, norm1, w_in, q_norm, k_norm, sinks, rel_bias, w_fourier, g_fourier_out,
           g_attn_out, w_out, norm2, w_router, b_router, w_gate_up, b_gate_up, w_down, b_down):
    t, d = x2d.shape
    n_experts = w_router.shape[1]
    u, q, kv = _inproj(x2d, norm1, w_in, q_norm, k_norm)
    yf = _fourier(u, w_fourier, g_fourier_out, batch, seq)
    ya = _attention(q, kv, sinks, rel_bias, g_attn_out, batch, seq)
    x1, h2, idx, gate, rank, counts = _outproj(yf, ya, x2d, w_out, norm2, w_router, b_router)
    bm = EXPERT_ROWS
    n_rows = t * TOP_K + n_experts * bm
    n_blocks = n_rows // bm
    dest, blk_e, meta = _routing_plan(idx, rank, counts, n_experts, min(TOKEN_TILE, t), bm, n_blocks)
    xs = _dispatch(dest, h2, n_rows)
    ys = _experts(blk_e, meta, xs, w_gate_up, b_gate_up, w_down, b_down)
    return _combine(dest, x1, gate, ys)


def kernel(x, norm1, w_in, q_norm, k_norm, sinks, rel_bias, w_fourier, g_fourier_out, g_attn_out,
           w_out, norm2, w_router, b_router, w_gate_up, b_gate_up, w_down, b_down):
    b, s, d = x.shape
    x2d = x.reshape(b * s, d)
    for l in range(norm1.shape[0]):
        x2d = _layer(x2d, b, s, norm1[l], w_in[l], q_norm[l], k_norm[l], sinks[l], rel_bias,
                     w_fourier[l], g_fourier_out[l], g_attn_out[l], w_out[l], norm2[l],
                     w_router[l], b_router[l], w_gate_up[l], b_gate_up[l], w_down[l], b_down[l])
    return x2d.reshape(b, s, d)
```

```python
import functools
import math

import jax
import jax.numpy as jnp
import numpy as np
from jax import lax
from jax.experimental import pallas as pl
from jax.experimental.pallas import tpu as pltpu

F32 = jnp.float32
BF16 = jnp.bfloat16
I32 = jnp.int32

NORM_EPS = 1e-5
QK_EPS = 1e-6
HEAD_DIM = 64
N_Q_HEADS = 8
N_KV_HEADS = 2
FOURIER_GROUPS = 4
FOURIER_CH = 128
FOURIER_WIDTH = FOURIER_GROUPS * FOURIER_CH
ATTN_WIDTH = N_Q_HEADS * HEAD_DIM
KV_WIDTH = N_KV_HEADS * HEAD_DIM
WINDOW = 128
Q_BLOCK = 128
N_BUCKETS = 32
MAX_DISTANCE = 128
TOP_K = 4
SWIGLU_ALPHA = 1.702
SWIGLU_LIMIT = 7.0
MASK_VALUE = -1e30

LANES = 128
TOKEN_TILE = 512
MOVE_TILE = 256
EXPERT_ROWS = 512
VMEM_LIMIT = 56 * 1024 * 1024


def _params(n_axes, vmem=None):
    return pltpu.CompilerParams(
        dimension_semantics=("arbitrary",) * n_axes, vmem_limit_bytes=vmem)


def _pair_head_norm(xc, gain, lo):
    x2 = xc * xc
    s_lo = jnp.sum(jnp.where(lo, x2, 0.0), axis=-1, keepdims=True)
    s_hi = jnp.sum(jnp.where(lo, 0.0, x2), axis=-1, keepdims=True)
    inv = jnp.where(lo, lax.rsqrt(s_lo * (1.0 / HEAD_DIM) + QK_EPS),
                    lax.rsqrt(s_hi * (1.0 / HEAD_DIM) + QK_EPS))
    return xc * inv * gain


def _inproj_kernel(x_ref, g1_ref, w_ref, qg_ref, kg_ref, u_ref, q_ref, kv_ref):
    x = x_ref[...]
    ms = jnp.mean(x * x, axis=-1, keepdims=True)
    h = (x * lax.rsqrt(ms + NORM_EPS) * g1_ref[...]).astype(BF16)
    z = jnp.dot(h, w_ref[...], preferred_element_type=F32)
    u_ref[...] = z[:, :FOURIER_WIDTH].astype(BF16)
    rows = x.shape[0]
    lo = lax.broadcasted_iota(I32, (rows, LANES), 1) < HEAD_DIM
    q0 = FOURIER_WIDTH
    for c in range(ATTN_WIDTH // LANES):
        qc = _pair_head_norm(z[:, q0 + c * LANES:q0 + (c + 1) * LANES], qg_ref[...], lo)
        q_ref[:, c * LANES:(c + 1) * LANES] = (qc * (HEAD_DIM ** -0.5)).astype(BF16)
    k0 = q0 + ATTN_WIDTH
    kc = _pair_head_norm(z[:, k0:k0 + KV_WIDTH], kg_ref[...], lo)
    vc = z[:, k0 + KV_WIDTH:k0 + 2 * KV_WIDTH]
    kv_ref[:, 0:LANES] = kc.astype(BF16)
    kv_ref[:, LANES:2 * LANES] = pltpu.roll(kc, HEAD_DIM, 1).astype(BF16)
    kv_ref[:, 2 * LANES:3 * LANES] = vc.astype(BF16)
    kv_ref[:, 3 * LANES:4 * LANES] = pltpu.roll(vc, HEAD_DIM, 1).astype(BF16)


def _inproj(x2d, norm1, w_in, q_norm, k_norm):
    t, d = x2d.shape
    tm = min(TOKEN_TILE, t)
    n_in = w_in.shape[1]
    qg = jnp.tile(q_norm, LANES // HEAD_DIM).reshape(1, LANES)
    kg = jnp.tile(k_norm, LANES // HEAD_DIM).reshape(1, LANES)
    full = lambda i: (0, 0)
    return pl.pallas_call(
        _inproj_kernel,
        grid=(t // tm,),
        in_specs=[
            pl.BlockSpec((tm, d), lambda i: (i, 0)),
            pl.BlockSpec((1, d), full),
            pl.BlockSpec((d, n_in), full),
            pl.BlockSpec((1, LANES), full),
            pl.BlockSpec((1, LANES), full),
        ],
        out_specs=[
            pl.BlockSpec((tm, FOURIER_WIDTH), lambda i: (i, 0)),
            pl.BlockSpec((tm, ATTN_WIDTH), lambda i: (i, 0)),
            pl.BlockSpec((tm, 4 * LANES), lambda i: (i, 0)),
        ],
        out_shape=[
            jax.ShapeDtypeStruct((t, FOURIER_WIDTH), BF16),
            jax.ShapeDtypeStruct((t, ATTN_WIDTH), BF16),
            jax.ShapeDtypeStruct((t, 4 * LANES), BF16),
        ],
        compiler_params=_params(1, VMEM_LIMIT),
        name="inproj",
    )(x2d, norm1.reshape(1, d), w_in.astype(BF16), qg, kg)


def _fourier_kernel(u_ref, cs_ref, ss_ref, cc_ref, sc_ref, wf_ref, g_ref, o_ref, p_scr, q_scr,
                    *, scale, row_block):
    for g in range(FOURIER_GROUPS):
        sl = slice(g * FOURIER_CH, (g + 1) * FOURIER_CH)
        w = wf_ref[g].astype(BF16)
        a = (jnp.dot(cc_ref[...], w, preferred_element_type=F32) * scale).astype(BF16)
        b = (jnp.dot(sc_ref[...], w, preferred_element_type=F32) * scale).astype(BF16)
        ug = u_ref[:, sl]
        p_scr[:, sl] = jnp.dot(ug, a, preferred_element_type=F32).astype(BF16)
        q_scr[:, sl] = jnp.dot(ug, b, preferred_element_type=F32).astype(BF16)
    s = u_ref.shape[0]
    for r in range(s // row_block):
        rs = slice(r * row_block, (r + 1) * row_block)
        y = (jnp.dot(cs_ref[rs, :], p_scr[...], preferred_element_type=F32)
             + jnp.dot(ss_ref[rs, :], q_scr[...], preferred_element_type=F32))
        ms = jnp.mean(y * y, axis=-1, keepdims=True)
        o_ref[rs, :] = (y * lax.rsqrt(ms + NORM_EPS) * g_ref[...]).astype(BF16)


def _dft_tables(n):
    k = np.arange(n, dtype=np.int64)
    ang = 2.0 * np.pi * ((k[:, None] * k[None, :]) % n).astype(np.float64) / n
    return np.cos(ang), np.sin(ang)


def _fourier(u, w_fourier, g_out, batch, seq):
    cs, ss = _dft_tables(seq)
    cc, sc = _dft_tables(FOURIER_CH)
    scale = 1.0 / math.sqrt(seq * FOURIER_CH)
    row_block = min(512, seq)
    full2 = lambda b: (0, 0)
    return pl.pallas_call(
        functools.partial(_fourier_kernel, scale=scale, row_block=row_block),
        grid=(batch,),
        in_specs=[
            pl.BlockSpec((seq, FOURIER_WIDTH), lambda b: (b, 0)),
            pl.BlockSpec((seq, seq), full2),
            pl.BlockSpec((seq, seq), full2),
            pl.BlockSpec((FOURIER_CH, FOURIER_CH), full2),
            pl.BlockSpec((FOURIER_CH, FOURIER_CH), full2),
            pl.BlockSpec((FOURIER_GROUPS, FOURIER_CH, FOURIER_CH), lambda b: (0, 0, 0)),
            pl.BlockSpec((1, FOURIER_WIDTH), full2),
        ],
        out_specs=pl.BlockSpec((seq, FOURIER_WIDTH), lambda b: (b, 0)),
        out_shape=jax.ShapeDtypeStruct((batch * seq, FOURIER_WIDTH), BF16),
        scratch_shapes=[pltpu.VMEM((seq, FOURIER_WIDTH), BF16),
                        pltpu.VMEM((seq, FOURIER_WIDTH), BF16)],
        compiler_params=_params(1, VMEM_LIMIT),
        name="fourier",
    )(u, jnp.asarray(cs, BF16), jnp.asarray(ss, BF16), jnp.asarray(cc, BF16),
      jnp.asarray(-sc, BF16), w_fourier, g_out.reshape(1, FOURIER_WIDTH))


def _attn_kernel(sink_ref, q_ref, kvp_ref, kvo_ref, kvn_ref, bias_ref, g_ref, o_ref, acc_ref):
    i = pl.program_id(1)
    nb = pl.num_programs(1)
    kv = jnp.concatenate([kvp_ref[...], kvo_ref[...], kvn_ref[...]], axis=0)
    nk = kv.shape[0]
    lo = lax.broadcasted_iota(I32, (nk, LANES), 1) < HEAD_DIM
    k_a, k_b = kv[:, 0:LANES], kv[:, LANES:2 * LANES]
    v_a, v_b = kv[:, 2 * LANES:3 * LANES], kv[:, 3 * LANES:4 * LANES]
    zero = jnp.zeros_like(k_a)
    k_lo = (jnp.where(lo, k_a, zero), jnp.where(lo, k_b, zero))
    k_hi = (jnp.where(lo, zero, k_b), jnp.where(lo, zero, k_a))
    v_lo = (jnp.where(lo, v_a, zero), jnp.where(lo, v_b, zero))
    v_hi = (jnp.where(lo, zero, v_b), jnp.where(lo, zero, v_a))

    row = lax.broadcasted_iota(I32, (Q_BLOCK, nk), 0)
    col = lax.broadcasted_iota(I32, (Q_BLOCK, nk), 1)
    rel = col - Q_BLOCK - row
    valid = ((jnp.abs(rel) <= WINDOW)
             & ((col >= Q_BLOCK) | (i > 0))
             & ((col < 2 * Q_BLOCK) | (i < nb - 1)))
    nt = (((1,), (1,)), ((), ()))
    for h in range(N_KV_HEADS):
        qs = jnp.concatenate([q_ref[:, (2 * h) * LANES:(2 * h + 1) * LANES],
                              q_ref[:, (2 * h + 1) * LANES:(2 * h + 2) * LANES]], axis=0)
        s_par = (lax.dot_general(qs, k_lo[h], nt, preferred_element_type=F32),
                 lax.dot_general(qs, k_hi[h], nt, preferred_element_type=F32))
        vcat = jnp.concatenate([v_lo[h], v_hi[h]], axis=0)
        for c in range(2):
            probs = []
            for par in range(2):
                hq = 4 * h + 2 * c + par
                s = s_par[par][c * Q_BLOCK:(c + 1) * Q_BLOCK, :] + bias_ref[hq]
                s = jnp.where(valid, s, MASK_VALUE)
                sink = sink_ref[hq]
                m = jnp.maximum(jnp.max(s, axis=-1, keepdims=True), sink)
                p = jnp.exp(s - m)
                denom = jnp.sum(p, axis=-1, keepdims=True) + jnp.exp(sink - m)
                probs.append((p * (1.0 / denom)).astype(BF16))
            pcat = jnp.concatenate(probs, axis=1)
            chunk = 2 * h + c
            acc_ref[:, chunk * LANES:(chunk + 1) * LANES] = jnp.dot(
                pcat, vcat, preferred_element_type=F32)
    y = acc_ref[...]
    ms = jnp.mean(y * y, axis=-1, keepdims=True)
    o_ref[...] = (y * lax.rsqrt(ms + NORM_EPS) * g_ref[...]).astype(BF16)


def _t5_bucket(rel):
    nb = N_BUCKETS // 2
    max_exact = nb // 2
    ret = (rel > 0).astype(jnp.int32) * nb
    n = jnp.abs(rel)
    nf = jnp.maximum(n, 1).astype(jnp.float32)
    large = max_exact + (jnp.log(nf / max_exact) / math.log(MAX_DISTANCE / max_exact)
                         * (nb - max_exact)).astype(jnp.int32)
    large = jnp.minimum(large, nb - 1)
    return ret + jnp.where(n < max_exact, n, large)


def _attention(q, kv, sinks, rel_bias, g_out, batch, seq):
    nb = seq // Q_BLOCK
    qi = jnp.arange(Q_BLOCK, dtype=jnp.int32)[:, None]
    kj = jnp.arange(3 * Q_BLOCK, dtype=jnp.int32)[None, :]
    bucket = _t5_bucket(kj - Q_BLOCK - qi)
    hit = bucket[None, :, :, None] == jnp.arange(N_BUCKETS, dtype=jnp.int32)
    bias = jnp.sum(jnp.where(hit, rel_bias.astype(F32).T[:, None, None, :], 0.0), axis=-1)
    grid_spec = pltpu.PrefetchScalarGridSpec(
        num_scalar_prefetch=1,
        grid=(batch, nb),
        in_specs=[
            pl.BlockSpec((Q_BLOCK, ATTN_WIDTH), lambda b, i, s: (b * nb + i, 0)),
            pl.BlockSpec((Q_BLOCK, 4 * LANES), lambda b, i, s: (b * nb + jnp.maximum(i - 1, 0), 0)),
            pl.BlockSpec((Q_BLOCK, 4 * LANES), lambda b, i, s: (b * nb + i, 0)),
            pl.BlockSpec((Q_BLOCK, 4 * LANES),
                         lambda b, i, s: (b * nb + jnp.minimum(i + 1, nb - 1), 0)),
            pl.BlockSpec((N_Q_HEADS, Q_BLOCK, 3 * Q_BLOCK), lambda b, i, s: (0, 0, 0)),
            pl.BlockSpec((1, ATTN_WIDTH), lambda b, i, s: (0, 0)),
        ],
        out_specs=pl.BlockSpec((Q_BLOCK, ATTN_WIDTH), lambda b, i, s: (b * nb + i, 0)),
        scratch_shapes=[pltpu.VMEM((Q_BLOCK, ATTN_WIDTH), F32)],
    )
    return pl.pallas_call(
        _attn_kernel,
        grid_spec=grid_spec,
        out_shape=jax.ShapeDtypeStruct((batch * seq, ATTN_WIDTH), BF16),
        compiler_params=_params(2, VMEM_LIMIT),
        name="attention",
    )(sinks.astype(F32), q, kv, kv, kv, bias, g_out.reshape(1, ATTN_WIDTH))


def _outproj_kernel(yf_ref, ya_ref, x_ref, wo_ref, g2_ref, wr_ref, br_ref, tri_ref,
                    x1_ref, h2_ref, idx_ref, gate_ref, rank_ref, cnt_ref, *, n_experts):
    half = yf_ref.shape[1]
    mix = (jnp.dot(yf_ref[...], wo_ref[:half, :], preferred_element_type=F32)
           + jnp.dot(ya_ref[...], wo_ref[half:, :], preferred_element_type=F32))
    x1 = x_ref[...] + mix
    x1_ref[...] = x1
    ms = jnp.mean(x1 * x1, axis=-1, keepdims=True)
    h2 = x1 * lax.rsqrt(ms + NORM_EPS) * g2_ref[...]
    h2_ref[...] = h2
    h_hi = h2.astype(BF16)
    h_lo = (h2 - h_hi.astype(F32)).astype(BF16)
    logits = (jnp.dot(h_hi, wr_ref[0], preferred_element_type=F32)
              + jnp.dot(h_hi, wr_ref[1], preferred_element_type=F32)
              + jnp.dot(h_lo, wr_ref[0], preferred_element_type=F32)) + br_ref[...]
    rows = logits.shape[0]
    lane_e = lax.broadcasted_iota(I32, (rows, n_experts), 1).astype(F32)
    work = logits
    vals, idxs = [], []
    for _ in range(TOP_K):
        m = jnp.max(work, axis=-1, keepdims=True)
        ik = jnp.min(jnp.where(work == m, lane_e, float(n_experts)), axis=-1, keepdims=True)
        work = jnp.where(lane_e == ik, -jnp.inf, work)
        vals.append(m)
        idxs.append(ik)
    exps = [jnp.exp(v - vals[0]) for v in vals]
    inv = 1.0 / (exps[0] + exps[1] + exps[2] + exps[3])
    lane_k = lax.broadcasted_iota(I32, (rows, TOP_K), 1)
    gate = jnp.zeros((rows, TOP_K), F32)
    idx = jnp.zeros((rows, TOP_K), F32)
    for k in range(TOP_K):
        gate = jnp.where(lane_k == k, exps[k] * inv, gate)
        idx = jnp.where(lane_k == k, idxs[k], idx)
    gate_ref[...] = gate
    idx_ref[...] = idx.astype(I32)

    lane = lax.broadcasted_iota(I32, (rows, LANES), 1).astype(F32)
    onehot = jnp.zeros((rows, LANES), F32)
    for k in range(TOP_K):
        onehot = onehot + jnp.where(lane == idxs[k] + float(k * n_experts), 1.0, 0.0)
    before = jnp.dot(tri_ref[...], onehot.astype(BF16), preferred_element_type=F32)
    colsum = jnp.sum(onehot, axis=0, keepdims=True)
    lane1 = lax.broadcasted_iota(I32, (1, LANES), 1)
    prefix = jnp.zeros((1, LANES), F32)
    total = colsum
    for k in range(1, TOP_K):
        rolled = pltpu.roll(colsum, k * n_experts, 1)
        prefix = prefix + jnp.where(lane1 >= k * n_experts, rolled, 0.0)
        total = total + rolled
    ranked = (before + prefix) * onehot
    lane_i = lax.broadcasted_iota(I32, (rows, LANES), 1)
    rank = jnp.zeros((rows, TOP_K), F32)
    for k in range(TOP_K):
        seg = (lane_i >= k * n_experts) & (lane_i < (k + 1) * n_experts)
        rk = jnp.sum(jnp.where(seg, ranked, 0.0), axis=-1, keepdims=True)
        rank = jnp.where(lane_k == k, rk, rank)
    rank_ref[...] = rank.astype(I32)
    cnt_ref[0] = total.astype(I32)


def _outproj(yf, ya, x2d, w_out, norm2, w_router, b_router):
    t, d = x2d.shape
    tm = min(TOKEN_TILE, t)
    n_tiles = t // tm
    n_experts = w_router.shape[1]
    assert TOP_K * n_experts == LANES
    tri = np.tril(np.ones((tm, tm), np.float32), -1)
    wr_hi = w_router.astype(BF16)
    wr_lo = (w_router - wr_hi.astype(F32)).astype(BF16)
    full = lambda i: (0, 0)
    row = lambda i: (i, 0)
    return pl.pallas_call(
        functools.partial(_outproj_kernel, n_experts=n_experts),
        grid=(n_tiles,),
        in_specs=[
            pl.BlockSpec((tm, yf.shape[1]), row),
            pl.BlockSpec((tm, ya.shape[1]), row),
            pl.BlockSpec((tm, d), row),
            pl.BlockSpec((w_out.shape[0], d), full),
            pl.BlockSpec((1, d), full),
            pl.BlockSpec((2, d, n_experts), lambda i: (0, 0, 0)),
            pl.BlockSpec((1, n_experts), full),
            pl.BlockSpec((tm, tm), full),
        ],
        out_specs=[
            pl.BlockSpec((tm, d), row),
            pl.BlockSpec((tm, d), row),
            pl.BlockSpec((tm, TOP_K), row),
            pl.BlockSpec((tm, TOP_K), row),
            pl.BlockSpec((tm, TOP_K), row),
            pl.BlockSpec((1, 1, LANES), lambda i: (i, 0, 0)),
        ],
        out_shape=[
            jax.ShapeDtypeStruct((t, d), F32),
            jax.ShapeDtypeStruct((t, d), F32),
            jax.ShapeDtypeStruct((t, TOP_K), I32),
            jax.ShapeDtypeStruct((t, TOP_K), F32),
            jax.ShapeDtypeStruct((t, TOP_K), I32),
            jax.ShapeDtypeStruct((n_tiles, 1, LANES), I32),
        ],
        compiler_params=_params(1, VMEM_LIMIT),
        name="outproj_router",
    )(yf, ya, x2d, w_out.astype(BF16), norm2.reshape(1, d), jnp.stack([wr_hi, wr_lo]),
      b_router.reshape(1, n_experts), jnp.asarray(tri, BF16))


def _dispatch_kernel(dest_ref, h2_ref, xs_in_ref, xs_ref, sem):
    del xs_in_ref
    rows = h2_ref.shape[0]

    def issue(t, carry):
        for k in range(TOP_K):
            d = dest_ref[0, 0, TOP_K * t + k]
            pltpu.make_async_copy(h2_ref.at[pl.ds(t, 1), :], xs_ref.at[pl.ds(d, 1), :], sem).start(
                priority=k % 2)
        return carry

    lax.fori_loop(0, rows, issue, 0)
    for k in range(TOP_K):
        pltpu.make_async_copy(h2_ref, xs_ref.at[pl.ds(0, rows), :], sem).wait()


def _dispatch(dest, h2, n_rows):
    t, d = h2.shape
    tm = min(MOVE_TILE, t)
    dest3 = dest.reshape(t // tm, 1, TOP_K * tm)
    return pl.pallas_call(
        _dispatch_kernel,
        grid=(t // tm,),
        in_specs=[
            pl.BlockSpec((1, 1, TOP_K * tm), lambda i: (i, 0, 0), memory_space=pltpu.SMEM),
            pl.BlockSpec((tm, d), lambda i: (i, 0)),
            pl.BlockSpec(memory_space=pl.ANY),
        ],
        out_specs=pl.BlockSpec(memory_space=pl.ANY),
        out_shape=jax.ShapeDtypeStruct((n_rows, d), F32),
        scratch_shapes=[pltpu.SemaphoreType.DMA(())],
        input_output_aliases={2: 0},
        compiler_params=_params(1),
        name="dispatch",
    )(dest3, h2, jnp.zeros((n_rows, d), F32))


def _combine_kernel(dest_ref, x1_ref, gate_ref, ys_ref, o_ref, buf, sem):
    rows = x1_ref.shape[0]

    def issue(t, carry):
        for k in range(TOP_K):
            d = dest_ref[0, 0, TOP_K * t + k]
            pltpu.make_async_copy(ys_ref.at[pl.ds(d, 1), :], buf.at[k, pl.ds(t, 1), :], sem).start(
                priority=k % 2)
        return carry

    lax.fori_loop(0, rows, issue, 0)
    for k in range(TOP_K):
        pltpu.make_async_copy(ys_ref.at[pl.ds(0, rows), :], buf.at[k], sem).wait()
    acc = x1_ref[...]
    gate = gate_ref[...]
    for k in range(TOP_K):
        acc = acc + gate[:, k:k + 1] * buf[k]
    o_ref[...] = acc


def _combine(dest, x1, gate, ys):
    t, d = x1.shape
    tm = min(MOVE_TILE, t)
    dest3 = dest.reshape(t // tm, 1, TOP_K * tm)
    return pl.pallas_call(
        _combine_kernel,
        grid=(t // tm,),
        in_specs=[
            pl.BlockSpec((1, 1, TOP_K * tm), lambda i: (i, 0, 0), memory_space=pltpu.SMEM),
            pl.BlockSpec((tm, d), lambda i: (i, 0)),
            pl.BlockSpec((tm, TOP_K), lambda i: (i, 0)),
            pl.BlockSpec(memory_space=pl.ANY),
        ],
        out_specs=pl.BlockSpec((tm, d), lambda i: (i, 0)),
        out_shape=jax.ShapeDtypeStruct((t, d), F32),
        scratch_shapes=[pltpu.VMEM((TOP_K, tm, d), F32), pltpu.SemaphoreType.DMA(())],
        compiler_params=_params(1, VMEM_LIMIT),
        name="combine",
    )(dest3, x1, gate, ys)


def _expert_kernel(be_ref, meta_ref, xs_ref, wgu_ref, bg_ref, bu_ref, wd_ref, bd_ref, perm_ref,
                   ys_ref, wg_s, wu_s, wd_s):
    i = pl.program_id(0)
    n_used = meta_ref[0]
    active = i < n_used
    new_expert = (i == 0) | (be_ref[i] != be_ref[jnp.maximum(i - 1, 0)])

    @pl.when(active & new_expert)
    def _():
        width = perm_ref.shape[0]
        for c in range(wgu_ref.shape[2] // width):
            wc = wgu_ref[0, :, c * width:(c + 1) * width].astype(BF16)
            r = jnp.dot(wc, perm_ref[...], preferred_element_type=F32)
            wg_s[:, c * LANES:(c + 1) * LANES] = r[:, :LANES].astype(BF16)
            wu_s[:, c * LANES:(c + 1) * LANES] = r[:, LANES:].astype(BF16)
        wd_s[...] = wd_ref[0].astype(BF16)

    @pl.when(active)
    def _():
        xb = xs_ref[...].astype(BF16)
        g = jnp.dot(xb, wg_s[...], preferred_element_type=F32) + bg_ref[0]
        up = jnp.dot(xb, wu_s[...], preferred_element_type=F32) + bu_ref[0]
        g = jnp.minimum(g, SWIGLU_LIMIT)
        up = jnp.clip(up, -SWIGLU_LIMIT, SWIGLU_LIMIT)
        act = g * (1.0 / (1.0 + jnp.exp(-SWIGLU_ALPHA * g))) * (up + 1.0)
        ys_ref[...] = jnp.dot(act.astype(BF16), wd_s[...], preferred_element_type=F32) + bd_ref[0]

    @pl.when(jnp.logical_not(active))
    def _():
        ys_ref[...] = jnp.zeros_like(ys_ref)


def _experts(blk_e, meta, xs, w_gate_up, b_gate_up, w_down, b_down):
    n_rows, d = xs.shape
    n_experts, _, f2 = w_gate_up.shape
    f = f2 // 2
    bm = EXPERT_ROWS
    n_blocks = n_rows // bm
    width = 2 * LANES
    perm = np.zeros((width, width), np.float32)
    perm[2 * np.arange(LANES), np.arange(LANES)] = 1.0
    perm[2 * np.arange(LANES) + 1, LANES + np.arange(LANES)] = 1.0
    bg = b_gate_up[:, 0::2].reshape(n_experts, 1, f)
    bu = b_gate_up[:, 1::2].reshape(n_experts, 1, f)
    rows = lambda i, be, meta: (jnp.minimum(i, meta[0] - 1), 0)
    per_e = lambda i, be, meta: (be[i], 0, 0)
    grid_spec = pltpu.PrefetchScalarGridSpec(
        num_scalar_prefetch=2,
        grid=(n_blocks,),
        in_specs=[
            pl.BlockSpec((bm, d), rows),
            pl.BlockSpec((1, d, f2), per_e),
            pl.BlockSpec((1, 1, f), per_e),
            pl.BlockSpec((1, 1, f), per_e),
            pl.BlockSpec((1, f, d), per_e),
            pl.BlockSpec((1, 1, d), per_e),
            pl.BlockSpec((width, width), lambda i, be, meta: (0, 0)),
        ],
        out_specs=pl.BlockSpec((bm, d), lambda i, be, meta: (i, 0)),
        scratch_shapes=[pltpu.VMEM((d, f), BF16), pltpu.VMEM((d, f), BF16),
                        pltpu.VMEM((f, d), BF16)],
    )
    return pl.pallas_call(
        _expert_kernel,
        grid_spec=grid_spec,
        out_shape=jax.ShapeDtypeStruct((n_rows, d), F32),
        compiler_params=_params(1, VMEM_LIMIT),
        name="experts",
    )(blk_e, meta, xs, w_gate_up, bg, bu, w_down, b_down.reshape(n_experts, 1, d),
      jnp.asarray(perm, BF16))


def _routing_plan(idx, rank, counts, n_experts, tm, bm, n_blocks):
    t = idx.shape[0]
    n_tiles = t // tm
    cnt = counts[:, 0, :n_experts]
    per_expert = jnp.sum(cnt, axis=0)
    padded = (per_expert + bm - 1) // bm * bm
    pend = jnp.cumsum(padded)
    pstart = pend - padded
    base = pstart[None, :] + jnp.cumsum(cnt, axis=0) - cnt
    onehot = idx.reshape(n_tiles, tm, TOP_K, 1) == jnp.arange(n_experts, dtype=I32)
    dest = rank.reshape(n_tiles, tm, TOP_K) + jnp.sum(
        jnp.where(onehot, base[:, None, None, :], 0), axis=-1)
    n_used = pend[-1] // bm
    starts = jnp.arange(n_blocks, dtype=I32) * bm
    blk = jnp.sum((starts[:, None] >= pend[None, :]).astype(I32), axis=1)
    blk = jnp.minimum(blk, n_experts - 1)
    last = jnp.sum((((n_used - 1) * bm) >= pend).astype(I32))
    blk_e = jnp.where(jnp.arange(n_blocks) < n_used, blk, jnp.minimum(last, n_experts - 1))
    return dest.reshape(t, TOP_K).astype(I32), blk_e.astype(I32), n_used.astype(I32).reshape(1)


def _layer(x2d, batch, seq, norm1, w_in, q_norm, k_norm, sinks, rel_bias, w_fourier, g_fourier_out,
           g_attn_out, w_out, norm2, w_router, b_router, w_gate_up, b_gate_up, w_down, b_down):
    t, d = x2d.shape
    n_experts = w_router.shape[1]
    u, q, kv = _inproj(x2d, norm1, w_in, q_norm, k_norm)
    yf = _fourier(u, w_fourier, g_fourier_out, batch, seq)
    ya = _attention(q, kv, sinks, rel_bias, g_attn_out, batch, seq)
    x1, h2, idx, gate, rank, counts = _outproj(yf, ya, x2d, w_out, norm2, w_router, b_router)
    bm = EXPERT_ROWS
    n_rows = t * TOP_K + n_experts * bm
    n_blocks = n_rows // bm
    dest, blk_e, meta = _routing_plan(idx, rank, counts, n_experts, min(TOKEN_TILE, t), bm, n_blocks)
    xs = _dispatch(dest, h2, n_rows)
    ys = _experts(blk_e, meta, xs, w_gate_up, b_gate_up, w_down, b_down)
    return _combine(dest, x1, gate, ys)


def kernel(x, norm1, w_in, q_norm, k_norm, sinks, rel_bias, w_fourier, g_fourier_out, g_attn_out,
           w_out, norm2, w_router, b_router, w_gate_up, b_gate_up, w_down, b_down):
    b, s, d = x.shape
    x2d = x.reshape(b * s, d)
    for l in range(norm1.shape[0]):
        x2d = _layer(x2d, b, s, norm1[l], w_in[l], q_norm[l], k_norm[l], sinks[l], rel_bias,
                     w_fourier[l], g_fourier_out[l], g_attn_out[l], w_out[l], norm2[l],
                     w_router[l], b_router[l], w_gate_up[l], b_gate_up[l], w_down[l], b_down[l])
    return x2d.reshape(b, s, d)
```

```python
import functools
import math

import jax
import jax.numpy as jnp
import numpy as np
from jax import lax
from jax.experimental import pallas as pl
from jax.experimental.pallas import tpu as pltpu

F32 = jnp.float32
BF16 = jnp.bfloat16
I32 = jnp.int32

NORM_EPS = 1e-5
QK_EPS = 1e-6
HEAD_DIM = 64
N_Q_HEADS = 8
N_KV_HEADS = 2
FOURIER_GROUPS = 4
FOURIER_CH = 128
FOURIER_WIDTH = FOURIER_GROUPS * FOURIER_CH
ATTN_WIDTH = N_Q_HEADS * HEAD_DIM
KV_WIDTH = N_KV_HEADS * HEAD_DIM
WINDOW = 128
Q_BLOCK = 128
N_BUCKETS = 32
MAX_DISTANCE = 128
TOP_K = 4
SWIGLU_ALPHA = 1.702
SWIGLU_LIMIT = 7.0
MASK_VALUE = -1e30

LANES = 128
SUBLANES = 8
TOKEN_TILE = 512
RUN_ALIGN = SUBLANES
PERM_CHUNK = 256
EXPERT_ROWS = 512
VMEM_LIMIT = 56 * 1024 * 1024


def _params(n_axes, vmem=None):
    return pltpu.CompilerParams(
        dimension_semantics=("arbitrary",) * n_axes, vmem_limit_bytes=vmem)


def _pair_head_norm(xc, gain, lo):
    x2 = xc * xc
    s_lo = jnp.sum(jnp.where(lo, x2, 0.0), axis=-1, keepdims=True)
    s_hi = jnp.sum(jnp.where(lo, 0.0, x2), axis=-1, keepdims=True)
    inv = jnp.where(lo, lax.rsqrt(s_lo * (1.0 / HEAD_DIM) + QK_EPS),
                    lax.rsqrt(s_hi * (1.0 / HEAD_DIM) + QK_EPS))
    return xc * inv * gain


def _inproj_kernel(x_ref, g1_ref, w_ref, qg_ref, kg_ref, u_ref, q_ref, kv_ref):
    x = x_ref[...]
    ms = jnp.mean(x * x, axis=-1, keepdims=True)
    h = (x * lax.rsqrt(ms + NORM_EPS) * g1_ref[...]).astype(BF16)
    z = jnp.dot(h, w_ref[...], preferred_element_type=F32)
    u_ref[...] = z[:, :FOURIER_WIDTH].astype(BF16)
    rows = x.shape[0]
    lo = lax.broadcasted_iota(I32, (rows, LANES), 1) < HEAD_DIM
    q0 = FOURIER_WIDTH
    for c in range(ATTN_WIDTH // LANES):
        qc = _pair_head_norm(z[:, q0 + c * LANES:q0 + (c + 1) * LANES], qg_ref[...], lo)
        q_ref[:, c * LANES:(c + 1) * LANES] = (qc * (HEAD_DIM ** -0.5)).astype(BF16)
    k0 = q0 + ATTN_WIDTH
    kc = _pair_head_norm(z[:, k0:k0 + KV_WIDTH], kg_ref[...], lo)
    vc = z[:, k0 + KV_WIDTH:k0 + 2 * KV_WIDTH]
    kv_ref[:, 0:LANES] = kc.astype(BF16)
    kv_ref[:, LANES:2 * LANES] = pltpu.roll(kc, HEAD_DIM, 1).astype(BF16)
    kv_ref[:, 2 * LANES:3 * LANES] = vc.astype(BF16)
    kv_ref[:, 3 * LANES:4 * LANES] = pltpu.roll(vc, HEAD_DIM, 1).astype(BF16)


def _inproj(x2d, norm1, w_in, q_norm, k_norm):
    t, d = x2d.shape
    tm = min(TOKEN_TILE, t)
    n_in = w_in.shape[1]
    qg = jnp.tile(q_norm, LANES // HEAD_DIM).reshape(1, LANES)
    kg = jnp.tile(k_norm, LANES // HEAD_DIM).reshape(1, LANES)
    full = lambda i: (0, 0)
    return pl.pallas_call(
        _inproj_kernel,
        grid=(t // tm,),
        in_specs=[
            pl.BlockSpec((tm, d), lambda i: (i, 0)),
            pl.BlockSpec((1, d), full),
            pl.BlockSpec((d, n_in), full),
            pl.BlockSpec((1, LANES), full),
            pl.BlockSpec((1, LANES), full),
        ],
        out_specs=[
            pl.BlockSpec((tm, FOURIER_WIDTH), lambda i: (i, 0)),
            pl.BlockSpec((tm, ATTN_WIDTH), lambda i: (i, 0)),
            pl.BlockSpec((tm, 4 * LANES), lambda i: (i, 0)),
        ],
        out_shape=[
            jax.ShapeDtypeStruct((t, FOURIER_WIDTH), BF16),
            jax.ShapeDtypeStruct((t, ATTN_WIDTH), BF16),
            jax.ShapeDtypeStruct((t, 4 * LANES), BF16),
        ],
        compiler_params=_params(1, VMEM_LIMIT),
        name="inproj",
    )(x2d, norm1.reshape(1, d), w_in.astype(BF16), qg, kg)


def _fourier_kernel(u_ref, cs_ref, ss_ref, cc_ref, sc_ref, wf_ref, g_ref, o_ref, p_scr, q_scr,
                    *, scale, row_block):
    for g in range(FOURIER_GROUPS):
        sl = slice(g * FOURIER_CH, (g + 1) * FOURIER_CH)
        w = wf_ref[g].astype(BF16)
        a = (jnp.dot(cc_ref[...], w, preferred_element_type=F32) * scale).astype(BF16)
        b = (jnp.dot(sc_ref[...], w, preferred_element_type=F32) * scale).astype(BF16)
        ug = u_ref[:, sl]
        p_scr[:, sl] = jnp.dot(ug, a, preferred_element_type=F32).astype(BF16)
        q_scr[:, sl] = jnp.dot(ug, b, preferred_element_type=F32).astype(BF16)
    s = u_ref.shape[0]
    for r in range(s // row_block):
        rs = slice(r * row_block, (r + 1) * row_block)
        y = (jnp.dot(cs_ref[rs, :], p_scr[...], preferred_element_type=F32)
             + jnp.dot(ss_ref[rs, :], q_scr[...], preferred_element_type=F32))
        ms = jnp.mean(y * y, axis=-1, keepdims=True)
        o_ref[rs, :] = (y * lax.rsqrt(ms + NORM_EPS) * g_ref[...]).astype(BF16)


def _dft_tables(n):
    k = np.arange(n, dtype=np.int64)
    ang = 2.0 * np.pi * ((k[:, None] * k[None, :]) % n).astype(np.float64) / n
    return np.cos(ang), np.sin(ang)


def _fourier(u, w_fourier, g_out, batch, seq):
    cs, ss = _dft_tables(seq)
    cc, sc = _dft_tables(FOURIER_CH)
    scale = 1.0 / math.sqrt(seq * FOURIER_CH)
    row_block = min(512, seq)
    full2 = lambda b: (0, 0)
    return pl.pallas_call(
        functools.partial(_fourier_kernel, scale=scale, row_block=row_block),
        grid=(batch,),
        in_specs=[
            pl.BlockSpec((seq, FOURIER_WIDTH), lambda b: (b, 0)),
            pl.BlockSpec((seq, seq), full2),
            pl.BlockSpec((seq, seq), full2),
            pl.BlockSpec((FOURIER_CH, FOURIER_CH), full2),
            pl.BlockSpec((FOURIER_CH, FOURIER_CH), full2),
            pl.BlockSpec((FOURIER_GROUPS, FOURIER_CH, FOURIER_CH), lambda b: (0, 0, 0)),
            pl.BlockSpec((1, FOURIER_WIDTH), full2),
        ],
        out_specs=pl.BlockSpec((seq, FOURIER_WIDTH), lambda b: (b, 0)),
        out_shape=jax.ShapeDtypeStruct((batch * seq, FOURIER_WIDTH), BF16),
        scratch_shapes=[pltpu.VMEM((seq, FOURIER_WIDTH), BF16),
                        pltpu.VMEM((seq, FOURIER_WIDTH), BF16)],
        compiler_params=_params(1, VMEM_LIMIT),
        name="fourier",
    )(u, jnp.asarray(cs, BF16), jnp.asarray(ss, BF16), jnp.asarray(cc, BF16),
      jnp.asarray(-sc, BF16), w_fourier, g_out.reshape(1, FOURIER_WIDTH))


def _attn_kernel(sink_ref, q_ref, kvp_ref, kvo_ref, kvn_ref, bias_ref, g_ref, o_ref, acc_ref):
    i = pl.program_id(1)
    nb = pl.num_programs(1)
    kv = jnp.concatenate([kvp_ref[...], kvo_ref[...], kvn_ref[...]], axis=0)
    nk = kv.shape[0]
    lo = lax.broadcasted_iota(I32, (nk, LANES), 1) < HEAD_DIM
    k_a, k_b = kv[:, 0:LANES], kv[:, LANES:2 * LANES]
    v_a, v_b = kv[:, 2 * LANES:3 * LANES], kv[:, 3 * LANES:4 * LANES]
    zero = jnp.zeros_like(k_a)
    k_lo = (jnp.where(lo, k_a, zero), jnp.where(lo, k_b, zero))
    k_hi = (jnp.where(lo, zero, k_b), jnp.where(lo, zero, k_a))
    v_lo = (jnp.where(lo, v_a, zero), jnp.where(lo, v_b, zero))
    v_hi = (jnp.where(lo, zero, v_b), jnp.where(lo, zero, v_a))

    row = lax.broadcasted_iota(I32, (Q_BLOCK, nk), 0)
    col = lax.broadcasted_iota(I32, (Q_BLOCK, nk), 1)
    rel = col - Q_BLOCK - row
    valid = ((jnp.abs(rel) <= WINDOW)
             & ((col >= Q_BLOCK) | (i > 0))
             & ((col < 2 * Q_BLOCK) | (i < nb - 1)))
    nt = (((1,), (1,)), ((), ()))
    for h in range(N_KV_HEADS):
        qs = jnp.concatenate([q_ref[:, (2 * h) * LANES:(2 * h + 1) * LANES],
                              q_ref[:, (2 * h + 1) * LANES:(2 * h + 2) * LANES]], axis=0)
        s_par = (lax.dot_general(qs, k_lo[h], nt, preferred_element_type=F32),
                 lax.dot_general(qs, k_hi[h], nt, preferred_element_type=F32))
        vcat = jnp.concatenate([v_lo[h], v_hi[h]], axis=0)
        for c in range(2):
            probs = []
            for par in range(2):
                hq = 4 * h + 2 * c + par
                s = s_par[par][c * Q_BLOCK:(c + 1) * Q_BLOCK, :] + bias_ref[hq]
                s = jnp.where(valid, s, MASK_VALUE)
                sink = sink_ref[hq]
                m = jnp.maximum(jnp.max(s, axis=-1, keepdims=True), sink)
                p = jnp.exp(s - m)
                denom = jnp.sum(p, axis=-1, keepdims=True) + jnp.exp(sink - m)
                probs.append((p * (1.0 / denom)).astype(BF16))
            pcat = jnp.concatenate(probs, axis=1)
            chunk = 2 * h + c
            acc_ref[:, chunk * LANES:(chunk + 1) * LANES] = jnp.dot(
                pcat, vcat, preferred_element_type=F32)
    y = acc_ref[...]
    ms = jnp.mean(y * y, axis=-1, keepdims=True)
    o_ref[...] = (y * lax.rsqrt(ms + NORM_EPS) * g_ref[...]).astype(BF16)


def _t5_bucket(rel):
    nb = N_BUCKETS // 2
    max_exact = nb // 2
    ret = (rel > 0).astype(jnp.int32) * nb
    n = jnp.abs(rel)
    nf = jnp.maximum(n, 1).astype(jnp.float32)
    large = max_exact + (jnp.log(nf / max_exact) / math.log(MAX_DISTANCE / max_exact)
                         * (nb - max_exact)).astype(jnp.int32)
    large = jnp.minimum(large, nb - 1)
    return ret + jnp.where(n < max_exact, n, large)


def _attention(q, kv, sinks, rel_bias, g_out, batch, seq):
    nb = seq // Q_BLOCK
    qi = jnp.arange(Q_BLOCK, dtype=jnp.int32)[:, None]
    kj = jnp.arange(3 * Q_BLOCK, dtype=jnp.int32)[None, :]
    bucket = _t5_bucket(kj - Q_BLOCK - qi)
    hit = bucket[None, :, :, None] == jnp.arange(N_BUCKETS, dtype=jnp.int32)
    bias = jnp.sum(jnp.where(hit, rel_bias.astype(F32).T[:, None, None, :], 0.0), axis=-1)
    grid_spec = pltpu.PrefetchScalarGridSpec(
        num_scalar_prefetch=1,
        grid=(batch, nb),
        in_specs=[
            pl.BlockSpec((Q_BLOCK, ATTN_WIDTH), lambda b, i, s: (b * nb + i, 0)),
            pl.BlockSpec((Q_BLOCK, 4 * LANES), lambda b, i, s: (b * nb + jnp.maximum(i - 1, 0), 0)),
            pl.BlockSpec((Q_BLOCK, 4 * LANES), lambda b, i, s: (b * nb + i, 0)),
            pl.BlockSpec((Q_BLOCK, 4 * LANES),
                         lambda b, i, s: (b * nb + jnp.minimum(i + 1, nb - 1), 0)),
            pl.BlockSpec((N_Q_HEADS, Q_BLOCK, 3 * Q_BLOCK), lambda b, i, s: (0, 0, 0)),
            pl.BlockSpec((1, ATTN_WIDTH), lambda b, i, s: (0, 0)),
        ],
        out_specs=pl.BlockSpec((Q_BLOCK, ATTN_WIDTH), lambda b, i, s: (b * nb + i, 0)),
        scratch_shapes=[pltpu.VMEM((Q_BLOCK, ATTN_WIDTH), F32)],
    )
    return pl.pallas_call(
        _attn_kernel,
        grid_spec=grid_spec,
        out_shape=jax.ShapeDtypeStruct((batch * seq, ATTN_WIDTH), BF16),
        compiler_params=_params(2, VMEM_LIMIT),
        name="attention",
    )(sinks.astype(F32), q, kv, kv, kv, bias, g_out.reshape(1, ATTN_WIDTH))


def _outproj_kernel(yf_ref, ya_ref, x_ref, wo_ref, g2_ref, wr_ref, br_ref, tri_ref,
                    x1_ref, h2_ref, gate_ref, pos_ref, post_ref, cnt_ref, *, n_experts):
    half = yf_ref.shape[1]
    mix = (jnp.dot(yf_ref[...], wo_ref[:half, :], preferred_element_type=F32)
           + jnp.dot(ya_ref[...], wo_ref[half:, :], preferred_element_type=F32))
    x1 = x_ref[...] + mix
    x1_ref[...] = x1
    ms = jnp.mean(x1 * x1, axis=-1, keepdims=True)
    h2 = x1 * lax.rsqrt(ms + NORM_EPS) * g2_ref[...]
    h2_ref[...] = h2.astype(BF16)
    h_hi = h2.astype(BF16)
    h_lo = (h2 - h_hi.astype(F32)).astype(BF16)
    logits = (jnp.dot(h_hi, wr_ref[0], preferred_element_type=F32)
              + jnp.dot(h_hi, wr_ref[1], preferred_element_type=F32)
              + jnp.dot(h_lo, wr_ref[0], preferred_element_type=F32)) + br_ref[...]
    rows = logits.shape[0]
    lane_e = lax.broadcasted_iota(I32, (rows, n_experts), 1).astype(F32)
    work = logits
    vals, idxs = [], []
    for _ in range(TOP_K):
        m = jnp.max(work, axis=-1, keepdims=True)
        ik = jnp.min(jnp.where(work == m, lane_e, float(n_experts)), axis=-1, keepdims=True)
        work = jnp.where(lane_e == ik, -jnp.inf, work)
        vals.append(m)
        idxs.append(ik)
    exps = [jnp.exp(v - vals[0]) for v in vals]
    inv = 1.0 / (exps[0] + exps[1] + exps[2] + exps[3])
    lane_k = lax.broadcasted_iota(I32, (rows, TOP_K), 1)
    gate = jnp.zeros((rows, TOP_K), F32)
    for k in range(TOP_K):
        gate = jnp.where(lane_k == k, exps[k] * inv, gate)
    gate_ref[...] = gate

    lane = lax.broadcasted_iota(I32, (rows, LANES), 1).astype(F32)
    onehot = jnp.zeros((rows, LANES), F32)
    for k in range(TOP_K):
        onehot = onehot + jnp.where(lane == idxs[k] + float(k * n_experts), 1.0, 0.0)
    before = jnp.dot(tri_ref[...], onehot.astype(BF16), preferred_element_type=F32)
    colsum = jnp.sum(onehot, axis=0, keepdims=True)
    lane1 = lax.broadcasted_iota(I32, (1, LANES), 1)
    prefix = jnp.zeros((1, LANES), F32)
    total = colsum
    for k in range(1, TOP_K):
        rolled = pltpu.roll(colsum, k * n_experts, 1)
        prefix = prefix + jnp.where(lane1 >= k * n_experts, rolled, 0.0)
        total = total + rolled
    run = jnp.floor((total + (RUN_ALIGN - 1)) * (1.0 / RUN_ALIGN)) * RUN_ALIGN
    incl = run
    lane_in_seg = lane1 & (n_experts - 1)
    shift = 1
    while shift < n_experts:
        incl = incl + jnp.where(lane_in_seg >= shift, pltpu.roll(incl, shift, 1), 0.0)
        shift *= 2
    run_start = incl - run
    placed = (before + prefix + run_start) * onehot
    lane_i = lax.broadcasted_iota(I32, (rows, LANES), 1)
    pos_lanes = jnp.zeros((rows, LANES), F32)
    for k in range(TOP_K):
        seg = (lane_i >= k * n_experts) & (lane_i < (k + 1) * n_experts)
        pk = jnp.sum(jnp.where(seg, placed, 0.0), axis=-1, keepdims=True)
        pos_lanes = jnp.where(lane_i == k, pk, pos_lanes)
    pos_ref[...] = pos_lanes[:, :TOP_K].astype(I32)
    post_ref[...] = jnp.transpose(pos_lanes)[:SUBLANES, :].astype(I32)
    cnt_ref[0] = total.astype(I32)


def _outproj(yf, ya, x2d, w_out, norm2, w_router, b_router):
    t, d = x2d.shape
    tm = min(TOKEN_TILE, t)
    n_tiles = t // tm
    n_experts = w_router.shape[1]
    assert TOP_K * n_experts == LANES
    tri = np.tril(np.ones((tm, tm), np.float32), -1)
    wr_hi = w_router.astype(BF16)
    wr_lo = (w_router - wr_hi.astype(F32)).astype(BF16)
    full = lambda i: (0, 0)
    row = lambda i: (i, 0)
    return pl.pallas_call(
        functools.partial(_outproj_kernel, n_experts=n_experts),
        grid=(n_tiles,),
        in_specs=[
            pl.BlockSpec((tm, yf.shape[1]), row),
            pl.BlockSpec((tm, ya.shape[1]), row),
            pl.BlockSpec((tm, d), row),
            pl.BlockSpec((w_out.shape[0], d), full),
            pl.BlockSpec((1, d), full),
            pl.BlockSpec((2, d, n_experts), lambda i: (0, 0, 0)),
            pl.BlockSpec((1, n_experts), full),
            pl.BlockSpec((tm, tm), full),
        ],
        out_specs=[
            pl.BlockSpec((tm, d), row),
            pl.BlockSpec((tm, d), row),
            pl.BlockSpec((tm, TOP_K), row),
            pl.BlockSpec((tm, TOP_K), row),
            pl.BlockSpec((SUBLANES, tm), row),
            pl.BlockSpec((1, 1, LANES), lambda i: (i, 0, 0)),
        ],
        out_shape=[
            jax.ShapeDtypeStruct((t, d), F32),
            jax.ShapeDtypeStruct((t, d), BF16),
            jax.ShapeDtypeStruct((t, TOP_K), F32),
            jax.ShapeDtypeStruct((t, TOP_K), I32),
            jax.ShapeDtypeStruct((n_tiles * SUBLANES, tm), I32),
            jax.ShapeDtypeStruct((n_tiles, 1, LANES), I32),
        ],
        compiler_params=_params(1, VMEM_LIMIT),
        name="outproj_router",
    )(yf, ya, x2d, w_out.astype(BF16), norm2.reshape(1, d), jnp.stack([wr_hi, wr_lo]),
      b_router.reshape(1, n_experts), jnp.asarray(tri, BF16))


def _chunk_sizes(limit):
    sizes, s = [], RUN_ALIGN
    while s <= limit:
        sizes.append(s)
        s *= 2
    return tuple(reversed(sizes))


def _for_each_chunk(n_rows, src0, dst0, sizes, fn):
    off = jnp.int32(0)
    for size in sizes:
        take = n_rows & size

        @pl.when(take != 0)
        def _(off=off, size=size):
            fn(pl.multiple_of(src0 + off, RUN_ALIGN), pl.multiple_of(dst0 + off, RUN_ALIGN), size)

        off = off + take


def _dispatch_kernel(cnt_ref, lst_ref, base_ref, tail_ref, post_ref, h2_ref, xs_ref,
                     buf, zbuf, sem, zsem, *, n_experts):
    j = pl.program_id(0)
    tm = h2_ref.shape[0]
    n_local = buf.shape[0]
    run_sizes = _chunk_sizes(tm)
    tail_sizes = _chunk_sizes(zbuf.shape[0])

    def tail_copy(src, dst, size):
        return pltpu.make_async_copy(zbuf.at[pl.ds(0, size), :], xs_ref.at[pl.ds(dst, size), :], zsem)

    def run_copy(src, dst, size):
        return pltpu.make_async_copy(buf.at[pl.ds(src, size), :], xs_ref.at[pl.ds(dst, size), :], sem)

    @pl.when(j == 0)
    def _():
        zbuf[...] = jnp.zeros_like(zbuf)
        for op in ("start", "wait"):
            def tails(e, carry, op=op):
                _for_each_chunk(tail_ref[n_experts + e], 0, tail_ref[e], tail_sizes,
                                lambda s, d, size: getattr(tail_copy(s, d, size), op)())
                return carry
            lax.fori_loop(0, n_experts, tails, 0)

            def spare(b, carry, op=op):
                for half in range(EXPERT_ROWS // zbuf.shape[0]):
                    dst = pl.multiple_of(b * EXPERT_ROWS + half * zbuf.shape[0], RUN_ALIGN)
                    getattr(tail_copy(0, dst, zbuf.shape[0]), op)()
                return carry
            lax.fori_loop(tail_ref[2 * n_experts], xs_ref.shape[0] // EXPERT_ROWS, spare, 0)

    h = h2_ref[...]
    for rc in range(n_local // PERM_CHUNK):
        rows = lax.broadcasted_iota(I32, (PERM_CHUNK, tm), 0) + rc * PERM_CHUNK
        perm = jnp.zeros((PERM_CHUNK, tm), F32)
        for k in range(TOP_K):
            perm = jnp.where(rows == post_ref[k:k + 1, :], 1.0, perm)
        buf[rc * PERM_CHUNK:(rc + 1) * PERM_CHUNK, :] = jnp.dot(
            perm.astype(BF16), h, preferred_element_type=F32)

    for op in ("start", "wait"):
        def runs(e, carry, op=op):
            r = j * n_experts + e
            _for_each_chunk(cnt_ref[r], lst_ref[r], base_ref[r], run_sizes,
                            lambda s, d, size: getattr(run_copy(s, d, size), op)())
            return carry
        lax.fori_loop(0, n_experts, runs, 0)


def _local_rows(tm, n_experts):
    worst = TOP_K * tm + n_experts * (RUN_ALIGN - 1)
    return -(-worst // PERM_CHUNK) * PERM_CHUNK


def _dispatch(plan, post, h2, n_rows, n_experts):
    t, d = h2.shape
    tm = min(TOKEN_TILE, t)
    grid_spec = pltpu.PrefetchScalarGridSpec(
        num_scalar_prefetch=4,
        grid=(t // tm,),
        in_specs=[
            pl.BlockSpec((SUBLANES, tm), lambda i, *_: (i, 0)),
            pl.BlockSpec((tm, d), lambda i, *_: (i, 0)),
        ],
        out_specs=pl.BlockSpec(memory_space=pl.ANY),
        scratch_shapes=[pltpu.VMEM((_local_rows(tm, n_experts), d), F32),
                        pltpu.VMEM((EXPERT_ROWS // 2, d), F32),
                        pltpu.SemaphoreType.DMA(()), pltpu.SemaphoreType.DMA(())],
    )
    return pl.pallas_call(
        functools.partial(_dispatch_kernel, n_experts=n_experts),
        grid_spec=grid_spec,
        out_shape=jax.ShapeDtypeStruct((n_rows, d), F32),
        compiler_params=_params(1, VMEM_LIMIT),
        name="dispatch",
    )(plan["cnt"], plan["lst"], plan["base"], plan["tail"], post, h2)


def _combine_kernel(cnt_ref, lst_ref, base_ref, pos_ref, gate_ref, x1_ref, ys_ref, o_ref,
                    buf, sem, *, n_experts):
    j = pl.program_id(0)
    tm = x1_ref.shape[0]
    n_local = buf.shape[0]
    run_sizes = _chunk_sizes(tm)

    def run_copy(src, dst, size):
        return pltpu.make_async_copy(ys_ref.at[pl.ds(src, size), :], buf.at[pl.ds(dst, size), :], sem)

    @pl.when(j == 0)
    def _():
        buf[...] = jnp.zeros_like(buf)

    for op in ("start", "wait"):
        def runs(e, carry, op=op):
            r = j * n_experts + e
            _for_each_chunk(cnt_ref[r], base_ref[r], lst_ref[r], run_sizes,
                            lambda s, d, size: getattr(run_copy(s, d, size), op)())
            return carry
        lax.fori_loop(0, n_experts, runs, 0)

    acc = x1_ref[...]
    pos = pos_ref[...]
    gate = gate_ref[...]
    for rc in range(n_local // PERM_CHUNK):
        cols = lax.broadcasted_iota(I32, (tm, PERM_CHUNK), 1) + rc * PERM_CHUNK
        g = jnp.zeros((tm, PERM_CHUNK), F32)
        for k in range(TOP_K):
            g = jnp.where(cols == pos[:, k:k + 1], gate[:, k:k + 1], g)
        y = buf[rc * PERM_CHUNK:(rc + 1) * PERM_CHUNK, :].astype(BF16)
        acc = acc + jnp.dot(g.astype(BF16), y, preferred_element_type=F32)
    o_ref[...] = acc


def _combine(plan, pos, gate, x1, ys, n_experts):
    t, d = x1.shape
    tm = min(TOKEN_TILE, t)
    grid_spec = pltpu.PrefetchScalarGridSpec(
        num_scalar_prefetch=3,
        grid=(t // tm,),
        in_specs=[
            pl.BlockSpec((tm, TOP_K), lambda i, *_: (i, 0)),
            pl.BlockSpec((tm, TOP_K), lambda i, *_: (i, 0)),
            pl.BlockSpec((tm, d), lambda i, *_: (i, 0)),
            pl.BlockSpec(memory_space=pl.ANY),
        ],
        out_specs=pl.BlockSpec((tm, d), lambda i, *_: (i, 0)),
        scratch_shapes=[pltpu.VMEM((_local_rows(tm, n_experts), d), F32),
                        pltpu.SemaphoreType.DMA(())],
    )
    return pl.pallas_call(
        functools.partial(_combine_kernel, n_experts=n_experts),
        grid_spec=grid_spec,
        out_shape=jax.ShapeDtypeStruct((t, d), F32),
        compiler_params=_params(1, VMEM_LIMIT),
        name="combine",
    )(plan["cnt"], plan["lst"], plan["base"], pos, gate, x1, ys)


def _expert_kernel(be_ref, meta_ref, xs_ref, wgu_ref, bg_ref, bu_ref, wd_ref, bd_ref, perm_ref,
                   ys_ref, wg_s, wu_s, wd_s):
    i = pl.program_id(0)
    n_used = meta_ref[0]
    active = i < n_used
    new_expert = (i == 0) | (be_ref[i] != be_ref[jnp.maximum(i - 1, 0)])

    @pl.when(active & new_expert)
    def _():
        width = perm_ref.shape[0]
        for c in range(wgu_ref.shape[2] // width):
            wc = wgu_ref[0, :, c * width:(c + 1) * width].astype(BF16)
            r = jnp.dot(wc, perm_ref[...], preferred_element_type=F32)
            wg_s[:, c * LANES:(c + 1) * LANES] = r[:, :LANES].astype(BF16)
            wu_s[:, c * LANES:(c + 1) * LANES] = r[:, LANES:].astype(BF16)
        wd_s[...] = wd_ref[0].astype(BF16)

    @pl.when(active)
    def _():
        xb = xs_ref[...].astype(BF16)
        g = jnp.dot(xb, wg_s[...], preferred_element_type=F32) + bg_ref[0]
        up = jnp.dot(xb, wu_s[...], preferred_element_type=F32) + bu_ref[0]
        g = jnp.minimum(g, SWIGLU_LIMIT)
        up = jnp.clip(up, -SWIGLU_LIMIT, SWIGLU_LIMIT)
        act = g * (1.0 / (1.0 + jnp.exp(-SWIGLU_ALPHA * g))) * (up + 1.0)
        ys_ref[...] = jnp.dot(act.astype(BF16), wd_s[...], preferred_element_type=F32) + bd_ref[0]

    @pl.when(jnp.logical_not(active))
    def _():
        ys_ref[...] = jnp.zeros_like(ys_ref)


def _experts(blk_e, meta, xs, w_gate_up, b_gate_up, w_down, b_down):
    n_rows, d = xs.shape
    n_experts, _, f2 = w_gate_up.shape
    f = f2 // 2
    bm = EXPERT_ROWS
    n_blocks = n_rows // bm
    width = 2 * LANES
    perm = np.zeros((width, width), np.float32)
    perm[2 * np.arange(LANES), np.arange(LANES)] = 1.0
    perm[2 * np.arange(LANES) + 1, LANES + np.arange(LANES)] = 1.0
    bg = b_gate_up[:, 0::2].reshape(n_experts, 1, f)
    bu = b_gate_up[:, 1::2].reshape(n_experts, 1, f)
    rows = lambda i, be, meta: (jnp.minimum(i, meta[0] - 1), 0)
    per_e = lambda i, be, meta: (be[i], 0, 0)
    grid_spec = pltpu.PrefetchScalarGridSpec(
        num_scalar_prefetch=2,
        grid=(n_blocks,),
        in_specs=[
            pl.BlockSpec((bm, d), rows),
            pl.BlockSpec((1, d, f2), per_e),
            pl.BlockSpec((1, 1, f), per_e),
            pl.BlockSpec((1, 1, f), per_e),
            pl.BlockSpec((1, f, d), per_e),
            pl.BlockSpec((1, 1, d), per_e),
            pl.BlockSpec((width, width), lambda i, be, meta: (0, 0)),
        ],
        out_specs=pl.BlockSpec((bm, d), lambda i, be, meta: (i, 0)),
        scratch_shapes=[pltpu.VMEM((d, f), BF16), pltpu.VMEM((d, f), BF16),
                        pltpu.VMEM((f, d), BF16)],
    )
    return pl.pallas_call(
        _expert_kernel,
        grid_spec=grid_spec,
        out_shape=jax.ShapeDtypeStruct((n_rows, d), F32),
        compiler_params=_params(1, VMEM_LIMIT),
        name="experts",
    )(blk_e, meta, xs, w_gate_up, bg, bu, w_down, b_down.reshape(n_experts, 1, d),
      jnp.asarray(perm, BF16))


def _routing_plan(counts, n_experts, bm, n_blocks):
    cnt = counts[:, 0, :n_experts]
    run = (cnt + RUN_ALIGN - 1) // RUN_ALIGN * RUN_ALIGN
    per_expert = jnp.sum(run, axis=0)
    padded = (per_expert + bm - 1) // bm * bm
    pend = jnp.cumsum(padded)
    pstart = pend - padded
    base = pstart[None, :] + jnp.cumsum(run, axis=0) - run
    lst = jnp.cumsum(run, axis=1) - run
    n_used = pend[-1] // bm
    tail = jnp.concatenate([pstart + per_expert, padded - per_expert, n_used[None]])
    starts = jnp.arange(n_blocks, dtype=I32) * bm
    blk = jnp.sum((starts[:, None] >= pend[None, :]).astype(I32), axis=1)
    blk = jnp.minimum(blk, n_experts - 1)
    last = jnp.sum((((n_used - 1) * bm) >= pend).astype(I32))
    blk_e = jnp.where(jnp.arange(n_blocks) < n_used, blk, jnp.minimum(last, n_experts - 1))
    plan = {"cnt": run.reshape(-1).astype(I32), "lst": lst.reshape(-1).astype(I32),
            "base": base.reshape(-1).astype(I32), "tail": tail.astype(I32)}
    return plan, blk_e.astype(I32), n_used.astype(I32).reshape(1)


def _layer(x2d, batch, seq, norm1, w_in, q_norm, k_norm, sinks, rel_bias, w_fourier, g_fourier_out,
           g_attn_out, w_out, norm2, w_router, b_router, w_gate_up, b_gate_up, w_down, b_down):
    t, d = x2d.shape
    n_experts = w_router.shape[1]
    u, q, kv = _inproj(x2d, norm1, w_in, q_norm, k_norm)
    yf = _fourier(u, w_fourier, g_fourier_out, batch, seq)
    ya = _attention(q, kv, sinks, rel_bias, g_attn_out, batch, seq)
    x1, h2, gate, pos, post, counts = _outproj(yf, ya, x2d, w_out, norm2, w_router, b_router)
    bm = EXPERT_ROWS
    n_tiles = t // min(TOKEN_TILE, t)
    worst_rows = t * TOP_K + n_tiles * n_experts * (RUN_ALIGN - 1) + n_experts * (bm - RUN_ALIGN)
    n_blocks = -(-worst_rows // bm)
    plan, blk_e, meta = _routing_plan(counts, n_experts, bm, n_blocks)
    xs = _dispatch(plan, post, h2, n_blocks * bm, n_experts)
    ys = _experts(blk_e, meta, xs, w_gate_up, b_gate_up, w_down, b_down)
    return _combine(plan, pos, gate, x1, ys, n_experts)


def kernel(x, norm1, w_in, q_norm, k_norm, sinks, rel_bias, w_fourier, g_fourier_out, g_attn_out,
           w_out, norm2, w_router, b_router, w_gate_up, b_gate_up, w_down, b_down):
    b, s, d = x.shape
    x2d = x.reshape(b * s, d)
    for l in range(norm1.shape[0]):
        x2d = _layer(x2d, b, s, norm1[l], w_in[l], q_norm[l], k_norm[l], sinks[l], rel_bias,
                     w_fourier[l], g_fourier_out[l], g_attn_out[l], w_out[l], norm2[l],
                     w_router[l], b_router[l], w_gate_up[l], b_gate_up[l], w_down[l], b_down[l])
    return x2d.reshape(b, s, d)
```

```python
import functools
import math

import jax
import jax.numpy as jnp
import numpy as np
from jax import lax
from jax.experimental import pallas as pl
from jax.experimental.pallas import tpu as pltpu

F32 = jnp.float32
BF16 = jnp.bfloat16
I32 = jnp.int32

NORM_EPS = 1e-5
QK_EPS = 1e-6
HEAD_DIM = 64
N_Q_HEADS = 8
N_KV_HEADS = 2
FOURIER_GROUPS = 4
FOURIER_CH = 128
FOURIER_WIDTH = FOURIER_GROUPS * FOURIER_CH
ATTN_WIDTH = N_Q_HEADS * HEAD_DIM
KV_WIDTH = N_KV_HEADS * HEAD_DIM
WINDOW = 128
Q_BLOCK = 128
N_BUCKETS = 32
MAX_DISTANCE = 128
TOP_K = 4
SWIGLU_ALPHA = 1.702
SWIGLU_LIMIT = 7.0
MASK_VALUE = -1e30

LANES = 128
SUBLANES = 8
TOKEN_TILE = 512
RUN_ALIGN = SUBLANES
PERM_CHUNK = 256
EXPERT_ROWS = 512
VMEM_LIMIT = 56 * 1024 * 1024


def _params(n_axes, vmem=None):
    return pltpu.CompilerParams(
        dimension_semantics=("arbitrary",) * n_axes, vmem_limit_bytes=vmem)


def _pair_head_norm(xc, gain, lo):
    x2 = xc * xc
    s_lo = jnp.sum(jnp.where(lo, x2, 0.0), axis=-1, keepdims=True)
    s_hi = jnp.sum(jnp.where(lo, 0.0, x2), axis=-1, keepdims=True)
    inv = jnp.where(lo, lax.rsqrt(s_lo * (1.0 / HEAD_DIM) + QK_EPS),
                    lax.rsqrt(s_hi * (1.0 / HEAD_DIM) + QK_EPS))
    return xc * inv * gain


def _inproj_kernel(x_ref, g1_ref, w_ref, qg_ref, kg_ref, u_ref, q_ref, kv_ref):
    x = x_ref[...]
    ms = jnp.mean(x * x, axis=-1, keepdims=True)
    h = (x * lax.rsqrt(ms + NORM_EPS) * g1_ref[...]).astype(BF16)
    z = jnp.dot(h, w_ref[...], preferred_element_type=F32)
    u_ref[...] = z[:, :FOURIER_WIDTH].astype(BF16)
    rows = x.shape[0]
    lo = lax.broadcasted_iota(I32, (rows, LANES), 1) < HEAD_DIM
    q0 = FOURIER_WIDTH
    for c in range(ATTN_WIDTH // LANES):
        qc = _pair_head_norm(z[:, q0 + c * LANES:q0 + (c + 1) * LANES], qg_ref[...], lo)
        q_ref[:, c * LANES:(c + 1) * LANES] = (qc * (HEAD_DIM ** -0.5)).astype(BF16)
    k0 = q0 + ATTN_WIDTH
    kc = _pair_head_norm(z[:, k0:k0 + KV_WIDTH], kg_ref[...], lo)
    vc = z[:, k0 + KV_WIDTH:k0 + 2 * KV_WIDTH]
    kv_ref[:, 0:LANES] = kc.astype(BF16)
    kv_ref[:, LANES:2 * LANES] = pltpu.roll(kc, HEAD_DIM, 1).astype(BF16)
    kv_ref[:, 2 * LANES:3 * LANES] = vc.astype(BF16)
    kv_ref[:, 3 * LANES:4 * LANES] = pltpu.roll(vc, HEAD_DIM, 1).astype(BF16)


def _inproj(x2d, norm1, w_in, q_norm, k_norm):
    t, d = x2d.shape
    tm = min(TOKEN_TILE, t)
    n_in = w_in.shape[1]
    qg = jnp.tile(q_norm, LANES // HEAD_DIM).reshape(1, LANES)
    kg = jnp.tile(k_norm, LANES // HEAD_DIM).reshape(1, LANES)
    full = lambda i: (0, 0)
    return pl.pallas_call(
        _inproj_kernel,
        grid=(t // tm,),
        in_specs=[
            pl.BlockSpec((tm, d), lambda i: (i, 0)),
            pl.BlockSpec((1, d), full),
            pl.BlockSpec((d, n_in), full),
            pl.BlockSpec((1, LANES), full),
            pl.BlockSpec((1, LANES), full),
        ],
        out_specs=[
            pl.BlockSpec((tm, FOURIER_WIDTH), lambda i: (i, 0)),
            pl.BlockSpec((tm, ATTN_WIDTH), lambda i: (i, 0)),
            pl.BlockSpec((tm, 4 * LANES), lambda i: (i, 0)),
        ],
        out_shape=[
            jax.ShapeDtypeStruct((t, FOURIER_WIDTH), BF16),
            jax.ShapeDtypeStruct((t, ATTN_WIDTH), BF16),
            jax.ShapeDtypeStruct((t, 4 * LANES), BF16),
        ],
        compiler_params=_params(1, VMEM_LIMIT),
        name="inproj",
    )(x2d, norm1.reshape(1, d), w_in.astype(BF16), qg, kg)


def _fourier_kernel(u_ref, cs_ref, ss_ref, cc_ref, sc_ref, wf_ref, g_ref, o_ref, p_scr, q_scr,
                    *, scale, row_block):
    for g in range(FOURIER_GROUPS):
        sl = slice(g * FOURIER_CH, (g + 1) * FOURIER_CH)
        w = wf_ref[g].astype(BF16)
        a = (jnp.dot(cc_ref[...], w, preferred_element_type=F32) * scale).astype(BF16)
        b = (jnp.dot(sc_ref[...], w, preferred_element_type=F32) * scale).astype(BF16)
        ug = u_ref[:, sl]
        p_scr[:, sl] = jnp.dot(ug, a, preferred_element_type=F32).astype(BF16)
        q_scr[:, sl] = jnp.dot(ug, b, preferred_element_type=F32).astype(BF16)
    s = u_ref.shape[0]
    for r in range(s // row_block):
        rs = slice(r * row_block, (r + 1) * row_block)
        y = (jnp.dot(cs_ref[rs, :], p_scr[...], preferred_element_type=F32)
             + jnp.dot(ss_ref[rs, :], q_scr[...], preferred_element_type=F32))
        ms = jnp.mean(y * y, axis=-1, keepdims=True)
        o_ref[rs, :] = (y * lax.rsqrt(ms + NORM_EPS) * g_ref[...]).astype(BF16)


def _dft_tables(n):
    k = np.arange(n, dtype=np.int64)
    ang = 2.0 * np.pi * ((k[:, None] * k[None, :]) % n).astype(np.float64) / n
    return np.cos(ang), np.sin(ang)


def _fourier(u, w_fourier, g_out, batch, seq):
    cs, ss = _dft_tables(seq)
    cc, sc = _dft_tables(FOURIER_CH)
    scale = 1.0 / math.sqrt(seq * FOURIER_CH)
    row_block = min(512, seq)
    full2 = lambda b: (0, 0)
    return pl.pallas_call(
        functools.partial(_fourier_kernel, scale=scale, row_block=row_block),
        grid=(batch,),
        in_specs=[
            pl.BlockSpec((seq, FOURIER_WIDTH), lambda b: (b, 0)),
            pl.BlockSpec((seq, seq), full2),
            pl.BlockSpec((seq, seq), full2),
            pl.BlockSpec((FOURIER_CH, FOURIER_CH), full2),
            pl.BlockSpec((FOURIER_CH, FOURIER_CH), full2),
            pl.BlockSpec((FOURIER_GROUPS, FOURIER_CH, FOURIER_CH), lambda b: (0, 0, 0)),
            pl.BlockSpec((1, FOURIER_WIDTH), full2),
        ],
        out_specs=pl.BlockSpec((seq, FOURIER_WIDTH), lambda b: (b, 0)),
        out_shape=jax.ShapeDtypeStruct((batch * seq, FOURIER_WIDTH), BF16),
        scratch_shapes=[pltpu.VMEM((seq, FOURIER_WIDTH), BF16),
                        pltpu.VMEM((seq, FOURIER_WIDTH), BF16)],
        compiler_params=_params(1, VMEM_LIMIT),
        name="fourier",
    )(u, jnp.asarray(cs, BF16), jnp.asarray(ss, BF16), jnp.asarray(cc, BF16),
      jnp.asarray(-sc, BF16), w_fourier, g_out.reshape(1, FOURIER_WIDTH))


def _attn_kernel(sink_ref, q_ref, kvp_ref, kvo_ref, kvn_ref, bias_ref, g_ref, o_ref, acc_ref):
    i = pl.program_id(1)
    nb = pl.num_programs(1)
    kv = jnp.concatenate([kvp_ref[...], kvo_ref[...], kvn_ref[...]], axis=0)
    nk = kv.shape[0]
    lo = lax.broadcasted_iota(I32, (nk, LANES), 1) < HEAD_DIM
    k_a, k_b = kv[:, 0:LANES], kv[:, LANES:2 * LANES]
    v_a, v_b = kv[:, 2 * LANES:3 * LANES], kv[:, 3 * LANES:4 * LANES]
    zero = jnp.zeros_like(k_a)
    k_lo = (jnp.where(lo, k_a, zero), jnp.where(lo, k_b, zero))
    k_hi = (jnp.where(lo, zero, k_b), jnp.where(lo, zero, k_a))
    v_lo = (jnp.where(lo, v_a, zero), jnp.where(lo, v_b, zero))
    v_hi = (jnp.where(lo, zero, v_b), jnp.where(lo, zero, v_a))

    row = lax.broadcasted_iota(I32, (Q_BLOCK, nk), 0)
    col = lax.broadcasted_iota(I32, (Q_BLOCK, nk), 1)
    rel = col - Q_BLOCK - row
    valid = ((jnp.abs(rel) <= WINDOW)
             & ((col >= Q_BLOCK) | (i > 0))
             & ((col < 2 * Q_BLOCK) | (i < nb - 1)))
    nt = (((1,), (1,)), ((), ()))
    for h in range(N_KV_HEADS):
        qs = jnp.concatenate([q_ref[:, (2 * h) * LANES:(2 * h + 1) * LANES],
                              q_ref[:, (2 * h + 1) * LANES:(2 * h + 2) * LANES]], axis=0)
        s_par = (lax.dot_general(qs, k_lo[h], nt, preferred_element_type=F32),
                 lax.dot_general(qs, k_hi[h], nt, preferred_element_type=F32))
        vcat = jnp.concatenate([v_lo[h], v_hi[h]], axis=0)
        for c in range(2):
            probs = []
            for par in range(2):
                hq = 4 * h + 2 * c + par
                s = s_par[par][c * Q_BLOCK:(c + 1) * Q_BLOCK, :] + bias_ref[hq]
                s = jnp.where(valid, s, MASK_VALUE)
                sink = sink_ref[hq]
                m = jnp.maximum(jnp.max(s, axis=-1, keepdims=True), sink)
                p = jnp.exp(s - m)
                denom = jnp.sum(p, axis=-1, keepdims=True) + jnp.exp(sink - m)
                probs.append((p * (1.0 / denom)).astype(BF16))
            pcat = jnp.concatenate(probs, axis=1)
            chunk = 2 * h + c
            acc_ref[:, chunk * LANES:(chunk + 1) * LANES] = jnp.dot(
                pcat, vcat, preferred_element_type=F32)
    y = acc_ref[...]
    ms = jnp.mean(y * y, axis=-1, keepdims=True)
    o_ref[...] = (y * lax.rsqrt(ms + NORM_EPS) * g_ref[...]).astype(BF16)


def _t5_bucket(rel):
    nb = N_BUCKETS // 2
    max_exact = nb // 2
    ret = (rel > 0).astype(jnp.int32) * nb
    n = jnp.abs(rel)
    nf = jnp.maximum(n, 1).astype(jnp.float32)
    large = max_exact + (jnp.log(nf / max_exact) / math.log(MAX_DISTANCE / max_exact)
                         * (nb - max_exact)).astype(jnp.int32)
    large = jnp.minimum(large, nb - 1)
    return ret + jnp.where(n < max_exact, n, large)


def _attention(q, kv, sinks, rel_bias, g_out, batch, seq):
    nb = seq // Q_BLOCK
    qi = jnp.arange(Q_BLOCK, dtype=jnp.int32)[:, None]
    kj = jnp.arange(3 * Q_BLOCK, dtype=jnp.int32)[None, :]
    bucket = _t5_bucket(kj - Q_BLOCK - qi)
    hit = bucket[None, :, :, None] == jnp.arange(N_BUCKETS, dtype=jnp.int32)
    bias = jnp.sum(jnp.where(hit, rel_bias.astype(F32).T[:, None, None, :], 0.0), axis=-1)
    grid_spec = pltpu.PrefetchScalarGridSpec(
        num_scalar_prefetch=1,
        grid=(batch, nb),
        in_specs=[
            pl.BlockSpec((Q_BLOCK, ATTN_WIDTH), lambda b, i, s: (b * nb + i, 0)),
            pl.BlockSpec((Q_BLOCK, 4 * LANES), lambda b, i, s: (b * nb + jnp.maximum(i - 1, 0), 0)),
            pl.BlockSpec((Q_BLOCK, 4 * LANES), lambda b, i, s: (b * nb + i, 0)),
            pl.BlockSpec((Q_BLOCK, 4 * LANES),
                         lambda b, i, s: (b * nb + jnp.minimum(i + 1, nb - 1), 0)),
            pl.BlockSpec((N_Q_HEADS, Q_BLOCK, 3 * Q_BLOCK), lambda b, i, s: (0, 0, 0)),
            pl.BlockSpec((1, ATTN_WIDTH), lambda b, i, s: (0, 0)),
        ],
        out_specs=pl.BlockSpec((Q_BLOCK, ATTN_WIDTH), lambda b, i, s: (b * nb + i, 0)),
        scratch_shapes=[pltpu.VMEM((Q_BLOCK, ATTN_WIDTH), F32)],
    )
    return pl.pallas_call(
        _attn_kernel,
        grid_spec=grid_spec,
        out_shape=jax.ShapeDtypeStruct((batch * seq, ATTN_WIDTH), BF16),
        compiler_params=_params(2, VMEM_LIMIT),
        name="attention",
    )(sinks.astype(F32), q, kv, kv, kv, bias, g_out.reshape(1, ATTN_WIDTH))


def _outproj_kernel(yf_ref, ya_ref, x_ref, wo_ref, g2_ref, wr_ref, br_ref, tri_ref,
                    x1_ref, h2_ref, gate_ref, pos_ref, post_ref, cnt_ref, *, n_experts):
    half = yf_ref.shape[1]
    mix = (jnp.dot(yf_ref[...], wo_ref[:half, :], preferred_element_type=F32)
           + jnp.dot(ya_ref[...], wo_ref[half:, :], preferred_element_type=F32))
    x1 = x_ref[...] + mix
    x1_ref[...] = x1
    ms = jnp.mean(x1 * x1, axis=-1, keepdims=True)
    h2 = x1 * lax.rsqrt(ms + NORM_EPS) * g2_ref[...]
    h2_ref[...] = h2.astype(BF16)
    h_hi = h2.astype(BF16)
    h_lo = (h2 - h_hi.astype(F32)).astype(BF16)
    logits = (jnp.dot(h_hi, wr_ref[0], preferred_element_type=F32)
              + jnp.dot(h_hi, wr_ref[1], preferred_element_type=F32)
              + jnp.dot(h_lo, wr_ref[0], preferred_element_type=F32)) + br_ref[...]
    rows = logits.shape[0]
    lane_e = lax.broadcasted_iota(I32, (rows, n_experts), 1).astype(F32)
    work = logits
    vals, idxs = [], []
    for _ in range(TOP_K):
        m = jnp.max(work, axis=-1, keepdims=True)
        ik = jnp.min(jnp.where(work == m, lane_e, float(n_experts)), axis=-1, keepdims=True)
        work = jnp.where(lane_e == ik, -jnp.inf, work)
        vals.append(m)
        idxs.append(ik)
    exps = [jnp.exp(v - vals[0]) for v in vals]
    inv = 1.0 / (exps[0] + exps[1] + exps[2] + exps[3])
    lane_k = lax.broadcasted_iota(I32, (rows, TOP_K), 1)
    gate = jnp.zeros((rows, TOP_K), F32)
    for k in range(TOP_K):
        gate = jnp.where(lane_k == k, exps[k] * inv, gate)
    gate_ref[...] = gate

    lane = lax.broadcasted_iota(I32, (rows, LANES), 1).astype(F32)
    onehot = jnp.zeros((rows, LANES), F32)
    for k in range(TOP_K):
        onehot = onehot + jnp.where(lane == idxs[k] + float(k * n_experts), 1.0, 0.0)
    before = jnp.dot(tri_ref[...], onehot.astype(BF16), preferred_element_type=F32)
    colsum = jnp.sum(onehot, axis=0, keepdims=True)
    lane1 = lax.broadcasted_iota(I32, (1, LANES), 1)
    prefix = jnp.zeros((1, LANES), F32)
    total = colsum
    for k in range(1, TOP_K):
        rolled = pltpu.roll(colsum, k * n_experts, 1)
        prefix = prefix + jnp.where(lane1 >= k * n_experts, rolled, 0.0)
        total = total + rolled
    run = jnp.floor((total + (RUN_ALIGN - 1)) * (1.0 / RUN_ALIGN)) * RUN_ALIGN
    incl = run
    lane_in_seg = lane1 & (n_experts - 1)
    shift = 1
    while shift < n_experts:
        incl = incl + jnp.where(lane_in_seg >= shift, pltpu.roll(incl, shift, 1), 0.0)
        shift *= 2
    run_start = incl - run
    placed = (before + prefix + run_start) * onehot
    lane_i = lax.broadcasted_iota(I32, (rows, LANES), 1)
    pos_lanes = jnp.zeros((rows, LANES), F32)
    for k in range(TOP_K):
        seg = (lane_i >= k * n_experts) & (lane_i < (k + 1) * n_experts)
        pk = jnp.sum(jnp.where(seg, placed, 0.0), axis=-1, keepdims=True)
        pos_lanes = jnp.where(lane_i == k, pk, pos_lanes)
    pos_ref[...] = pos_lanes[:, :TOP_K].astype(I32)
    post_ref[...] = jnp.transpose(pos_lanes)[:SUBLANES, :].astype(I32)
    cnt_ref[0] = total.astype(I32)


def _outproj(yf, ya, x2d, w_out, norm2, w_router, b_router):
    t, d = x2d.shape
    tm = min(TOKEN_TILE, t)
    n_tiles = t // tm
    n_experts = w_router.shape[1]
    assert TOP_K * n_experts == LANES
    tri = np.tril(np.ones((tm, tm), np.float32), -1)
    wr_hi = w_router.astype(BF16)
    wr_lo = (w_router - wr_hi.astype(F32)).astype(BF16)
    full = lambda i: (0, 0)
    row = lambda i: (i, 0)
    return pl.pallas_call(
        functools.partial(_outproj_kernel, n_experts=n_experts),
        grid=(n_tiles,),
        in_specs=[
            pl.BlockSpec((tm, yf.shape[1]), row),
            pl.BlockSpec((tm, ya.shape[1]), row),
            pl.BlockSpec((tm, d), row),
            pl.BlockSpec((w_out.shape[0], d), full),
            pl.BlockSpec((1, d), full),
            pl.BlockSpec((2, d, n_experts), lambda i: (0, 0, 0)),
            pl.BlockSpec((1, n_experts), full),
            pl.BlockSpec((tm, tm), full),
        ],
        out_specs=[
            pl.BlockSpec((tm, d), row),
            pl.BlockSpec((tm, d), row),
            pl.BlockSpec((tm, TOP_K), row),
            pl.BlockSpec((tm, TOP_K), row),
            pl.BlockSpec((SUBLANES, tm), row),
            pl.BlockSpec((1, 1, LANES), lambda i: (i, 0, 0)),
        ],
        out_shape=[
            jax.ShapeDtypeStruct((t, d), F32),
            jax.ShapeDtypeStruct((t, d), BF16),
            jax.ShapeDtypeStruct((t, TOP_K), F32),
            jax.ShapeDtypeStruct((t, TOP_K), I32),
            jax.ShapeDtypeStruct((n_tiles * SUBLANES, tm), I32),
            jax.ShapeDtypeStruct((n_tiles, 1, LANES), I32),
        ],
        compiler_params=_params(1, VMEM_LIMIT),
        name="outproj_router",
    )(yf, ya, x2d, w_out.astype(BF16), norm2.reshape(1, d), jnp.stack([wr_hi, wr_lo]),
      b_router.reshape(1, n_experts), jnp.asarray(tri, BF16))


def _chunk_sizes(limit):
    sizes, s = [], RUN_ALIGN
    while s <= limit:
        sizes.append(s)
        s *= 2
    return tuple(reversed(sizes))


def _for_each_chunk(n_rows, src0, dst0, sizes, fn):
    off = jnp.int32(0)
    for size in sizes:
        take = n_rows & size

        @pl.when(take != 0)
        def _(off=off, size=size):
            fn(pl.multiple_of(src0 + off, RUN_ALIGN), pl.multiple_of(dst0 + off, RUN_ALIGN), size)

        off = off + take


def _dispatch_kernel(cnt_ref, lst_ref, base_ref, tail_ref, post_ref, h2_ref, xs_ref,
                     buf, zbuf, sem, zsem, *, n_experts):
    j = pl.program_id(0)
    tm = h2_ref.shape[0]
    n_local = buf.shape[1]
    run_sizes = _chunk_sizes(tm)
    tail_sizes = _chunk_sizes(zbuf.shape[0])

    def tail_copy(src, dst, size):
        return pltpu.make_async_copy(zbuf.at[pl.ds(0, size), :], xs_ref.at[pl.ds(dst, size), :], zsem)

    def walk_runs(tile, slot, op):
        def runs(e, carry):
            r = tile * n_experts + e
            _for_each_chunk(
                cnt_ref[r], lst_ref[r], base_ref[r], run_sizes,
                lambda s, d, size: getattr(pltpu.make_async_copy(
                    buf.at[slot, pl.ds(s, size), :], xs_ref.at[pl.ds(d, size), :], sem.at[slot]),
                    op)())
            return carry
        lax.fori_loop(0, n_experts, runs, 0)

    slot = j % 2
    @pl.when(j >= 2)
    def _():
        walk_runs(j - 2, slot, "wait")

    @pl.when(j == 0)
    def _():
        zbuf[...] = jnp.zeros_like(zbuf)
        for op in ("start", "wait"):
            def tails(e, carry, op=op):
                _for_each_chunk(tail_ref[n_experts + e], 0, tail_ref[e], tail_sizes,
                                lambda s, d, size: getattr(tail_copy(s, d, size), op)())
                return carry
            lax.fori_loop(0, n_experts, tails, 0)

            def spare(b, carry, op=op):
                for half in range(EXPERT_ROWS // zbuf.shape[0]):
                    dst = pl.multiple_of(b * EXPERT_ROWS + half * zbuf.shape[0], RUN_ALIGN)
                    getattr(tail_copy(0, dst, zbuf.shape[0]), op)()
                return carry
            lax.fori_loop(tail_ref[2 * n_experts], xs_ref.shape[0] // EXPERT_ROWS, spare, 0)

    h = h2_ref[...]
    for rc in range(n_local // PERM_CHUNK):
        rows = lax.broadcasted_iota(I32, (PERM_CHUNK, tm), 0) + rc * PERM_CHUNK
        perm = jnp.zeros((PERM_CHUNK, tm), F32)
        for k in range(TOP_K):
            perm = jnp.where(rows == post_ref[k:k + 1, :], 1.0, perm)
        buf[slot, rc * PERM_CHUNK:(rc + 1) * PERM_CHUNK, :] = jnp.dot(
            perm.astype(BF16), h, preferred_element_type=F32)

    walk_runs(j, slot, "start")

    last = pl.num_programs(0) - 1
    @pl.when(j == last)
    def _():
        @pl.when(j >= 1)
        def _():
            walk_runs(j - 1, 1 - slot, "wait")
        walk_runs(j, slot, "wait")


def _local_rows(tm, n_experts):
    worst = TOP_K * tm + n_experts * (RUN_ALIGN - 1)
    return -(-worst // PERM_CHUNK) * PERM_CHUNK


def _dispatch(plan, post, h2, n_rows, n_experts):
    t, d = h2.shape
    tm = min(TOKEN_TILE, t)
    grid_spec = pltpu.PrefetchScalarGridSpec(
        num_scalar_prefetch=4,
        grid=(t // tm,),
        in_specs=[
            pl.BlockSpec((SUBLANES, tm), lambda i, *_: (i, 0)),
            pl.BlockSpec((tm, d), lambda i, *_: (i, 0)),
        ],
        out_specs=pl.BlockSpec(memory_space=pl.ANY),
        scratch_shapes=[pltpu.VMEM((2, _local_rows(tm, n_experts), d), F32),
                        pltpu.VMEM((EXPERT_ROWS // 2, d), F32),
                        pltpu.SemaphoreType.DMA((2,)), pltpu.SemaphoreType.DMA(())],
    )
    return pl.pallas_call(
        functools.partial(_dispatch_kernel, n_experts=n_experts),
        grid_spec=grid_spec,
        out_shape=jax.ShapeDtypeStruct((n_rows, d), F32),
        compiler_params=_params(1, VMEM_LIMIT),
        name="dispatch",
    )(plan["cnt"], plan["lst"], plan["base"], plan["tail"], post, h2)


def _combine_kernel(cnt_ref, lst_ref, base_ref, pos_ref, gate_ref, x1_ref, ys_ref, o_ref,
                    buf, sem, *, n_experts):
    j = pl.program_id(0)
    tm = x1_ref.shape[0]
    n_local = buf.shape[1]
    run_sizes = _chunk_sizes(tm)

    def walk_runs(tile, slot, op):
        def runs(e, carry):
            r = tile * n_experts + e
            _for_each_chunk(
                cnt_ref[r], base_ref[r], lst_ref[r], run_sizes,
                lambda s, d, size: getattr(pltpu.make_async_copy(
                    ys_ref.at[pl.ds(s, size), :], buf.at[slot, pl.ds(d, size), :], sem.at[slot]),
                    op)())
            return carry
        lax.fori_loop(0, n_experts, runs, 0)

    slot = j % 2
    @pl.when(j == 0)
    def _():
        buf[...] = jnp.zeros_like(buf)
        walk_runs(j, slot, "start")

    @pl.when(j + 1 < pl.num_programs(0))
    def _():
        walk_runs(j + 1, 1 - slot, "start")

    walk_runs(j, slot, "wait")

    acc = x1_ref[...]
    pos = pos_ref[...]
    gate = gate_ref[...]
    for rc in range(n_local // PERM_CHUNK):
        cols = lax.broadcasted_iota(I32, (tm, PERM_CHUNK), 1) + rc * PERM_CHUNK
        g = jnp.zeros((tm, PERM_CHUNK), F32)
        for k in range(TOP_K):
            g = jnp.where(cols == pos[:, k:k + 1], gate[:, k:k + 1], g)
        y = buf[slot, rc * PERM_CHUNK:(rc + 1) * PERM_CHUNK, :].astype(BF16)
        acc = acc + jnp.dot(g.astype(BF16), y, preferred_element_type=F32)
    o_ref[...] = acc


def _combine(plan, pos, gate, x1, ys, n_experts):
    t, d = x1.shape
    tm = min(TOKEN_TILE, t)
    grid_spec = pltpu.PrefetchScalarGridSpec(
        num_scalar_prefetch=3,
        grid=(t // tm,),
        in_specs=[
            pl.BlockSpec((tm, TOP_K), lambda i, *_: (i, 0)),
            pl.BlockSpec((tm, TOP_K), lambda i, *_: (i, 0)),
            pl.BlockSpec((tm, d), lambda i, *_: (i, 0)),
            pl.BlockSpec(memory_space=pl.ANY),
        ],
        out_specs=pl.BlockSpec((tm, d), lambda i, *_: (i, 0)),
        scratch_shapes=[pltpu.VMEM((2, _local_rows(tm, n_experts), d), F32),
                        pltpu.SemaphoreType.DMA((2,))],
    )
    return pl.pallas_call(
        functools.partial(_combine_kernel, n_experts=n_experts),
        grid_spec=grid_spec,
        out_shape=jax.ShapeDtypeStruct((t, d), F32),
        compiler_params=_params(1, VMEM_LIMIT),
        name="combine",
    )(plan["cnt"], plan["lst"], plan["base"], pos, gate, x1, ys)


def _expert_kernel(be_ref, meta_ref, xs_ref, wgu_ref, bg_ref, bu_ref, wd_ref, bd_ref, perm_ref,
                   ys_ref, wg_s, wu_s, wd_s):
    i = pl.program_id(0)
    n_used = meta_ref[0]
    active = i < n_used
    new_expert = (i == 0) | (be_ref[i] != be_ref[jnp.maximum(i - 1, 0)])

    @pl.when(active & new_expert)
    def _():
        width = perm_ref.shape[0]
        for c in range(wgu_ref.shape[2] // width):
            wc = wgu_ref[0, :, c * width:(c + 1) * width].astype(BF16)
            r = jnp.dot(wc, perm_ref[...], preferred_element_type=F32)
            wg_s[:, c * LANES:(c + 1) * LANES] = r[:, :LANES].astype(BF16)
            wu_s[:, c * LANES:(c + 1) * LANES] = r[:, LANES:].astype(BF16)
        wd_s[...] = wd_ref[0].astype(BF16)

    @pl.when(active)
    def _():
        xb = xs_ref[...].astype(BF16)
        g = jnp.dot(xb, wg_s[...], preferred_element_type=F32) + bg_ref[0]
        up = jnp.dot(xb, wu_s[...], preferred_element_type=F32) + bu_ref[0]
        g = jnp.minimum(g, SWIGLU_LIMIT)
        up = jnp.clip(up, -SWIGLU_LIMIT, SWIGLU_LIMIT)
        act = g * (1.0 / (1.0 + jnp.exp(-SWIGLU_ALPHA * g))) * (up + 1.0)
        ys_ref[...] = jnp.dot(act.astype(BF16), wd_s[...], preferred_element_type=F32) + bd_ref[0]

    @pl.when(jnp.logical_not(active))
    def _():
        ys_ref[...] = jnp.zeros_like(ys_ref)


def _experts(blk_e, meta, xs, w_gate_up, b_gate_up, w_down, b_down):
    n_rows, d = xs.shape
    n_experts, _, f2 = w_gate_up.shape
    f = f2 // 2
    bm = EXPERT_ROWS
    n_blocks = n_rows // bm
    width = 2 * LANES
    perm = np.zeros((width, width), np.float32)
    perm[2 * np.arange(LANES), np.arange(LANES)] = 1.0
    perm[2 * np.arange(LANES) + 1, LANES + np.arange(LANES)] = 1.0
    bg = b_gate_up[:, 0::2].reshape(n_experts, 1, f)
    bu = b_gate_up[:, 1::2].reshape(n_experts, 1, f)
    rows = lambda i, be, meta: (jnp.minimum(i, meta[0] - 1), 0)
    per_e = lambda i, be, meta: (be[i], 0, 0)
    grid_spec = pltpu.PrefetchScalarGridSpec(
        num_scalar_prefetch=2,
        grid=(n_blocks,),
        in_specs=[
            pl.BlockSpec((bm, d), rows),
            pl.BlockSpec((1, d, f2), per_e),
            pl.BlockSpec((1, 1, f), per_e),
            pl.BlockSpec((1, 1, f), per_e),
            pl.BlockSpec((1, f, d), per_e),
            pl.BlockSpec((1, 1, d), per_e),
            pl.BlockSpec((width, width), lambda i, be, meta: (0, 0)),
        ],
        out_specs=pl.BlockSpec((bm, d), lambda i, be, meta: (i, 0)),
        scratch_shapes=[pltpu.VMEM((d, f), BF16), pltpu.VMEM((d, f), BF16),
                        pltpu.VMEM((f, d), BF16)],
    )
    return pl.pallas_call(
        _expert_kernel,
        grid_spec=grid_spec,
        out_shape=jax.ShapeDtypeStruct((n_rows, d), F32),
        compiler_params=_params(1, VMEM_LIMIT),
        name="experts",
    )(blk_e, meta, xs, w_gate_up, bg, bu, w_down, b_down.reshape(n_experts, 1, d),
      jnp.asarray(perm, BF16))


def _routing_plan(counts, n_experts, bm, n_blocks):
    cnt = counts[:, 0, :n_experts]
    run = (cnt + RUN_ALIGN - 1) // RUN_ALIGN * RUN_ALIGN
    per_expert = jnp.sum(run, axis=0)
    padded = (per_expert + bm - 1) // bm * bm
    pend = jnp.cumsum(padded)
    pstart = pend - padded
    base = pstart[None, :] + jnp.cumsum(run, axis=0) - run
    lst = jnp.cumsum(run, axis=1) - run
    n_used = pend[-1] // bm
    tail = jnp.concatenate([pstart + per_expert, padded - per_expert, n_used[None]])
    starts = jnp.arange(n_blocks, dtype=I32) * bm
    blk = jnp.sum((starts[:, None] >= pend[None, :]).astype(I32), axis=1)
    blk = jnp.minimum(blk, n_experts - 1)
    last = jnp.sum((((n_used - 1) * bm) >= pend).astype(I32))
    blk_e = jnp.where(jnp.arange(n_blocks) < n_used, blk, jnp.minimum(last, n_experts - 1))
    plan = {"cnt": run.reshape(-1).astype(I32), "lst": lst.reshape(-1).astype(I32),
            "base": base.reshape(-1).astype(I32), "tail": tail.astype(I32)}
    return plan, blk_e.astype(I32), n_used.astype(I32).reshape(1)


def _layer(x2d, batch, seq, norm1, w_in, q_norm, k_norm, sinks, rel_bias, w_fourier, g_fourier_out,
           g_attn_out, w_out, norm2, w_router, b_router, w_gate_up, b_gate_up, w_down, b_down):
    t, d = x2d.shape
    n_experts = w_router.shape[1]
    u, q, kv = _inproj(x2d, norm1, w_in, q_norm, k_norm)
    yf = _fourier(u, w_fourier, g_fourier_out, batch, seq)
    ya = _attention(q, kv, sinks, rel_bias, g_attn_out, batch, seq)
    x1, h2, gate, pos, post, counts = _outproj(yf, ya, x2d, w_out, norm2, w_router, b_router)
    bm = EXPERT_ROWS
    n_tiles = t // min(TOKEN_TILE, t)
    worst_rows = t * TOP_K + n_tiles * n_experts * (RUN_ALIGN - 1) + n_experts * (bm - RUN_ALIGN)
    n_blocks = -(-worst_rows // bm)
    plan, blk_e, meta = _routing_plan(counts, n_experts, bm, n_blocks)
    xs = _dispatch(plan, post, h2, n_blocks * bm, n_experts)
    ys = _experts(blk_e, meta, xs, w_gate_up, b_gate_up, w_down, b_down)
    return _combine(plan, pos, gate, x1, ys, n_experts)


def kernel(x, norm1, w_in, q_norm, k_norm, sinks, rel_bias, w_fourier, g_fourier_out, g_attn_out,
           w_out, norm2, w_router, b_router, w_gate_up, b_gate_up, w_down, b_down):
    b, s, d = x.shape
    x2d = x.reshape(b * s, d)
    for l in range(norm1.shape[0]):
        x2d = _layer(x2d, b, s, norm1[l], w_in[l], q_norm[l], k_norm[l], sinks[l], rel_bias,
                     w_fourier[l], g_fourier_out[l], g_attn_out[l], w_out[l], norm2[l],
                     w_router[l], b_router[l], w_gate_up[l], b_gate_up[l], w_down[l], b_down[l])
    return x2d.reshape(b, s, d)
```

```python
import functools
import math

import jax
import jax.numpy as jnp
import numpy as np
from jax import lax
from jax.experimental import pallas as pl
from jax.experimental.pallas import tpu as pltpu

F32 = jnp.float32
BF16 = jnp.bfloat16
I32 = jnp.int32

NORM_EPS = 1e-5
QK_EPS = 1e-6
HEAD_DIM = 64
N_Q_HEADS = 8
N_KV_HEADS = 2
FOURIER_GROUPS = 4
FOURIER_CH = 128
FOURIER_WIDTH = FOURIER_GROUPS * FOURIER_CH
ATTN_WIDTH = N_Q_HEADS * HEAD_DIM
KV_WIDTH = N_KV_HEADS * HEAD_DIM
WINDOW = 128
Q_BLOCK = 128
N_BUCKETS = 32
MAX_DISTANCE = 128
TOP_K = 4
SWIGLU_ALPHA = 1.702
SWIGLU_LIMIT = 7.0
MASK_VALUE = -1e30
LOG2E = math.log2(math.e)

LANES = 128
SUBLANES = 8
TOKEN_TILE = 512
RUN_ALIGN = SUBLANES
PERM_CHUNK = 256
EXPERT_ROWS = 512
VMEM_LIMIT = 56 * 1024 * 1024


def _params(n_axes, vmem=None):
    return pltpu.CompilerParams(
        dimension_semantics=("arbitrary",) * n_axes, vmem_limit_bytes=vmem)


def _pair_head_norm(xc, gain, lo):
    x2 = xc * xc
    s_lo = jnp.sum(jnp.where(lo, x2, 0.0), axis=-1, keepdims=True)
    s_hi = jnp.sum(jnp.where(lo, 0.0, x2), axis=-1, keepdims=True)
    inv = jnp.where(lo, lax.rsqrt(s_lo * (1.0 / HEAD_DIM) + QK_EPS),
                    lax.rsqrt(s_hi * (1.0 / HEAD_DIM) + QK_EPS))
    return xc * inv * gain


def _inproj_kernel(x_ref, g1_ref, w_ref, qg_ref, kg_ref, u_ref, q_ref, kv_ref):
    x = x_ref[...]
    ms = jnp.mean(x * x, axis=-1, keepdims=True)
    h = (x * lax.rsqrt(ms + NORM_EPS) * g1_ref[...]).astype(BF16)
    z = jnp.dot(h, w_ref[...], preferred_element_type=F32)
    u_ref[...] = z[:, :FOURIER_WIDTH].astype(BF16)
    rows = x.shape[0]
    lo = lax.broadcasted_iota(I32, (rows, LANES), 1) < HEAD_DIM
    q0 = FOURIER_WIDTH
    for c in range(ATTN_WIDTH // LANES):
        qc = _pair_head_norm(z[:, q0 + c * LANES:q0 + (c + 1) * LANES], qg_ref[...], lo)
        q_ref[:, c * LANES:(c + 1) * LANES] = (qc * (HEAD_DIM ** -0.5 * LOG2E)).astype(BF16)
    k0 = q0 + ATTN_WIDTH
    kc = _pair_head_norm(z[:, k0:k0 + KV_WIDTH], kg_ref[...], lo)
    vc = z[:, k0 + KV_WIDTH:k0 + 2 * KV_WIDTH]
    kv_ref[:, 0:LANES] = kc.astype(BF16)
    kv_ref[:, LANES:2 * LANES] = pltpu.roll(kc, HEAD_DIM, 1).astype(BF16)
    kv_ref[:, 2 * LANES:3 * LANES] = vc.astype(BF16)
    kv_ref[:, 3 * LANES:4 * LANES] = pltpu.roll(vc, HEAD_DIM, 1).astype(BF16)


def _inproj(x2d, norm1, w_in, q_norm, k_norm):
    t, d = x2d.shape
    tm = min(TOKEN_TILE, t)
    n_in = w_in.shape[1]
    qg = jnp.tile(q_norm, LANES // HEAD_DIM).reshape(1, LANES)
    kg = jnp.tile(k_norm, LANES // HEAD_DIM).reshape(1, LANES)
    full = lambda i: (0, 0)
    return pl.pallas_call(
        _inproj_kernel,
        grid=(t // tm,),
        in_specs=[
            pl.BlockSpec((tm, d), lambda i: (i, 0)),
            pl.BlockSpec((1, d), full),
            pl.BlockSpec((d, n_in), full),
            pl.BlockSpec((1, LANES), full),
            pl.BlockSpec((1, LANES), full),
        ],
        out_specs=[
            pl.BlockSpec((tm, FOURIER_WIDTH), lambda i: (i, 0)),
            pl.BlockSpec((tm, ATTN_WIDTH), lambda i: (i, 0)),
            pl.BlockSpec((tm, 4 * LANES), lambda i: (i, 0)),
        ],
        out_shape=[
            jax.ShapeDtypeStruct((t, FOURIER_WIDTH), BF16),
            jax.ShapeDtypeStruct((t, ATTN_WIDTH), BF16),
            jax.ShapeDtypeStruct((t, 4 * LANES), BF16),
        ],
        compiler_params=_params(1, VMEM_LIMIT),
        name="inproj",
    )(x2d, norm1.reshape(1, d), w_in.astype(BF16), qg, kg)


def _fourier_kernel(u_ref, cs_ref, ss_ref, cc_ref, sc_ref, wf_ref, g_ref, o_ref, p_scr, q_scr,
                    *, scale, row_block):
    for g in range(FOURIER_GROUPS):
        sl = slice(g * FOURIER_CH, (g + 1) * FOURIER_CH)
        w = wf_ref[g].astype(BF16)
        a = (jnp.dot(cc_ref[...], w, preferred_element_type=F32) * scale).astype(BF16)
        b = (jnp.dot(sc_ref[...], w, preferred_element_type=F32) * scale).astype(BF16)
        ug = u_ref[:, sl]
        p_scr[:, sl] = jnp.dot(ug, a, preferred_element_type=F32).astype(BF16)
        q_scr[:, sl] = jnp.dot(ug, b, preferred_element_type=F32).astype(BF16)
    s = u_ref.shape[0]
    for r in range(s // row_block):
        rs = slice(r * row_block, (r + 1) * row_block)
        y = (jnp.dot(cs_ref[rs, :], p_scr[...], preferred_element_type=F32)
             + jnp.dot(ss_ref[rs, :], q_scr[...], preferred_element_type=F32))
        ms = jnp.mean(y * y, axis=-1, keepdims=True)
        o_ref[rs, :] = (y * lax.rsqrt(ms + NORM_EPS) * g_ref[...]).astype(BF16)


def _dft_tables(n):
    k = np.arange(n, dtype=np.int64)
    ang = 2.0 * np.pi * ((k[:, None] * k[None, :]) % n).astype(np.float64) / n
    return np.cos(ang), np.sin(ang)


def _fourier(u, w_fourier, g_out, batch, seq):
    cs, ss = _dft_tables(seq)
    cc, sc = _dft_tables(FOURIER_CH)
    scale = 1.0 / math.sqrt(seq * FOURIER_CH)
    row_block = min(512, seq)
    full2 = lambda b: (0, 0)
    return pl.pallas_call(
        functools.partial(_fourier_kernel, scale=scale, row_block=row_block),
        grid=(batch,),
        in_specs=[
            pl.BlockSpec((seq, FOURIER_WIDTH), lambda b: (b, 0)),
            pl.BlockSpec((seq, seq), full2),
            pl.BlockSpec((seq, seq), full2),
            pl.BlockSpec((FOURIER_CH, FOURIER_CH), full2),
            pl.BlockSpec((FOURIER_CH, FOURIER_CH), full2),
            pl.BlockSpec((FOURIER_GROUPS, FOURIER_CH, FOURIER_CH), lambda b: (0, 0, 0)),
            pl.BlockSpec((1, FOURIER_WIDTH), full2),
        ],
        out_specs=pl.BlockSpec((seq, FOURIER_WIDTH), lambda b: (b, 0)),
        out_shape=jax.ShapeDtypeStruct((batch * seq, FOURIER_WIDTH), BF16),
        scratch_shapes=[pltpu.VMEM((seq, FOURIER_WIDTH), BF16),
                        pltpu.VMEM((seq, FOURIER_WIDTH), BF16)],
        compiler_params=_params(1, VMEM_LIMIT),
        name="fourier",
    )(u, jnp.asarray(cs, BF16), jnp.asarray(ss, BF16), jnp.asarray(cc, BF16),
      jnp.asarray(-sc, BF16), w_fourier, g_out.reshape(1, FOURIER_WIDTH))


def _attn_kernel(sink_ref, q_ref, kvp_ref, kvo_ref, kvn_ref, bias_a_ref, bias_b_ref, g_ref, o_ref,
                 acc_ref):
    kv = jnp.concatenate([kvp_ref[...], kvo_ref[...], kvn_ref[...]], axis=0)
    nk = kv.shape[0]
    lo = lax.broadcasted_iota(I32, (nk, LANES), 1) < HEAD_DIM
    k_a, k_b = kv[:, 0:LANES], kv[:, LANES:2 * LANES]
    v_a, v_b = kv[:, 2 * LANES:3 * LANES], kv[:, 3 * LANES:4 * LANES]
    zero = jnp.zeros_like(k_a)
    k_lo = (jnp.where(lo, k_a, zero), jnp.where(lo, k_b, zero))
    k_hi = (jnp.where(lo, zero, k_b), jnp.where(lo, zero, k_a))
    v_lo = (jnp.where(lo, v_a, zero), jnp.where(lo, v_b, zero))
    v_hi = (jnp.where(lo, zero, v_b), jnp.where(lo, zero, v_a))
    lo_out = lax.broadcasted_iota(I32, (Q_BLOCK, LANES), 1) < HEAD_DIM
    bias_refs = (bias_a_ref, bias_b_ref)
    rows2 = q_ref.shape[0]
    nt = (((1,), (1,)), ((), ()))
    for h in range(N_KV_HEADS):
        qs = jnp.concatenate([q_ref[:, (2 * h) * LANES:(2 * h + 1) * LANES],
                              q_ref[:, (2 * h + 1) * LANES:(2 * h + 2) * LANES]], axis=0)
        s_par = (lax.dot_general(qs, k_lo[h], nt, preferred_element_type=F32),
                 lax.dot_general(qs, k_hi[h], nt, preferred_element_type=F32))
        for sb in range(2):
            keys = slice(sb * Q_BLOCK, sb * Q_BLOCK + 3 * Q_BLOCK)
            vcat = jnp.concatenate([v_lo[h][keys, :], v_hi[h][keys, :]], axis=0)
            for c in range(2):
                r0 = c * rows2 + sb * Q_BLOCK
                probs, invs = [], []
                for par in range(2):
                    hq = 4 * h + 2 * c + par
                    s = s_par[par][r0:r0 + Q_BLOCK, keys] + bias_refs[sb][hq]
                    sink = sink_ref[hq]
                    m = jnp.maximum(jnp.max(s, axis=-1, keepdims=True), sink)
                    p = jnp.exp2(s - m)
                    denom = jnp.sum(p, axis=-1, keepdims=True) + jnp.exp2(sink - m)
                    probs.append(p.astype(BF16))
                    invs.append(1.0 / denom)
                pcat = jnp.concatenate(probs, axis=1)
                chunk = 2 * h + c
                o = jnp.dot(pcat, vcat, preferred_element_type=F32)
                acc_ref[sb * Q_BLOCK:(sb + 1) * Q_BLOCK, chunk * LANES:(chunk + 1) * LANES] = (
                    o * jnp.where(lo_out, invs[0], invs[1]))
    y = acc_ref[...]
    ms = jnp.mean(y * y, axis=-1, keepdims=True)
    o_ref[...] = (y * lax.rsqrt(ms + NORM_EPS) * g_ref[...]).astype(BF16)


def _t5_bucket(rel):
    nb = N_BUCKETS // 2
    max_exact = nb // 2
    ret = (rel > 0).astype(jnp.int32) * nb
    n = jnp.abs(rel)
    nf = jnp.maximum(n, 1).astype(jnp.float32)
    large = max_exact + (jnp.log(nf / max_exact) / math.log(MAX_DISTANCE / max_exact)
                         * (nb - max_exact)).astype(jnp.int32)
    large = jnp.minimum(large, nb - 1)
    return ret + jnp.where(n < max_exact, n, large)


def _attention(q, kv, sinks, rel_bias, g_out, batch, seq):
    nb = seq // Q_BLOCK
    assert nb % 2 == 0
    nb2 = nb // 2
    qi = jnp.arange(Q_BLOCK, dtype=jnp.int32)[:, None]
    kj = jnp.arange(3 * Q_BLOCK, dtype=jnp.int32)[None, :]
    rel = kj - Q_BLOCK - qi
    hit = _t5_bucket(rel)[None, :, :, None] == jnp.arange(N_BUCKETS, dtype=jnp.int32)
    bias = jnp.sum(jnp.where(hit, rel_bias.astype(F32).T[:, None, None, :], 0.0), axis=-1)
    band = jnp.abs(rel) <= WINDOW
    first = band & (kj >= Q_BLOCK)
    last = band & (kj < 2 * Q_BLOCK)
    table = jnp.stack([jnp.where(msk[None], bias * LOG2E, MASK_VALUE) for msk in (first, band, last)])
    q_rows = 2 * Q_BLOCK
    grid_spec = pltpu.PrefetchScalarGridSpec(
        num_scalar_prefetch=1,
        grid=(batch, nb2),
        in_specs=[
            pl.BlockSpec((q_rows, ATTN_WIDTH), lambda b, i, s: (b * nb2 + i, 0)),
            pl.BlockSpec((Q_BLOCK, 4 * LANES),
                         lambda b, i, s: (b * nb + jnp.maximum(2 * i - 1, 0), 0)),
            pl.BlockSpec((q_rows, 4 * LANES), lambda b, i, s: (b * nb2 + i, 0)),
            pl.BlockSpec((Q_BLOCK, 4 * LANES),
                         lambda b, i, s: (b * nb + jnp.minimum(2 * i + 2, nb - 1), 0)),
            pl.BlockSpec((None, N_Q_HEADS, Q_BLOCK, 3 * Q_BLOCK),
                         lambda b, i, s: (jnp.where(i == 0, 0, 1), 0, 0, 0)),
            pl.BlockSpec((None, N_Q_HEADS, Q_BLOCK, 3 * Q_BLOCK),
                         lambda b, i, s: (jnp.where(i == nb2 - 1, 2, 1), 0, 0, 0)),
            pl.BlockSpec((1, ATTN_WIDTH), lambda b, i, s: (0, 0)),
        ],
        out_specs=pl.BlockSpec((q_rows, ATTN_WIDTH), lambda b, i, s: (b * nb2 + i, 0)),
        scratch_shapes=[pltpu.VMEM((q_rows, ATTN_WIDTH), F32)],
    )
    return pl.pallas_call(
        _attn_kernel,
        grid_spec=grid_spec,
        out_shape=jax.ShapeDtypeStruct((batch * seq, ATTN_WIDTH), BF16),
        compiler_params=_params(2, VMEM_LIMIT),
        name="attention",
    )(sinks.astype(F32) * LOG2E, q, kv, kv, kv, table, table, g_out.reshape(1, ATTN_WIDTH))


def _outproj_kernel(yf_ref, ya_ref, x_ref, wo_ref, g2_ref, wr_ref, br_ref, tri_ref,
                    x1_ref, h2_ref, gate_ref, pos_ref, post_ref, cnt_ref, *, n_experts):
    half = yf_ref.shape[1]
    mix = (jnp.dot(yf_ref[...], wo_ref[:half, :], preferred_element_type=F32)
           + jnp.dot(ya_ref[...], wo_ref[half:, :], preferred_element_type=F32))
    x1 = x_ref[...] + mix
    x1_ref[...] = x1
    ms = jnp.mean(x1 * x1, axis=-1, keepdims=True)
    h2 = x1 * lax.rsqrt(ms + NORM_EPS) * g2_ref[...]
    h2_ref[...] = h2.astype(BF16)
    h_hi = h2.astype(BF16)
    h_lo = (h2 - h_hi.astype(F32)).astype(BF16)
    logits = (jnp.dot(h_hi, wr_ref[0], preferred_element_type=F32)
              + jnp.dot(h_hi, wr_ref[1], preferred_element_type=F32)
              + jnp.dot(h_lo, wr_ref[0], preferred_element_type=F32)) + br_ref[...]
    rows = logits.shape[0]
    lane_e = lax.broadcasted_iota(I32, (rows, n_experts), 1).astype(F32)
    work = logits
    vals, idxs = [], []
    for _ in range(TOP_K):
        m = jnp.max(work, axis=-1, keepdims=True)
        ik = jnp.min(jnp.where(work == m, lane_e, float(n_experts)), axis=-1, keepdims=True)
        work = jnp.where(lane_e == ik, -jnp.inf, work)
        vals.append(m)
        idxs.append(ik)
    exps = [jnp.exp(v - vals[0]) for v in vals]
    inv = 1.0 / (exps[0] + exps[1] + exps[2] + exps[3])
    lane_k = lax.broadcasted_iota(I32, (rows, TOP_K), 1)
    gate = jnp.zeros((rows, TOP_K), F32)
    for k in range(TOP_K):
        gate = jnp.where(lane_k == k, exps[k] * inv, gate)
    gate_ref[...] = gate

    lane = lax.broadcasted_iota(I32, (rows, LANES), 1).astype(F32)
    onehot = jnp.zeros((rows, LANES), F32)
    for k in range(TOP_K):
        onehot = onehot + jnp.where(lane == idxs[k] + float(k * n_experts), 1.0, 0.0)
    before = jnp.dot(tri_ref[...], onehot.astype(BF16), preferred_element_type=F32)
    colsum = jnp.sum(onehot, axis=0, keepdims=True)
    lane1 = lax.broadcasted_iota(I32, (1, LANES), 1)
    prefix = jnp.zeros((1, LANES), F32)
    total = colsum
    for k in range(1, TOP_K):
        rolled = pltpu.roll(colsum, k * n_experts, 1)
        prefix = prefix + jnp.where(lane1 >= k * n_experts, rolled, 0.0)
        total = total + rolled
    run = jnp.floor((total + (RUN_ALIGN - 1)) * (1.0 / RUN_ALIGN)) * RUN_ALIGN
    incl = run
    lane_in_seg = lane1 & (n_experts - 1)
    shift = 1
    while shift < n_experts:
        incl = incl + jnp.where(lane_in_seg >= shift, pltpu.roll(incl, shift, 1), 0.0)
        shift *= 2
    run_start = incl - run
    placed = (before + prefix + run_start) * onehot
    lane_i = lax.broadcasted_iota(I32, (rows, LANES), 1)
    pos_lanes = jnp.zeros((rows, LANES), F32)
    for k in range(TOP_K):
        seg = (lane_i >= k * n_experts) & (lane_i < (k + 1) * n_experts)
        pk = jnp.sum(jnp.where(seg, placed, 0.0), axis=-1, keepdims=True)
        pos_lanes = jnp.where(lane_i == k, pk, pos_lanes)
    pos_ref[...] = pos_lanes[:, :TOP_K].astype(I32)
    post_ref[...] = jnp.transpose(pos_lanes)[:SUBLANES, :].astype(I32)
    cnt_ref[0] = total.astype(I32)


def _outproj(yf, ya, x2d, w_out, norm2, w_router, b_router):
    t, d = x2d.shape
    tm = min(TOKEN_TILE, t)
    n_tiles = t // tm
    n_experts = w_router.shape[1]
    assert TOP_K * n_experts == LANES
    tri = np.tril(np.ones((tm, tm), np.float32), -1)
    wr_hi = w_router.astype(BF16)
    wr_lo = (w_router - wr_hi.astype(F32)).astype(BF16)
    full = lambda i: (0, 0)
    row = lambda i: (i, 0)
    return pl.pallas_call(
        functools.partial(_outproj_kernel, n_experts=n_experts),
        grid=(n_tiles,),
        in_specs=[
            pl.BlockSpec((tm, yf.shape[1]), row),
            pl.BlockSpec((tm, ya.shape[1]), row),
            pl.BlockSpec((tm, d), row),
            pl.BlockSpec((w_out.shape[0], d), full),
            pl.BlockSpec((1, d), full),
            pl.BlockSpec((2, d, n_experts), lambda i: (0, 0, 0)),
            pl.BlockSpec((1, n_experts), full),
            pl.BlockSpec((tm, tm), full),
        ],
        out_specs=[
            pl.BlockSpec((tm, d), row),
            pl.BlockSpec((tm, d), row),
            pl.BlockSpec((tm, TOP_K), row),
            pl.BlockSpec((tm, TOP_K), row),
            pl.BlockSpec((SUBLANES, tm), row),
            pl.BlockSpec((1, 1, LANES), lambda i: (i, 0, 0)),
        ],
        out_shape=[
            jax.ShapeDtypeStruct((t, d), F32),
            jax.ShapeDtypeStruct((t, d), BF16),
            jax.ShapeDtypeStruct((t, TOP_K), F32),
            jax.ShapeDtypeStruct((t, TOP_K), I32),
            jax.ShapeDtypeStruct((n_tiles * SUBLANES, tm), I32),
            jax.ShapeDtypeStruct((n_tiles, 1, LANES), I32),
        ],
        compiler_params=_params(1, VMEM_LIMIT),
        name="outproj_router",
    )(yf, ya, x2d, w_out.astype(BF16), norm2.reshape(1, d), jnp.stack([wr_hi, wr_lo]),
      b_router.reshape(1, n_experts), jnp.asarray(tri, BF16))


def _chunk_sizes(limit):
    sizes, s = [], RUN_ALIGN
    while s <= limit:
        sizes.append(s)
        s *= 2
    return tuple(reversed(sizes))


def _for_each_chunk(n_rows, src0, dst0, sizes, fn):
    off = jnp.int32(0)
    for size in sizes:
        take = n_rows & size

        @pl.when(take != 0)
        def _(off=off, size=size):
            fn(pl.multiple_of(src0 + off, RUN_ALIGN), pl.multiple_of(dst0 + off, RUN_ALIGN), size)

        off = off + take


def _dispatch_kernel(cnt_ref, lst_ref, base_ref, tail_ref, post_ref, h2_ref, xs_ref,
                     buf, zbuf, sem, zsem, *, n_experts):
    j = pl.program_id(0)
    tm = h2_ref.shape[0]
    n_local = buf.shape[1]
    run_sizes = _chunk_sizes(tm)
    tail_sizes = _chunk_sizes(zbuf.shape[0])

    def tail_copy(src, dst, size):
        return pltpu.make_async_copy(zbuf.at[pl.ds(0, size), :], xs_ref.at[pl.ds(dst, size), :], zsem)

    def walk_runs(tile, slot, op):
        def runs(e, carry):
            r = tile * n_experts + e
            _for_each_chunk(
                cnt_ref[r], lst_ref[r], base_ref[r], run_sizes,
                lambda s, d, size: getattr(pltpu.make_async_copy(
                    buf.at[slot, pl.ds(s, size), :], xs_ref.at[pl.ds(d, size), :], sem.at[slot]),
                    op)())
            return carry
        lax.fori_loop(0, n_experts, runs, 0)

    slot = j % 2
    @pl.when(j >= 2)
    def _():
        walk_runs(j - 2, slot, "wait")

    @pl.when(j == 0)
    def _():
        zbuf[...] = jnp.zeros_like(zbuf)
        for op in ("start", "wait"):
            def tails(e, carry, op=op):
                _for_each_chunk(tail_ref[n_experts + e], 0, tail_ref[e], tail_sizes,
                                lambda s, d, size: getattr(tail_copy(s, d, size), op)())
                return carry
            lax.fori_loop(0, n_experts, tails, 0)

            def spare(b, carry, op=op):
                for half in range(EXPERT_ROWS // zbuf.shape[0]):
                    dst = pl.multiple_of(b * EXPERT_ROWS + half * zbuf.shape[0], RUN_ALIGN)
                    getattr(tail_copy(0, dst, zbuf.shape[0]), op)()
                return carry
            lax.fori_loop(tail_ref[2 * n_experts], xs_ref.shape[0] // EXPERT_ROWS, spare, 0)

    h = h2_ref[...]
    for rc in range(n_local // PERM_CHUNK):
        rows = lax.broadcasted_iota(I32, (PERM_CHUNK, tm), 0) + rc * PERM_CHUNK
        perm = jnp.zeros((PERM_CHUNK, tm), F32)
        for k in range(TOP_K):
            perm = jnp.where(rows == post_ref[k:k + 1, :], 1.0, perm)
        buf[slot, rc * PERM_CHUNK:(rc + 1) * PERM_CHUNK, :] = jnp.dot(
            perm.astype(BF16), h, preferred_element_type=F32)

    walk_runs(j, slot, "start")

    last = pl.num_programs(0) - 1
    @pl.when(j == last)
    def _():
        @pl.when(j >= 1)
        def _():
            walk_runs(j - 1, 1 - slot, "wait")
        walk_runs(j, slot, "wait")


def _local_rows(tm, n_experts):
    worst = TOP_K * tm + n_experts * (RUN_ALIGN - 1)
    return -(-worst // PERM_CHUNK) * PERM_CHUNK


def _dispatch(plan, post, h2, n_rows, n_experts):
    t, d = h2.shape
    tm = min(TOKEN_TILE, t)
    grid_spec = pltpu.PrefetchScalarGridSpec(
        num_scalar_prefetch=4,
        grid=(t // tm,),
        in_specs=[
            pl.BlockSpec((SUBLANES, tm), lambda i, *_: (i, 0)),
            pl.BlockSpec((tm, d), lambda i, *_: (i, 0)),
        ],
        out_specs=pl.BlockSpec(memory_space=pl.ANY),
        scratch_shapes=[pltpu.VMEM((2, _local_rows(tm, n_experts), d), F32),
                        pltpu.VMEM((EXPERT_ROWS // 2, d), F32),
                        pltpu.SemaphoreType.DMA((2,)), pltpu.SemaphoreType.DMA(())],
    )
    return pl.pallas_call(
        functools.partial(_dispatch_kernel, n_experts=n_experts),
        grid_spec=grid_spec,
        out_shape=jax.ShapeDtypeStruct((n_rows, d), F32),
        compiler_params=_params(1, VMEM_LIMIT),
        name="dispatch",
    )(plan["cnt"], plan["lst"], plan["base"], plan["tail"], post, h2)


def _combine_kernel(cnt_ref, lst_ref, base_ref, pos_ref, gate_ref, x1_ref, ys_ref, o_ref,
                    buf, sem, *, n_experts):
    j = pl.program_id(0)
    tm = x1_ref.shape[0]
    n_local = buf.shape[1]
    run_sizes = _chunk_sizes(tm)

    def walk_runs(tile, slot, op):
        def runs(e, carry):
            r = tile * n_experts + e
            _for_each_chunk(
                cnt_ref[r], base_ref[r], lst_ref[r], run_sizes,
                lambda s, d, size: getattr(pltpu.make_async_copy(
                    ys_ref.at[pl.ds(s, size), :], buf.at[slot, pl.ds(d, size), :], sem.at[slot]),
                    op)())
            return carry
        lax.fori_loop(0, n_experts, runs, 0)

    slot = j % 2
    @pl.when(j == 0)
    def _():
        buf[...] = jnp.zeros_like(buf)
        walk_runs(j, slot, "start")

    @pl.when(j + 1 < pl.num_programs(0))
    def _():
        walk_runs(j + 1, 1 - slot, "start")

    walk_runs(j, slot, "wait")

    acc = x1_ref[...]
    pos = pos_ref[...]
    gate = gate_ref[...]
    for rc in range(n_local // PERM_CHUNK):
        cols = lax.broadcasted_iota(I32, (tm, PERM_CHUNK), 1) + rc * PERM_CHUNK
        g = jnp.zeros((tm, PERM_CHUNK), F32)
        for k in range(TOP_K):
            g = jnp.where(cols == pos[:, k:k + 1], gate[:, k:k + 1], g)
        y = buf[slot, rc * PERM_CHUNK:(rc + 1) * PERM_CHUNK, :].astype(BF16)
        acc = acc + jnp.dot(g.astype(BF16), y, preferred_element_type=F32)
    o_ref[...] = acc


def _combine(plan, pos, gate, x1, ys, n_experts):
    t, d = x1.shape
    tm = min(TOKEN_TILE, t)
    grid_spec = pltpu.PrefetchScalarGridSpec(
        num_scalar_prefetch=3,
        grid=(t // tm,),
        in_specs=[
            pl.BlockSpec((tm, TOP_K), lambda i, *_: (i, 0)),
            pl.BlockSpec((tm, TOP_K), lambda i, *_: (i, 0)),
            pl.BlockSpec((tm, d), lambda i, *_: (i, 0)),
            pl.BlockSpec(memory_space=pl.ANY),
        ],
        out_specs=pl.BlockSpec((tm, d), lambda i, *_: (i, 0)),
        scratch_shapes=[pltpu.VMEM((2, _local_rows(tm, n_experts), d), F32),
                        pltpu.SemaphoreType.DMA((2,))],
    )
    return pl.pallas_call(
        functools.partial(_combine_kernel, n_experts=n_experts),
        grid_spec=grid_spec,
        out_shape=jax.ShapeDtypeStruct((t, d), F32),
        compiler_params=_params(1, VMEM_LIMIT),
        name="combine",
    )(plan["cnt"], plan["lst"], plan["base"], pos, gate, x1, ys)


def _expert_kernel(be_ref, meta_ref, xs_ref, wgu_ref, bg_ref, bu_ref, wd_ref, bd_ref, perm_ref,
                   ys_ref, wg_s, wu_s, wd_s):
    i = pl.program_id(0)
    n_used = meta_ref[0]
    active = i < n_used
    new_expert = (i == 0) | (be_ref[i] != be_ref[jnp.maximum(i - 1, 0)])

    @pl.when(active & new_expert)
    def _():
        width = perm_ref.shape[0]
        for c in range(wgu_ref.shape[2] // width):
            wc = wgu_ref[0, :, c * width:(c + 1) * width].astype(BF16)
            r = jnp.dot(wc, perm_ref[...], preferred_element_type=F32)
            wg_s[:, c * LANES:(c + 1) * LANES] = r[:, :LANES].astype(BF16)
            wu_s[:, c * LANES:(c + 1) * LANES] = r[:, LANES:].astype(BF16)
        wd_s[...] = wd_ref[0].astype(BF16)

    @pl.when(active)
    def _():
        xb = xs_ref[...].astype(BF16)
        g = jnp.dot(xb, wg_s[...], preferred_element_type=F32) + bg_ref[0]
        up = jnp.dot(xb, wu_s[...], preferred_element_type=F32) + bu_ref[0]
        g = jnp.minimum(g, SWIGLU_LIMIT)
        up = jnp.clip(up, -SWIGLU_LIMIT, SWIGLU_LIMIT)
        act = g * (1.0 / (1.0 + jnp.exp(-SWIGLU_ALPHA * g))) * (up + 1.0)
        ys_ref[...] = jnp.dot(act.astype(BF16), wd_s[...], preferred_element_type=F32) + bd_ref[0]

    @pl.when(jnp.logical_not(active))
    def _():
        ys_ref[...] = jnp.zeros_like(ys_ref)


def _experts(blk_e, meta, xs, w_gate_up, b_gate_up, w_down, b_down):
    n_rows, d = xs.shape
    n_experts, _, f2 = w_gate_up.shape
    f = f2 // 2
    bm = EXPERT_ROWS
    n_blocks = n_rows // bm
    width = 2 * LANES
    perm = np.zeros((width, width), np.float32)
    perm[2 * np.arange(LANES), np.arange(LANES)] = 1.0
    perm[2 * np.arange(LANES) + 1, LANES + np.arange(LANES)] = 1.0
    bg = b_gate_up[:, 0::2].reshape(n_experts, 1, f)
    bu = b_gate_up[:, 1::2].reshape(n_experts, 1, f)
    rows = lambda i, be, meta: (jnp.minimum(i, meta[0] - 1), 0)
    per_e = lambda i, be, meta: (be[i], 0, 0)
    grid_spec = pltpu.PrefetchScalarGridSpec(
        num_scalar_prefetch=2,
        grid=(n_blocks,),
        in_specs=[
            pl.BlockSpec((bm, d), rows),
            pl.BlockSpec((1, d, f2), per_e),
            pl.BlockSpec((1, 1, f), per_e),
            pl.BlockSpec((1, 1, f), per_e),
            pl.BlockSpec((1, f, d), per_e),
            pl.BlockSpec((1, 1, d), per_e),
            pl.BlockSpec((width, width), lambda i, be, meta: (0, 0)),
        ],
        out_specs=pl.BlockSpec((bm, d), lambda i, be, meta: (i, 0)),
        scratch_shapes=[pltpu.VMEM((d, f), BF16), pltpu.VMEM((d, f), BF16),
                        pltpu.VMEM((f, d), BF16)],
    )
    return pl.pallas_call(
        _expert_kernel,
        grid_spec=grid_spec,
        out_shape=jax.ShapeDtypeStruct((n_rows, d), F32),
        compiler_params=_params(1, VMEM_LIMIT),
        name="experts",
    )(blk_e, meta, xs, w_gate_up, bg, bu, w_down, b_down.reshape(n_experts, 1, d),
      jnp.asarray(perm, BF16))


def _routing_plan(counts, n_experts, bm, n_blocks):
    cnt = counts[:, 0, :n_experts]
    run = (cnt + RUN_ALIGN - 1) // RUN_ALIGN * RUN_ALIGN
    per_expert = jnp.sum(run, axis=0)
    padded = (per_expert + bm - 1) // bm * bm
    pend = jnp.cumsum(padded)
    pstart = pend - padded
    base = pstart[None, :] + jnp.cumsum(run, axis=0) - run
    lst = jnp.cumsum(run, axis=1) - run
    n_used = pend[-1] // bm
    tail = jnp.concatenate([pstart + per_expert, padded - per_expert, n_used[None]])
    starts = jnp.arange(n_blocks, dtype=I32) * bm
    blk = jnp.sum((starts[:, None] >= pend[None, :]).astype(I32), axis=1)
    blk = jnp.minimum(blk, n_experts - 1)
    last = jnp.sum((((n_used - 1) * bm) >= pend).astype(I32))
    blk_e = jnp.where(jnp.arange(n_blocks) < n_used, blk, jnp.minimum(last, n_experts - 1))
    plan = {"cnt": run.reshape(-1).astype(I32), "lst": lst.reshape(-1).astype(I32),
            "base": base.reshape(-1).astype(I32), "tail": tail.astype(I32)}
    return plan, blk_e.astype(I32), n_used.astype(I32).reshape(1)


def _layer(x2d, batch, seq, norm1, w_in, q_norm, k_norm, sinks, rel_bias, w_fourier, g_fourier_out,
           g_attn_out, w_out, norm2, w_router, b_router, w_gate_up, b_gate_up, w_down, b_down):
    t, d = x2d.shape
    n_experts = w_router.shape[1]
    u, q, kv = _inproj(x2d, norm1, w_in, q_norm, k_norm)
    yf = _fourier(u, w_fourier, g_fourier_out, batch, seq)
    ya = _attention(q, kv, sinks, rel_bias, g_attn_out, batch, seq)
    x1, h2, gate, pos, post, counts = _outproj(yf, ya, x2d, w_out, norm2, w_router, b_router)
    bm = EXPERT_ROWS
    n_tiles = t // min(TOKEN_TILE, t)
    worst_rows = t * TOP_K + n_tiles * n_experts * (RUN_ALIGN - 1) + n_experts * (bm - RUN_ALIGN)
    n_blocks = -(-worst_rows // bm)
    plan, blk_e, meta = _routing_plan(counts, n_experts, bm, n_blocks)
    xs = _dispatch(plan, post, h2, n_blocks * bm, n_experts)
    ys = _experts(blk_e, meta, xs, w_gate_up, b_gate_up, w_down, b_down)
    return _combine(plan, pos, gate, x1, ys, n_experts)


def kernel(x, norm1, w_in, q_norm, k_norm, sinks, rel_bias, w_fourier, g_fourier_out, g_attn_out,
           w_out, norm2, w_router, b_router, w_gate_up, b_gate_up, w_down, b_down):
    b, s, d = x.shape
    x2d = x.reshape(b * s, d)
    for l in range(norm1.shape[0]):
        x2d = _layer(x2d, b, s, norm1[l], w_in[l], q_norm[l], k_norm[l], sinks[l], rel_bias,
                     w_fourier[l], g_fourier_out[l], g_attn_out[l], w_out[l], norm2[l],
                     w_router[l], b_router[l], w_gate_up[l], b_gate_up[l], w_down[l], b_down[l])
    return x2d.reshape(b, s, d)
```

```python
import functools
import math

import jax
import jax.numpy as jnp
import numpy as np
from jax import lax
from jax.experimental import pallas as pl
from jax.experimental.pallas import tpu as pltpu

F32 = jnp.float32
BF16 = jnp.bfloat16
I32 = jnp.int32

NORM_EPS = 1e-5
QK_EPS = 1e-6
HEAD_DIM = 64
N_Q_HEADS = 8
N_KV_HEADS = 2
FOURIER_GROUPS = 4
FOURIER_CH = 128
FOURIER_WIDTH = FOURIER_GROUPS * FOURIER_CH
ATTN_WIDTH = N_Q_HEADS * HEAD_DIM
KV_WIDTH = N_KV_HEADS * HEAD_DIM
WINDOW = 128
Q_BLOCK = 128
N_BUCKETS = 32
MAX_DISTANCE = 128
TOP_K = 4
SWIGLU_ALPHA = 1.702
SWIGLU_LIMIT = 7.0
MASK_VALUE = -1e30
LOG2E = math.log2(math.e)

LANES = 128
SUBLANES = 8
TOKEN_TILE = 512
RUN_ALIGN = SUBLANES
PERM_CHUNK = 256
EXPERT_ROWS = 512
VMEM_LIMIT = 56 * 1024 * 1024


def _params(n_axes, vmem=None):
    return pltpu.CompilerParams(
        dimension_semantics=("arbitrary",) * n_axes, vmem_limit_bytes=vmem)


def _pair_head_norm(xc, gain, lo):
    x2 = xc * xc
    s_lo = jnp.sum(jnp.where(lo, x2, 0.0), axis=-1, keepdims=True)
    s_hi = jnp.sum(jnp.where(lo, 0.0, x2), axis=-1, keepdims=True)
    inv = jnp.where(lo, lax.rsqrt(s_lo * (1.0 / HEAD_DIM) + QK_EPS),
                    lax.rsqrt(s_hi * (1.0 / HEAD_DIM) + QK_EPS))
    return xc * inv * gain


def _inproj_kernel(x_ref, g1_ref, w_ref, qg_ref, kg_ref, u_ref, q_ref, kv_ref):
    x = x_ref[...]
    ms = jnp.mean(x * x, axis=-1, keepdims=True)
    h = (x * lax.rsqrt(ms + NORM_EPS) * g1_ref[...]).astype(BF16)
    z = jnp.dot(h, w_ref[...], preferred_element_type=F32)
    u_ref[...] = z[:, :FOURIER_WIDTH].astype(BF16)
    rows = x.shape[0]
    lo = lax.broadcasted_iota(I32, (rows, LANES), 1) < HEAD_DIM
    q0 = FOURIER_WIDTH
    for c in range(ATTN_WIDTH // LANES):
        qc = _pair_head_norm(z[:, q0 + c * LANES:q0 + (c + 1) * LANES], qg_ref[...], lo)
        q_ref[:, c * LANES:(c + 1) * LANES] = (qc * (HEAD_DIM ** -0.5 * LOG2E)).astype(BF16)
    k0 = q0 + ATTN_WIDTH
    kc = _pair_head_norm(z[:, k0:k0 + KV_WIDTH], kg_ref[...], lo)
    vc = z[:, k0 + KV_WIDTH:k0 + 2 * KV_WIDTH]
    kv_ref[:, 0:LANES] = kc.astype(BF16)
    kv_ref[:, LANES:2 * LANES] = pltpu.roll(kc, HEAD_DIM, 1).astype(BF16)
    kv_ref[:, 2 * LANES:3 * LANES] = vc.astype(BF16)
    kv_ref[:, 3 * LANES:4 * LANES] = pltpu.roll(vc, HEAD_DIM, 1).astype(BF16)


def _inproj(x2d, norm1, w_in, q_norm, k_norm):
    t, d = x2d.shape
    tm = min(TOKEN_TILE, t)
    n_in = w_in.shape[1]
    qg = jnp.tile(q_norm, LANES // HEAD_DIM).reshape(1, LANES)
    kg = jnp.tile(k_norm, LANES // HEAD_DIM).reshape(1, LANES)
    full = lambda i: (0, 0)
    return pl.pallas_call(
        _inproj_kernel,
        grid=(t // tm,),
        in_specs=[
            pl.BlockSpec((tm, d), lambda i: (i, 0)),
            pl.BlockSpec((1, d), full),
            pl.BlockSpec((d, n_in), full),
            pl.BlockSpec((1, LANES), full),
            pl.BlockSpec((1, LANES), full),
        ],
        out_specs=[
            pl.BlockSpec((tm, FOURIER_WIDTH), lambda i: (i, 0)),
            pl.BlockSpec((tm, ATTN_WIDTH), lambda i: (i, 0)),
            pl.BlockSpec((tm, 4 * LANES), lambda i: (i, 0)),
        ],
        out_shape=[
            jax.ShapeDtypeStruct((t, FOURIER_WIDTH), BF16),
            jax.ShapeDtypeStruct((t, ATTN_WIDTH), BF16),
            jax.ShapeDtypeStruct((t, 4 * LANES), BF16),
        ],
        compiler_params=_params(1, VMEM_LIMIT),
        name="inproj",
    )(x2d, norm1.reshape(1, d), w_in.astype(BF16), qg, kg)


def _fourier_kernel(u_ref, cs_ref, ss_ref, cc_ref, sc_ref, wf_ref, g_ref, o_ref, p_scr, q_scr,
                    *, scale, row_block):
    for g in range(FOURIER_GROUPS):
        sl = slice(g * FOURIER_CH, (g + 1) * FOURIER_CH)
        w = wf_ref[g].astype(BF16)
        a = (jnp.dot(cc_ref[...], w, preferred_element_type=F32) * scale).astype(BF16)
        b = (jnp.dot(sc_ref[...], w, preferred_element_type=F32) * scale).astype(BF16)
        ug = u_ref[:, sl]
        p_scr[:, sl] = jnp.dot(ug, a, preferred_element_type=F32).astype(BF16)
        q_scr[:, sl] = jnp.dot(ug, b, preferred_element_type=F32).astype(BF16)
    s = u_ref.shape[0]
    for r in range(s // row_block):
        rs = slice(r * row_block, (r + 1) * row_block)
        y = (jnp.dot(cs_ref[rs, :], p_scr[...], preferred_element_type=F32)
             + jnp.dot(ss_ref[rs, :], q_scr[...], preferred_element_type=F32))
        ms = jnp.mean(y * y, axis=-1, keepdims=True)
        o_ref[rs, :] = (y * lax.rsqrt(ms + NORM_EPS) * g_ref[...]).astype(BF16)


def _dft_tables(n):
    k = np.arange(n, dtype=np.int64)
    ang = 2.0 * np.pi * ((k[:, None] * k[None, :]) % n).astype(np.float64) / n
    return np.cos(ang), np.sin(ang)


def _fourier(u, w_fourier, g_out, batch, seq):
    cs, ss = _dft_tables(seq)
    cc, sc = _dft_tables(FOURIER_CH)
    scale = 1.0 / math.sqrt(seq * FOURIER_CH)
    row_block = min(512, seq)
    full2 = lambda b: (0, 0)
    return pl.pallas_call(
        functools.partial(_fourier_kernel, scale=scale, row_block=row_block),
        grid=(batch,),
        in_specs=[
            pl.BlockSpec((seq, FOURIER_WIDTH), lambda b: (b, 0)),
            pl.BlockSpec((seq, seq), full2),
            pl.BlockSpec((seq, seq), full2),
            pl.BlockSpec((FOURIER_CH, FOURIER_CH), full2),
            pl.BlockSpec((FOURIER_CH, FOURIER_CH), full2),
            pl.BlockSpec((FOURIER_GROUPS, FOURIER_CH, FOURIER_CH), lambda b: (0, 0, 0)),
            pl.BlockSpec((1, FOURIER_WIDTH), full2),
        ],
        out_specs=pl.BlockSpec((seq, FOURIER_WIDTH), lambda b: (b, 0)),
        out_shape=jax.ShapeDtypeStruct((batch * seq, FOURIER_WIDTH), BF16),
        scratch_shapes=[pltpu.VMEM((seq, FOURIER_WIDTH), BF16),
                        pltpu.VMEM((seq, FOURIER_WIDTH), BF16)],
        compiler_params=_params(1, VMEM_LIMIT),
        name="fourier",
    )(u, jnp.asarray(cs, BF16), jnp.asarray(ss, BF16), jnp.asarray(cc, BF16),
      jnp.asarray(-sc, BF16), w_fourier, g_out.reshape(1, FOURIER_WIDTH))


def _attn_kernel(sink_ref, q_ref, kvp_ref, kvo_ref, kvn_ref, bias_a_ref, bias_b_ref, g_ref, o_ref,
                 acc_ref):
    kv = jnp.concatenate([kvp_ref[...], kvo_ref[...], kvn_ref[...]], axis=0)
    nk = kv.shape[0]
    lo = lax.broadcasted_iota(I32, (nk, LANES), 1) < HEAD_DIM
    k_a, k_b = kv[:, 0:LANES], kv[:, LANES:2 * LANES]
    v_a, v_b = kv[:, 2 * LANES:3 * LANES], kv[:, 3 * LANES:4 * LANES]
    zero = jnp.zeros_like(k_a)
    k_lo = (jnp.where(lo, k_a, zero), jnp.where(lo, k_b, zero))
    k_hi = (jnp.where(lo, zero, k_b), jnp.where(lo, zero, k_a))
    v_lo = (jnp.where(lo, v_a, zero), jnp.where(lo, v_b, zero))
    v_hi = (jnp.where(lo, zero, v_b), jnp.where(lo, zero, v_a))
    lo_out = lax.broadcasted_iota(I32, (Q_BLOCK, LANES), 1) < HEAD_DIM
    bias_refs = (bias_a_ref, bias_b_ref)
    rows2 = q_ref.shape[0]
    nt = (((1,), (1,)), ((), ()))
    for h in range(N_KV_HEADS):
        qs = jnp.concatenate([q_ref[:, (2 * h) * LANES:(2 * h + 1) * LANES],
                              q_ref[:, (2 * h + 1) * LANES:(2 * h + 2) * LANES]], axis=0)
        s_par = (lax.dot_general(qs, k_lo[h], nt, preferred_element_type=F32),
                 lax.dot_general(qs, k_hi[h], nt, preferred_element_type=F32))
        for sb in range(2):
            keys = slice(sb * Q_BLOCK, sb * Q_BLOCK + 3 * Q_BLOCK)
            vcat = jnp.concatenate([v_lo[h][keys, :], v_hi[h][keys, :]], axis=0)
            for c in range(2):
                r0 = c * rows2 + sb * Q_BLOCK
                probs, invs = [], []
                for par in range(2):
                    hq = 4 * h + 2 * c + par
                    s = s_par[par][r0:r0 + Q_BLOCK, keys] + bias_refs[sb][hq]
                    sink = sink_ref[hq]
                    m = jnp.maximum(jnp.max(s, axis=-1, keepdims=True), sink)
                    p = jnp.exp2(s - m)
                    denom = jnp.sum(p, axis=-1, keepdims=True) + jnp.exp2(sink - m)
                    probs.append(p.astype(BF16))
                    invs.append(1.0 / denom)
                pcat = jnp.concatenate(probs, axis=1)
                chunk = 2 * h + c
                o = jnp.dot(pcat, vcat, preferred_element_type=F32)
                acc_ref[sb * Q_BLOCK:(sb + 1) * Q_BLOCK, chunk * LANES:(chunk + 1) * LANES] = (
                    o * jnp.where(lo_out, invs[0], invs[1]))
    y = acc_ref[...]
    ms = jnp.mean(y * y, axis=-1, keepdims=True)
    o_ref[...] = (y * lax.rsqrt(ms + NORM_EPS) * g_ref[...]).astype(BF16)


def _t5_bucket(rel):
    nb = N_BUCKETS // 2
    max_exact = nb // 2
    ret = (rel > 0).astype(jnp.int32) * nb
    n = jnp.abs(rel)
    nf = jnp.maximum(n, 1).astype(jnp.float32)
    large = max_exact + (jnp.log(nf / max_exact) / math.log(MAX_DISTANCE / max_exact)
                         * (nb - max_exact)).astype(jnp.int32)
    large = jnp.minimum(large, nb - 1)
    return ret + jnp.where(n < max_exact, n, large)


def _attention(q, kv, sinks, rel_bias, g_out, batch, seq):
    nb = seq // Q_BLOCK
    assert nb % 2 == 0
    nb2 = nb // 2
    qi = jnp.arange(Q_BLOCK, dtype=jnp.int32)[:, None]
    kj = jnp.arange(3 * Q_BLOCK, dtype=jnp.int32)[None, :]
    rel = kj - Q_BLOCK - qi
    period = 4 * Q_BLOCK
    p = jnp.arange(period, dtype=jnp.int32)
    off = jnp.where(p < 3 * Q_BLOCK, p, p - period) - Q_BLOCK
    hit = _t5_bucket(off)[None, :, None] == jnp.arange(N_BUCKETS, dtype=jnp.int32)
    by_off = jnp.sum(jnp.where(hit, rel_bias.astype(F32).T[:, None, :], 0.0), axis=-1)
    bias = jnp.tile(by_off, (1, Q_BLOCK))[:, :Q_BLOCK * (period - 1)]
    bias = bias.reshape(N_Q_HEADS, Q_BLOCK, period - 1)[:, :, :3 * Q_BLOCK]
    band = jnp.abs(rel) <= WINDOW
    first = band & (kj >= Q_BLOCK)
    last = band & (kj < 2 * Q_BLOCK)
    table = jnp.stack([jnp.where(msk[None], bias * LOG2E, MASK_VALUE) for msk in (first, band, last)])
    q_rows = 2 * Q_BLOCK
    grid_spec = pltpu.PrefetchScalarGridSpec(
        num_scalar_prefetch=1,
        grid=(batch, nb2),
        in_specs=[
            pl.BlockSpec((q_rows, ATTN_WIDTH), lambda b, i, s: (b * nb2 + i, 0)),
            pl.BlockSpec((Q_BLOCK, 4 * LANES),
                         lambda b, i, s: (b * nb + jnp.maximum(2 * i - 1, 0), 0)),
            pl.BlockSpec((q_rows, 4 * LANES), lambda b, i, s: (b * nb2 + i, 0)),
            pl.BlockSpec((Q_BLOCK, 4 * LANES),
                         lambda b, i, s: (b * nb + jnp.minimum(2 * i + 2, nb - 1), 0)),
            pl.BlockSpec((None, N_Q_HEADS, Q_BLOCK, 3 * Q_BLOCK),
                         lambda b, i, s: (jnp.where(i == 0, 0, 1), 0, 0, 0)),
            pl.BlockSpec((None, N_Q_HEADS, Q_BLOCK, 3 * Q_BLOCK),
                         lambda b, i, s: (jnp.where(i == nb2 - 1, 2, 1), 0, 0, 0)),
            pl.BlockSpec((1, ATTN_WIDTH), lambda b, i, s: (0, 0)),
        ],
        out_specs=pl.BlockSpec((q_rows, ATTN_WIDTH), lambda b, i, s: (b * nb2 + i, 0)),
        scratch_shapes=[pltpu.VMEM((q_rows, ATTN_WIDTH), F32)],
    )
    return pl.pallas_call(
        _attn_kernel,
        grid_spec=grid_spec,
        out_shape=jax.ShapeDtypeStruct((batch * seq, ATTN_WIDTH), BF16),
        compiler_params=_params(2, VMEM_LIMIT),
        name="attention",
    )(sinks.astype(F32) * LOG2E, q, kv, kv, kv, table, table, g_out.reshape(1, ATTN_WIDTH))


def _outproj_kernel(yf_ref, ya_ref, x_ref, wo_ref, g2_ref, wr_ref, br_ref, tri_ref,
                    x1_ref, h2_ref, gate_ref, pos_ref, post_ref, cnt_ref, *, n_experts):
    half = yf_ref.shape[1]
    mix = (jnp.dot(yf_ref[...], wo_ref[:half, :], preferred_element_type=F32)
           + jnp.dot(ya_ref[...], wo_ref[half:, :], preferred_element_type=F32))
    x1 = x_ref[...] + mix
    x1_ref[...] = x1
    ms = jnp.mean(x1 * x1, axis=-1, keepdims=True)
    h2 = x1 * lax.rsqrt(ms + NORM_EPS) * g2_ref[...]
    h2_ref[...] = h2.astype(BF16)
    h_hi = h2.astype(BF16)
    h_lo = (h2 - h_hi.astype(F32)).astype(BF16)
    logits = (jnp.dot(h_hi, wr_ref[0], preferred_element_type=F32)
              + jnp.dot(h_hi, wr_ref[1], preferred_element_type=F32)
              + jnp.dot(h_lo, wr_ref[0], preferred_element_type=F32)) + br_ref[...]
    rows = logits.shape[0]
    lane_e = lax.broadcasted_iota(I32, (rows, n_experts), 1).astype(F32)
    work = logits
    vals, idxs = [], []
    for _ in range(TOP_K):
        m = jnp.max(work, axis=-1, keepdims=True)
        ik = jnp.min(jnp.where(work == m, lane_e, float(n_experts)), axis=-1, keepdims=True)
        work = jnp.where(lane_e == ik, -jnp.inf, work)
        vals.append(m)
        idxs.append(ik)
    exps = [jnp.exp(v - vals[0]) for v in vals]
    inv = 1.0 / (exps[0] + exps[1] + exps[2] + exps[3])
    lane_k = lax.broadcasted_iota(I32, (rows, TOP_K), 1)
    gate = jnp.zeros((rows, TOP_K), F32)
    for k in range(TOP_K):
        gate = jnp.where(lane_k == k, exps[k] * inv, gate)
    gate_ref[...] = gate

    lane = lax.broadcasted_iota(I32, (rows, LANES), 1).astype(F32)
    onehot = jnp.zeros((rows, LANES), F32)
    for k in range(TOP_K):
        onehot = onehot + jnp.where(lane == idxs[k] + float(k * n_experts), 1.0, 0.0)
    before = jnp.dot(tri_ref[...], onehot.astype(BF16), preferred_element_type=F32)
    colsum = jnp.sum(onehot, axis=0, keepdims=True)
    lane1 = lax.broadcasted_iota(I32, (1, LANES), 1)
    prefix = jnp.zeros((1, LANES), F32)
    total = colsum
    for k in range(1, TOP_K):
        rolled = pltpu.roll(colsum, k * n_experts, 1)
        prefix = prefix + jnp.where(lane1 >= k * n_experts, rolled, 0.0)
        total = total + rolled
    run = jnp.floor((total + (RUN_ALIGN - 1)) * (1.0 / RUN_ALIGN)) * RUN_ALIGN
    incl = run
    lane_in_seg = lane1 & (n_experts - 1)
    shift = 1
    while shift < n_experts:
        incl = incl + jnp.where(lane_in_seg >= shift, pltpu.roll(incl, shift, 1), 0.0)
        shift *= 2
    run_start = incl - run
    placed = (before + prefix + run_start) * onehot
    lane_i = lax.broadcasted_iota(I32, (rows, LANES), 1)
    pos_lanes = jnp.zeros((rows, LANES), F32)
    for k in range(TOP_K):
        seg = (lane_i >= k * n_experts) & (lane_i < (k + 1) * n_experts)
        pk = jnp.sum(jnp.where(seg, placed, 0.0), axis=-1, keepdims=True)
        pos_lanes = jnp.where(lane_i == k, pk, pos_lanes)
    pos_ref[...] = pos_lanes[:, :TOP_K].astype(I32)
    post_ref[...] = jnp.transpose(pos_lanes)[:SUBLANES, :].astype(I32)
    cnt_ref[0] = total.astype(I32)


def _outproj(yf, ya, x2d, w_out, norm2, w_router, b_router):
    t, d = x2d.shape
    tm = min(TOKEN_TILE, t)
    n_tiles = t // tm
    n_experts = w_router.shape[1]
    assert TOP_K * n_experts == LANES
    tri = np.tril(np.ones((tm, tm), np.float32), -1)
    wr_hi = w_router.astype(BF16)
    wr_lo = (w_router - wr_hi.astype(F32)).astype(BF16)
    full = lambda i: (0, 0)
    row = lambda i: (i, 0)
    return pl.pallas_call(
        functools.partial(_outproj_kernel, n_experts=n_experts),
        grid=(n_tiles,),
        in_specs=[
            pl.BlockSpec((tm, yf.shape[1]), row),
            pl.BlockSpec((tm, ya.shape[1]), row),
            pl.BlockSpec((tm, d), row),
            pl.BlockSpec((w_out.shape[0], d), full),
            pl.BlockSpec((1, d), full),
            pl.BlockSpec((2, d, n_experts), lambda i: (0, 0, 0)),
            pl.BlockSpec((1, n_experts), full),
            pl.BlockSpec((tm, tm), full),
        ],
        out_specs=[
            pl.BlockSpec((tm, d), row),
            pl.BlockSpec((tm, d), row),
            pl.BlockSpec((tm, TOP_K), row),
            pl.BlockSpec((tm, TOP_K), row),
            pl.BlockSpec((SUBLANES, tm), row),
            pl.BlockSpec((1, 1, LANES), lambda i: (i, 0, 0)),
        ],
        out_shape=[
            jax.ShapeDtypeStruct((t, d), F32),
            jax.ShapeDtypeStruct((t, d), BF16),
            jax.ShapeDtypeStruct((t, TOP_K), F32),
            jax.ShapeDtypeStruct((t, TOP_K), I32),
            jax.ShapeDtypeStruct((n_tiles * SUBLANES, tm), I32),
            jax.ShapeDtypeStruct((n_tiles, 1, LANES), I32),
        ],
        compiler_params=_params(1, VMEM_LIMIT),
        name="outproj_router",
    )(yf, ya, x2d, w_out.astype(BF16), norm2.reshape(1, d), jnp.stack([wr_hi, wr_lo]),
      b_router.reshape(1, n_experts), jnp.asarray(tri, BF16))


def _rows(start, size):
    if not isinstance(size, int):
        size = pl.multiple_of(size, RUN_ALIGN)
    return pl.ds(pl.multiple_of(start, RUN_ALIGN), size)


def _dispatch_kernel(cnt_ref, lst_ref, base_ref, rows_ref, tail_ref, post_ref, h2_ref, xs_ref,
                     buf, zbuf, sem, zsem, *, n_experts):
    j = pl.program_id(0)
    tm = h2_ref.shape[0]
    n_local = buf.shape[1]

    def start_runs(tile, slot):
        def run(e, carry):
            r = tile * n_experts + e
            n = cnt_ref[r]

            @pl.when(n > 0)
            def _():
                pltpu.make_async_copy(buf.at[slot, _rows(lst_ref[r], n), :],
                                      xs_ref.at[_rows(base_ref[r], n), :], sem.at[slot]).start()
            return carry
        lax.fori_loop(0, n_experts, run, 0)

    def wait_runs(tile, slot):
        n = rows_ref[tile]

        @pl.when(n > 0)
        def _():
            pltpu.make_async_copy(buf.at[slot, _rows(0, n), :], xs_ref.at[_rows(0, n), :],
                                  sem.at[slot]).wait()

    def zero_fill(op):
        def tail(e, carry):
            n = tail_ref[n_experts + e]

            @pl.when(n > 0)
            def _():
                getattr(pltpu.make_async_copy(zbuf.at[_rows(0, n), :],
                                              xs_ref.at[_rows(tail_ref[e], n), :], zsem), op)()
            return carry
        lax.fori_loop(0, n_experts, tail, 0)

        def spare(b, carry):
            getattr(pltpu.make_async_copy(zbuf, xs_ref.at[_rows(b * EXPERT_ROWS, EXPERT_ROWS), :],
                                          zsem), op)()
            return carry
        lax.fori_loop(tail_ref[2 * n_experts], xs_ref.shape[0] // EXPERT_ROWS, spare, 0)

    slot = j % 2
    @pl.when(j >= 2)
    def _():
        wait_runs(j - 2, slot)

    @pl.when(j == 0)
    def _():
        zbuf[...] = jnp.zeros_like(zbuf)
        zero_fill("start")
        zero_fill("wait")

    h = h2_ref[...]
    for rc in range(n_local // PERM_CHUNK):
        rows = lax.broadcasted_iota(I32, (PERM_CHUNK, tm), 0) + rc * PERM_CHUNK
        perm = jnp.zeros((PERM_CHUNK, tm), F32)
        for k in range(TOP_K):
            perm = jnp.where(rows == post_ref[k:k + 1, :], 1.0, perm)
        buf[slot, rc * PERM_CHUNK:(rc + 1) * PERM_CHUNK, :] = jnp.dot(
            perm.astype(BF16), h, preferred_element_type=F32)

    start_runs(j, slot)

    @pl.when(j == pl.num_programs(0) - 1)
    def _():
        @pl.when(j >= 1)
        def _():
            wait_runs(j - 1, 1 - slot)
        wait_runs(j, slot)


def _local_rows(tm, n_experts):
    worst = TOP_K * tm + n_experts * (RUN_ALIGN - 1)
    return -(-worst // PERM_CHUNK) * PERM_CHUNK


def _dispatch(plan, post, h2, n_rows, n_experts):
    t, d = h2.shape
    tm = min(TOKEN_TILE, t)
    grid_spec = pltpu.PrefetchScalarGridSpec(
        num_scalar_prefetch=5,
        grid=(t // tm,),
        in_specs=[
            pl.BlockSpec((SUBLANES, tm), lambda i, *_: (i, 0)),
            pl.BlockSpec((tm, d), lambda i, *_: (i, 0)),
        ],
        out_specs=pl.BlockSpec(memory_space=pl.ANY),
        scratch_shapes=[pltpu.VMEM((2, _local_rows(tm, n_experts), d), F32),
                        pltpu.VMEM((EXPERT_ROWS, d), F32),
                        pltpu.SemaphoreType.DMA((2,)), pltpu.SemaphoreType.DMA(())],
    )
    return pl.pallas_call(
        functools.partial(_dispatch_kernel, n_experts=n_experts),
        grid_spec=grid_spec,
        out_shape=jax.ShapeDtypeStruct((n_rows, d), F32),
        compiler_params=_params(1, VMEM_LIMIT),
        name="dispatch",
    )(plan["cnt"], plan["lst"], plan["base"], plan["rows"], plan["tail"], post, h2)


def _combine_kernel(cnt_ref, lst_ref, base_ref, rows_ref, pos_ref, gate_ref, x1_ref, ys_ref, o_ref,
                    buf, sem, *, n_experts):
    j = pl.program_id(0)
    tm = x1_ref.shape[0]
    n_local = buf.shape[1]

    def start_runs(tile, slot):
        def run(e, carry):
            r = tile * n_experts + e
            n = cnt_ref[r]

            @pl.when(n > 0)
            def _():
                pltpu.make_async_copy(ys_ref.at[_rows(base_ref[r], n), :],
                                      buf.at[slot, _rows(lst_ref[r], n), :], sem.at[slot]).start()
            return carry
        lax.fori_loop(0, n_experts, run, 0)

    def wait_runs(tile, slot):
        n = rows_ref[tile]

        @pl.when(n > 0)
        def _():
            pltpu.make_async_copy(ys_ref.at[_rows(0, n), :], buf.at[slot, _rows(0, n), :],
                                  sem.at[slot]).wait()

    slot = j % 2
    @pl.when(j == 0)
    def _():
        buf[...] = jnp.zeros_like(buf)
        start_runs(j, slot)

    @pl.when(j + 1 < pl.num_programs(0))
    def _():
        start_runs(j + 1, 1 - slot)

    wait_runs(j, slot)

    acc = x1_ref[...]
    pos = pos_ref[...]
    gate = gate_ref[...]
    for rc in range(n_local // PERM_CHUNK):
        cols = lax.broadcasted_iota(I32, (tm, PERM_CHUNK), 1) + rc * PERM_CHUNK
        g = jnp.zeros((tm, PERM_CHUNK), F32)
        for k in range(TOP_K):
            g = jnp.where(cols == pos[:, k:k + 1], gate[:, k:k + 1], g)
        y = buf[slot, rc * PERM_CHUNK:(rc + 1) * PERM_CHUNK, :].astype(BF16)
        acc = acc + jnp.dot(g.astype(BF16), y, preferred_element_type=F32)
    o_ref[...] = acc


def _combine(plan, pos, gate, x1, ys, n_experts):
    t, d = x1.shape
    tm = min(TOKEN_TILE, t)
    grid_spec = pltpu.PrefetchScalarGridSpec(
        num_scalar_prefetch=4,
        grid=(t // tm,),
        in_specs=[
            pl.BlockSpec((tm, TOP_K), lambda i, *_: (i, 0)),
            pl.BlockSpec((tm, TOP_K), lambda i, *_: (i, 0)),
            pl.BlockSpec((tm, d), lambda i, *_: (i, 0)),
            pl.BlockSpec(memory_space=pl.ANY),
        ],
        out_specs=pl.BlockSpec((tm, d), lambda i, *_: (i, 0)),
        scratch_shapes=[pltpu.VMEM((2, _local_rows(tm, n_experts), d), F32),
                        pltpu.SemaphoreType.DMA((2,))],
    )
    return pl.pallas_call(
        functools.partial(_combine_kernel, n_experts=n_experts),
        grid_spec=grid_spec,
        out_shape=jax.ShapeDtypeStruct((t, d), F32),
        compiler_params=_params(1, VMEM_LIMIT),
        name="combine",
    )(plan["cnt"], plan["lst"], plan["base"], plan["rows"], pos, gate, x1, ys)


def _expert_kernel(be_ref, meta_ref, xs_ref, wgu_ref, bg_ref, bu_ref, wd_ref, bd_ref, perm_ref,
                   ys_ref, wg_s, wu_s, wd_s):
    i = pl.program_id(0)
    n_used = meta_ref[0]
    active = i < n_used
    new_expert = (i == 0) | (be_ref[i] != be_ref[jnp.maximum(i - 1, 0)])

    @pl.when(active & new_expert)
    def _():
        width = perm_ref.shape[0]
        for c in range(wgu_ref.shape[2] // width):
            wc = wgu_ref[0, :, c * width:(c + 1) * width].astype(BF16)
            r = jnp.dot(wc, perm_ref[...], preferred_element_type=F32)
            wg_s[:, c * LANES:(c + 1) * LANES] = r[:, :LANES].astype(BF16)
            wu_s[:, c * LANES:(c + 1) * LANES] = r[:, LANES:].astype(BF16)
        wd_s[...] = wd_ref[0].astype(BF16)

    @pl.when(active)
    def _():
        xb = xs_ref[...].astype(BF16)
        g = jnp.dot(xb, wg_s[...], preferred_element_type=F32) + bg_ref[0]
        up = jnp.dot(xb, wu_s[...], preferred_element_type=F32) + bu_ref[0]
        g = jnp.minimum(g, SWIGLU_LIMIT)
        up = jnp.clip(up, -SWIGLU_LIMIT, SWIGLU_LIMIT)
        act = g * (1.0 / (1.0 + jnp.exp(-SWIGLU_ALPHA * g))) * (up + 1.0)
        ys_ref[...] = jnp.dot(act.astype(BF16), wd_s[...], preferred_element_type=F32) + bd_ref[0]

    @pl.when(jnp.logical_not(active))
    def _():
        ys_ref[...] = jnp.zeros_like(ys_ref)


def _experts(blk_e, meta, xs, w_gate_up, b_gate_up, w_down, b_down):
    n_rows, d = xs.shape
    n_experts, _, f2 = w_gate_up.shape
    f = f2 // 2
    bm = EXPERT_ROWS
    n_blocks = n_rows // bm
    width = 2 * LANES
    perm = np.zeros((width, width), np.float32)
    perm[2 * np.arange(LANES), np.arange(LANES)] = 1.0
    perm[2 * np.arange(LANES) + 1, LANES + np.arange(LANES)] = 1.0
    bg = b_gate_up[:, 0::2].reshape(n_experts, 1, f)
    bu = b_gate_up[:, 1::2].reshape(n_experts, 1, f)
    rows = lambda i, be, meta: (jnp.minimum(i, meta[0] - 1), 0)
    per_e = lambda i, be, meta: (be[i], 0, 0)
    grid_spec = pltpu.PrefetchScalarGridSpec(
        num_scalar_prefetch=2,
        grid=(n_blocks,),
        in_specs=[
            pl.BlockSpec((bm, d), rows),
            pl.BlockSpec((1, d, f2), per_e),
            pl.BlockSpec((1, 1, f), per_e),
            pl.BlockSpec((1, 1, f), per_e),
            pl.BlockSpec((1, f, d), per_e),
            pl.BlockSpec((1, 1, d), per_e),
            pl.BlockSpec((width, width), lambda i, be, meta: (0, 0)),
        ],
        out_specs=pl.BlockSpec((bm, d), lambda i, be, meta: (i, 0)),
        scratch_shapes=[pltpu.VMEM((d, f), BF16), pltpu.VMEM((d, f), BF16),
                        pltpu.VMEM((f, d), BF16)],
    )
    return pl.pallas_call(
        _expert_kernel,
        grid_spec=grid_spec,
        out_shape=jax.ShapeDtypeStruct((n_rows, d), F32),
        compiler_params=_params(1, VMEM_LIMIT),
        name="experts",
    )(blk_e, meta, xs, w_gate_up, bg, bu, w_down, b_down.reshape(n_experts, 1, d),
      jnp.asarray(perm, BF16))


def _routing_plan(counts, n_experts, bm, n_blocks):
    cnt = counts[:, 0, :n_experts]
    run = (cnt + RUN_ALIGN - 1) // RUN_ALIGN * RUN_ALIGN
    per_expert = jnp.sum(run, axis=0)
    padded = (per_expert + bm - 1) // bm * bm
    pend = jnp.cumsum(padded)
    pstart = pend - padded
    base = pstart[None, :] + jnp.cumsum(run, axis=0) - run
    lst = jnp.cumsum(run, axis=1) - run
    n_used = pend[-1] // bm
    tail = jnp.concatenate([pstart + per_expert, padded - per_expert, n_used[None]])
    starts = jnp.arange(n_blocks, dtype=I32) * bm
    blk = jnp.sum((starts[:, None] >= pend[None, :]).astype(I32), axis=1)
    blk = jnp.minimum(blk, n_experts - 1)
    last = jnp.sum((((n_used - 1) * bm) >= pend).astype(I32))
    blk_e = jnp.where(jnp.arange(n_blocks) < n_used, blk, jnp.minimum(last, n_experts - 1))
    plan = {"cnt": run.reshape(-1).astype(I32), "lst": lst.reshape(-1).astype(I32),
            "base": base.reshape(-1).astype(I32), "rows": jnp.sum(run, axis=1).astype(I32),
            "tail": tail.astype(I32)}
    return plan, blk_e.astype(I32), n_used.astype(I32).reshape(1)


def _layer(x2d, batch, seq, norm1, w_in, q_norm, k_norm, sinks, rel_bias, w_fourier, g_fourier_out,
           g_attn_out, w_out, norm2, w_router, b_router, w_gate_up, b_gate_up, w_down, b_down):
    t, d = x2d.shape
    n_experts = w_router.shape[1]
    u, q, kv = _inproj(x2d, norm1, w_in, q_norm, k_norm)
    yf = _fourier(u, w_fourier, g_fourier_out, batch, seq)
    ya = _attention(q, kv, sinks, rel_bias, g_attn_out, batch, seq)
    x1, h2, gate, pos, post, counts = _outproj(yf, ya, x2d, w_out, norm2, w_router, b_router)
    bm = EXPERT_ROWS
    n_tiles = t // min(TOKEN_TILE, t)
    worst_rows = t * TOP_K + n_tiles * n_experts * (RUN_ALIGN - 1) + n_experts * (bm - RUN_ALIGN)
    n_blocks = -(-worst_rows // bm)
    plan, blk_e, meta = _routing_plan(counts, n_experts, bm, n_blocks)
    xs = _dispatch(plan, post, h2, n_blocks * bm, n_experts)
    ys = _experts(blk_e, meta, xs, w_gate_up, b_gate_up, w_down, b_down)
    return _combine(plan, pos, gate, x1, ys, n_experts)


def kernel(x, norm1, w_in, q_norm, k_norm, sinks, rel_bias, w_fourier, g_fourier_out, g_attn_out,
           w_out, norm2, w_router, b_router, w_gate_up, b_gate_up, w_down, b_down):
    b, s, d = x.shape
    x2d = x.reshape(b * s, d)
    for l in range(norm1.shape[0]):
        x2d = _layer(x2d, b, s, norm1[l], w_in[l], q_norm[l], k_norm[l], sinks[l], rel_bias,
                     w_fourier[l], g_fourier_out[l], g_attn_out[l], w_out[l], norm2[l],
                     w_router[l], b_router[l], w_gate_up[l], b_gate_up[l], w_down[l], b_down[l])
    return x2d.reshape(b, s, d)
```

```python
import functools
import math

import jax
import jax.numpy as jnp
import numpy as np
from jax import lax
from jax.experimental import pallas as pl
from jax.experimental.pallas import tpu as pltpu

F32 = jnp.float32
BF16 = jnp.bfloat16
I32 = jnp.int32
U32 = jnp.uint32

NORM_EPS = 1e-5
QK_EPS = 1e-6
HEAD_DIM = 64
N_Q_HEADS = 8
N_KV_HEADS = 2
FOURIER_GROUPS = 4
FOURIER_CH = 128
FOURIER_WIDTH = FOURIER_GROUPS * FOURIER_CH
ATTN_WIDTH = N_Q_HEADS * HEAD_DIM
KV_WIDTH = N_KV_HEADS * HEAD_DIM
WINDOW = 128
Q_BLOCK = 128
N_BUCKETS = 32
MAX_DISTANCE = 128
TOP_K = 4
SWIGLU_ALPHA = 1.702
SWIGLU_LIMIT = 7.0
MASK_VALUE = -1e30
LOG2E = math.log2(math.e)

LANES = 128
SUBLANES = 8
TOKEN_TILE = 512
RUN_ALIGN = SUBLANES
ROW_GROUPS = 2
PERM_CHUNK = 256
EXPERT_ROWS = 512
VMEM_LIMIT = 56 * 1024 * 1024


def _params(n_axes, vmem=None):
    return pltpu.CompilerParams(
        dimension_semantics=("arbitrary",) * n_axes, vmem_limit_bytes=vmem)


def _pair_head_norm(xc, gain, lo):
    x2 = xc * xc
    s_lo = jnp.sum(jnp.where(lo, x2, 0.0), axis=-1, keepdims=True)
    s_hi = jnp.sum(jnp.where(lo, 0.0, x2), axis=-1, keepdims=True)
    inv = jnp.where(lo, lax.rsqrt(s_lo * (1.0 / HEAD_DIM) + QK_EPS),
                    lax.rsqrt(s_hi * (1.0 / HEAD_DIM) + QK_EPS))
    return xc * inv * gain


def _inproj_kernel(x_ref, g1_ref, w_ref, qg_ref, kg_ref, u_ref, q_ref, kv_ref):
    rows = x_ref.shape[0] // ROW_GROUPS
    lo = lax.broadcasted_iota(I32, (rows, LANES), 1) < HEAD_DIM
    q0 = FOURIER_WIDTH
    k0 = q0 + ATTN_WIDTH
    for grp in range(ROW_GROUPS):
        rs = slice(grp * rows, (grp + 1) * rows)
        x = x_ref[rs, :]
        ms = jnp.mean(x * x, axis=-1, keepdims=True)
        h = (x * lax.rsqrt(ms + NORM_EPS) * g1_ref[...]).astype(BF16)
        z = jnp.dot(h, w_ref[...], preferred_element_type=F32)
        u_ref[rs, :] = z[:, :FOURIER_WIDTH].astype(BF16)
        for c in range(ATTN_WIDTH // LANES):
            qc = _pair_head_norm(z[:, q0 + c * LANES:q0 + (c + 1) * LANES], qg_ref[...], lo)
            q_ref[rs, c * LANES:(c + 1) * LANES] = (qc * (HEAD_DIM ** -0.5 * LOG2E)).astype(BF16)
        kc = _pair_head_norm(z[:, k0:k0 + KV_WIDTH], kg_ref[...], lo)
        vc = z[:, k0 + KV_WIDTH:k0 + 2 * KV_WIDTH]
        kv_ref[rs, 0:LANES] = kc.astype(BF16)
        kv_ref[rs, LANES:2 * LANES] = pltpu.roll(kc, HEAD_DIM, 1).astype(BF16)
        kv_ref[rs, 2 * LANES:3 * LANES] = vc.astype(BF16)
        kv_ref[rs, 3 * LANES:4 * LANES] = pltpu.roll(vc, HEAD_DIM, 1).astype(BF16)


def _inproj(x2d, norm1, w_in, q_norm, k_norm):
    t, d = x2d.shape
    tm = min(TOKEN_TILE, t)
    n_in = w_in.shape[1]
    qg = jnp.tile(q_norm, LANES // HEAD_DIM).reshape(1, LANES)
    kg = jnp.tile(k_norm, LANES // HEAD_DIM).reshape(1, LANES)
    full = lambda i: (0, 0)
    return pl.pallas_call(
        _inproj_kernel,
        grid=(t // tm,),
        in_specs=[
            pl.BlockSpec((tm, d), lambda i: (i, 0)),
            pl.BlockSpec((1, d), full),
            pl.BlockSpec((d, n_in), full),
            pl.BlockSpec((1, LANES), full),
            pl.BlockSpec((1, LANES), full),
        ],
        out_specs=[
            pl.BlockSpec((tm, FOURIER_WIDTH), lambda i: (i, 0)),
            pl.BlockSpec((tm, ATTN_WIDTH), lambda i: (i, 0)),
            pl.BlockSpec((tm, 4 * LANES), lambda i: (i, 0)),
        ],
        out_shape=[
            jax.ShapeDtypeStruct((t, FOURIER_WIDTH), BF16),
            jax.ShapeDtypeStruct((t, ATTN_WIDTH), BF16),
            jax.ShapeDtypeStruct((t, 4 * LANES), BF16),
        ],
        compiler_params=_params(1, VMEM_LIMIT),
        name="inproj",
    )(x2d, norm1.reshape(1, d), w_in.astype(BF16), qg, kg)


def _fourier_kernel(u_ref, cs_ref, ss_ref, cc_ref, sc_ref, wf_ref, g_ref, o_ref, p_scr, q_scr,
                    *, scale, row_block):
    for g in range(FOURIER_GROUPS):
        sl = slice(g * FOURIER_CH, (g + 1) * FOURIER_CH)
        w = wf_ref[g].astype(BF16)
        a = (jnp.dot(cc_ref[...], w, preferred_element_type=F32) * scale).astype(BF16)
        b = (jnp.dot(sc_ref[...], w, preferred_element_type=F32) * scale).astype(BF16)
        ug = u_ref[:, sl]
        p_scr[:, sl] = jnp.dot(ug, a, preferred_element_type=F32).astype(BF16)
        q_scr[:, sl] = jnp.dot(ug, b, preferred_element_type=F32).astype(BF16)
    s = u_ref.shape[0]
    for r in range(s // row_block):
        rs = slice(r * row_block, (r + 1) * row_block)
        y = (jnp.dot(cs_ref[rs, :], p_scr[...], preferred_element_type=F32)
             + jnp.dot(ss_ref[rs, :], q_scr[...], preferred_element_type=F32))
        ms = jnp.mean(y * y, axis=-1, keepdims=True)
        o_ref[rs, :] = (y * lax.rsqrt(ms + NORM_EPS) * g_ref[...]).astype(BF16)


def _dft_tables(n):
    k = np.arange(n, dtype=np.int64)
    ang = 2.0 * np.pi * ((k[:, None] * k[None, :]) % n).astype(np.float64) / n
    return np.cos(ang), np.sin(ang)


def _fourier(u, w_fourier, g_out, batch, seq):
    cs, ss = _dft_tables(seq)
    cc, sc = _dft_tables(FOURIER_CH)
    scale = 1.0 / math.sqrt(seq * FOURIER_CH)
    row_block = min(512, seq)
    full2 = lambda b: (0, 0)
    return pl.pallas_call(
        functools.partial(_fourier_kernel, scale=scale, row_block=row_block),
        grid=(batch,),
        in_specs=[
            pl.BlockSpec((seq, FOURIER_WIDTH), lambda b: (b, 0)),
            pl.BlockSpec((seq, seq), full2),
            pl.BlockSpec((seq, seq), full2),
            pl.BlockSpec((FOURIER_CH, FOURIER_CH), full2),
            pl.BlockSpec((FOURIER_CH, FOURIER_CH), full2),
            pl.BlockSpec((FOURIER_GROUPS, FOURIER_CH, FOURIER_CH), lambda b: (0, 0, 0)),
            pl.BlockSpec((1, FOURIER_WIDTH), full2),
        ],
        out_specs=pl.BlockSpec((seq, FOURIER_WIDTH), lambda b: (b, 0)),
        out_shape=jax.ShapeDtypeStruct((batch * seq, FOURIER_WIDTH), BF16),
        scratch_shapes=[pltpu.VMEM((seq, FOURIER_WIDTH), BF16),
                        pltpu.VMEM((seq, FOURIER_WIDTH), BF16)],
        compiler_params=_params(1, VMEM_LIMIT),
        name="fourier",
    )(u, jnp.asarray(cs, BF16), jnp.asarray(ss, BF16), jnp.asarray(cc, BF16),
      jnp.asarray(-sc, BF16), w_fourier, g_out.reshape(1, FOURIER_WIDTH))


def _attn_kernel(sink_ref, q_ref, kvp_ref, kvo_ref, kvn_ref, bias_a_ref, bias_b_ref, g_ref, o_ref,
                 acc_ref):
    kv = jnp.concatenate([kvp_ref[...], kvo_ref[...], kvn_ref[...]], axis=0)
    nk = kv.shape[0]
    lo = lax.broadcasted_iota(I32, (nk, LANES), 1) < HEAD_DIM
    k_a, k_b = kv[:, 0:LANES], kv[:, LANES:2 * LANES]
    v_a, v_b = kv[:, 2 * LANES:3 * LANES], kv[:, 3 * LANES:4 * LANES]
    zero = jnp.zeros_like(k_a)
    k_lo = (jnp.where(lo, k_a, zero), jnp.where(lo, k_b, zero))
    k_hi = (jnp.where(lo, zero, k_b), jnp.where(lo, zero, k_a))
    v_lo = (jnp.where(lo, v_a, zero), jnp.where(lo, v_b, zero))
    v_hi = (jnp.where(lo, zero, v_b), jnp.where(lo, zero, v_a))
    lo_out = lax.broadcasted_iota(I32, (Q_BLOCK, LANES), 1) < HEAD_DIM
    bias_refs = (bias_a_ref, bias_b_ref)
    rows2 = q_ref.shape[0]
    nt = (((1,), (1,)), ((), ()))
    for h in range(N_KV_HEADS):
        qs = jnp.concatenate([q_ref[:, (2 * h) * LANES:(2 * h + 1) * LANES],
                              q_ref[:, (2 * h + 1) * LANES:(2 * h + 2) * LANES]], axis=0)
        s_par = (lax.dot_general(qs, k_lo[h], nt, preferred_element_type=F32),
                 lax.dot_general(qs, k_hi[h], nt, preferred_element_type=F32))
        for sb in range(2):
            keys = slice(sb * Q_BLOCK, sb * Q_BLOCK + 3 * Q_BLOCK)
            vcat = jnp.concatenate([v_lo[h][keys, :], v_hi[h][keys, :]], axis=0)
            for c in range(2):
                r0 = c * rows2 + sb * Q_BLOCK
                probs, invs = [], []
                for par in range(2):
                    hq = 4 * h + 2 * c + par
                    s = s_par[par][r0:r0 + Q_BLOCK, keys] + bias_refs[sb][hq]
                    sink = sink_ref[hq]
                    m = jnp.maximum(jnp.max(s, axis=-1, keepdims=True), sink)
                    p = jnp.exp2(s - m)
                    denom = jnp.sum(p, axis=-1, keepdims=True) + jnp.exp2(sink - m)
                    probs.append(p.astype(BF16))
                    invs.append(1.0 / denom)
                pcat = jnp.concatenate(probs, axis=1)
                chunk = 2 * h + c
                o = jnp.dot(pcat, vcat, preferred_element_type=F32)
                acc_ref[sb * Q_BLOCK:(sb + 1) * Q_BLOCK, chunk * LANES:(chunk + 1) * LANES] = (
                    o * jnp.where(lo_out, invs[0], invs[1]))
    y = acc_ref[...]
    ms = jnp.mean(y * y, axis=-1, keepdims=True)
    o_ref[...] = (y * lax.rsqrt(ms + NORM_EPS) * g_ref[...]).astype(BF16)


def _t5_bucket(rel):
    nb = N_BUCKETS // 2
    max_exact = nb // 2
    ret = (rel > 0).astype(jnp.int32) * nb
    n = jnp.abs(rel)
    nf = jnp.maximum(n, 1).astype(jnp.float32)
    large = max_exact + (jnp.log(nf / max_exact) / math.log(MAX_DISTANCE / max_exact)
                         * (nb - max_exact)).astype(jnp.int32)
    large = jnp.minimum(large, nb - 1)
    return ret + jnp.where(n < max_exact, n, large)


def _attention(q, kv, sinks, rel_bias, g_out, batch, seq):
    nb = seq // Q_BLOCK
    assert nb % 2 == 0
    nb2 = nb // 2
    qi = jnp.arange(Q_BLOCK, dtype=jnp.int32)[:, None]
    kj = jnp.arange(3 * Q_BLOCK, dtype=jnp.int32)[None, :]
    rel = kj - Q_BLOCK - qi
    period = 4 * Q_BLOCK
    p = jnp.arange(period, dtype=jnp.int32)
    off = jnp.where(p < 3 * Q_BLOCK, p, p - period) - Q_BLOCK
    hit = _t5_bucket(off)[None, :, None] == jnp.arange(N_BUCKETS, dtype=jnp.int32)
    by_off = jnp.sum(jnp.where(hit, rel_bias.astype(F32).T[:, None, :], 0.0), axis=-1)
    bias = jnp.tile(by_off, (1, Q_BLOCK))[:, :Q_BLOCK * (period - 1)]
    bias = bias.reshape(N_Q_HEADS, Q_BLOCK, period - 1)[:, :, :3 * Q_BLOCK]
    band = jnp.abs(rel) <= WINDOW
    first = band & (kj >= Q_BLOCK)
    last = band & (kj < 2 * Q_BLOCK)
    table = jnp.stack([jnp.where(msk[None], bias * LOG2E, MASK_VALUE) for msk in (first, band, last)])
    q_rows = 2 * Q_BLOCK
    grid_spec = pltpu.PrefetchScalarGridSpec(
        num_scalar_prefetch=1,
        grid=(batch, nb2),
        in_specs=[
            pl.BlockSpec((q_rows, ATTN_WIDTH), lambda b, i, s: (b * nb2 + i, 0)),
            pl.BlockSpec((Q_BLOCK, 4 * LANES),
                         lambda b, i, s: (b * nb + jnp.maximum(2 * i - 1, 0), 0)),
            pl.BlockSpec((q_rows, 4 * LANES), lambda b, i, s: (b * nb2 + i, 0)),
            pl.BlockSpec((Q_BLOCK, 4 * LANES),
                         lambda b, i, s: (b * nb + jnp.minimum(2 * i + 2, nb - 1), 0)),
            pl.BlockSpec((None, N_Q_HEADS, Q_BLOCK, 3 * Q_BLOCK),
                         lambda b, i, s: (jnp.where(i == 0, 0, 1), 0, 0, 0)),
            pl.BlockSpec((None, N_Q_HEADS, Q_BLOCK, 3 * Q_BLOCK),
                         lambda b, i, s: (jnp.where(i == nb2 - 1, 2, 1), 0, 0, 0)),
            pl.BlockSpec((1, ATTN_WIDTH), lambda b, i, s: (0, 0)),
        ],
        out_specs=pl.BlockSpec((q_rows, ATTN_WIDTH), lambda b, i, s: (b * nb2 + i, 0)),
        scratch_shapes=[pltpu.VMEM((q_rows, ATTN_WIDTH), F32)],
    )
    return pl.pallas_call(
        _attn_kernel,
        grid_spec=grid_spec,
        out_shape=jax.ShapeDtypeStruct((batch * seq, ATTN_WIDTH), BF16),
        compiler_params=_params(2, VMEM_LIMIT),
        name="attention",
    )(sinks.astype(F32) * LOG2E, q, kv, kv, kv, table, table, g_out.reshape(1, ATTN_WIDTH))


def _outproj_kernel(yf_ref, ya_ref, x_ref, wo_ref, g2_ref, wr_ref, br_ref, tri_ref,
                    x1_ref, h2_ref, gate_ref, pos_ref, post_ref, cnt_ref, *, n_experts):
    half = yf_ref.shape[1]
    mix = (jnp.dot(yf_ref[...], wo_ref[:half, :], preferred_element_type=F32)
           + jnp.dot(ya_ref[...], wo_ref[half:, :], preferred_element_type=F32))
    x1 = x_ref[...] + mix
    x1_ref[...] = x1
    ms = jnp.mean(x1 * x1, axis=-1, keepdims=True)
    h2 = x1 * lax.rsqrt(ms + NORM_EPS) * g2_ref[...]
    h2_ref[...] = h2.astype(BF16)
    h_hi = h2.astype(BF16)
    h_lo = (h2 - h_hi.astype(F32)).astype(BF16)
    logits = (jnp.dot(h_hi, wr_ref[0], preferred_element_type=F32)
              + jnp.dot(h_hi, wr_ref[1], preferred_element_type=F32)
              + jnp.dot(h_lo, wr_ref[0], preferred_element_type=F32)) + br_ref[...]
    rows = logits.shape[0]
    lane_e = lax.broadcasted_iota(I32, (rows, n_experts), 1).astype(F32)
    work = logits
    vals, idxs = [], []
    for _ in range(TOP_K):
        m = jnp.max(work, axis=-1, keepdims=True)
        ik = jnp.min(jnp.where(work == m, lane_e, float(n_experts)), axis=-1, keepdims=True)
        work = jnp.where(lane_e == ik, -jnp.inf, work)
        vals.append(m)
        idxs.append(ik)
    exps = [jnp.exp(v - vals[0]) for v in vals]
    inv = 1.0 / (exps[0] + exps[1] + exps[2] + exps[3])
    lane_k = lax.broadcasted_iota(I32, (rows, TOP_K), 1)
    gate = jnp.zeros((rows, TOP_K), F32)
    for k in range(TOP_K):
        gate = jnp.where(lane_k == k, exps[k] * inv, gate)
    gate_ref[...] = gate

    lane = lax.broadcasted_iota(I32, (rows, LANES), 1).astype(F32)
    onehot = jnp.zeros((rows, LANES), F32)
    for k in range(TOP_K):
        onehot = onehot + jnp.where(lane == idxs[k] + float(k * n_experts), 1.0, 0.0)
    before = jnp.dot(tri_ref[...], onehot.astype(BF16), preferred_element_type=F32)
    colsum = jnp.sum(onehot, axis=0, keepdims=True)
    lane1 = lax.broadcasted_iota(I32, (1, LANES), 1)
    prefix = jnp.zeros((1, LANES), F32)
    total = colsum
    for k in range(1, TOP_K):
        rolled = pltpu.roll(colsum, k * n_experts, 1)
        prefix = prefix + jnp.where(lane1 >= k * n_experts, rolled, 0.0)
        total = total + rolled
    run = jnp.floor((total + (RUN_ALIGN - 1)) * (1.0 / RUN_ALIGN)) * RUN_ALIGN
    incl = run
    lane_in_seg = lane1 & (n_experts - 1)
    shift = 1
    while shift < n_experts:
        incl = incl + jnp.where(lane_in_seg >= shift, pltpu.roll(incl, shift, 1), 0.0)
        shift *= 2
    run_start = incl - run
    placed = (before + prefix + run_start) * onehot
    lane_i = lax.broadcasted_iota(I32, (rows, LANES), 1)
    pos_lanes = jnp.zeros((rows, LANES), F32)
    for k in range(TOP_K):
        seg = (lane_i >= k * n_experts) & (lane_i < (k + 1) * n_experts)
        pk = jnp.sum(jnp.where(seg, placed, 0.0), axis=-1, keepdims=True)
        pos_lanes = jnp.where(lane_i == k, pk, pos_lanes)
    pos_ref[...] = pos_lanes[:, :TOP_K].astype(I32)
    post_ref[...] = jnp.transpose(pos_lanes)[:SUBLANES, :].astype(I32)
    cnt_ref[0] = total.astype(I32)


def _outproj(yf, ya, x2d, w_out, norm2, w_router, b_router):
    t, d = x2d.shape
    tm = min(TOKEN_TILE, t)
    n_tiles = t // tm
    n_experts = w_router.shape[1]
    assert TOP_K * n_experts == LANES
    tri = np.tril(np.ones((tm, tm), np.float32), -1)
    wr_hi = w_router.astype(BF16)
    wr_lo = (w_router - wr_hi.astype(F32)).astype(BF16)
    full = lambda i: (0, 0)
    row = lambda i: (i, 0)
    return pl.pallas_call(
        functools.partial(_outproj_kernel, n_experts=n_experts),
        grid=(n_tiles,),
        in_specs=[
            pl.BlockSpec((tm, yf.shape[1]), row),
            pl.BlockSpec((tm, ya.shape[1]), row),
            pl.BlockSpec((tm, d), row),
            pl.BlockSpec((w_out.shape[0], d), full),
            pl.BlockSpec((1, d), full),
            pl.BlockSpec((2, d, n_experts), lambda i: (0, 0, 0)),
            pl.BlockSpec((1, n_experts), full),
            pl.BlockSpec((tm, tm), full),
        ],
        out_specs=[
            pl.BlockSpec((tm, d), row),
            pl.BlockSpec((tm, d), row),
            pl.BlockSpec((tm, TOP_K), row),
            pl.BlockSpec((tm, TOP_K), row),
            pl.BlockSpec((SUBLANES, tm), row),
            pl.BlockSpec((1, 1, LANES), lambda i: (i, 0, 0)),
        ],
        out_shape=[
            jax.ShapeDtypeStruct((t, d), F32),
            jax.ShapeDtypeStruct((t, d), BF16),
            jax.ShapeDtypeStruct((t, TOP_K), F32),
            jax.ShapeDtypeStruct((t, TOP_K), I32),
            jax.ShapeDtypeStruct((n_tiles * SUBLANES, tm), I32),
            jax.ShapeDtypeStruct((n_tiles, 1, LANES), I32),
        ],
        compiler_params=_params(1, VMEM_LIMIT),
        name="outproj_router",
    )(yf, ya, x2d, w_out.astype(BF16), norm2.reshape(1, d), jnp.stack([wr_hi, wr_lo]),
      b_router.reshape(1, n_experts), jnp.asarray(tri, BF16))


def _pack_pairs(x):
    half = x.shape[1] // 2
    hi = lax.bitcast_convert_type(x[:, :half].astype(BF16).astype(F32), U32)
    lo = lax.bitcast_convert_type(x[:, half:].astype(BF16).astype(F32), U32)
    return hi | (lo >> 16)


def _unpack_pairs(w):
    hi = lax.bitcast_convert_type(w & U32(0xFFFF0000), F32)
    lo = lax.bitcast_convert_type(w << 16, F32)
    return hi.astype(BF16), lo.astype(BF16)


def _rows(start, size):
    if not isinstance(size, int):
        size = pl.multiple_of(size, RUN_ALIGN)
    return pl.ds(pl.multiple_of(start, RUN_ALIGN), size)


def _dispatch_kernel(cnt_ref, lst_ref, base_ref, rows_ref, tail_ref, post_ref, h2_ref, xs_ref,
                     buf, zbuf, sem, zsem, *, n_experts):
    j = pl.program_id(0)
    tm = h2_ref.shape[0]
    n_local = buf.shape[1]

    def start_runs(tile, slot):
        def run(e, carry):
            r = tile * n_experts + e
            n = cnt_ref[r]

            @pl.when(n > 0)
            def _():
                pltpu.make_async_copy(buf.at[slot, _rows(lst_ref[r], n), :],
                                      xs_ref.at[_rows(base_ref[r], n), :], sem.at[slot]).start()
            return carry
        lax.fori_loop(0, n_experts, run, 0)

    def wait_runs(tile, slot):
        n = rows_ref[tile]

        @pl.when(n > 0)
        def _():
            pltpu.make_async_copy(buf.at[slot, _rows(0, n), :], xs_ref.at[_rows(0, n), :],
                                  sem.at[slot]).wait()

    def zero_fill(op):
        def tail(e, carry):
            n = tail_ref[n_experts + e]

            @pl.when(n > 0)
            def _():
                getattr(pltpu.make_async_copy(zbuf.at[_rows(0, n), :],
                                              xs_ref.at[_rows(tail_ref[e], n), :], zsem), op)()
            return carry
        lax.fori_loop(0, n_experts, tail, 0)

        def spare(b, carry):
            getattr(pltpu.make_async_copy(zbuf, xs_ref.at[_rows(b * EXPERT_ROWS, EXPERT_ROWS), :],
                                          zsem), op)()
            return carry
        lax.fori_loop(tail_ref[2 * n_experts], xs_ref.shape[0] // EXPERT_ROWS, spare, 0)

    slot = j % 2
    @pl.when(j >= 2)
    def _():
        wait_runs(j - 2, slot)

    @pl.when(j == 0)
    def _():
        zbuf[...] = jnp.zeros_like(zbuf)
        zero_fill("start")
        zero_fill("wait")

    h = h2_ref[...]
    for rc in range(n_local // PERM_CHUNK):
        rows = lax.broadcasted_iota(I32, (PERM_CHUNK, tm), 0) + rc * PERM_CHUNK
        perm = jnp.zeros((PERM_CHUNK, tm), F32)
        for k in range(TOP_K):
            perm = jnp.where(rows == post_ref[k:k + 1, :], 1.0, perm)
        buf[slot, rc * PERM_CHUNK:(rc + 1) * PERM_CHUNK, :] = _pack_pairs(jnp.dot(
            perm.astype(BF16), h, preferred_element_type=F32))

    start_runs(j, slot)

    @pl.when(j == pl.num_programs(0) - 1)
    def _():
        @pl.when(j >= 1)
        def _():
            wait_runs(j - 1, 1 - slot)
        wait_runs(j, slot)


def _local_rows(tm, n_experts):
    worst = TOP_K * tm + n_experts * (RUN_ALIGN - 1)
    return -(-worst // PERM_CHUNK) * PERM_CHUNK


def _dispatch(plan, post, h2, n_rows, n_experts):
    t, d = h2.shape
    tm = min(TOKEN_TILE, t)
    grid_spec = pltpu.PrefetchScalarGridSpec(
        num_scalar_prefetch=5,
        grid=(t // tm,),
        in_specs=[
            pl.BlockSpec((SUBLANES, tm), lambda i, *_: (i, 0)),
            pl.BlockSpec((tm, d), lambda i, *_: (i, 0)),
        ],
        out_specs=pl.BlockSpec(memory_space=pl.ANY),
        scratch_shapes=[pltpu.VMEM((2, _local_rows(tm, n_experts), d // 2), U32),
                        pltpu.VMEM((EXPERT_ROWS, d // 2), U32),
                        pltpu.SemaphoreType.DMA((2,)), pltpu.SemaphoreType.DMA(())],
    )
    return pl.pallas_call(
        functools.partial(_dispatch_kernel, n_experts=n_experts),
        grid_spec=grid_spec,
        out_shape=jax.ShapeDtypeStruct((n_rows, d // 2), U32),
        compiler_params=_params(1, VMEM_LIMIT),
        name="dispatch",
    )(plan["cnt"], plan["lst"], plan["base"], plan["rows"], plan["tail"], post, h2)


def _combine_kernel(cnt_ref, lst_ref, base_ref, rows_ref, pos_ref, gate_ref, x1_ref, ys_ref, o_ref,
                    buf, g_scr, y_scr, sem, *, n_experts):
    j = pl.program_id(0)
    tm, d = x1_ref.shape
    n_local = buf.shape[1]

    def start_runs(tile, slot):
        def run(e, carry):
            r = tile * n_experts + e
            n = cnt_ref[r]

            @pl.when(n > 0)
            def _():
                pltpu.make_async_copy(ys_ref.at[_rows(base_ref[r], n), :],
                                      buf.at[slot, _rows(lst_ref[r], n), :], sem.at[slot]).start()
            return carry
        lax.fori_loop(0, n_experts, run, 0)

    def wait_runs(tile, slot):
        n = rows_ref[tile]

        @pl.when(n > 0)
        def _():
            pltpu.make_async_copy(ys_ref.at[_rows(0, n), :], buf.at[slot, _rows(0, n), :],
                                  sem.at[slot]).wait()

    slot = j % 2
    @pl.when(j == 0)
    def _():
        buf[...] = jnp.zeros_like(buf)
        start_runs(j, slot)

    @pl.when(j + 1 < pl.num_programs(0))
    def _():
        start_runs(j + 1, 1 - slot)

    wait_runs(j, slot)

    pos = pos_ref[...]
    gate = gate_ref[...]
    for rc in range(n_local // PERM_CHUNK):
        chunk = slice(rc * PERM_CHUNK, (rc + 1) * PERM_CHUNK)
        cols = lax.broadcasted_iota(I32, (tm, PERM_CHUNK), 1) + rc * PERM_CHUNK
        g = jnp.zeros((tm, PERM_CHUNK), F32)
        for k in range(TOP_K):
            g = jnp.where(cols == pos[:, k:k + 1], gate[:, k:k + 1], g)
        g_scr[:, chunk] = g.astype(BF16)
        y_scr[chunk, :d // 2], y_scr[chunk, d // 2:] = _unpack_pairs(buf[slot, chunk, :])
    o_ref[...] = x1_ref[...] + jnp.dot(g_scr[...], y_scr[...], preferred_element_type=F32)


def _combine(plan, pos, gate, x1, ys, n_experts):
    t, d = x1.shape
    tm = min(TOKEN_TILE, t)
    grid_spec = pltpu.PrefetchScalarGridSpec(
        num_scalar_prefetch=4,
        grid=(t // tm,),
        in_specs=[
            pl.BlockSpec((tm, TOP_K), lambda i, *_: (i, 0)),
            pl.BlockSpec((tm, TOP_K), lambda i, *_: (i, 0)),
            pl.BlockSpec((tm, d), lambda i, *_: (i, 0)),
            pl.BlockSpec(memory_space=pl.ANY),
        ],
        out_specs=pl.BlockSpec((tm, d), lambda i, *_: (i, 0)),
        scratch_shapes=[pltpu.VMEM((2, _local_rows(tm, n_experts), d // 2), U32),
                        pltpu.VMEM((tm, _local_rows(tm, n_experts)), BF16),
                        pltpu.VMEM((_local_rows(tm, n_experts), d), BF16),
                        pltpu.SemaphoreType.DMA((2,))],
    )
    return pl.pallas_call(
        functools.partial(_combine_kernel, n_experts=n_experts),
        grid_spec=grid_spec,
        out_shape=jax.ShapeDtypeStruct((t, d), F32),
        compiler_params=_params(1, VMEM_LIMIT),
        name="combine",
    )(plan["cnt"], plan["lst"], plan["base"], plan["rows"], pos, gate, x1, ys)


def _expert_kernel(be_ref, meta_ref, xs_ref, wgu_ref, bg_ref, bu_ref, wd_ref, bd_ref, perm_ref,
                   ys_ref, wg_s, wu_s, wd_s):
    i = pl.program_id(0)
    n_used = meta_ref[0]
    active = i < n_used
    new_expert = (i == 0) | (be_ref[i] != be_ref[jnp.maximum(i - 1, 0)])

    @pl.when(active & new_expert)
    def _():
        width = perm_ref.shape[0]
        for c in range(wgu_ref.shape[2] // width):
            wc = wgu_ref[0, :, c * width:(c + 1) * width].astype(BF16)
            r = jnp.dot(wc, perm_ref[...], preferred_element_type=F32)
            wg_s[:, c * LANES:(c + 1) * LANES] = r[:, :LANES].astype(BF16)
            wu_s[:, c * LANES:(c + 1) * LANES] = r[:, LANES:].astype(BF16)
        wd_s[...] = wd_ref[0].astype(BF16)

    @pl.when(active)
    def _():
        xb = jnp.concatenate(_unpack_pairs(xs_ref[...]), axis=1)
        g = jnp.dot(xb, wg_s[...], preferred_element_type=F32) + bg_ref[0]
        up = jnp.dot(xb, wu_s[...], preferred_element_type=F32) + bu_ref[0]
        g = jnp.minimum(g, SWIGLU_LIMIT)
        up = jnp.clip(up, -SWIGLU_LIMIT, SWIGLU_LIMIT)
        act = g * (1.0 / (1.0 + jnp.exp(-SWIGLU_ALPHA * g))) * (up + 1.0)
        ys_ref[...] = _pack_pairs(
            jnp.dot(act.astype(BF16), wd_s[...], preferred_element_type=F32) + bd_ref[0])

    @pl.when(jnp.logical_not(active))
    def _():
        ys_ref[...] = jnp.zeros_like(ys_ref)


def _experts(blk_e, meta, xs, w_gate_up, b_gate_up, w_down, b_down):
    n_rows = xs.shape[0]
    n_experts, d, f2 = w_gate_up.shape
    f = f2 // 2
    bm = EXPERT_ROWS
    n_blocks = n_rows // bm
    width = 2 * LANES
    perm = np.zeros((width, width), np.float32)
    perm[2 * np.arange(LANES), np.arange(LANES)] = 1.0
    perm[2 * np.arange(LANES) + 1, LANES + np.arange(LANES)] = 1.0
    bg = b_gate_up[:, 0::2].reshape(n_experts, 1, f)
    bu = b_gate_up[:, 1::2].reshape(n_experts, 1, f)
    rows = lambda i, be, meta: (jnp.minimum(i, meta[0] - 1), 0)
    per_e = lambda i, be, meta: (be[i], 0, 0)
    grid_spec = pltpu.PrefetchScalarGridSpec(
        num_scalar_prefetch=2,
        grid=(n_blocks,),
        in_specs=[
            pl.BlockSpec((bm, d // 2), rows),
            pl.BlockSpec((1, d, f2), per_e),
            pl.BlockSpec((1, 1, f), per_e),
            pl.BlockSpec((1, 1, f), per_e),
            pl.BlockSpec((1, f, d), per_e),
            pl.BlockSpec((1, 1, d), per_e),
            pl.BlockSpec((width, width), lambda i, be, meta: (0, 0)),
        ],
        out_specs=pl.BlockSpec((bm, d // 2), lambda i, be, meta: (i, 0)),
        scratch_shapes=[pltpu.VMEM((d, f), BF16), pltpu.VMEM((d, f), BF16),
                        pltpu.VMEM((f, d), BF16)],
    )
    return pl.pallas_call(
        _expert_kernel,
        grid_spec=grid_spec,
        out_shape=jax.ShapeDtypeStruct((n_rows, d // 2), U32),
        compiler_params=_params(1, VMEM_LIMIT),
        name="experts",
    )(blk_e, meta, xs, w_gate_up, bg, bu, w_down, b_down.reshape(n_experts, 1, d),
      jnp.asarray(perm, BF16))


def _routing_plan(counts, n_experts, bm, n_blocks):
    cnt = counts[:, 0, :n_experts]
    run = (cnt + RUN_ALIGN - 1) // RUN_ALIGN * RUN_ALIGN
    per_expert = jnp.sum(run, axis=0)
    padded = (per_expert + bm - 1) // bm * bm
    pend = jnp.cumsum(padded)
    pstart = pend - padded
    base = pstart[None, :] + jnp.cumsum(run, axis=0) - run
    lst = jnp.cumsum(run, axis=1) - run
    n_used = pend[-1] // bm
    tail = jnp.concatenate([pstart + per_expert, padded - per_expert, n_used[None]])
    starts = jnp.arange(n_blocks, dtype=I32) * bm
    blk = jnp.sum((starts[:, None] >= pend[None, :]).astype(I32), axis=1)
    blk = jnp.minimum(blk, n_experts - 1)
    last = jnp.sum((((n_used - 1) * bm) >= pend).astype(I32))
    blk_e = jnp.where(jnp.arange(n_blocks) < n_used, blk, jnp.minimum(last, n_experts - 1))
    plan = {"cnt": run.reshape(-1).astype(I32), "lst": lst.reshape(-1).astype(I32),
            "base": base.reshape(-1).astype(I32), "rows": jnp.sum(run, axis=1).astype(I32),
            "tail": tail.astype(I32)}
    return plan, blk_e.astype(I32), n_used.astype(I32).reshape(1)


def _layer(x2d, batch, seq, norm1, w_in, q_norm, k_norm, sinks, rel_bias, w_fourier, g_fourier_out,
           g_attn_out, w_out, norm2, w_router, b_router, w_gate_up, b_gate_up, w_down, b_down):
    t, d = x2d.shape
    n_experts = w_router.shape[1]
    u, q, kv = _inproj(x2d, norm1, w_in, q_norm, k_norm)
    yf = _fourier(u, w_fourier, g_fourier_out, batch, seq)
    ya = _attention(q, kv, sinks, rel_bias, g_attn_out, batch, seq)
    x1, h2, gate, pos, post, counts = _outproj(yf, ya, x2d, w_out, norm2, w_router, b_router)
    bm = EXPERT_ROWS
    n_tiles = t // min(TOKEN_TILE, t)
    worst_rows = t * TOP_K + n_tiles * n_experts * (RUN_ALIGN - 1) + n_experts * (bm - RUN_ALIGN)
    n_blocks = -(-worst_rows // bm)
    plan, blk_e, meta = _routing_plan(counts, n_experts, bm, n_blocks)
    xs = _dispatch(plan, post, h2, n_blocks * bm, n_experts)
    ys = _experts(blk_e, meta, xs, w_gate_up, b_gate_up, w_down, b_down)
    return _combine(plan, pos, gate, x1, ys, n_experts)


def kernel(x, norm1, w_in, q_norm, k_norm, sinks, rel_bias, w_fourier, g_fourier_out, g_attn_out,
           w_out, norm2, w_router, b_router, w_gate_up, b_gate_up, w_down, b_down):
    b, s, d = x.shape
    x2d = x.reshape(b * s, d)
    for l in range(norm1.shape[0]):
        x2d = _layer(x2d, b, s, norm1[l], w_in[l], q_norm[l], k_norm[l], sinks[l], rel_bias,
                     w_fourier[l], g_fourier_out[l], g_attn_out[l], w_out[l], norm2[l],
                     w_router[l], b_router[l], w_gate_up[l], b_gate_up[l], w_down[l], b_down[l])
    return x2d.reshape(b, s, d)
```

```python
import functools
import math

import jax
import jax.numpy as jnp
import numpy as np
from jax import lax
from jax.experimental import pallas as pl
from jax.experimental.pallas import tpu as pltpu

F32 = jnp.float32
BF16 = jnp.bfloat16
I32 = jnp.int32
U32 = jnp.uint32

NORM_EPS = 1e-5
QK_EPS = 1e-6
HEAD_DIM = 64
N_Q_HEADS = 8
N_KV_HEADS = 2
FOURIER_GROUPS = 4
FOURIER_CH = 128
FOURIER_WIDTH = FOURIER_GROUPS * FOURIER_CH
ATTN_WIDTH = N_Q_HEADS * HEAD_DIM
KV_WIDTH = N_KV_HEADS * HEAD_DIM
WINDOW = 128
Q_BLOCK = 128
N_BUCKETS = 32
MAX_DISTANCE = 128
TOP_K = 4
SWIGLU_ALPHA = 1.702
SWIGLU_LIMIT = 7.0
MASK_VALUE = -1e30
LOG2E = math.log2(math.e)

LANES = 128
SUBLANES = 8
TOKEN_TILE = 512
RUN_ALIGN = SUBLANES
ROW_GROUPS = 2
PERM_CHUNK = 256
DISPATCH_ROWS = 3 * PERM_CHUNK
EXPERT_ROWS = 512
VMEM_LIMIT = 56 * 1024 * 1024


def _params(n_axes, vmem=None):
    return pltpu.CompilerParams(
        dimension_semantics=("arbitrary",) * n_axes, vmem_limit_bytes=vmem)


def _pair_head_norm(xc, gain, lo):
    x2 = xc * xc
    s_lo = jnp.sum(jnp.where(lo, x2, 0.0), axis=-1, keepdims=True)
    s_hi = jnp.sum(jnp.where(lo, 0.0, x2), axis=-1, keepdims=True)
    inv = jnp.where(lo, lax.rsqrt(s_lo * (1.0 / HEAD_DIM) + QK_EPS),
                    lax.rsqrt(s_hi * (1.0 / HEAD_DIM) + QK_EPS))
    return xc * inv * gain


def _inproj_kernel(x_ref, g1_ref, w_ref, qg_ref, kg_ref, u_ref, q_ref, kv_ref):
    rows = x_ref.shape[0] // ROW_GROUPS
    lo = lax.broadcasted_iota(I32, (rows, LANES), 1) < HEAD_DIM
    q0 = FOURIER_WIDTH
    k0 = q0 + ATTN_WIDTH
    for grp in range(ROW_GROUPS):
        rs = slice(grp * rows, (grp + 1) * rows)
        x = x_ref[rs, :]
        ms = jnp.mean(x * x, axis=-1, keepdims=True)
        h = (x * lax.rsqrt(ms + NORM_EPS) * g1_ref[...]).astype(BF16)
        z = jnp.dot(h, w_ref[...], preferred_element_type=F32)
        u_ref[rs, :] = z[:, :FOURIER_WIDTH].astype(BF16)
        for c in range(ATTN_WIDTH // LANES):
            qc = _pair_head_norm(z[:, q0 + c * LANES:q0 + (c + 1) * LANES], qg_ref[...], lo)
            q_ref[rs, c * LANES:(c + 1) * LANES] = (qc * (HEAD_DIM ** -0.5 * LOG2E)).astype(BF16)
        kc = _pair_head_norm(z[:, k0:k0 + KV_WIDTH], kg_ref[...], lo)
        vc = z[:, k0 + KV_WIDTH:k0 + 2 * KV_WIDTH]
        kv_ref[rs, 0:LANES] = kc.astype(BF16)
        kv_ref[rs, LANES:2 * LANES] = pltpu.roll(kc, HEAD_DIM, 1).astype(BF16)
        kv_ref[rs, 2 * LANES:3 * LANES] = vc.astype(BF16)
        kv_ref[rs, 3 * LANES:4 * LANES] = pltpu.roll(vc, HEAD_DIM, 1).astype(BF16)


def _inproj(x2d, norm1, w_in, q_norm, k_norm):
    t, d = x2d.shape
    tm = min(TOKEN_TILE, t)
    n_in = w_in.shape[1]
    qg = jnp.tile(q_norm, LANES // HEAD_DIM).reshape(1, LANES)
    kg = jnp.tile(k_norm, LANES // HEAD_DIM).reshape(1, LANES)
    full = lambda i: (0, 0)
    return pl.pallas_call(
        _inproj_kernel,
        grid=(t // tm,),
        in_specs=[
            pl.BlockSpec((tm, d), lambda i: (i, 0)),
            pl.BlockSpec((1, d), full),
            pl.BlockSpec((d, n_in), full),
            pl.BlockSpec((1, LANES), full),
            pl.BlockSpec((1, LANES), full),
        ],
        out_specs=[
            pl.BlockSpec((tm, FOURIER_WIDTH), lambda i: (i, 0)),
            pl.BlockSpec((tm, ATTN_WIDTH), lambda i: (i, 0)),
            pl.BlockSpec((tm, 4 * LANES), lambda i: (i, 0)),
        ],
        out_shape=[
            jax.ShapeDtypeStruct((t, FOURIER_WIDTH), BF16),
            jax.ShapeDtypeStruct((t, ATTN_WIDTH), BF16),
            jax.ShapeDtypeStruct((t, 4 * LANES), BF16),
        ],
        compiler_params=_params(1, VMEM_LIMIT),
        name="inproj",
    )(x2d, norm1.reshape(1, d), w_in.astype(BF16), qg, kg)


def _fourier_kernel(u_ref, cs_ref, ss_ref, rev_ref, cc_ref, sc_ref, wf_ref, g_ref, o_ref,
                    p_scr, q_scr, e_scr, *, scale, row_block):
    for g in range(FOURIER_GROUPS):
        sl = slice(g * FOURIER_CH, (g + 1) * FOURIER_CH)
        w = wf_ref[g].astype(BF16)
        a = (jnp.dot(cc_ref[...], w, preferred_element_type=F32) * scale).astype(BF16)
        b = (jnp.dot(sc_ref[...], w, preferred_element_type=F32) * scale).astype(BF16)
        ug = u_ref[:, sl]
        p_scr[:, sl] = jnp.dot(ug, a, preferred_element_type=F32).astype(BF16)
        q_scr[:, sl] = jnp.dot(ug, b, preferred_element_type=F32).astype(BF16)
    half = u_ref.shape[0] // 2
    gain = g_ref[...]

    def norm(y):
        ms = jnp.mean(y * y, axis=-1, keepdims=True)
        return y * lax.rsqrt(ms + NORM_EPS) * gain

    n_blk = half // row_block
    mid = None
    for r in range(n_blk):
        rs = slice(r * row_block, (r + 1) * row_block)
        extra = SUBLANES if r == n_blk - 1 else 0
        c = jnp.dot(cs_ref[r * row_block:(r + 1) * row_block + extra, :], p_scr[...],
                    preferred_element_type=F32)
        d = jnp.dot(ss_ref[rs, :], q_scr[...], preferred_element_type=F32)
        o_ref[rs, :] = norm(c[:row_block] + d).astype(BF16)
        e_scr[rs, :] = norm(c[:row_block] - d).astype(BF16)
        if extra:
            mid = norm(c[row_block:row_block + 1])
    for r in range(n_blk):
        z = jnp.dot(rev_ref[r * row_block:(r + 1) * row_block, :], e_scr[...],
                    preferred_element_type=F32)
        if r == 0:
            z = jnp.where(lax.broadcasted_iota(I32, z.shape, 0) == 0, mid, z)
        o_ref[half + r * row_block:half + (r + 1) * row_block, :] = z.astype(BF16)


def _dft_tables(n):
    k = np.arange(n, dtype=np.int64)
    ang = 2.0 * np.pi * ((k[:, None] * k[None, :]) % n).astype(np.float64) / n
    return np.cos(ang), np.sin(ang)


def _fourier(u, w_fourier, g_out, batch, seq):
    cs, ss = _dft_tables(seq)
    cc, sc = _dft_tables(FOURIER_CH)
    scale = 1.0 / math.sqrt(seq * FOURIER_CH)
    half = seq // 2
    row_block = min(512, half)
    rev = np.zeros((half, half), np.float32)
    rev[np.arange(1, half), half - np.arange(1, half)] = 1.0
    full2 = lambda b: (0, 0)
    return pl.pallas_call(
        functools.partial(_fourier_kernel, scale=scale, row_block=row_block),
        grid=(batch,),
        in_specs=[
            pl.BlockSpec((seq, FOURIER_WIDTH), lambda b: (b, 0)),
            pl.BlockSpec((half + SUBLANES, seq), full2),
            pl.BlockSpec((half, seq), full2),
            pl.BlockSpec((half, half), full2),
            pl.BlockSpec((FOURIER_CH, FOURIER_CH), full2),
            pl.BlockSpec((FOURIER_CH, FOURIER_CH), full2),
            pl.BlockSpec((FOURIER_GROUPS, FOURIER_CH, FOURIER_CH), lambda b: (0, 0, 0)),
            pl.BlockSpec((1, FOURIER_WIDTH), full2),
        ],
        out_specs=pl.BlockSpec((seq, FOURIER_WIDTH), lambda b: (b, 0)),
        out_shape=jax.ShapeDtypeStruct((batch * seq, FOURIER_WIDTH), BF16),
        scratch_shapes=[pltpu.VMEM((seq, FOURIER_WIDTH), BF16),
                        pltpu.VMEM((seq, FOURIER_WIDTH), BF16),
                        pltpu.VMEM((half, FOURIER_WIDTH), BF16)],
        compiler_params=_params(1, VMEM_LIMIT),
        name="fourier",
    )(u, jnp.asarray(cs[:half + SUBLANES], BF16), jnp.asarray(ss[:half], BF16),
      jnp.asarray(rev, BF16), jnp.asarray(cc, BF16), jnp.asarray(-sc, BF16), w_fourier,
      g_out.reshape(1, FOURIER_WIDTH))


def _attn_kernel(sink_ref, q_ref, kvp_ref, kvo_ref, kvn_ref, bias_a_ref, bias_b_ref, g_ref, o_ref,
                 acc_ref):
    kv = jnp.concatenate([kvp_ref[...], kvo_ref[...], kvn_ref[...]], axis=0)
    nk = kv.shape[0]
    lo = lax.broadcasted_iota(I32, (nk, LANES), 1) < HEAD_DIM
    k_a, k_b = kv[:, 0:LANES], kv[:, LANES:2 * LANES]
    v_a, v_b = kv[:, 2 * LANES:3 * LANES], kv[:, 3 * LANES:4 * LANES]
    zero = jnp.zeros_like(k_a)
    k_lo = (jnp.where(lo, k_a, zero), jnp.where(lo, k_b, zero))
    k_hi = (jnp.where(lo, zero, k_b), jnp.where(lo, zero, k_a))
    v_lo = (jnp.where(lo, v_a, zero), jnp.where(lo, v_b, zero))
    v_hi = (jnp.where(lo, zero, v_b), jnp.where(lo, zero, v_a))
    lo_out = lax.broadcasted_iota(I32, (Q_BLOCK, LANES), 1) < HEAD_DIM
    bias_refs = (bias_a_ref, bias_b_ref)
    rows2 = q_ref.shape[0]
    nt = (((1,), (1,)), ((), ()))
    for h in range(N_KV_HEADS):
        qs = jnp.concatenate([q_ref[:, (2 * h) * LANES:(2 * h + 1) * LANES],
                              q_ref[:, (2 * h + 1) * LANES:(2 * h + 2) * LANES]], axis=0)
        s_par = (lax.dot_general(qs, k_lo[h], nt, preferred_element_type=F32),
                 lax.dot_general(qs, k_hi[h], nt, preferred_element_type=F32))
        for sb in range(2):
            keys = slice(sb * Q_BLOCK, sb * Q_BLOCK + 3 * Q_BLOCK)
            vcat = jnp.concatenate([v_lo[h][keys, :], v_hi[h][keys, :]], axis=0)
            for c in range(2):
                r0 = c * rows2 + sb * Q_BLOCK
                probs, invs = [], []
                for par in range(2):
                    hq = 4 * h + 2 * c + par
                    s = s_par[par][r0:r0 + Q_BLOCK, keys] + bias_refs[sb][hq]
                    sink = sink_ref[hq]
                    m = jnp.maximum(jnp.max(s, axis=-1, keepdims=True), sink)
                    p = jnp.exp2(s - m)
                    denom = jnp.sum(p, axis=-1, keepdims=True) + jnp.exp2(sink - m)
                    probs.append(p.astype(BF16))
                    invs.append(1.0 / denom)
                pcat = jnp.concatenate(probs, axis=1)
                chunk = 2 * h + c
                o = jnp.dot(pcat, vcat, preferred_element_type=F32)
                acc_ref[sb * Q_BLOCK:(sb + 1) * Q_BLOCK, chunk * LANES:(chunk + 1) * LANES] = (
                    o * jnp.where(lo_out, invs[0], invs[1]))
    y = acc_ref[...]
    ms = jnp.mean(y * y, axis=-1, keepdims=True)
    o_ref[...] = (y * lax.rsqrt(ms + NORM_EPS) * g_ref[...]).astype(BF16)


def _t5_bucket(rel):
    nb = N_BUCKETS // 2
    max_exact = nb // 2
    ret = (rel > 0).astype(jnp.int32) * nb
    n = jnp.abs(rel)
    nf = jnp.maximum(n, 1).astype(jnp.float32)
    large = max_exact + (jnp.log(nf / max_exact) / math.log(MAX_DISTANCE / max_exact)
                         * (nb - max_exact)).astype(jnp.int32)
    large = jnp.minimum(large, nb - 1)
    return ret + jnp.where(n < max_exact, n, large)


def _attention(q, kv, sinks, rel_bias, g_out, batch, seq):
    nb = seq // Q_BLOCK
    assert nb % 2 == 0
    nb2 = nb // 2
    qi = jnp.arange(Q_BLOCK, dtype=jnp.int32)[:, None]
    kj = jnp.arange(3 * Q_BLOCK, dtype=jnp.int32)[None, :]
    rel = kj - Q_BLOCK - qi
    period = 4 * Q_BLOCK
    p = jnp.arange(period, dtype=jnp.int32)
    off = jnp.where(p < 3 * Q_BLOCK, p, p - period) - Q_BLOCK
    hit = _t5_bucket(off)[None, :, None] == jnp.arange(N_BUCKETS, dtype=jnp.int32)
    by_off = jnp.sum(jnp.where(hit, rel_bias.astype(F32).T[:, None, :], 0.0), axis=-1)
    bias = jnp.tile(by_off, (1, Q_BLOCK))[:, :Q_BLOCK * (period - 1)]
    bias = bias.reshape(N_Q_HEADS, Q_BLOCK, period - 1)[:, :, :3 * Q_BLOCK]
    band = jnp.abs(rel) <= WINDOW
    first = band & (kj >= Q_BLOCK)
    last = band & (kj < 2 * Q_BLOCK)
    table = jnp.stack([jnp.where(msk[None], bias * LOG2E, MASK_VALUE) for msk in (first, band, last)])
    q_rows = 2 * Q_BLOCK
    grid_spec = pltpu.PrefetchScalarGridSpec(
        num_scalar_prefetch=1,
        grid=(batch, nb2),
        in_specs=[
            pl.BlockSpec((q_rows, ATTN_WIDTH), lambda b, i, s: (b * nb2 + i, 0)),
            pl.BlockSpec((Q_BLOCK, 4 * LANES),
                         lambda b, i, s: (b * nb + jnp.maximum(2 * i - 1, 0), 0)),
            pl.BlockSpec((q_rows, 4 * LANES), lambda b, i, s: (b * nb2 + i, 0)),
            pl.BlockSpec((Q_BLOCK, 4 * LANES),
                         lambda b, i, s: (b * nb + jnp.minimum(2 * i + 2, nb - 1), 0)),
            pl.BlockSpec((None, N_Q_HEADS, Q_BLOCK, 3 * Q_BLOCK),
                         lambda b, i, s: (jnp.where(i == 0, 0, 1), 0, 0, 0)),
            pl.BlockSpec((None, N_Q_HEADS, Q_BLOCK, 3 * Q_BLOCK),
                         lambda b, i, s: (jnp.where(i == nb2 - 1, 2, 1), 0, 0, 0)),
            pl.BlockSpec((1, ATTN_WIDTH), lambda b, i, s: (0, 0)),
        ],
        out_specs=pl.BlockSpec((q_rows, ATTN_WIDTH), lambda b, i, s: (b * nb2 + i, 0)),
        scratch_shapes=[pltpu.VMEM((q_rows, ATTN_WIDTH), F32)],
    )
    return pl.pallas_call(
        _attn_kernel,
        grid_spec=grid_spec,
        out_shape=jax.ShapeDtypeStruct((batch * seq, ATTN_WIDTH), BF16),
        compiler_params=_params(2, VMEM_LIMIT),
        name="attention",
    )(sinks.astype(F32) * LOG2E, q, kv, kv, kv, table, table, g_out.reshape(1, ATTN_WIDTH))


def _outproj_kernel(yf_ref, ya_ref, x_ref, wo_ref, g2_ref, wr_ref, br_ref, tri_ref,
                    x1_ref, h2_ref, gate_ref, pos_ref, post_ref, cnt_ref, *, n_experts):
    half = yf_ref.shape[1]
    mix = (jnp.dot(yf_ref[...], wo_ref[:half, :], preferred_element_type=F32)
           + jnp.dot(ya_ref[...], wo_ref[half:, :], preferred_element_type=F32))
    x1 = x_ref[...] + mix
    x1_ref[...] = x1
    ms = jnp.mean(x1 * x1, axis=-1, keepdims=True)
    h2 = x1 * lax.rsqrt(ms + NORM_EPS) * g2_ref[...]
    h2_ref[...] = h2.astype(BF16)
    h_hi = h2.astype(BF16)
    h_lo = (h2 - h_hi.astype(F32)).astype(BF16)
    logits = (jnp.dot(h_hi, wr_ref[0], preferred_element_type=F32)
              + jnp.dot(h_hi, wr_ref[1], preferred_element_type=F32)
              + jnp.dot(h_lo, wr_ref[0], preferred_element_type=F32)) + br_ref[...]
    rows = logits.shape[0]
    lane_e = lax.broadcasted_iota(I32, (rows, n_experts), 1).astype(F32)
    work = logits
    vals, idxs = [], []
    for _ in range(TOP_K):
        m = jnp.max(work, axis=-1, keepdims=True)
        ik = jnp.min(jnp.where(work == m, lane_e, float(n_experts)), axis=-1, keepdims=True)
        work = jnp.where(lane_e == ik, -jnp.inf, work)
        vals.append(m)
        idxs.append(ik)
    exps = [jnp.exp(v - vals[0]) for v in vals]
    inv = 1.0 / (exps[0] + exps[1] + exps[2] + exps[3])
    lane_k = lax.broadcasted_iota(I32, (rows, TOP_K), 1)
    gate = jnp.zeros((rows, TOP_K), F32)
    for k in range(TOP_K):
        gate = jnp.where(lane_k == k, exps[k] * inv, gate)
    gate_ref[...] = gate

    lane = lax.broadcasted_iota(I32, (rows, LANES), 1).astype(F32)
    onehot = jnp.zeros((rows, LANES), F32)
    for k in range(TOP_K):
        onehot = onehot + jnp.where(lane == idxs[k] + float(k * n_experts), 1.0, 0.0)
    before = jnp.dot(tri_ref[...], onehot.astype(BF16), preferred_element_type=F32)
    colsum = jnp.sum(onehot, axis=0, keepdims=True)
    lane1 = lax.broadcasted_iota(I32, (1, LANES), 1)
    prefix = jnp.zeros((1, LANES), F32)
    total = colsum
    for k in range(1, TOP_K):
        rolled = pltpu.roll(colsum, k * n_experts, 1)
        prefix = prefix + jnp.where(lane1 >= k * n_experts, rolled, 0.0)
        total = total + rolled
    run = jnp.floor((total + (RUN_ALIGN - 1)) * (1.0 / RUN_ALIGN)) * RUN_ALIGN
    incl = run
    lane_in_seg = lane1 & (n_experts - 1)
    shift = 1
    while shift < n_experts:
        incl = incl + jnp.where(lane_in_seg >= shift, pltpu.roll(incl, shift, 1), 0.0)
        shift *= 2
    run_start = incl - run
    placed = (before + prefix + run_start) * onehot
    lane_i = lax.broadcasted_iota(I32, (rows, LANES), 1)
    pos_lanes = jnp.zeros((rows, LANES), F32)
    for k in range(TOP_K):
        seg = (lane_i >= k * n_experts) & (lane_i < (k + 1) * n_experts)
        pk = jnp.sum(jnp.where(seg, placed, 0.0), axis=-1, keepdims=True)
        pos_lanes = jnp.where(lane_i == k, pk, pos_lanes)
    pos_ref[...] = pos_lanes[:, :TOP_K].astype(I32)
    post_ref[...] = jnp.transpose(pos_lanes)[:SUBLANES, :].astype(I32)
    cnt_ref[0] = total.astype(I32)


def _outproj(yf, ya, x2d, w_out, norm2, w_router, b_router):
    t, d = x2d.shape
    tm = min(TOKEN_TILE, t)
    n_tiles = t // tm
    n_experts = w_router.shape[1]
    assert TOP_K * n_experts == LANES
    tri = np.tril(np.ones((tm, tm), np.float32), -1)
    wr_hi = w_router.astype(BF16)
    wr_lo = (w_router - wr_hi.astype(F32)).astype(BF16)
    full = lambda i: (0, 0)
    row = lambda i: (i, 0)
    return pl.pallas_call(
        functools.partial(_outproj_kernel, n_experts=n_experts),
        grid=(n_tiles,),
        in_specs=[
            pl.BlockSpec((tm, yf.shape[1]), row),
            pl.BlockSpec((tm, ya.shape[1]), row),
            pl.BlockSpec((tm, d), row),
            pl.BlockSpec((w_out.shape[0], d), full),
            pl.BlockSpec((1, d), full),
            pl.BlockSpec((2, d, n_experts), lambda i: (0, 0, 0)),
            pl.BlockSpec((1, n_experts), full),
            pl.BlockSpec((tm, tm), full),
        ],
        out_specs=[
            pl.BlockSpec((tm, d), row),
            pl.BlockSpec((tm, d), row),
            pl.BlockSpec((tm, TOP_K), row),
            pl.BlockSpec((tm, TOP_K), row),
            pl.BlockSpec((SUBLANES, tm), row),
            pl.BlockSpec((1, 1, LANES), lambda i: (i, 0, 0)),
        ],
        out_shape=[
            jax.ShapeDtypeStruct((t, d), F32),
            jax.ShapeDtypeStruct((t, d), BF16),
            jax.ShapeDtypeStruct((t, TOP_K), F32),
            jax.ShapeDtypeStruct((t, TOP_K), I32),
            jax.ShapeDtypeStruct((n_tiles * SUBLANES, tm), I32),
            jax.ShapeDtypeStruct((n_tiles, 1, LANES), I32),
        ],
        compiler_params=_params(1, VMEM_LIMIT),
        name="outproj_router",
    )(yf, ya, x2d, w_out.astype(BF16), norm2.reshape(1, d), jnp.stack([wr_hi, wr_lo]),
      b_router.reshape(1, n_experts), jnp.asarray(tri, BF16))


def _pack_pairs(x, is_bf16_exact=False):
    half = x.shape[1] // 2
    a, b = x[:, :half], x[:, half:]
    if not is_bf16_exact:
        a, b = a.astype(BF16).astype(F32), b.astype(BF16).astype(F32)
    return lax.bitcast_convert_type(a, U32) | (lax.bitcast_convert_type(b, U32) >> 16)


def _unpack_pairs(w):
    hi = lax.bitcast_convert_type(w & U32(0xFFFF0000), F32)
    lo = lax.bitcast_convert_type(w << 16, F32)
    return hi.astype(BF16), lo.astype(BF16)


def _rows(start, size):
    if not isinstance(size, int):
        size = pl.multiple_of(size, RUN_ALIGN)
    return pl.ds(pl.multiple_of(start, RUN_ALIGN), size)


def _dispatch_kernel(cnt_ref, lst_ref, base_ref, rows_ref, tail_ref, post_ref, h2_ref, xs_ref,
                     buf, zbuf, sem, zsem, *, n_experts):
    j = pl.program_id(0)
    tm = h2_ref.shape[0]
    n_local = buf.shape[1]

    def start_runs(tile, slot):
        def run(e, carry):
            r = tile * n_experts + e
            n = cnt_ref[r]

            @pl.when(n > 0)
            def _():
                pltpu.make_async_copy(buf.at[slot, _rows(lst_ref[r], n), :],
                                      xs_ref.at[_rows(base_ref[r], n), :], sem.at[slot]).start()
            return carry
        lax.fori_loop(0, n_experts, run, 0)

    def wait_runs(tile, slot):
        n = rows_ref[tile]

        @pl.when(n > 0)
        def _():
            pltpu.make_async_copy(buf.at[slot, _rows(0, n), :], xs_ref.at[_rows(0, n), :],
                                  sem.at[slot]).wait()

    def zero_fill(op):
        def tail(e, carry):
            n = tail_ref[n_experts + e]

            @pl.when(n > 0)
            def _():
                getattr(pltpu.make_async_copy(zbuf.at[_rows(0, n), :],
                                              xs_ref.at[_rows(tail_ref[e], n), :], zsem), op)()
            return carry
        lax.fori_loop(0, n_experts, tail, 0)

        def spare(b, carry):
            getattr(pltpu.make_async_copy(zbuf, xs_ref.at[_rows(b * EXPERT_ROWS, EXPERT_ROWS), :],
                                          zsem), op)()
            return carry
        lax.fori_loop(tail_ref[2 * n_experts], xs_ref.shape[0] // EXPERT_ROWS, spare, 0)

    slot = j % 2
    @pl.when(j >= 2)
    def _():
        wait_runs(j - 2, slot)

    @pl.when(j == 0)
    def _():
        zbuf[...] = jnp.zeros_like(zbuf)
        zero_fill("start")
        zero_fill("wait")

    h = h2_ref[...]
    post = post_ref[0:TOP_K, :]
    chunk_of = lax.shift_right_logical(post, PERM_CHUNK.bit_length() - 1)
    offset = (post & (PERM_CHUNK - 1)).astype(F32)
    rows = lax.broadcasted_iota(I32, (PERM_CHUNK, tm), 0).astype(F32).astype(BF16)
    one = jnp.ones((PERM_CHUNK, tm), BF16)
    per = DISPATCH_ROWS // PERM_CHUNK
    for mc in range(n_local // DISPATCH_ROWS):
        parts = []
        for rc in range(mc * per, (mc + 1) * per):
            off = jnp.where(chunk_of == rc, offset, -1.0).astype(BF16)
            perm = jnp.zeros((PERM_CHUNK, tm), BF16)
            for k in range(TOP_K):
                perm = jnp.where(rows == off[k:k + 1, :], one, perm)
            parts.append(perm)
        rs = slice(mc * DISPATCH_ROWS, (mc + 1) * DISPATCH_ROWS)
        buf[slot, rs, :] = _pack_pairs(
            jnp.dot(jnp.concatenate(parts, axis=0), h, preferred_element_type=F32), True)

    start_runs(j, slot)

    @pl.when(j == pl.num_programs(0) - 1)
    def _():
        @pl.when(j >= 1)
        def _():
            wait_runs(j - 1, 1 - slot)
        wait_runs(j, slot)


def _local_rows(tm, n_experts):
    worst = TOP_K * tm + n_experts * (RUN_ALIGN - 1)
    return -(-worst // DISPATCH_ROWS) * DISPATCH_ROWS


def _dispatch(plan, post, h2, n_rows, n_experts):
    t, d = h2.shape
    tm = min(TOKEN_TILE, t)
    grid_spec = pltpu.PrefetchScalarGridSpec(
        num_scalar_prefetch=5,
        grid=(t // tm,),
        in_specs=[
            pl.BlockSpec((SUBLANES, tm), lambda i, *_: (i, 0)),
            pl.BlockSpec((tm, d), lambda i, *_: (i, 0)),
        ],
        out_specs=pl.BlockSpec(memory_space=pl.ANY),
        scratch_shapes=[pltpu.VMEM((2, _local_rows(tm, n_experts), d // 2), U32),
                        pltpu.VMEM((EXPERT_ROWS, d // 2), U32),
                        pltpu.SemaphoreType.DMA((2,)), pltpu.SemaphoreType.DMA(())],
    )
    return pl.pallas_call(
        functools.partial(_dispatch_kernel, n_experts=n_experts),
        grid_spec=grid_spec,
        out_shape=jax.ShapeDtypeStruct((n_rows, d // 2), U32),
        compiler_params=_params(1, VMEM_LIMIT),
        name="dispatch",
    )(plan["cnt"], plan["lst"], plan["base"], plan["rows"], plan["tail"], post, h2)


def _combine_kernel(cnt_ref, lst_ref, base_ref, rows_ref, pos_ref, gate_ref, x1_ref, ys_ref, o_ref,
                    buf, g_scr, y_scr, sem, *, n_experts):
    j = pl.program_id(0)
    tm, d = x1_ref.shape
    n_local = buf.shape[1]

    def start_runs(tile, slot):
        def run(e, carry):
            r = tile * n_experts + e
            n = cnt_ref[r]

            @pl.when(n > 0)
            def _():
                pltpu.make_async_copy(ys_ref.at[_rows(base_ref[r], n), :],
                                      buf.at[slot, _rows(lst_ref[r], n), :], sem.at[slot]).start()
            return carry
        lax.fori_loop(0, n_experts, run, 0)

    def wait_runs(tile, slot):
        n = rows_ref[tile]

        @pl.when(n > 0)
        def _():
            pltpu.make_async_copy(ys_ref.at[_rows(0, n), :], buf.at[slot, _rows(0, n), :],
                                  sem.at[slot]).wait()

    slot = j % 2
    @pl.when(j == 0)
    def _():
        buf[...] = jnp.zeros_like(buf)
        start_runs(j, slot)

    @pl.when(j + 1 < pl.num_programs(0))
    def _():
        start_runs(j + 1, 1 - slot)

    wait_runs(j, slot)

    pos = pos_ref[...]
    gate = gate_ref[...]
    for rc in range(n_local // PERM_CHUNK):
        chunk = slice(rc * PERM_CHUNK, (rc + 1) * PERM_CHUNK)
        cols = lax.broadcasted_iota(I32, (tm, PERM_CHUNK), 1) + rc * PERM_CHUNK
        g = jnp.zeros((tm, PERM_CHUNK), F32)
        for k in range(TOP_K):
            g = jnp.where(cols == pos[:, k:k + 1], gate[:, k:k + 1], g)
        g_scr[:, chunk] = g.astype(BF16)
        y_scr[chunk, :d // 2], y_scr[chunk, d // 2:] = _unpack_pairs(buf[slot, chunk, :])
    o_ref[...] = x1_ref[...] + jnp.dot(g_scr[...], y_scr[...], preferred_element_type=F32)


def _combine(plan, pos, gate, x1, ys, n_experts):
    t, d = x1.shape
    tm = min(TOKEN_TILE, t)
    grid_spec = pltpu.PrefetchScalarGridSpec(
        num_scalar_prefetch=4,
        grid=(t // tm,),
        in_specs=[
            pl.BlockSpec((tm, TOP_K), lambda i, *_: (i, 0)),
            pl.BlockSpec((tm, TOP_K), lambda i, *_: (i, 0)),
            pl.BlockSpec((tm, d), lambda i, *_: (i, 0)),
            pl.BlockSpec(memory_space=pl.ANY),
        ],
        out_specs=pl.BlockSpec((tm, d), lambda i, *_: (i, 0)),
        scratch_shapes=[pltpu.VMEM((2, _local_rows(tm, n_experts), d // 2), U32),
                        pltpu.VMEM((tm, _local_rows(tm, n_experts)), BF16),
                        pltpu.VMEM((_local_rows(tm, n_experts), d), BF16),
                        pltpu.SemaphoreType.DMA((2,))],
    )
    return pl.pallas_call(
        functools.partial(_combine_kernel, n_experts=n_experts),
        grid_spec=grid_spec,
        out_shape=jax.ShapeDtypeStruct((t, d), F32),
        compiler_params=_params(1, VMEM_LIMIT),
        name="combine",
    )(plan["cnt"], plan["lst"], plan["base"], plan["rows"], pos, gate, x1, ys)


def _expert_kernel(be_ref, meta_ref, xs_ref, wgu_ref, bg_ref, bu_ref, wd_ref, bd_ref, perm_ref,
                   ys_ref, wg_s, wu_s, wd_s):
    i = pl.program_id(0)
    n_used = meta_ref[0]
    active = i < n_used
    new_expert = (i == 0) | (be_ref[i] != be_ref[jnp.maximum(i - 1, 0)])

    @pl.when(active & new_expert)
    def _():
        width = perm_ref.shape[0]
        for c in range(wgu_ref.shape[2] // width):
            wc = wgu_ref[0, :, c * width:(c + 1) * width].astype(BF16)
            r = jnp.dot(wc, perm_ref[...], preferred_element_type=F32)
            wg_s[:, c * LANES:(c + 1) * LANES] = r[:, :LANES].astype(BF16)
            wu_s[:, c * LANES:(c + 1) * LANES] = r[:, LANES:].astype(BF16)
        wd_s[...] = wd_ref[0].astype(BF16)

    @pl.when(active)
    def _():
        xb = jnp.concatenate(_unpack_pairs(xs_ref[...]), axis=1)
        g = jnp.dot(xb, wg_s[...], preferred_element_type=F32) + bg_ref[0]
        up = jnp.dot(xb, wu_s[...], preferred_element_type=F32) + bu_ref[0]
        g = jnp.minimum(g, SWIGLU_LIMIT)
        up = jnp.clip(up, -SWIGLU_LIMIT, SWIGLU_LIMIT)
        act = g * (1.0 / (1.0 + jnp.exp(-SWIGLU_ALPHA * g))) * (up + 1.0)
        ys_ref[...] = _pack_pairs(
            jnp.dot(act.astype(BF16), wd_s[...], preferred_element_type=F32) + bd_ref[0])

    @pl.when(jnp.logical_not(active))
    def _():
        ys_ref[...] = jnp.zeros_like(ys_ref)


def _experts(blk_e, meta, xs, w_gate_up, b_gate_up, w_down, b_down):
    n_rows = xs.shape[0]
    n_experts, d, f2 = w_gate_up.shape
    f = f2 // 2
    bm = EXPERT_ROWS
    n_blocks = n_rows // bm
    width = 2 * LANES
    perm = np.zeros((width, width), np.float32)
    perm[2 * np.arange(LANES), np.arange(LANES)] = 1.0
    perm[2 * np.arange(LANES) + 1, LANES + np.arange(LANES)] = 1.0
    bg = b_gate_up[:, 0::2].reshape(n_experts, 1, f)
    bu = b_gate_up[:, 1::2].reshape(n_experts, 1, f)
    rows = lambda i, be, meta: (jnp.minimum(i, meta[0] - 1), 0)
    per_e = lambda i, be, meta: (be[i], 0, 0)
    grid_spec = pltpu.PrefetchScalarGridSpec(
        num_scalar_prefetch=2,
        grid=(n_blocks,),
        in_specs=[
            pl.BlockSpec((bm, d // 2), rows),
            pl.BlockSpec((1, d, f2), per_e),
            pl.BlockSpec((1, 1, f), per_e),
            pl.BlockSpec((1, 1, f), per_e),
            pl.BlockSpec((1, f, d), per_e),
            pl.BlockSpec((1, 1, d), per_e),
            pl.BlockSpec((width, width), lambda i, be, meta: (0, 0)),
        ],
        out_specs=pl.BlockSpec((bm, d // 2), lambda i, be, meta: (i, 0)),
        scratch_shapes=[pltpu.VMEM((d, f), BF16), pltpu.VMEM((d, f), BF16),
                        pltpu.VMEM((f, d), BF16)],
    )
    return pl.pallas_call(
        _expert_kernel,
        grid_spec=grid_spec,
        out_shape=jax.ShapeDtypeStruct((n_rows, d // 2), U32),
        compiler_params=_params(1, VMEM_LIMIT),
        name="experts",
    )(blk_e, meta, xs, w_gate_up, bg, bu, w_down, b_down.reshape(n_experts, 1, d),
      jnp.asarray(perm, BF16))


def _routing_plan(counts, n_experts, bm, n_blocks):
    cnt = counts[:, 0, :n_experts]
    run = (cnt + RUN_ALIGN - 1) // RUN_ALIGN * RUN_ALIGN
    per_expert = jnp.sum(run, axis=0)
    padded = (per_expert + bm - 1) // bm * bm
    pend = jnp.cumsum(padded)
    pstart = pend - padded
    base = pstart[None, :] + jnp.cumsum(run, axis=0) - run
    lst = jnp.cumsum(run, axis=1) - run
    n_used = pend[-1] // bm
    tail = jnp.concatenate([pstart + per_expert, padded - per_expert, n_used[None]])
    starts = jnp.arange(n_blocks, dtype=I32) * bm
    blk = jnp.sum((starts[:, None] >= pend[None, :]).astype(I32), axis=1)
    blk = jnp.minimum(blk, n_experts - 1)
    last = jnp.sum((((n_used - 1) * bm) >= pend).astype(I32))
    blk_e = jnp.where(jnp.arange(n_blocks) < n_used, blk, jnp.minimum(last, n_experts - 1))
    plan = {"cnt": run.reshape(-1).astype(I32), "lst": lst.reshape(-1).astype(I32),
            "base": base.reshape(-1).astype(I32), "rows": jnp.sum(run, axis=1).astype(I32),
            "tail": tail.astype(I32)}
    return plan, blk_e.astype(I32), n_used.astype(I32).reshape(1)


def _layer(x2d, batch, seq, norm1, w_in, q_norm, k_norm, sinks, rel_bias, w_fourier, g_fourier_out,
           g_attn_out, w_out, norm2, w_router, b_router, w_gate_up, b_gate_up, w_down, b_down):
    t, d = x2d.shape
    n_experts = w_router.shape[1]
    u, q, kv = _inproj(x2d, norm1, w_in, q_norm, k_norm)
    yf = _fourier(u, w_fourier, g_fourier_out, batch, seq)
    ya = _attention(q, kv, sinks, rel_bias, g_attn_out, batch, seq)
    x1, h2, gate, pos, post, counts = _outproj(yf, ya, x2d, w_out, norm2, w_router, b_router)
    bm = EXPERT_ROWS
    n_tiles = t // min(TOKEN_TILE, t)
    worst_rows = t * TOP_K + n_tiles * n_experts * (RUN_ALIGN - 1) + n_experts * (bm - RUN_ALIGN)
    n_blocks = -(-worst_rows // bm)
    plan, blk_e, meta = _routing_plan(counts, n_experts, bm, n_blocks)
    xs = _dispatch(plan, post, h2, n_blocks * bm, n_experts)
    ys = _experts(blk_e, meta, xs, w_gate_up, b_gate_up, w_down, b_down)
    return _combine(plan, pos, gate, x1, ys, n_experts)


def kernel(x, norm1, w_in, q_norm, k_norm, sinks, rel_bias, w_fourier, g_fourier_out, g_attn_out,
           w_out, norm2, w_router, b_router, w_gate_up, b_gate_up, w_down, b_down):
    b, s, d = x.shape
    x2d = x.reshape(b * s, d)
    for l in range(norm1.shape[0]):
        x2d = _layer(x2d, b, s, norm1[l], w_in[l], q_norm[l], k_norm[l], sinks[l], rel_bias,
                     w_fourier[l], g_fourier_out[l], g_attn_out[l], w_out[l], norm2[l],
                     w_router[l], b_router[l], w_gate_up[l], b_gate_up[l], w_down[l], b_down[l])
    return x2d.reshape(b, s, d)
```

```python
import functools
import math

import jax
import jax.numpy as jnp
import numpy as np
from jax import lax
from jax.experimental import pallas as pl
from jax.experimental.pallas import tpu as pltpu

F32 = jnp.float32
BF16 = jnp.bfloat16
I32 = jnp.int32
U32 = jnp.uint32

NORM_EPS = 1e-5
QK_EPS = 1e-6
HEAD_DIM = 64
N_Q_HEADS = 8
N_KV_HEADS = 2
FOURIER_GROUPS = 4
FOURIER_CH = 128
FOURIER_WIDTH = FOURIER_GROUPS * FOURIER_CH
ATTN_WIDTH = N_Q_HEADS * HEAD_DIM
KV_WIDTH = N_KV_HEADS * HEAD_DIM
WINDOW = 128
Q_BLOCK = 128
N_BUCKETS = 32
MAX_DISTANCE = 128
TOP_K = 4
SWIGLU_ALPHA = 1.702
SWIGLU_LIMIT = 7.0
MASK_VALUE = -1e30
LOG2E = math.log2(math.e)

LANES = 128
SUBLANES = 8
TOKEN_TILE = 512
RUN_ALIGN = SUBLANES
ROW_GROUPS = 2
PERM_CHUNK = 256
DISPATCH_ROWS = 3 * PERM_CHUNK
EXPERT_ROWS = 512
VMEM_LIMIT = 56 * 1024 * 1024


def _params(n_axes, vmem=None):
    return pltpu.CompilerParams(
        dimension_semantics=("arbitrary",) * n_axes, vmem_limit_bytes=vmem)


def _pair_head_norm(xc, gain, lo):
    x2 = xc * xc
    s_lo = jnp.sum(jnp.where(lo, x2, 0.0), axis=-1, keepdims=True)
    s_hi = jnp.sum(jnp.where(lo, 0.0, x2), axis=-1, keepdims=True)
    inv = jnp.where(lo, lax.rsqrt(s_lo * (1.0 / HEAD_DIM) + QK_EPS),
                    lax.rsqrt(s_hi * (1.0 / HEAD_DIM) + QK_EPS))
    return xc * inv * gain


def _inproj_kernel(x_ref, g1_ref, w_ref, qg_ref, kg_ref, u_ref, q_ref, kv_ref):
    rows = x_ref.shape[0] // ROW_GROUPS
    lo = lax.broadcasted_iota(I32, (rows, LANES), 1) < HEAD_DIM
    q0 = FOURIER_WIDTH
    k0 = q0 + ATTN_WIDTH
    for grp in range(ROW_GROUPS):
        rs = slice(grp * rows, (grp + 1) * rows)
        x = x_ref[rs, :]
        ms = jnp.mean(x * x, axis=-1, keepdims=True)
        h = (x * lax.rsqrt(ms + NORM_EPS) * g1_ref[...]).astype(BF16)
        z = jnp.dot(h, w_ref[...], preferred_element_type=F32)
        u_ref[rs, :] = z[:, :FOURIER_WIDTH].astype(BF16)
        for c in range(ATTN_WIDTH // LANES):
            qc = _pair_head_norm(z[:, q0 + c * LANES:q0 + (c + 1) * LANES], qg_ref[...], lo)
            q_ref[rs, c * LANES:(c + 1) * LANES] = (qc * (HEAD_DIM ** -0.5 * LOG2E)).astype(BF16)
        kc = _pair_head_norm(z[:, k0:k0 + KV_WIDTH], kg_ref[...], lo)
        vc = z[:, k0 + KV_WIDTH:k0 + 2 * KV_WIDTH]
        kv_ref[rs, 0:LANES] = kc.astype(BF16)
        kv_ref[rs, LANES:2 * LANES] = pltpu.roll(kc, HEAD_DIM, 1).astype(BF16)
        kv_ref[rs, 2 * LANES:3 * LANES] = vc.astype(BF16)
        kv_ref[rs, 3 * LANES:4 * LANES] = pltpu.roll(vc, HEAD_DIM, 1).astype(BF16)


def _inproj(x2d, norm1, w_in, q_norm, k_norm):
    t, d = x2d.shape
    tm = min(TOKEN_TILE, t)
    n_in = w_in.shape[1]
    qg = jnp.tile(q_norm, LANES // HEAD_DIM).reshape(1, LANES)
    kg = jnp.tile(k_norm, LANES // HEAD_DIM).reshape(1, LANES)
    full = lambda i: (0, 0)
    return pl.pallas_call(
        _inproj_kernel,
        grid=(t // tm,),
        in_specs=[
            pl.BlockSpec((tm, d), lambda i: (i, 0)),
            pl.BlockSpec((1, d), full),
            pl.BlockSpec((d, n_in), full),
            pl.BlockSpec((1, LANES), full),
            pl.BlockSpec((1, LANES), full),
        ],
        out_specs=[
            pl.BlockSpec((tm, FOURIER_WIDTH), lambda i: (i, 0)),
            pl.BlockSpec((tm, ATTN_WIDTH), lambda i: (i, 0)),
            pl.BlockSpec((tm, 4 * LANES), lambda i: (i, 0)),
        ],
        out_shape=[
            jax.ShapeDtypeStruct((t, FOURIER_WIDTH), BF16),
            jax.ShapeDtypeStruct((t, ATTN_WIDTH), BF16),
            jax.ShapeDtypeStruct((t, 4 * LANES), BF16),
        ],
        compiler_params=_params(1, VMEM_LIMIT),
        name="inproj",
    )(x2d, norm1.reshape(1, d), w_in.astype(BF16), qg, kg)


def _fourier_kernel(u_ref, cs_ref, ss_ref, rev_ref, cc_ref, sc_ref, wf_ref, g_ref, o_ref,
                    p_scr, q_scr, e_scr, *, scale, row_block):
    for g in range(FOURIER_GROUPS):
        sl = slice(g * FOURIER_CH, (g + 1) * FOURIER_CH)
        w = wf_ref[g].astype(BF16)
        a = (jnp.dot(cc_ref[...], w, preferred_element_type=F32) * scale).astype(BF16)
        b = (jnp.dot(sc_ref[...], w, preferred_element_type=F32) * scale).astype(BF16)
        ug = u_ref[:, sl]
        p_scr[:, sl] = jnp.dot(ug, a, preferred_element_type=F32).astype(BF16)
        q_scr[:, sl] = jnp.dot(ug, b, preferred_element_type=F32).astype(BF16)
    half = u_ref.shape[0] // 2
    gain = g_ref[...]

    def norm(y):
        ms = jnp.mean(y * y, axis=-1, keepdims=True)
        return y * lax.rsqrt(ms + NORM_EPS) * gain

    n_blk = half // row_block
    mid = None
    for r in range(n_blk):
        rs = slice(r * row_block, (r + 1) * row_block)
        extra = SUBLANES if r == n_blk - 1 else 0
        c = jnp.dot(cs_ref[r * row_block:(r + 1) * row_block + extra, :], p_scr[...],
                    preferred_element_type=F32)
        d = jnp.dot(ss_ref[rs, :], q_scr[...], preferred_element_type=F32)
        o_ref[rs, :] = norm(c[:row_block] + d).astype(BF16)
        e_scr[rs, :] = norm(c[:row_block] - d).astype(BF16)
        if extra:
            mid = norm(c[row_block:row_block + 1])
    for r in range(n_blk):
        z = jnp.dot(rev_ref[r * row_block:(r + 1) * row_block, :], e_scr[...],
                    preferred_element_type=F32)
        if r == 0:
            z = jnp.where(lax.broadcasted_iota(I32, z.shape, 0) == 0, mid, z)
        o_ref[half + r * row_block:half + (r + 1) * row_block, :] = z.astype(BF16)


def _dft_tables(n):
    k = np.arange(n, dtype=np.int64)
    ang = 2.0 * np.pi * ((k[:, None] * k[None, :]) % n).astype(np.float64) / n
    return np.cos(ang), np.sin(ang)


def _fourier(u, w_fourier, g_out, batch, seq):
    cs, ss = _dft_tables(seq)
    cc, sc = _dft_tables(FOURIER_CH)
    scale = 1.0 / math.sqrt(seq * FOURIER_CH)
    half = seq // 2
    row_block = min(512, half)
    rev = np.zeros((half, half), np.float32)
    rev[np.arange(1, half), half - np.arange(1, half)] = 1.0
    full2 = lambda b: (0, 0)
    return pl.pallas_call(
        functools.partial(_fourier_kernel, scale=scale, row_block=row_block),
        grid=(batch,),
        in_specs=[
            pl.BlockSpec((seq, FOURIER_WIDTH), lambda b: (b, 0)),
            pl.BlockSpec((half + SUBLANES, seq), full2),
            pl.BlockSpec((half, seq), full2),
            pl.BlockSpec((half, half), full2),
            pl.BlockSpec((FOURIER_CH, FOURIER_CH), full2),
            pl.BlockSpec((FOURIER_CH, FOURIER_CH), full2),
            pl.BlockSpec((FOURIER_GROUPS, FOURIER_CH, FOURIER_CH), lambda b: (0, 0, 0)),
            pl.BlockSpec((1, FOURIER_WIDTH), full2),
        ],
        out_specs=pl.BlockSpec((seq, FOURIER_WIDTH), lambda b: (b, 0)),
        out_shape=jax.ShapeDtypeStruct((batch * seq, FOURIER_WIDTH), BF16),
        scratch_shapes=[pltpu.VMEM((seq, FOURIER_WIDTH), BF16),
                        pltpu.VMEM((seq, FOURIER_WIDTH), BF16),
                        pltpu.VMEM((half, FOURIER_WIDTH), BF16)],
        compiler_params=_params(1, VMEM_LIMIT),
        name="fourier",
    )(u, jnp.asarray(cs[:half + SUBLANES], BF16), jnp.asarray(ss[:half], BF16),
      jnp.asarray(rev, BF16), jnp.asarray(cc, BF16), jnp.asarray(-sc, BF16), w_fourier,
      g_out.reshape(1, FOURIER_WIDTH))


def _attn_kernel(sink_ref, q_ref, kvp_ref, kvo_ref, kvn_ref, bias_a_ref, bias_b_ref, g_ref, o_ref,
                 acc_ref):
    kv = jnp.concatenate([kvp_ref[...], kvo_ref[...], kvn_ref[...]], axis=0)
    nk = kv.shape[0]
    lo = lax.broadcasted_iota(I32, (nk, LANES), 1) < HEAD_DIM
    k_a, k_b = kv[:, 0:LANES], kv[:, LANES:2 * LANES]
    v_a, v_b = kv[:, 2 * LANES:3 * LANES], kv[:, 3 * LANES:4 * LANES]
    zero = jnp.zeros_like(k_a)
    k_lo = (jnp.where(lo, k_a, zero), jnp.where(lo, k_b, zero))
    k_hi = (jnp.where(lo, zero, k_b), jnp.where(lo, zero, k_a))
    v_lo = (jnp.where(lo, v_a, zero), jnp.where(lo, v_b, zero))
    v_hi = (jnp.where(lo, zero, v_b), jnp.where(lo, zero, v_a))
    lo_out = lax.broadcasted_iota(I32, (Q_BLOCK, LANES), 1) < HEAD_DIM
    bias_refs = (bias_a_ref, bias_b_ref)
    rows2 = q_ref.shape[0]
    nt = (((1,), (1,)), ((), ()))
    for h in range(N_KV_HEADS):
        qs = jnp.concatenate([q_ref[:, (2 * h) * LANES:(2 * h + 1) * LANES],
                              q_ref[:, (2 * h + 1) * LANES:(2 * h + 2) * LANES]], axis=0)
        s_par = (lax.dot_general(qs, k_lo[h], nt, preferred_element_type=F32),
                 lax.dot_general(qs, k_hi[h], nt, preferred_element_type=F32))
        for sb in range(2):
            keys = slice(sb * Q_BLOCK, sb * Q_BLOCK + 3 * Q_BLOCK)
            vcat = jnp.concatenate([v_lo[h][keys, :], v_hi[h][keys, :]], axis=0)
            for c in range(2):
                r0 = c * rows2 + sb * Q_BLOCK
                probs, invs = [], []
                for par in range(2):
                    hq = 4 * h + 2 * c + par
                    s = s_par[par][r0:r0 + Q_BLOCK, keys] + bias_refs[sb][hq]
                    sink = sink_ref[hq]
                    m = jnp.maximum(jnp.max(s, axis=-1, keepdims=True), sink)
                    p = jnp.exp2(s - m)
                    denom = jnp.sum(p, axis=-1, keepdims=True) + jnp.exp2(sink - m)
                    probs.append(p.astype(BF16))
                    invs.append(1.0 / denom)
                pcat = jnp.concatenate(probs, axis=1)
                chunk = 2 * h + c
                o = jnp.dot(pcat, vcat, preferred_element_type=F32)
                acc_ref[sb * Q_BLOCK:(sb + 1) * Q_BLOCK, chunk * LANES:(chunk + 1) * LANES] = (
                    o * jnp.where(lo_out, invs[0], invs[1]))
    y = acc_ref[...]
    ms = jnp.mean(y * y, axis=-1, keepdims=True)
    o_ref[...] = (y * lax.rsqrt(ms + NORM_EPS) * g_ref[...]).astype(BF16)


def _t5_bucket(rel):
    nb = N_BUCKETS // 2
    max_exact = nb // 2
    ret = (rel > 0).astype(jnp.int32) * nb
    n = jnp.abs(rel)
    nf = jnp.maximum(n, 1).astype(jnp.float32)
    large = max_exact + (jnp.log(nf / max_exact) / math.log(MAX_DISTANCE / max_exact)
                         * (nb - max_exact)).astype(jnp.int32)
    large = jnp.minimum(large, nb - 1)
    return ret + jnp.where(n < max_exact, n, large)


def _attention(q, kv, sinks, rel_bias, g_out, batch, seq):
    nb = seq // Q_BLOCK
    assert nb % 2 == 0
    nb2 = nb // 2
    qi = jnp.arange(Q_BLOCK, dtype=jnp.int32)[:, None]
    kj = jnp.arange(3 * Q_BLOCK, dtype=jnp.int32)[None, :]
    rel = kj - Q_BLOCK - qi
    period = 4 * Q_BLOCK
    p = jnp.arange(period, dtype=jnp.int32)
    off = jnp.where(p < 3 * Q_BLOCK, p, p - period) - Q_BLOCK
    hit = _t5_bucket(off)[None, :, None] == jnp.arange(N_BUCKETS, dtype=jnp.int32)
    by_off = jnp.sum(jnp.where(hit, rel_bias.astype(F32).T[:, None, :], 0.0), axis=-1)
    bias = jnp.tile(by_off, (1, Q_BLOCK))[:, :Q_BLOCK * (period - 1)]
    bias = bias.reshape(N_Q_HEADS, Q_BLOCK, period - 1)[:, :, :3 * Q_BLOCK]
    band = jnp.abs(rel) <= WINDOW
    first = band & (kj >= Q_BLOCK)
    last = band & (kj < 2 * Q_BLOCK)
    table = jnp.stack([jnp.where(msk[None], bias * LOG2E, MASK_VALUE) for msk in (first, band, last)])
    q_rows = 2 * Q_BLOCK
    grid_spec = pltpu.PrefetchScalarGridSpec(
        num_scalar_prefetch=1,
        grid=(batch, nb2),
        in_specs=[
            pl.BlockSpec((q_rows, ATTN_WIDTH), lambda b, i, s: (b * nb2 + i, 0)),
            pl.BlockSpec((Q_BLOCK, 4 * LANES),
                         lambda b, i, s: (b * nb + jnp.maximum(2 * i - 1, 0), 0)),
            pl.BlockSpec((q_rows, 4 * LANES), lambda b, i, s: (b * nb2 + i, 0)),
            pl.BlockSpec((Q_BLOCK, 4 * LANES),
                         lambda b, i, s: (b * nb + jnp.minimum(2 * i + 2, nb - 1), 0)),
            pl.BlockSpec((None, N_Q_HEADS, Q_BLOCK, 3 * Q_BLOCK),
                         lambda b, i, s: (jnp.where(i == 0, 0, 1), 0, 0, 0)),
            pl.BlockSpec((None, N_Q_HEADS, Q_BLOCK, 3 * Q_BLOCK),
                         lambda b, i, s: (jnp.where(i == nb2 - 1, 2, 1), 0, 0, 0)),
            pl.BlockSpec((1, ATTN_WIDTH), lambda b, i, s: (0, 0)),
        ],
        out_specs=pl.BlockSpec((q_rows, ATTN_WIDTH), lambda b, i, s: (b * nb2 + i, 0)),
        scratch_shapes=[pltpu.VMEM((q_rows, ATTN_WIDTH), F32)],
    )
    return pl.pallas_call(
        _attn_kernel,
        grid_spec=grid_spec,
        out_shape=jax.ShapeDtypeStruct((batch * seq, ATTN_WIDTH), BF16),
        compiler_params=_params(2, VMEM_LIMIT),
        name="attention",
    )(sinks.astype(F32) * LOG2E, q, kv, kv, kv, table, table, g_out.reshape(1, ATTN_WIDTH))


def _outproj_kernel(yf_ref, ya_ref, x_ref, wo_ref, g2_ref, wrt_ref, brt_ref, triu_ref, scan_ref,
                    x1_ref, h2_ref, post_ref, gatet_ref, cnt_ref, *, n_experts):
    half = yf_ref.shape[1]
    mix = (jnp.dot(yf_ref[...], wo_ref[:half, :], preferred_element_type=F32)
           + jnp.dot(ya_ref[...], wo_ref[half:, :], preferred_element_type=F32))
    x1 = x_ref[...] + mix
    x1_ref[...] = x1
    ms = jnp.mean(x1 * x1, axis=-1, keepdims=True)
    h2 = x1 * lax.rsqrt(ms + NORM_EPS) * g2_ref[...]
    h2_ref[...] = h2.astype(BF16)
    h_hi = h2.astype(BF16)
    h_lo = (h2 - h_hi.astype(F32)).astype(BF16)
    nt = (((1,), (1,)), ((), ()))
    t1 = lax.dot_general(wrt_ref[...], h_hi, nt, preferred_element_type=F32)
    t2 = lax.dot_general(wrt_ref[:n_experts, :], h_lo, nt, preferred_element_type=F32)
    logits = t1[:n_experts] + t1[n_experts:] + t2 + brt_ref[...]
    tm = logits.shape[1]
    sub_e = lax.broadcasted_iota(I32, (n_experts, tm), 0).astype(F32)
    work = logits
    vals, idxs = [], []
    for _ in range(TOP_K):
        m = jnp.max(work, axis=0, keepdims=True)
        ik = jnp.min(jnp.where(work == m, sub_e, float(n_experts)), axis=0, keepdims=True)
        work = jnp.where(sub_e == ik, -jnp.inf, work)
        vals.append(m)
        idxs.append(ik)
    exps = [jnp.exp(v - vals[0]) for v in vals]
    inv = 1.0 / (exps[0] + exps[1] + exps[2] + exps[3])
    gates = [e * inv for e in exps]

    sub = lax.broadcasted_iota(I32, (LANES, tm), 0).astype(F32)
    onehot = jnp.zeros((LANES, tm), F32)
    for k in range(TOP_K):
        onehot = onehot + jnp.where(sub == idxs[k] + float(k * n_experts), 1.0, 0.0)
    onehot_b = onehot.astype(BF16)
    before = jnp.dot(onehot_b, triu_ref[...], preferred_element_type=F32)
    counts = lax.dot_general(jnp.ones((SUBLANES, tm), BF16), onehot_b, nt,
                             preferred_element_type=F32)[0:1, :]
    total = counts
    for k in range(1, TOP_K):
        total = total + pltpu.roll(counts, k * n_experts, 1)
    cnt_ref[0] = total.astype(I32)
    colsum = jnp.sum(onehot, axis=1, keepdims=True)
    blocks = [colsum[k * n_experts:(k + 1) * n_experts] for k in range(TOP_K)]
    total_e = blocks[0] + blocks[1] + blocks[2] + blocks[3]
    run_e = jnp.floor((total_e + (RUN_ALIGN - 1)) * (1.0 / RUN_ALIGN)) * RUN_ALIGN
    run_start = jnp.dot(scan_ref[...], jnp.broadcast_to(run_e, (n_experts, LANES)).astype(BF16),
                        preferred_element_type=F32)[:, 0:1]
    adds, acc = [], run_start
    for k in range(TOP_K):
        adds.append(acc)
        acc = acc + blocks[k]
    placed = (before + jnp.concatenate(adds, axis=0)) * onehot
    pos = [jnp.sum(placed[k * n_experts:(k + 1) * n_experts], axis=0, keepdims=True)
           for k in range(TOP_K)]
    post_ref[...] = jnp.concatenate(pos + [jnp.zeros((SUBLANES - TOP_K, tm), F32)], axis=0).astype(I32)
    gatet_ref[...] = jnp.concatenate(gates + [jnp.zeros((SUBLANES - TOP_K, tm), F32)], axis=0)


def _outproj(yf, ya, x2d, w_out, norm2, w_router, b_router):
    t, d = x2d.shape
    tm = min(TOKEN_TILE, t)
    n_tiles = t // tm
    n_experts = w_router.shape[1]
    assert TOP_K * n_experts == LANES
    triu = np.triu(np.ones((tm, tm), np.float32), 1)
    scan = np.tril(np.ones((n_experts, n_experts), np.float32), -1)
    wr_hi = w_router.astype(BF16)
    wr_lo = (w_router - wr_hi.astype(F32)).astype(BF16)
    wrt = jnp.concatenate([wr_hi.T, wr_lo.T], axis=0)
    full = lambda i: (0, 0)
    row = lambda i: (i, 0)
    return pl.pallas_call(
        functools.partial(_outproj_kernel, n_experts=n_experts),
        grid=(n_tiles,),
        in_specs=[
            pl.BlockSpec((tm, yf.shape[1]), row),
            pl.BlockSpec((tm, ya.shape[1]), row),
            pl.BlockSpec((tm, d), row),
            pl.BlockSpec((w_out.shape[0], d), full),
            pl.BlockSpec((1, d), full),
            pl.BlockSpec((2 * n_experts, d), full),
            pl.BlockSpec((n_experts, 1), full),
            pl.BlockSpec((tm, tm), full),
            pl.BlockSpec((n_experts, n_experts), full),
        ],
        out_specs=[
            pl.BlockSpec((tm, d), row),
            pl.BlockSpec((tm, d), row),
            pl.BlockSpec((SUBLANES, tm), row),
            pl.BlockSpec((SUBLANES, tm), row),
            pl.BlockSpec((1, 1, LANES), lambda i: (i, 0, 0)),
        ],
        out_shape=[
            jax.ShapeDtypeStruct((t, d), F32),
            jax.ShapeDtypeStruct((t, d), BF16),
            jax.ShapeDtypeStruct((n_tiles * SUBLANES, tm), I32),
            jax.ShapeDtypeStruct((n_tiles * SUBLANES, tm), F32),
            jax.ShapeDtypeStruct((n_tiles, 1, LANES), I32),
        ],
        compiler_params=_params(1, VMEM_LIMIT),
        name="outproj_router",
    )(yf, ya, x2d, w_out.astype(BF16), norm2.reshape(1, d), wrt, b_router.reshape(n_experts, 1),
      jnp.asarray(triu, BF16), jnp.asarray(scan, BF16))


def _pack_pairs(x, is_bf16_exact=False):
    half = x.shape[1] // 2
    a, b = x[:, :half], x[:, half:]
    if not is_bf16_exact:
        a, b = a.astype(BF16).astype(F32), b.astype(BF16).astype(F32)
    return lax.bitcast_convert_type(a, U32) | (lax.bitcast_convert_type(b, U32) >> 16)


def _unpack_pairs(w):
    hi = lax.bitcast_convert_type(w & U32(0xFFFF0000), F32)
    lo = lax.bitcast_convert_type(w << 16, F32)
    return hi.astype(BF16), lo.astype(BF16)


def _rows(start, size):
    if not isinstance(size, int):
        size = pl.multiple_of(size, RUN_ALIGN)
    return pl.ds(pl.multiple_of(start, RUN_ALIGN), size)


def _dispatch_kernel(cnt_ref, lst_ref, base_ref, rows_ref, tail_ref, post_ref, h2_ref, xs_ref,
                     buf, zbuf, sem, zsem, *, n_experts):
    j = pl.program_id(0)
    tm = h2_ref.shape[0]
    n_local = buf.shape[1]

    def start_runs(tile, slot):
        def run(e, carry):
            r = tile * n_experts + e
            n = cnt_ref[r]

            @pl.when(n > 0)
            def _():
                pltpu.make_async_copy(buf.at[slot, _rows(lst_ref[r], n), :],
                                      xs_ref.at[_rows(base_ref[r], n), :], sem.at[slot]).start()
            return carry
        lax.fori_loop(0, n_experts, run, 0)

    def wait_runs(tile, slot):
        n = rows_ref[tile]

        @pl.when(n > 0)
        def _():
            pltpu.make_async_copy(buf.at[slot, _rows(0, n), :], xs_ref.at[_rows(0, n), :],
                                  sem.at[slot]).wait()

    def zero_fill(op):
        def tail(e, carry):
            n = tail_ref[n_experts + e]

            @pl.when(n > 0)
            def _():
                getattr(pltpu.make_async_copy(zbuf.at[_rows(0, n), :],
                                              xs_ref.at[_rows(tail_ref[e], n), :], zsem), op)()
            return carry
        lax.fori_loop(0, n_experts, tail, 0)

        def spare(b, carry):
            getattr(pltpu.make_async_copy(zbuf, xs_ref.at[_rows(b * EXPERT_ROWS, EXPERT_ROWS), :],
                                          zsem), op)()
            return carry
        lax.fori_loop(tail_ref[2 * n_experts], xs_ref.shape[0] // EXPERT_ROWS, spare, 0)

    slot = j % 2
    @pl.when(j >= 2)
    def _():
        wait_runs(j - 2, slot)

    @pl.when(j == 0)
    def _():
        zbuf[...] = jnp.zeros_like(zbuf)
        zero_fill("start")
        zero_fill("wait")

    h = h2_ref[...]
    post = post_ref[0:TOP_K, :]
    chunk_of = lax.shift_right_logical(post, PERM_CHUNK.bit_length() - 1)
    offset = (post & (PERM_CHUNK - 1)).astype(F32)
    rows = lax.broadcasted_iota(I32, (PERM_CHUNK, tm), 0).astype(F32).astype(BF16)
    one = jnp.ones((PERM_CHUNK, tm), BF16)
    per = DISPATCH_ROWS // PERM_CHUNK
    for mc in range(n_local // DISPATCH_ROWS):
        parts = []
        for rc in range(mc * per, (mc + 1) * per):
            off = jnp.where(chunk_of == rc, offset, -1.0).astype(BF16)
            perm = jnp.zeros((PERM_CHUNK, tm), BF16)
            for k in range(TOP_K):
                perm = jnp.where(rows == off[k:k + 1, :], one, perm)
            parts.append(perm)
        rs = slice(mc * DISPATCH_ROWS, (mc + 1) * DISPATCH_ROWS)
        buf[slot, rs, :] = _pack_pairs(
            jnp.dot(jnp.concatenate(parts, axis=0), h, preferred_element_type=F32), True)

    start_runs(j, slot)

    @pl.when(j == pl.num_programs(0) - 1)
    def _():
        @pl.when(j >= 1)
        def _():
            wait_runs(j - 1, 1 - slot)
        wait_runs(j, slot)


def _local_rows(tm, n_experts):
    worst = TOP_K * tm + n_experts * (RUN_ALIGN - 1)
    return -(-worst // DISPATCH_ROWS) * DISPATCH_ROWS


def _dispatch(plan, post, h2, n_rows, n_experts):
    t, d = h2.shape
    tm = min(TOKEN_TILE, t)
    grid_spec = pltpu.PrefetchScalarGridSpec(
        num_scalar_prefetch=5,
        grid=(t // tm,),
        in_specs=[
            pl.BlockSpec((SUBLANES, tm), lambda i, *_: (i, 0)),
            pl.BlockSpec((tm, d), lambda i, *_: (i, 0)),
        ],
        out_specs=pl.BlockSpec(memory_space=pl.ANY),
        scratch_shapes=[pltpu.VMEM((2, _local_rows(tm, n_experts), d // 2), U32),
                        pltpu.VMEM((EXPERT_ROWS, d // 2), U32),
                        pltpu.SemaphoreType.DMA((2,)), pltpu.SemaphoreType.DMA(())],
    )
    return pl.pallas_call(
        functools.partial(_dispatch_kernel, n_experts=n_experts),
        grid_spec=grid_spec,
        out_shape=jax.ShapeDtypeStruct((n_rows, d // 2), U32),
        compiler_params=_params(1, VMEM_LIMIT),
        name="dispatch",
    )(plan["cnt"], plan["lst"], plan["base"], plan["rows"], plan["tail"], post, h2)


def _combine_kernel(cnt_ref, lst_ref, base_ref, rows_ref, post_ref, gatet_ref, x1_ref, ys_ref, o_ref,
                    buf, g_scr, y_scr, sem, *, n_experts):
    j = pl.program_id(0)
    tm, d = x1_ref.shape
    n_local = buf.shape[1]

    def start_runs(tile, slot):
        def run(e, carry):
            r = tile * n_experts + e
            n = cnt_ref[r]

            @pl.when(n > 0)
            def _():
                pltpu.make_async_copy(ys_ref.at[_rows(base_ref[r], n), :],
                                      buf.at[slot, _rows(lst_ref[r], n), :], sem.at[slot]).start()
            return carry
        lax.fori_loop(0, n_experts, run, 0)

    def wait_runs(tile, slot):
        n = rows_ref[tile]

        @pl.when(n > 0)
        def _():
            pltpu.make_async_copy(ys_ref.at[_rows(0, n), :], buf.at[slot, _rows(0, n), :],
                                  sem.at[slot]).wait()

    slot = j % 2
    @pl.when(j == 0)
    def _():
        buf[...] = jnp.zeros_like(buf)
        start_runs(j, slot)

    @pl.when(j + 1 < pl.num_programs(0))
    def _():
        start_runs(j + 1, 1 - slot)

    wait_runs(j, slot)

    post = post_ref[0:TOP_K, :]
    gate = gatet_ref[0:TOP_K, :].astype(BF16)
    chunk_of = lax.shift_right_logical(post, PERM_CHUNK.bit_length() - 1)
    offset = (post & (PERM_CHUNK - 1)).astype(F32)
    rows = lax.broadcasted_iota(I32, (PERM_CHUNK, tm), 0).astype(F32).astype(BF16)
    for rc in range(n_local // PERM_CHUNK):
        chunk = slice(rc * PERM_CHUNK, (rc + 1) * PERM_CHUNK)
        off = jnp.where(chunk_of == rc, offset, -1.0).astype(BF16)
        g = jnp.zeros((PERM_CHUNK, tm), BF16)
        for k in range(TOP_K):
            g = jnp.where(rows == off[k:k + 1, :], jnp.broadcast_to(gate[k:k + 1, :], g.shape), g)
        g_scr[chunk, :] = g
        y_scr[chunk, :d // 2], y_scr[chunk, d // 2:] = _unpack_pairs(buf[slot, chunk, :])
    o_ref[...] = x1_ref[...] + lax.dot_general(g_scr[...], y_scr[...], (((0,), (0,)), ((), ())),
                                               preferred_element_type=F32)


def _combine(plan, post, gatet, x1, ys, n_experts):
    t, d = x1.shape
    tm = min(TOKEN_TILE, t)
    grid_spec = pltpu.PrefetchScalarGridSpec(
        num_scalar_prefetch=4,
        grid=(t // tm,),
        in_specs=[
            pl.BlockSpec((SUBLANES, tm), lambda i, *_: (i, 0)),
            pl.BlockSpec((SUBLANES, tm), lambda i, *_: (i, 0)),
            pl.BlockSpec((tm, d), lambda i, *_: (i, 0)),
            pl.BlockSpec(memory_space=pl.ANY),
        ],
        out_specs=pl.BlockSpec((tm, d), lambda i, *_: (i, 0)),
        scratch_shapes=[pltpu.VMEM((2, _local_rows(tm, n_experts), d // 2), U32),
                        pltpu.VMEM((_local_rows(tm, n_experts), tm), BF16),
                        pltpu.VMEM((_local_rows(tm, n_experts), d), BF16),
                        pltpu.SemaphoreType.DMA((2,))],
    )
    return pl.pallas_call(
        functools.partial(_combine_kernel, n_experts=n_experts),
        grid_spec=grid_spec,
        out_shape=jax.ShapeDtypeStruct((t, d), F32),
        compiler_params=_params(1, VMEM_LIMIT),
        name="combine",
    )(plan["cnt"], plan["lst"], plan["base"], plan["rows"], post, gatet, x1, ys)


def _expert_kernel(be_ref, meta_ref, xs_ref, wgu_ref, bg_ref, bu_ref, wd_ref, bd_ref, perm_ref,
                   ys_ref, wg_s, wu_s, wd_s):
    i = pl.program_id(0)
    n_used = meta_ref[0]
    active = i < n_used
    new_expert = (i == 0) | (be_ref[i] != be_ref[jnp.maximum(i - 1, 0)])

    @pl.when(active & new_expert)
    def _():
        width = perm_ref.shape[0]
        for c in range(wgu_ref.shape[2] // width):
            wc = wgu_ref[0, :, c * width:(c + 1) * width].astype(BF16)
            r = jnp.dot(wc, perm_ref[...], preferred_element_type=F32)
            wg_s[:, c * LANES:(c + 1) * LANES] = r[:, :LANES].astype(BF16)
            wu_s[:, c * LANES:(c + 1) * LANES] = r[:, LANES:].astype(BF16)
        wd_s[...] = wd_ref[0].astype(BF16)

    @pl.when(active)
    def _():
        xb = jnp.concatenate(_unpack_pairs(xs_ref[...]), axis=1)
        g = jnp.dot(xb, wg_s[...], preferred_element_type=F32) + bg_ref[0]
        up = jnp.dot(xb, wu_s[...], preferred_element_type=F32) + bu_ref[0]
        g = jnp.minimum(g, SWIGLU_LIMIT)
        up = jnp.clip(up, -SWIGLU_LIMIT, SWIGLU_LIMIT)
        act = g * (1.0 / (1.0 + jnp.exp(-SWIGLU_ALPHA * g))) * (up + 1.0)
        ys_ref[...] = _pack_pairs(
            jnp.dot(act.astype(BF16), wd_s[...], preferred_element_type=F32) + bd_ref[0])

    @pl.when(jnp.logical_not(active))
    def _():
        ys_ref[...] = jnp.zeros_like(ys_ref)


def _experts(blk_e, meta, xs, w_gate_up, b_gate_up, w_down, b_down):
    n_rows = xs.shape[0]
    n_experts, d, f2 = w_gate_up.shape
    f = f2 // 2
    bm = EXPERT_ROWS
    n_blocks = n_rows // bm
    width = 2 * LANES
    perm = np.zeros((width, width), np.float32)
    perm[2 * np.arange(LANES), np.arange(LANES)] = 1.0
    perm[2 * np.arange(LANES) + 1, LANES + np.arange(LANES)] = 1.0
    bg = b_gate_up[:, 0::2].reshape(n_experts, 1, f)
    bu = b_gate_up[:, 1::2].reshape(n_experts, 1, f)
    rows = lambda i, be, meta: (jnp.minimum(i, meta[0] - 1), 0)
    per_e = lambda i, be, meta: (be[i], 0, 0)
    grid_spec = pltpu.PrefetchScalarGridSpec(
        num_scalar_prefetch=2,
        grid=(n_blocks,),
        in_specs=[
            pl.BlockSpec((bm, d // 2), rows),
            pl.BlockSpec((1, d, f2), per_e),
            pl.BlockSpec((1, 1, f), per_e),
            pl.BlockSpec((1, 1, f), per_e),
            pl.BlockSpec((1, f, d), per_e),
            pl.BlockSpec((1, 1, d), per_e),
            pl.BlockSpec((width, width), lambda i, be, meta: (0, 0)),
        ],
        out_specs=pl.BlockSpec((bm, d // 2), lambda i, be, meta: (i, 0)),
        scratch_shapes=[pltpu.VMEM((d, f), BF16), pltpu.VMEM((d, f), BF16),
                        pltpu.VMEM((f, d), BF16)],
    )
    return pl.pallas_call(
        _expert_kernel,
        grid_spec=grid_spec,
        out_shape=jax.ShapeDtypeStruct((n_rows, d // 2), U32),
        compiler_params=_params(1, VMEM_LIMIT),
        name="experts",
    )(blk_e, meta, xs, w_gate_up, bg, bu, w_down, b_down.reshape(n_experts, 1, d),
      jnp.asarray(perm, BF16))


def _routing_plan(counts, n_experts, bm, n_blocks):
    cnt = counts[:, 0, :n_experts]
    run = (cnt + RUN_ALIGN - 1) // RUN_ALIGN * RUN_ALIGN
    per_expert = jnp.sum(run, axis=0)
    padded = (per_expert + bm - 1) // bm * bm
    pend = jnp.cumsum(padded)
    pstart = pend - padded
    base = pstart[None, :] + jnp.cumsum(run, axis=0) - run
    lst = jnp.cumsum(run, axis=1) - run
    n_used = pend[-1] // bm
    tail = jnp.concatenate([pstart + per_expert, padded - per_expert, n_used[None]])
    starts = jnp.arange(n_blocks, dtype=I32) * bm
    blk = jnp.sum((starts[:, None] >= pend[None, :]).astype(I32), axis=1)
    blk = jnp.minimum(blk, n_experts - 1)
    last = jnp.sum((((n_used - 1) * bm) >= pend).astype(I32))
    blk_e = jnp.where(jnp.arange(n_blocks) < n_used, blk, jnp.minimum(last, n_experts - 1))
    plan = {"cnt": run.reshape(-1).astype(I32), "lst": lst.reshape(-1).astype(I32),
            "base": base.reshape(-1).astype(I32), "rows": jnp.sum(run, axis=1).astype(I32),
            "tail": tail.astype(I32)}
    return plan, blk_e.astype(I32), n_used.astype(I32).reshape(1)


def _layer(x2d, batch, seq, norm1, w_in, q_norm, k_norm, sinks, rel_bias, w_fourier, g_fourier_out,
           g_attn_out, w_out, norm2, w_router, b_router, w_gate_up, b_gate_up, w_down, b_down):
    t, d = x2d.shape
    n_experts = w_router.shape[1]
    u, q, kv = _inproj(x2d, norm1, w_in, q_norm, k_norm)
    yf = _fourier(u, w_fourier, g_fourier_out, batch, seq)
    ya = _attention(q, kv, sinks, rel_bias, g_attn_out, batch, seq)
    x1, h2, post, gatet, counts = _outproj(yf, ya, x2d, w_out, norm2, w_router, b_router)
    bm = EXPERT_ROWS
    n_tiles = t // min(TOKEN_TILE, t)
    worst_rows = t * TOP_K + n_tiles * n_experts * (RUN_ALIGN - 1) + n_experts * (bm - RUN_ALIGN)
    n_blocks = -(-worst_rows // bm)
    plan, blk_e, meta = _routing_plan(counts, n_experts, bm, n_blocks)
    xs = _dispatch(plan, post, h2, n_blocks * bm, n_experts)
    ys = _experts(blk_e, meta, xs, w_gate_up, b_gate_up, w_down, b_down)
    return _combine(plan, post, gatet, x1, ys, n_experts)


def kernel(x, norm1, w_in, q_norm, k_norm, sinks, rel_bias, w_fourier, g_fourier_out, g_attn_out,
           w_out, norm2, w_router, b_router, w_gate_up, b_gate_up, w_down, b_down):
    b, s, d = x.shape
    x2d = x.reshape(b * s, d)
    for l in range(norm1.shape[0]):
        x2d = _layer(x2d, b, s, norm1[l], w_in[l], q_norm[l], k_norm[l], sinks[l], rel_bias,
                     w_fourier[l], g_fourier_out[l], g_attn_out[l], w_out[l], norm2[l],
                     w_router[l], b_router[l], w_gate_up[l], b_gate_up[l], w_down[l], b_down[l])
    return x2d.reshape(b, s, d)
```

```python
import functools
import math

import jax
import jax.numpy as jnp
import numpy as np
from jax import lax
from jax.experimental import pallas as pl
from jax.experimental.pallas import tpu as pltpu

F32 = jnp.float32
BF16 = jnp.bfloat16
I32 = jnp.int32
U32 = jnp.uint32

NORM_EPS = 1e-5
QK_EPS = 1e-6
HEAD_DIM = 64
N_Q_HEADS = 8
N_KV_HEADS = 2
FOURIER_GROUPS = 4
FOURIER_CH = 128
FOURIER_WIDTH = FOURIER_GROUPS * FOURIER_CH
ATTN_WIDTH = N_Q_HEADS * HEAD_DIM
KV_WIDTH = N_KV_HEADS * HEAD_DIM
WINDOW = 128
Q_BLOCK = 128
N_BUCKETS = 32
MAX_DISTANCE = 128
TOP_K = 4
SWIGLU_ALPHA = 1.702
SWIGLU_LIMIT = 7.0
MASK_VALUE = -1e30
LOG2E = math.log2(math.e)

LANES = 128
SUBLANES = 8
TOKEN_TILE = 512
RUN_ALIGN = SUBLANES
ROW_GROUPS = 2
PERM_CHUNK = 256
DISPATCH_ROWS = 3 * PERM_CHUNK
EXPERT_ROWS = 512
VMEM_LIMIT = 56 * 1024 * 1024


def _params(n_axes, vmem=None):
    return pltpu.CompilerParams(
        dimension_semantics=("arbitrary",) * n_axes, vmem_limit_bytes=vmem)


def _pair_head_norm(xc, gain, lo):
    x2 = xc * xc
    s_lo = jnp.sum(jnp.where(lo, x2, 0.0), axis=-1, keepdims=True)
    s_hi = jnp.sum(jnp.where(lo, 0.0, x2), axis=-1, keepdims=True)
    inv = jnp.where(lo, lax.rsqrt(s_lo * (1.0 / HEAD_DIM) + QK_EPS),
                    lax.rsqrt(s_hi * (1.0 / HEAD_DIM) + QK_EPS))
    return xc * inv * gain


def _inproj_kernel(x_ref, g1_ref, w_ref, qg_ref, kg_ref, u_ref, q_ref, kv_ref):
    rows = x_ref.shape[0] // ROW_GROUPS
    lo = lax.broadcasted_iota(I32, (rows, LANES), 1) < HEAD_DIM
    q0 = FOURIER_WIDTH
    k0 = q0 + ATTN_WIDTH
    for grp in range(ROW_GROUPS):
        rs = slice(grp * rows, (grp + 1) * rows)
        x = x_ref[rs, :]
        ms = jnp.mean(x * x, axis=-1, keepdims=True)
        h = (x * lax.rsqrt(ms + NORM_EPS) * g1_ref[...]).astype(BF16)
        z = jnp.dot(h, w_ref[...], preferred_element_type=F32)
        u_ref[rs, :] = z[:, :FOURIER_WIDTH].astype(BF16)
        for c in range(ATTN_WIDTH // LANES):
            qc = _pair_head_norm(z[:, q0 + c * LANES:q0 + (c + 1) * LANES], qg_ref[...], lo)
            q_ref[rs, c * LANES:(c + 1) * LANES] = (qc * (HEAD_DIM ** -0.5 * LOG2E)).astype(BF16)
        kc = _pair_head_norm(z[:, k0:k0 + KV_WIDTH], kg_ref[...], lo)
        vc = z[:, k0 + KV_WIDTH:k0 + 2 * KV_WIDTH]
        kv_ref[rs, 0:LANES] = kc.astype(BF16)
        kv_ref[rs, LANES:2 * LANES] = pltpu.roll(kc, HEAD_DIM, 1).astype(BF16)
        kv_ref[rs, 2 * LANES:3 * LANES] = vc.astype(BF16)
        kv_ref[rs, 3 * LANES:4 * LANES] = pltpu.roll(vc, HEAD_DIM, 1).astype(BF16)


def _inproj(x2d, norm1, w_in, q_norm, k_norm):
    t, d = x2d.shape
    tm = min(TOKEN_TILE, t)
    n_in = w_in.shape[1]
    qg = jnp.tile(q_norm, LANES // HEAD_DIM).reshape(1, LANES)
    kg = jnp.tile(k_norm, LANES // HEAD_DIM).reshape(1, LANES)
    full = lambda i: (0, 0)
    return pl.pallas_call(
        _inproj_kernel,
        grid=(t // tm,),
        in_specs=[
            pl.BlockSpec((tm, d), lambda i: (i, 0)),
            pl.BlockSpec((1, d), full),
            pl.BlockSpec((d, n_in), full),
            pl.BlockSpec((1, LANES), full),
            pl.BlockSpec((1, LANES), full),
        ],
        out_specs=[
            pl.BlockSpec((tm, FOURIER_WIDTH), lambda i: (i, 0)),
            pl.BlockSpec((tm, ATTN_WIDTH), lambda i: (i, 0)),
            pl.BlockSpec((tm, 4 * LANES), lambda i: (i, 0)),
        ],
        out_shape=[
            jax.ShapeDtypeStruct((t, FOURIER_WIDTH), BF16),
            jax.ShapeDtypeStruct((t, ATTN_WIDTH), BF16),
            jax.ShapeDtypeStruct((t, 4 * LANES), BF16),
        ],
        compiler_params=_params(1, VMEM_LIMIT),
        name="inproj",
    )(x2d, norm1.reshape(1, d), w_in.astype(BF16), qg, kg)


def _fourier_kernel(u_ref, cs_ref, ss_ref, rev_ref, cc_ref, sc_ref, wf_ref, g_ref, o_ref,
                    p_scr, q_scr, e_scr, *, scale, row_block):
    for g in range(FOURIER_GROUPS):
        sl = slice(g * FOURIER_CH, (g + 1) * FOURIER_CH)
        w = wf_ref[g].astype(BF16)
        a = (jnp.dot(cc_ref[...], w, preferred_element_type=F32) * scale).astype(BF16)
        b = (jnp.dot(sc_ref[...], w, preferred_element_type=F32) * scale).astype(BF16)
        ug = u_ref[:, sl]
        p_scr[:, sl] = jnp.dot(ug, a, preferred_element_type=F32).astype(BF16)
        q_scr[:, sl] = jnp.dot(ug, b, preferred_element_type=F32).astype(BF16)
    half = u_ref.shape[0] // 2
    gain = g_ref[...]

    def norm(y):
        ms = jnp.mean(y * y, axis=-1, keepdims=True)
        return y * lax.rsqrt(ms + NORM_EPS) * gain

    n_blk = half // row_block
    mid = None
    for r in range(n_blk):
        rs = slice(r * row_block, (r + 1) * row_block)
        extra = SUBLANES if r == n_blk - 1 else 0
        c = jnp.dot(cs_ref[r * row_block:(r + 1) * row_block + extra, :], p_scr[...],
                    preferred_element_type=F32)
        d = jnp.dot(ss_ref[rs, :], q_scr[...], preferred_element_type=F32)
        o_ref[rs, :] = norm(c[:row_block] + d).astype(BF16)
        e_scr[rs, :] = norm(c[:row_block] - d).astype(BF16)
        if extra:
            mid = norm(c[row_block:row_block + 1])
    for r in range(n_blk):
        z = jnp.dot(rev_ref[r * row_block:(r + 1) * row_block, :], e_scr[...],
                    preferred_element_type=F32)
        if r == 0:
            z = jnp.where(lax.broadcasted_iota(I32, z.shape, 0) == 0, mid, z)
        o_ref[half + r * row_block:half + (r + 1) * row_block, :] = z.astype(BF16)


def _dft_tables(n):
    k = np.arange(n, dtype=np.int64)
    ang = 2.0 * np.pi * ((k[:, None] * k[None, :]) % n).astype(np.float64) / n
    return np.cos(ang), np.sin(ang)


def _fourier(u, w_fourier, g_out, batch, seq):
    cs, ss = _dft_tables(seq)
    cc, sc = _dft_tables(FOURIER_CH)
    scale = 1.0 / math.sqrt(seq * FOURIER_CH)
    half = seq // 2
    row_block = min(512, half)
    rev = np.zeros((half, half), np.float32)
    rev[np.arange(1, half), half - np.arange(1, half)] = 1.0
    full2 = lambda b: (0, 0)
    return pl.pallas_call(
        functools.partial(_fourier_kernel, scale=scale, row_block=row_block),
        grid=(batch,),
        in_specs=[
            pl.BlockSpec((seq, FOURIER_WIDTH), lambda b: (b, 0)),
            pl.BlockSpec((half + SUBLANES, seq), full2),
            pl.BlockSpec((half, seq), full2),
            pl.BlockSpec((half, half), full2),
            pl.BlockSpec((FOURIER_CH, FOURIER_CH), full2),
            pl.BlockSpec((FOURIER_CH, FOURIER_CH), full2),
            pl.BlockSpec((FOURIER_GROUPS, FOURIER_CH, FOURIER_CH), lambda b: (0, 0, 0)),
            pl.BlockSpec((1, FOURIER_WIDTH), full2),
        ],
        out_specs=pl.BlockSpec((seq, FOURIER_WIDTH), lambda b: (b, 0)),
        out_shape=jax.ShapeDtypeStruct((batch * seq, FOURIER_WIDTH), BF16),
        scratch_shapes=[pltpu.VMEM((seq, FOURIER_WIDTH), BF16),
                        pltpu.VMEM((seq, FOURIER_WIDTH), BF16),
                        pltpu.VMEM((half, FOURIER_WIDTH), BF16)],
        compiler_params=_params(1, VMEM_LIMIT),
        name="fourier",
    )(u, jnp.asarray(cs[:half + SUBLANES], BF16), jnp.asarray(ss[:half], BF16),
      jnp.asarray(rev, BF16), jnp.asarray(cc, BF16), jnp.asarray(-sc, BF16), w_fourier,
      g_out.reshape(1, FOURIER_WIDTH))


def _attn_kernel(sink_ref, q_ref, kvp_ref, kvo_ref, kvn_ref, bias_a_ref, bias_b_ref, g_ref, o_ref,
                 acc_ref):
    kv = jnp.concatenate([kvp_ref[...], kvo_ref[...], kvn_ref[...]], axis=0)
    nk = kv.shape[0]
    lo = lax.broadcasted_iota(I32, (nk, LANES), 1) < HEAD_DIM
    k_a, k_b = kv[:, 0:LANES], kv[:, LANES:2 * LANES]
    v_a, v_b = kv[:, 2 * LANES:3 * LANES], kv[:, 3 * LANES:4 * LANES]
    zero = jnp.zeros_like(k_a)
    k_lo = (jnp.where(lo, k_a, zero), jnp.where(lo, k_b, zero))
    k_hi = (jnp.where(lo, zero, k_b), jnp.where(lo, zero, k_a))
    v_lo = (jnp.where(lo, v_a, zero), jnp.where(lo, v_b, zero))
    v_hi = (jnp.where(lo, zero, v_b), jnp.where(lo, zero, v_a))
    lo_out = lax.broadcasted_iota(I32, (Q_BLOCK, LANES), 1) < HEAD_DIM
    bias_refs = (bias_a_ref, bias_b_ref)
    rows2 = q_ref.shape[0]
    nt = (((1,), (1,)), ((), ()))
    for h in range(N_KV_HEADS):
        qs = jnp.concatenate([q_ref[:, (2 * h) * LANES:(2 * h + 1) * LANES],
                              q_ref[:, (2 * h + 1) * LANES:(2 * h + 2) * LANES]], axis=0)
        s_par = (lax.dot_general(qs, k_lo[h], nt, preferred_element_type=F32),
                 lax.dot_general(qs, k_hi[h], nt, preferred_element_type=F32))
        for sb in range(2):
            keys = slice(sb * Q_BLOCK, sb * Q_BLOCK + 3 * Q_BLOCK)
            vcat = jnp.concatenate([v_lo[h][keys, :], v_hi[h][keys, :]], axis=0)
            for c in range(2):
                r0 = c * rows2 + sb * Q_BLOCK
                probs, invs = [], []
                for par in range(2):
                    hq = 4 * h + 2 * c + par
                    s = s_par[par][r0:r0 + Q_BLOCK, keys] + bias_refs[sb][hq]
                    sink = sink_ref[hq]
                    m = jnp.maximum(jnp.max(s, axis=-1, keepdims=True), sink)
                    p = jnp.exp2(s - m)
                    denom = jnp.sum(p, axis=-1, keepdims=True) + jnp.exp2(sink - m)
                    probs.append(p.astype(BF16))
                    invs.append(1.0 / denom)
                pcat = jnp.concatenate(probs, axis=1)
                chunk = 2 * h + c
                o = jnp.dot(pcat, vcat, preferred_element_type=F32)
                acc_ref[sb * Q_BLOCK:(sb + 1) * Q_BLOCK, chunk * LANES:(chunk + 1) * LANES] = (
                    o * jnp.where(lo_out, invs[0], invs[1]))
    y = acc_ref[...]
    ms = jnp.mean(y * y, axis=-1, keepdims=True)
    o_ref[...] = (y * lax.rsqrt(ms + NORM_EPS) * g_ref[...]).astype(BF16)


def _t5_bucket(rel):
    nb = N_BUCKETS // 2
    max_exact = nb // 2
    ret = (rel > 0).astype(jnp.int32) * nb
    n = jnp.abs(rel)
    nf = jnp.maximum(n, 1).astype(jnp.float32)
    large = max_exact + (jnp.log(nf / max_exact) / math.log(MAX_DISTANCE / max_exact)
                         * (nb - max_exact)).astype(jnp.int32)
    large = jnp.minimum(large, nb - 1)
    return ret + jnp.where(n < max_exact, n, large)


def _attention(q, kv, sinks, rel_bias, g_out, batch, seq):
    nb = seq // Q_BLOCK
    assert nb % 2 == 0
    nb2 = nb // 2
    qi = jnp.arange(Q_BLOCK, dtype=jnp.int32)[:, None]
    kj = jnp.arange(3 * Q_BLOCK, dtype=jnp.int32)[None, :]
    rel = kj - Q_BLOCK - qi
    period = 4 * Q_BLOCK
    p = jnp.arange(period, dtype=jnp.int32)
    off = jnp.where(p < 3 * Q_BLOCK, p, p - period) - Q_BLOCK
    hit = _t5_bucket(off)[None, :, None] == jnp.arange(N_BUCKETS, dtype=jnp.int32)
    by_off = jnp.sum(jnp.where(hit, rel_bias.astype(F32).T[:, None, :], 0.0), axis=-1)
    bias = jnp.tile(by_off, (1, Q_BLOCK))[:, :Q_BLOCK * (period - 1)]
    bias = bias.reshape(N_Q_HEADS, Q_BLOCK, period - 1)[:, :, :3 * Q_BLOCK]
    band = jnp.abs(rel) <= WINDOW
    first = band & (kj >= Q_BLOCK)
    last = band & (kj < 2 * Q_BLOCK)
    table = jnp.stack([jnp.where(msk[None], bias * LOG2E, MASK_VALUE) for msk in (first, band, last)])
    q_rows = 2 * Q_BLOCK
    grid_spec = pltpu.PrefetchScalarGridSpec(
        num_scalar_prefetch=1,
        grid=(batch, nb2),
        in_specs=[
            pl.BlockSpec((q_rows, ATTN_WIDTH), lambda b, i, s: (b * nb2 + i, 0)),
            pl.BlockSpec((Q_BLOCK, 4 * LANES),
                         lambda b, i, s: (b * nb + jnp.maximum(2 * i - 1, 0), 0)),
            pl.BlockSpec((q_rows, 4 * LANES), lambda b, i, s: (b * nb2 + i, 0)),
            pl.BlockSpec((Q_BLOCK, 4 * LANES),
                         lambda b, i, s: (b * nb + jnp.minimum(2 * i + 2, nb - 1), 0)),
            pl.BlockSpec((None, N_Q_HEADS, Q_BLOCK, 3 * Q_BLOCK),
                         lambda b, i, s: (jnp.where(i == 0, 0, 1), 0, 0, 0)),
            pl.BlockSpec((None, N_Q_HEADS, Q_BLOCK, 3 * Q_BLOCK),
                         lambda b, i, s: (jnp.where(i == nb2 - 1, 2, 1), 0, 0, 0)),
            pl.BlockSpec((1, ATTN_WIDTH), lambda b, i, s: (0, 0)),
        ],
        out_specs=pl.BlockSpec((q_rows, ATTN_WIDTH), lambda b, i, s: (b * nb2 + i, 0)),
        scratch_shapes=[pltpu.VMEM((q_rows, ATTN_WIDTH), F32)],
    )
    return pl.pallas_call(
        _attn_kernel,
        grid_spec=grid_spec,
        out_shape=jax.ShapeDtypeStruct((batch * seq, ATTN_WIDTH), BF16),
        compiler_params=_params(2, VMEM_LIMIT),
        name="attention",
    )(sinks.astype(F32) * LOG2E, q, kv, kv, kv, table, table, g_out.reshape(1, ATTN_WIDTH))


def _outproj_kernel(yf_ref, ya_ref, x_ref, wo_ref, g2_ref, wrt_ref, brt_ref, triu_ref, scan_ref,
                    x1_ref, h2_ref, post_ref, gatet_ref, cnt_ref, *, n_experts):
    half = yf_ref.shape[1]
    mix = (jnp.dot(yf_ref[...], wo_ref[:half, :], preferred_element_type=F32)
           + jnp.dot(ya_ref[...], wo_ref[half:, :], preferred_element_type=F32))
    x1 = x_ref[...] + mix
    x1_ref[...] = x1
    ms = jnp.mean(x1 * x1, axis=-1, keepdims=True)
    h2 = x1 * lax.rsqrt(ms + NORM_EPS) * g2_ref[...]
    h2_ref[...] = h2.astype(BF16)
    h_hi = h2.astype(BF16)
    h_lo = (h2 - h_hi.astype(F32)).astype(BF16)
    nt = (((1,), (1,)), ((), ()))
    t1 = lax.dot_general(wrt_ref[...], h_hi, nt, preferred_element_type=F32)
    t2 = lax.dot_general(wrt_ref[:n_experts, :], h_lo, nt, preferred_element_type=F32)
    logits = t1[:n_experts] + t1[n_experts:] + t2 + brt_ref[...]
    tm = logits.shape[1]
    sub_e = lax.broadcasted_iota(I32, (n_experts, tm), 0).astype(F32)
    work = logits
    vals, idxs = [], []
    for _ in range(TOP_K):
        m = jnp.max(work, axis=0, keepdims=True)
        ik = jnp.min(jnp.where(work == m, sub_e, float(n_experts)), axis=0, keepdims=True)
        work = jnp.where(sub_e == ik, -jnp.inf, work)
        vals.append(m)
        idxs.append(ik)
    exps = [jnp.exp(v - vals[0]) for v in vals]
    inv = 1.0 / (exps[0] + exps[1] + exps[2] + exps[3])
    gates = [e * inv for e in exps]

    sub = lax.broadcasted_iota(I32, (LANES, tm), 0).astype(F32)
    onehot = jnp.zeros((LANES, tm), F32)
    for k in range(TOP_K):
        onehot = onehot + jnp.where(sub == idxs[k] + float(k * n_experts), 1.0, 0.0)
    onehot_b = onehot.astype(BF16)
    before = jnp.dot(onehot_b, triu_ref[...], preferred_element_type=F32)
    counts = lax.dot_general(jnp.ones((SUBLANES, tm), BF16), onehot_b, nt,
                             preferred_element_type=F32)[0:1, :]
    total = counts
    for k in range(1, TOP_K):
        total = total + pltpu.roll(counts, k * n_experts, 1)
    cnt_ref[0] = total.astype(I32)
    colsum = jnp.sum(onehot, axis=1, keepdims=True)
    blocks = [colsum[k * n_experts:(k + 1) * n_experts] for k in range(TOP_K)]
    total_e = blocks[0] + blocks[1] + blocks[2] + blocks[3]
    run_e = jnp.floor((total_e + (RUN_ALIGN - 1)) * (1.0 / RUN_ALIGN)) * RUN_ALIGN
    run_start = jnp.dot(scan_ref[...], jnp.broadcast_to(run_e, (n_experts, LANES)).astype(BF16),
                        preferred_element_type=F32)[:, 0:1]
    adds, acc = [], run_start
    for k in range(TOP_K):
        adds.append(acc)
        acc = acc + blocks[k]
    placed = (before + jnp.concatenate(adds, axis=0)) * onehot
    pos = [jnp.sum(placed[k * n_experts:(k + 1) * n_experts], axis=0, keepdims=True)
           for k in range(TOP_K)]
    post_ref[...] = jnp.concatenate(pos + [jnp.zeros((SUBLANES - TOP_K, tm), F32)], axis=0).astype(I32)
    gatet_ref[...] = jnp.concatenate(gates + [jnp.zeros((SUBLANES - TOP_K, tm), F32)], axis=0)


def _outproj(yf, ya, x2d, w_out, norm2, w_router, b_router):
    t, d = x2d.shape
    tm = min(TOKEN_TILE, t)
    n_tiles = t // tm
    n_experts = w_router.shape[1]
    assert TOP_K * n_experts == LANES
    triu = np.triu(np.ones((tm, tm), np.float32), 1)
    scan = np.tril(np.ones((n_experts, n_experts), np.float32), -1)
    wr_hi = w_router.astype(BF16)
    wr_lo = (w_router - wr_hi.astype(F32)).astype(BF16)
    wrt = jnp.concatenate([wr_hi.T, wr_lo.T], axis=0)
    full = lambda i: (0, 0)
    row = lambda i: (i, 0)
    return pl.pallas_call(
        functools.partial(_outproj_kernel, n_experts=n_experts),
        grid=(n_tiles,),
        in_specs=[
            pl.BlockSpec((tm, yf.shape[1]), row),
            pl.BlockSpec((tm, ya.shape[1]), row),
            pl.BlockSpec((tm, d), row),
            pl.BlockSpec((w_out.shape[0], d), full),
            pl.BlockSpec((1, d), full),
            pl.BlockSpec((2 * n_experts, d), full),
            pl.BlockSpec((n_experts, 1), full),
            pl.BlockSpec((tm, tm), full),
            pl.BlockSpec((n_experts, n_experts), full),
        ],
        out_specs=[
            pl.BlockSpec((tm, d), row),
            pl.BlockSpec((tm, d), row),
            pl.BlockSpec((SUBLANES, tm), row),
            pl.BlockSpec((SUBLANES, tm), row),
            pl.BlockSpec((1, 1, LANES), lambda i: (i, 0, 0)),
        ],
        out_shape=[
            jax.ShapeDtypeStruct((t, d), F32),
            jax.ShapeDtypeStruct((t, d), BF16),
            jax.ShapeDtypeStruct((n_tiles * SUBLANES, tm), I32),
            jax.ShapeDtypeStruct((n_tiles * SUBLANES, tm), F32),
            jax.ShapeDtypeStruct((n_tiles, 1, LANES), I32),
        ],
        compiler_params=_params(1, VMEM_LIMIT),
        name="outproj_router",
    )(yf, ya, x2d, w_out.astype(BF16), norm2.reshape(1, d), wrt, b_router.reshape(n_experts, 1),
      jnp.asarray(triu, BF16), jnp.asarray(scan, BF16))


def _pack_pairs(x, is_bf16_exact=False):
    half = x.shape[1] // 2
    a, b = x[:, :half], x[:, half:]
    if not is_bf16_exact:
        a, b = a.astype(BF16).astype(F32), b.astype(BF16).astype(F32)
    return lax.bitcast_convert_type(a, U32) | (lax.bitcast_convert_type(b, U32) >> 16)


def _unpack_pairs(w):
    hi = lax.bitcast_convert_type(w & U32(0xFFFF0000), F32)
    lo = lax.bitcast_convert_type(w << 16, F32)
    return hi.astype(BF16), lo.astype(BF16)


def _rows(start, size):
    if not isinstance(size, int):
        size = pl.multiple_of(size, RUN_ALIGN)
    return pl.ds(pl.multiple_of(start, RUN_ALIGN), size)


def _dispatch_kernel(cnt_ref, lst_ref, base_ref, rows_ref, tail_ref, post_ref, h2_ref, xs_ref,
                     buf, zbuf, sem, zsem, *, n_experts):
    j = pl.program_id(0)
    tm = h2_ref.shape[0]
    n_local = buf.shape[1]

    def start_runs(tile, slot):
        def run(e, carry):
            r = tile * n_experts + e
            n = cnt_ref[r]

            @pl.when(n > 0)
            def _():
                pltpu.make_async_copy(buf.at[slot, _rows(lst_ref[r], n), :],
                                      xs_ref.at[_rows(base_ref[r], n), :], sem.at[slot]).start()
            return carry
        lax.fori_loop(0, n_experts, run, 0)

    def wait_runs(tile, slot):
        n = rows_ref[tile]

        @pl.when(n > 0)
        def _():
            pltpu.make_async_copy(buf.at[slot, _rows(0, n), :], xs_ref.at[_rows(0, n), :],
                                  sem.at[slot]).wait()

    def zero_fill(op):
        def tail(e, carry):
            n = tail_ref[n_experts + e]

            @pl.when(n > 0)
            def _():
                getattr(pltpu.make_async_copy(zbuf.at[_rows(0, n), :],
                                              xs_ref.at[_rows(tail_ref[e], n), :], zsem), op)()
            return carry
        lax.fori_loop(0, n_experts, tail, 0)

        def spare(b, carry):
            getattr(pltpu.make_async_copy(zbuf, xs_ref.at[_rows(b * EXPERT_ROWS, EXPERT_ROWS), :],
                                          zsem), op)()
            return carry
        lax.fori_loop(tail_ref[2 * n_experts], xs_ref.shape[0] // EXPERT_ROWS, spare, 0)

    slot = j % 2
    @pl.when(j >= 2)
    def _():
        wait_runs(j - 2, slot)

    @pl.when(j == 0)
    def _():
        zbuf[...] = jnp.zeros_like(zbuf)
        zero_fill("start")
        zero_fill("wait")

    h = h2_ref[...]
    post = post_ref[0:TOP_K, :]
    chunk_of = lax.shift_right_logical(post, PERM_CHUNK.bit_length() - 1)
    offset = (post & (PERM_CHUNK - 1)).astype(F32)
    rows = lax.broadcasted_iota(I32, (PERM_CHUNK, tm), 0).astype(F32).astype(BF16)
    one = jnp.ones((PERM_CHUNK, tm), BF16)
    per = DISPATCH_ROWS // PERM_CHUNK
    for mc in range(n_local // DISPATCH_ROWS):
        parts = []
        for rc in range(mc * per, (mc + 1) * per):
            off = jnp.where(chunk_of == rc, offset, -1.0).astype(BF16)
            perm = jnp.zeros((PERM_CHUNK, tm), BF16)
            for k in range(TOP_K):
                perm = jnp.where(rows == off[k:k + 1, :], one, perm)
            parts.append(perm)
        rs = slice(mc * DISPATCH_ROWS, (mc + 1) * DISPATCH_ROWS)
        buf[slot, rs, :] = _pack_pairs(
            jnp.dot(jnp.concatenate(parts, axis=0), h, preferred_element_type=F32), True)

    start_runs(j, slot)

    @pl.when(j == pl.num_programs(0) - 1)
    def _():
        @pl.when(j >= 1)
        def _():
            wait_runs(j - 1, 1 - slot)
        wait_runs(j, slot)


def _local_rows(tm, n_experts):
    worst = TOP_K * tm + n_experts * (RUN_ALIGN - 1)
    return -(-worst // DISPATCH_ROWS) * DISPATCH_ROWS


def _dispatch(plan, post, h2, n_rows, n_experts):
    t, d = h2.shape
    tm = min(TOKEN_TILE, t)
    grid_spec = pltpu.PrefetchScalarGridSpec(
        num_scalar_prefetch=5,
        grid=(t // tm,),
        in_specs=[
            pl.BlockSpec((SUBLANES, tm), lambda i, *_: (i, 0)),
            pl.BlockSpec((tm, d), lambda i, *_: (i, 0)),
        ],
        out_specs=pl.BlockSpec(memory_space=pl.ANY),
        scratch_shapes=[pltpu.VMEM((2, _local_rows(tm, n_experts), d // 2), U32),
                        pltpu.VMEM((EXPERT_ROWS, d // 2), U32),
                        pltpu.SemaphoreType.DMA((2,)), pltpu.SemaphoreType.DMA(())],
    )
    return pl.pallas_call(
        functools.partial(_dispatch_kernel, n_experts=n_experts),
        grid_spec=grid_spec,
        out_shape=jax.ShapeDtypeStruct((n_rows, d // 2), U32),
        compiler_params=_params(1, VMEM_LIMIT),
        name="dispatch",
    )(plan["cnt"], plan["lst"], plan["base"], plan["rows"], plan["tail"], post, h2)


def _combine_kernel(cnt_ref, lst_ref, base_ref, rows_ref, post_ref, gatet_ref, x1_ref, ys_ref, o_ref,
                    buf, g_scr, y_scr, sem, *, n_experts):
    j = pl.program_id(0)
    tm, d = x1_ref.shape
    n_local = buf.shape[1]

    def start_runs(tile, slot):
        def run(e, carry):
            r = tile * n_experts + e
            n = cnt_ref[r]

            @pl.when(n > 0)
            def _():
                pltpu.make_async_copy(ys_ref.at[_rows(base_ref[r], n), :],
                                      buf.at[slot, _rows(lst_ref[r], n), :], sem.at[slot]).start()
            return carry
        lax.fori_loop(0, n_experts, run, 0)

    def wait_runs(tile, slot):
        n = rows_ref[tile]

        @pl.when(n > 0)
        def _():
            pltpu.make_async_copy(ys_ref.at[_rows(0, n), :], buf.at[slot, _rows(0, n), :],
                                  sem.at[slot]).wait()

    slot = j % 2
    @pl.when(j == 0)
    def _():
        buf[...] = jnp.zeros_like(buf)
        start_runs(j, slot)

    @pl.when(j + 1 < pl.num_programs(0))
    def _():
        start_runs(j + 1, 1 - slot)

    wait_runs(j, slot)

    post = post_ref[0:TOP_K, :]
    gate = gatet_ref[0:TOP_K, :].astype(BF16)
    chunk_of = lax.shift_right_logical(post, PERM_CHUNK.bit_length() - 1)
    offset = (post & (PERM_CHUNK - 1)).astype(F32)
    rows = lax.broadcasted_iota(I32, (PERM_CHUNK, tm), 0).astype(F32).astype(BF16)
    for rc in range(n_local // PERM_CHUNK):
        chunk = slice(rc * PERM_CHUNK, (rc + 1) * PERM_CHUNK)
        off = jnp.where(chunk_of == rc, offset, -1.0).astype(BF16)
        g = jnp.zeros((PERM_CHUNK, tm), BF16)
        for k in range(TOP_K):
            g = jnp.where(rows == off[k:k + 1, :], jnp.broadcast_to(gate[k:k + 1, :], g.shape), g)
        g_scr[chunk, :] = g
        y_scr[chunk, :d // 2], y_scr[chunk, d // 2:] = _unpack_pairs(buf[slot, chunk, :])
    o_ref[...] = x1_ref[...] + lax.dot_general(g_scr[...], y_scr[...], (((0,), (0,)), ((), ())),
                                               preferred_element_type=F32)


def _combine(plan, post, gatet, x1, ys, n_experts):
    t, d = x1.shape
    tm = min(TOKEN_TILE, t)
    grid_spec = pltpu.PrefetchScalarGridSpec(
        num_scalar_prefetch=4,
        grid=(t // tm,),
        in_specs=[
            pl.BlockSpec((SUBLANES, tm), lambda i, *_: (i, 0)),
            pl.BlockSpec((SUBLANES, tm), lambda i, *_: (i, 0)),
            pl.BlockSpec((tm, d), lambda i, *_: (i, 0)),
            pl.BlockSpec(memory_space=pl.ANY),
        ],
        out_specs=pl.BlockSpec((tm, d), lambda i, *_: (i, 0)),
        scratch_shapes=[pltpu.VMEM((2, _local_rows(tm, n_experts), d // 2), U32),
                        pltpu.VMEM((_local_rows(tm, n_experts), tm), BF16),
                        pltpu.VMEM((_local_rows(tm, n_experts), d), BF16),
                        pltpu.SemaphoreType.DMA((2,))],
    )
    return pl.pallas_call(
        functools.partial(_combine_kernel, n_experts=n_experts),
        grid_spec=grid_spec,
        out_shape=jax.ShapeDtypeStruct((t, d), F32),
        compiler_params=_params(1, VMEM_LIMIT),
        name="combine",
    )(plan["cnt"], plan["lst"], plan["base"], plan["rows"], post, gatet, x1, ys)


def _expert_kernel(nblk_ref, row0_ref, meta_ref, xs_ref, wgu_ref, bg_ref, bu_ref, wd_ref, bd_ref,
                   perm_ref, ys_ref, wg_s, wu_s, wd_s, xbuf, ybuf, xsem, ysem):
    e = pl.program_id(0)
    nb = nblk_ref[e]
    row0 = row0_ref[e]
    bm = xbuf.shape[1]

    def x_copy(blk, slot):
        return pltpu.make_async_copy(xs_ref.at[_rows(row0 + blk * bm, bm), :], xbuf.at[slot],
                                     xsem.at[slot])

    def y_copy(blk, slot):
        return pltpu.make_async_copy(ybuf.at[slot], ys_ref.at[_rows(row0 + blk * bm, bm), :],
                                     ysem.at[slot])

    @pl.when(nb > 0)
    def _():
        x_copy(0, 0).start()
        width = perm_ref.shape[0]
        for c in range(wgu_ref.shape[2] // width):
            wc = wgu_ref[0, :, c * width:(c + 1) * width].astype(BF16)
            r = jnp.dot(wc, perm_ref[...], preferred_element_type=F32)
            wg_s[:, c * LANES:(c + 1) * LANES] = r[:, :LANES].astype(BF16)
            wu_s[:, c * LANES:(c + 1) * LANES] = r[:, LANES:].astype(BF16)
        wd_s[...] = wd_ref[0].astype(BF16)

        def block(blk, carry):
            slot = blk % 2
            x_copy(blk, slot).wait()

            @pl.when(blk + 1 < nb)
            def _():
                x_copy(blk + 1, 1 - slot).start()

            @pl.when(blk >= 2)
            def _():
                y_copy(blk - 2, slot).wait()

            xb = jnp.concatenate(_unpack_pairs(xbuf[slot]), axis=1)
            g = jnp.dot(xb, wg_s[...], preferred_element_type=F32) + bg_ref[0]
            up = jnp.dot(xb, wu_s[...], preferred_element_type=F32) + bu_ref[0]
            g = jnp.minimum(g, SWIGLU_LIMIT)
            up = jnp.clip(up, -SWIGLU_LIMIT, SWIGLU_LIMIT)
            act = g * (1.0 / (1.0 + jnp.exp(-SWIGLU_ALPHA * g))) * (up + 1.0)
            ybuf[slot] = _pack_pairs(
                jnp.dot(act.astype(BF16), wd_s[...], preferred_element_type=F32) + bd_ref[0])
            y_copy(blk, slot).start()
            return carry

        lax.fori_loop(0, nb, block, 0)

        @pl.when(nb >= 2)
        def _():
            y_copy(nb - 2, nb % 2).wait()
        y_copy(nb - 1, (nb - 1) % 2).wait()

    @pl.when(e == pl.num_programs(0) - 1)
    def _():
        ybuf[0] = jnp.zeros_like(ybuf[0])

        def spare(blk, carry):
            cp = pltpu.make_async_copy(ybuf.at[0], ys_ref.at[_rows(blk * bm, bm), :], ysem.at[0])
            cp.start()
            cp.wait()
            return carry
        lax.fori_loop(meta_ref[0], ys_ref.shape[0] // bm, spare, 0)


def _experts(plan, meta, xs, w_gate_up, b_gate_up, w_down, b_down):
    n_rows = xs.shape[0]
    n_experts, d, f2 = w_gate_up.shape
    f = f2 // 2
    bm = EXPERT_ROWS
    width = 2 * LANES
    perm = np.zeros((width, width), np.float32)
    perm[2 * np.arange(LANES), np.arange(LANES)] = 1.0
    perm[2 * np.arange(LANES) + 1, LANES + np.arange(LANES)] = 1.0
    bg = b_gate_up[:, 0::2].reshape(n_experts, 1, f)
    bu = b_gate_up[:, 1::2].reshape(n_experts, 1, f)
    per_e = lambda e, *_: (e, 0, 0)
    grid_spec = pltpu.PrefetchScalarGridSpec(
        num_scalar_prefetch=3,
        grid=(n_experts,),
        in_specs=[
            pl.BlockSpec(memory_space=pl.ANY),
            pl.BlockSpec((1, d, f2), per_e),
            pl.BlockSpec((1, 1, f), per_e),
            pl.BlockSpec((1, 1, f), per_e),
            pl.BlockSpec((1, f, d), per_e),
            pl.BlockSpec((1, 1, d), per_e),
            pl.BlockSpec((width, width), lambda e, *_: (0, 0)),
        ],
        out_specs=pl.BlockSpec(memory_space=pl.ANY),
        scratch_shapes=[pltpu.VMEM((d, f), BF16), pltpu.VMEM((d, f), BF16),
                        pltpu.VMEM((f, d), BF16),
                        pltpu.VMEM((2, bm, d // 2), U32), pltpu.VMEM((2, bm, d // 2), U32),
                        pltpu.SemaphoreType.DMA((2,)), pltpu.SemaphoreType.DMA((2,))],
    )
    return pl.pallas_call(
        _expert_kernel,
        grid_spec=grid_spec,
        out_shape=jax.ShapeDtypeStruct((n_rows, d // 2), U32),
        compiler_params=_params(1, VMEM_LIMIT),
        name="experts",
    )(plan["nblk"], plan["row0"], meta, xs, w_gate_up, bg, bu, w_down,
      b_down.reshape(n_experts, 1, d), jnp.asarray(perm, BF16))


def _routing_plan(counts, n_experts, bm):
    cnt = counts[:, 0, :n_experts]
    run = (cnt + RUN_ALIGN - 1) // RUN_ALIGN * RUN_ALIGN
    per_expert = jnp.sum(run, axis=0)
    padded = (per_expert + bm - 1) // bm * bm
    pend = jnp.cumsum(padded)
    pstart = pend - padded
    base = pstart[None, :] + jnp.cumsum(run, axis=0) - run
    lst = jnp.cumsum(run, axis=1) - run
    n_used = pend[-1] // bm
    tail = jnp.concatenate([pstart + per_expert, padded - per_expert, n_used[None]])
    plan = {"cnt": run.reshape(-1).astype(I32), "lst": lst.reshape(-1).astype(I32),
            "base": base.reshape(-1).astype(I32), "rows": jnp.sum(run, axis=1).astype(I32),
            "tail": tail.astype(I32), "nblk": (padded // bm).astype(I32), "row0": pstart.astype(I32)}
    return plan, n_used.astype(I32).reshape(1)


def _layer(x2d, batch, seq, norm1, w_in, q_norm, k_norm, sinks, rel_bias, w_fourier, g_fourier_out,
           g_attn_out, w_out, norm2, w_router, b_router, w_gate_up, b_gate_up, w_down, b_down):
    t, d = x2d.shape
    n_experts = w_router.shape[1]
    u, q, kv = _inproj(x2d, norm1, w_in, q_norm, k_norm)
    yf = _fourier(u, w_fourier, g_fourier_out, batch, seq)
    ya = _attention(q, kv, sinks, rel_bias, g_attn_out, batch, seq)
    x1, h2, post, gatet, counts = _outproj(yf, ya, x2d, w_out, norm2, w_router, b_router)
    bm = EXPERT_ROWS
    n_tiles = t // min(TOKEN_TILE, t)
    worst_rows = t * TOP_K + n_tiles * n_experts * (RUN_ALIGN - 1) + n_experts * (bm - RUN_ALIGN)
    n_blocks = -(-worst_rows // bm)
    plan, meta = _routing_plan(counts, n_experts, bm)
    xs = _dispatch(plan, post, h2, n_blocks * bm, n_experts)
    ys = _experts(plan, meta, xs, w_gate_up, b_gate_up, w_down, b_down)
    return _combine(plan, post, gatet, x1, ys, n_experts)


def kernel(x, norm1, w_in, q_norm, k_norm, sinks, rel_bias, w_fourier, g_fourier_out, g_attn_out,
           w_out, norm2, w_router, b_router, w_gate_up, b_gate_up, w_down, b_down):
    b, s, d = x.shape
    x2d = x.reshape(b * s, d)
    for l in range(norm1.shape[0]):
        x2d = _layer(x2d, b, s, norm1[l], w_in[l], q_norm[l], k_norm[l], sinks[l], rel_bias,
                     w_fourier[l], g_fourier_out[l], g_attn_out[l], w_out[l], norm2[l],
                     w_router[l], b_router[l], w_gate_up[l], b_gate_up[l], w_down[l], b_down[l])
    return x2d.reshape(b, s, d)
```

```python
import functools
import math

import jax
import jax.numpy as jnp
import numpy as np
from jax import lax
from jax.experimental import pallas as pl
from jax.experimental.pallas import tpu as pltpu

F32 = jnp.float32
BF16 = jnp.bfloat16
I32 = jnp.int32
U32 = jnp.uint32

NORM_EPS = 1e-5
QK_EPS = 1e-6
HEAD_DIM = 64
N_Q_HEADS = 8
N_KV_HEADS = 2
FOURIER_GROUPS = 4
FOURIER_CH = 128
FOURIER_WIDTH = FOURIER_GROUPS * FOURIER_CH
ATTN_WIDTH = N_Q_HEADS * HEAD_DIM
KV_WIDTH = N_KV_HEADS * HEAD_DIM
WINDOW = 128
Q_BLOCK = 128
N_BUCKETS = 32
MAX_DISTANCE = 128
TOP_K = 4
SWIGLU_ALPHA = 1.702
SWIGLU_LIMIT = 7.0
MASK_VALUE = -1e30
LOG2E = math.log2(math.e)

LANES = 128
SUBLANES = 8
TOKEN_TILE = 512
RUN_ALIGN = SUBLANES
ROW_GROUPS = 2
PERM_CHUNK = 256
DISPATCH_ROWS = 3 * PERM_CHUNK
EXPERT_ROWS = 512
VMEM_LIMIT = 56 * 1024 * 1024


def _params(n_axes, vmem=None):
    return pltpu.CompilerParams(
        dimension_semantics=("arbitrary",) * n_axes, vmem_limit_bytes=vmem)


def _pair_head_norm(xc, gain, lo):
    x2 = xc * xc
    s_lo = jnp.sum(jnp.where(lo, x2, 0.0), axis=-1, keepdims=True)
    s_hi = jnp.sum(jnp.where(lo, 0.0, x2), axis=-1, keepdims=True)
    inv = jnp.where(lo, lax.rsqrt(s_lo * (1.0 / HEAD_DIM) + QK_EPS),
                    lax.rsqrt(s_hi * (1.0 / HEAD_DIM) + QK_EPS))
    return xc * inv * gain


def _inproj_kernel(x_ref, g1_ref, w_ref, qg_ref, kg_ref, u_ref, q_ref, kv_ref):
    rows = x_ref.shape[0] // ROW_GROUPS
    lo = lax.broadcasted_iota(I32, (rows, LANES), 1) < HEAD_DIM
    q0 = FOURIER_WIDTH
    k0 = q0 + ATTN_WIDTH
    for grp in range(ROW_GROUPS):
        rs = slice(grp * rows, (grp + 1) * rows)
        x = x_ref[rs, :]
        ms = jnp.mean(x * x, axis=-1, keepdims=True)
        h = (x * lax.rsqrt(ms + NORM_EPS) * g1_ref[...]).astype(BF16)
        z = jnp.dot(h, w_ref[...], preferred_element_type=F32)
        u_ref[rs, :] = z[:, :FOURIER_WIDTH].astype(BF16)
        for c in range(ATTN_WIDTH // LANES):
            qc = _pair_head_norm(z[:, q0 + c * LANES:q0 + (c + 1) * LANES], qg_ref[...], lo)
            q_ref[rs, c * LANES:(c + 1) * LANES] = (qc * (HEAD_DIM ** -0.5 * LOG2E)).astype(BF16)
        kc = _pair_head_norm(z[:, k0:k0 + KV_WIDTH], kg_ref[...], lo)
        vc = z[:, k0 + KV_WIDTH:k0 + 2 * KV_WIDTH]
        kv_ref[rs, 0:LANES] = kc.astype(BF16)
        kv_ref[rs, LANES:2 * LANES] = pltpu.roll(kc, HEAD_DIM, 1).astype(BF16)
        kv_ref[rs, 2 * LANES:3 * LANES] = vc.astype(BF16)
        kv_ref[rs, 3 * LANES:4 * LANES] = pltpu.roll(vc, HEAD_DIM, 1).astype(BF16)


def _inproj(x2d, norm1, w_in, q_norm, k_norm):
    t, d = x2d.shape
    tm = min(TOKEN_TILE, t)
    n_in = w_in.shape[1]
    qg = jnp.tile(q_norm, LANES // HEAD_DIM).reshape(1, LANES)
    kg = jnp.tile(k_norm, LANES // HEAD_DIM).reshape(1, LANES)
    full = lambda i: (0, 0)
    return pl.pallas_call(
        _inproj_kernel,
        grid=(t // tm,),
        in_specs=[
            pl.BlockSpec((tm, d), lambda i: (i, 0)),
            pl.BlockSpec((1, d), full),
            pl.BlockSpec((d, n_in), full),
            pl.BlockSpec((1, LANES), full),
            pl.BlockSpec((1, LANES), full),
        ],
        out_specs=[
            pl.BlockSpec((tm, FOURIER_WIDTH), lambda i: (i, 0)),
            pl.BlockSpec((tm, ATTN_WIDTH), lambda i: (i, 0)),
            pl.BlockSpec((tm, 4 * LANES), lambda i: (i, 0)),
        ],
        out_shape=[
            jax.ShapeDtypeStruct((t, FOURIER_WIDTH), BF16),
            jax.ShapeDtypeStruct((t, ATTN_WIDTH), BF16),
            jax.ShapeDtypeStruct((t, 4 * LANES), BF16),
        ],
        compiler_params=_params(1, VMEM_LIMIT),
        name="inproj",
    )(x2d, norm1.reshape(1, d), w_in.astype(BF16), qg, kg)


def _fourier_kernel(u_ref, cs_ref, ss_ref, rev_ref, cc_ref, sc_ref, wf_ref, g_ref, o_ref,
                    p_scr, q_scr, e_scr, *, scale, row_block):
    for g in range(FOURIER_GROUPS):
        sl = slice(g * FOURIER_CH, (g + 1) * FOURIER_CH)
        w = wf_ref[g].astype(BF16)
        a = (jnp.dot(cc_ref[...], w, preferred_element_type=F32) * scale).astype(BF16)
        b = (jnp.dot(sc_ref[...], w, preferred_element_type=F32) * scale).astype(BF16)
        ug = u_ref[:, sl]
        p_scr[:, sl] = jnp.dot(ug, a, preferred_element_type=F32).astype(BF16)
        q_scr[:, sl] = jnp.dot(ug, b, preferred_element_type=F32).astype(BF16)
    half = u_ref.shape[0] // 2
    gain = g_ref[...]

    def norm(y):
        ms = jnp.mean(y * y, axis=-1, keepdims=True)
        return y * lax.rsqrt(ms + NORM_EPS) * gain

    n_blk = half // row_block
    mid = None
    for r in range(n_blk):
        rs = slice(r * row_block, (r + 1) * row_block)
        extra = SUBLANES if r == n_blk - 1 else 0
        c = jnp.dot(cs_ref[r * row_block:(r + 1) * row_block + extra, :], p_scr[...],
                    preferred_element_type=F32)
        d = jnp.dot(ss_ref[rs, :], q_scr[...], preferred_element_type=F32)
        o_ref[rs, :] = norm(c[:row_block] + d).astype(BF16)
        e_scr[rs, :] = norm(c[:row_block] - d).astype(BF16)
        if extra:
            mid = norm(c[row_block:row_block + 1])
    for r in range(n_blk):
        z = jnp.dot(rev_ref[r * row_block:(r + 1) * row_block, :], e_scr[...],
                    preferred_element_type=F32)
        if r == 0:
            z = jnp.where(lax.broadcasted_iota(I32, z.shape, 0) == 0, mid, z)
        o_ref[half + r * row_block:half + (r + 1) * row_block, :] = z.astype(BF16)


def _dft_tables(n):
    k = np.arange(n, dtype=np.int64)
    ang = 2.0 * np.pi * ((k[:, None] * k[None, :]) % n).astype(np.float64) / n
    return np.cos(ang), np.sin(ang)


def _fourier(u, w_fourier, g_out, batch, seq):
    cs, ss = _dft_tables(seq)
    cc, sc = _dft_tables(FOURIER_CH)
    scale = 1.0 / math.sqrt(seq * FOURIER_CH)
    half = seq // 2
    row_block = min(512, half)
    rev = np.zeros((half, half), np.float32)
    rev[np.arange(1, half), half - np.arange(1, half)] = 1.0
    full2 = lambda b: (0, 0)
    return pl.pallas_call(
        functools.partial(_fourier_kernel, scale=scale, row_block=row_block),
        grid=(batch,),
        in_specs=[
            pl.BlockSpec((seq, FOURIER_WIDTH), lambda b: (b, 0)),
            pl.BlockSpec((half + SUBLANES, seq), full2),
            pl.BlockSpec((half, seq), full2),
            pl.BlockSpec((half, half), full2),
            pl.BlockSpec((FOURIER_CH, FOURIER_CH), full2),
            pl.BlockSpec((FOURIER_CH, FOURIER_CH), full2),
            pl.BlockSpec((FOURIER_GROUPS, FOURIER_CH, FOURIER_CH), lambda b: (0, 0, 0)),
            pl.BlockSpec((1, FOURIER_WIDTH), full2),
        ],
        out_specs=pl.BlockSpec((seq, FOURIER_WIDTH), lambda b: (b, 0)),
        out_shape=jax.ShapeDtypeStruct((batch * seq, FOURIER_WIDTH), BF16),
        scratch_shapes=[pltpu.VMEM((seq, FOURIER_WIDTH), BF16),
                        pltpu.VMEM((seq, FOURIER_WIDTH), BF16),
                        pltpu.VMEM((half, FOURIER_WIDTH), BF16)],
        compiler_params=_params(1, VMEM_LIMIT),
        name="fourier",
    )(u, jnp.asarray(cs[:half + SUBLANES], BF16), jnp.asarray(ss[:half], BF16),
      jnp.asarray(rev, BF16), jnp.asarray(cc, BF16), jnp.asarray(-sc, BF16), w_fourier,
      g_out.reshape(1, FOURIER_WIDTH))


def _attn_kernel(sink_ref, q_ref, kvp_ref, kvo_ref, kvn_ref, bias_a_ref, bias_b_ref, g_ref, o_ref,
                 acc_ref):
    kv = jnp.concatenate([kvp_ref[...], kvo_ref[...], kvn_ref[...]], axis=0)
    nk = kv.shape[0]
    lo = lax.broadcasted_iota(I32, (nk, LANES), 1) < HEAD_DIM
    k_a, k_b = kv[:, 0:LANES], kv[:, LANES:2 * LANES]
    v_a, v_b = kv[:, 2 * LANES:3 * LANES], kv[:, 3 * LANES:4 * LANES]
    zero = jnp.zeros_like(k_a)
    k_lo = (jnp.where(lo, k_a, zero), jnp.where(lo, k_b, zero))
    k_hi = (jnp.where(lo, zero, k_b), jnp.where(lo, zero, k_a))
    v_lo = (jnp.where(lo, v_a, zero), jnp.where(lo, v_b, zero))
    v_hi = (jnp.where(lo, zero, v_b), jnp.where(lo, zero, v_a))
    lo_out = lax.broadcasted_iota(I32, (Q_BLOCK, LANES), 1) < HEAD_DIM
    bias_refs = (bias_a_ref, bias_b_ref)
    rows2 = q_ref.shape[0]
    nt = (((1,), (1,)), ((), ()))
    for h in range(N_KV_HEADS):
        qs = jnp.concatenate([q_ref[:, (2 * h) * LANES:(2 * h + 1) * LANES],
                              q_ref[:, (2 * h + 1) * LANES:(2 * h + 2) * LANES]], axis=0)
        s_par = (lax.dot_general(qs, k_lo[h], nt, preferred_element_type=F32),
                 lax.dot_general(qs, k_hi[h], nt, preferred_element_type=F32))
        for sb in range(2):
            keys = slice(sb * Q_BLOCK, sb * Q_BLOCK + 3 * Q_BLOCK)
            vcat = jnp.concatenate([v_lo[h][keys, :], v_hi[h][keys, :]], axis=0)
            for c in range(2):
                r0 = c * rows2 + sb * Q_BLOCK
                probs, invs = [], []
                for par in range(2):
                    hq = 4 * h + 2 * c + par
                    s = s_par[par][r0:r0 + Q_BLOCK, keys] + bias_refs[sb][hq]
                    sink = sink_ref[hq]
                    m = jnp.maximum(jnp.max(s, axis=-1, keepdims=True), sink)
                    p = jnp.exp2(s - m)
                    denom = jnp.sum(p, axis=-1, keepdims=True) + jnp.exp2(sink - m)
                    probs.append(p.astype(BF16))
                    invs.append(1.0 / denom)
                pcat = jnp.concatenate(probs, axis=1)
                chunk = 2 * h + c
                o = jnp.dot(pcat, vcat, preferred_element_type=F32)
                acc_ref[sb * Q_BLOCK:(sb + 1) * Q_BLOCK, chunk * LANES:(chunk + 1) * LANES] = (
                    o * jnp.where(lo_out, invs[0], invs[1]))
    y = acc_ref[...]
    ms = jnp.mean(y * y, axis=-1, keepdims=True)
    o_ref[...] = (y * lax.rsqrt(ms + NORM_EPS) * g_ref[...]).astype(BF16)


def _t5_bucket(rel):
    nb = N_BUCKETS // 2
    max_exact = nb // 2
    ret = (rel > 0).astype(jnp.int32) * nb
    n = jnp.abs(rel)
    nf = jnp.maximum(n, 1).astype(jnp.float32)
    large = max_exact + (jnp.log(nf / max_exact) / math.log(MAX_DISTANCE / max_exact)
                         * (nb - max_exact)).astype(jnp.int32)
    large = jnp.minimum(large, nb - 1)
    return ret + jnp.where(n < max_exact, n, large)


def _attention(q, kv, sinks, rel_bias, g_out, batch, seq):
    nb = seq // Q_BLOCK
    assert nb % 2 == 0
    nb2 = nb // 2
    qi = jnp.arange(Q_BLOCK, dtype=jnp.int32)[:, None]
    kj = jnp.arange(3 * Q_BLOCK, dtype=jnp.int32)[None, :]
    rel = kj - Q_BLOCK - qi
    period = 4 * Q_BLOCK
    p = jnp.arange(period, dtype=jnp.int32)
    off = jnp.where(p < 3 * Q_BLOCK, p, p - period) - Q_BLOCK
    hit = _t5_bucket(off)[None, :, None] == jnp.arange(N_BUCKETS, dtype=jnp.int32)
    by_off = jnp.sum(jnp.where(hit, rel_bias.astype(F32).T[:, None, :], 0.0), axis=-1)
    bias = jnp.tile(by_off, (1, Q_BLOCK))[:, :Q_BLOCK * (period - 1)]
    bias = bias.reshape(N_Q_HEADS, Q_BLOCK, period - 1)[:, :, :3 * Q_BLOCK]
    band = jnp.abs(rel) <= WINDOW
    first = band & (kj >= Q_BLOCK)
    last = band & (kj < 2 * Q_BLOCK)
    table = jnp.stack([jnp.where(msk[None], bias * LOG2E, MASK_VALUE) for msk in (first, band, last)])
    q_rows = 2 * Q_BLOCK
    grid_spec = pltpu.PrefetchScalarGridSpec(
        num_scalar_prefetch=1,
        grid=(batch, nb2),
        in_specs=[
            pl.BlockSpec((q_rows, ATTN_WIDTH), lambda b, i, s: (b * nb2 + i, 0)),
            pl.BlockSpec((Q_BLOCK, 4 * LANES),
                         lambda b, i, s: (b * nb + jnp.maximum(2 * i - 1, 0), 0)),
            pl.BlockSpec((q_rows, 4 * LANES), lambda b, i, s: (b * nb2 + i, 0)),
            pl.BlockSpec((Q_BLOCK, 4 * LANES),
                         lambda b, i, s: (b * nb + jnp.minimum(2 * i + 2, nb - 1), 0)),
            pl.BlockSpec((None, N_Q_HEADS, Q_BLOCK, 3 * Q_BLOCK),
                         lambda b, i, s: (jnp.where(i == 0, 0, 1), 0, 0, 0)),
            pl.BlockSpec((None, N_Q_HEADS, Q_BLOCK, 3 * Q_BLOCK),
                         lambda b, i, s: (jnp.where(i == nb2 - 1, 2, 1), 0, 0, 0)),
            pl.BlockSpec((1, ATTN_WIDTH), lambda b, i, s: (0, 0)),
        ],
        out_specs=pl.BlockSpec((q_rows, ATTN_WIDTH), lambda b, i, s: (b * nb2 + i, 0)),
        scratch_shapes=[pltpu.VMEM((q_rows, ATTN_WIDTH), F32)],
    )
    return pl.pallas_call(
        _attn_kernel,
        grid_spec=grid_spec,
        out_shape=jax.ShapeDtypeStruct((batch * seq, ATTN_WIDTH), BF16),
        compiler_params=_params(2, VMEM_LIMIT),
        name="attention",
    )(sinks.astype(F32) * LOG2E, q, kv, kv, kv, table, table, g_out.reshape(1, ATTN_WIDTH))


def _outproj_kernel(yf_ref, ya_ref, x_ref, wo_ref, g2_ref, wrt_ref, brt_ref, triu_ref, scan_ref,
                    x1_ref, h2_ref, post_ref, gatet_ref, cnt_ref, *, n_experts):
    half = yf_ref.shape[1]
    mix = (jnp.dot(yf_ref[...], wo_ref[:half, :], preferred_element_type=F32)
           + jnp.dot(ya_ref[...], wo_ref[half:, :], preferred_element_type=F32))
    x1 = x_ref[...] + mix
    x1_ref[...] = x1
    ms = jnp.mean(x1 * x1, axis=-1, keepdims=True)
    h2 = x1 * lax.rsqrt(ms + NORM_EPS) * g2_ref[...]
    h2_ref[...] = h2.astype(BF16)
    h_hi = h2.astype(BF16)
    h_lo = (h2 - h_hi.astype(F32)).astype(BF16)
    nt = (((1,), (1,)), ((), ()))
    t1 = lax.dot_general(wrt_ref[...], h_hi, nt, preferred_element_type=F32)
    t2 = lax.dot_general(wrt_ref[:n_experts, :], h_lo, nt, preferred_element_type=F32)
    logits = t1[:n_experts] + t1[n_experts:] + t2 + brt_ref[...]
    tm = logits.shape[1]
    sub_e = lax.broadcasted_iota(I32, (n_experts, tm), 0).astype(F32)
    work = logits
    vals, idxs = [], []
    for _ in range(TOP_K):
        m = jnp.max(work, axis=0, keepdims=True)
        ik = jnp.min(jnp.where(work == m, sub_e, float(n_experts)), axis=0, keepdims=True)
        work = jnp.where(sub_e == ik, -jnp.inf, work)
        vals.append(m)
        idxs.append(ik)
    exps = [jnp.exp(v - vals[0]) for v in vals]
    inv = 1.0 / (exps[0] + exps[1] + exps[2] + exps[3])
    gates = [e * inv for e in exps]

    sub = lax.broadcasted_iota(I32, (LANES, tm), 0).astype(F32)
    onehot = jnp.zeros((LANES, tm), F32)
    for k in range(TOP_K):
        onehot = onehot + jnp.where(sub == idxs[k] + float(k * n_experts), 1.0, 0.0)
    onehot_b = onehot.astype(BF16)
    before = jnp.dot(onehot_b, triu_ref[...], preferred_element_type=F32)
    counts = lax.dot_general(jnp.ones((SUBLANES, tm), BF16), onehot_b, nt,
                             preferred_element_type=F32)[0:1, :]
    total = counts
    for k in range(1, TOP_K):
        total = total + pltpu.roll(counts, k * n_experts, 1)
    cnt_ref[0] = total.astype(I32)
    colsum = jnp.sum(onehot, axis=1, keepdims=True)
    blocks = [colsum[k * n_experts:(k + 1) * n_experts] for k in range(TOP_K)]
    total_e = blocks[0] + blocks[1] + blocks[2] + blocks[3]
    run_e = jnp.floor((total_e + (RUN_ALIGN - 1)) * (1.0 / RUN_ALIGN)) * RUN_ALIGN
    run_start = jnp.dot(scan_ref[...], jnp.broadcast_to(run_e, (n_experts, LANES)).astype(BF16),
                        preferred_element_type=F32)[:, 0:1]
    adds, acc = [], run_start
    for k in range(TOP_K):
        adds.append(acc)
        acc = acc + blocks[k]
    placed = (before + jnp.concatenate(adds, axis=0)) * onehot
    pos = [jnp.sum(placed[k * n_experts:(k + 1) * n_experts], axis=0, keepdims=True)
           for k in range(TOP_K)]
    post_ref[...] = jnp.concatenate(pos + [jnp.zeros((SUBLANES - TOP_K, tm), F32)], axis=0).astype(I32)
    gatet_ref[...] = jnp.concatenate(gates + [jnp.zeros((SUBLANES - TOP_K, tm), F32)], axis=0)


def _outproj(yf, ya, x2d, w_out, norm2, w_router, b_router):
    t, d = x2d.shape
    tm = min(TOKEN_TILE, t)
    n_tiles = t // tm
    n_experts = w_router.shape[1]
    assert TOP_K * n_experts == LANES
    triu = np.triu(np.ones((tm, tm), np.float32), 1)
    scan = np.tril(np.ones((n_experts, n_experts), np.float32), -1)
    wr_hi = w_router.astype(BF16)
    wr_lo = (w_router - wr_hi.astype(F32)).astype(BF16)
    wrt = jnp.concatenate([wr_hi.T, wr_lo.T], axis=0)
    full = lambda i: (0, 0)
    row = lambda i: (i, 0)
    return pl.pallas_call(
        functools.partial(_outproj_kernel, n_experts=n_experts),
        grid=(n_tiles,),
        in_specs=[
            pl.BlockSpec((tm, yf.shape[1]), row),
            pl.BlockSpec((tm, ya.shape[1]), row),
            pl.BlockSpec((tm, d), row),
            pl.BlockSpec((w_out.shape[0], d), full),
            pl.BlockSpec((1, d), full),
            pl.BlockSpec((2 * n_experts, d), full),
            pl.BlockSpec((n_experts, 1), full),
            pl.BlockSpec((tm, tm), full),
            pl.BlockSpec((n_experts, n_experts), full),
        ],
        out_specs=[
            pl.BlockSpec((tm, d), row),
            pl.BlockSpec((tm, d), row),
            pl.BlockSpec((SUBLANES, tm), row),
            pl.BlockSpec((SUBLANES, tm), row),
            pl.BlockSpec((1, 1, LANES), lambda i: (i, 0, 0)),
        ],
        out_shape=[
            jax.ShapeDtypeStruct((t, d), F32),
            jax.ShapeDtypeStruct((t, d), BF16),
            jax.ShapeDtypeStruct((n_tiles * SUBLANES, tm), I32),
            jax.ShapeDtypeStruct((n_tiles * SUBLANES, tm), F32),
            jax.ShapeDtypeStruct((n_tiles, 1, LANES), I32),
        ],
        compiler_params=_params(1, VMEM_LIMIT),
        name="outproj_router",
    )(yf, ya, x2d, w_out.astype(BF16), norm2.reshape(1, d), wrt, b_router.reshape(n_experts, 1),
      jnp.asarray(triu, BF16), jnp.asarray(scan, BF16))


def _pack_pairs(x, is_bf16_exact=False):
    half = x.shape[1] // 2
    a, b = x[:, :half], x[:, half:]
    if not is_bf16_exact:
        a, b = a.astype(BF16).astype(F32), b.astype(BF16).astype(F32)
    return lax.bitcast_convert_type(a, U32) | (lax.bitcast_convert_type(b, U32) >> 16)


def _unpack_pairs(w):
    hi = lax.bitcast_convert_type(w & U32(0xFFFF0000), F32)
    lo = lax.bitcast_convert_type(w << 16, F32)
    return hi.astype(BF16), lo.astype(BF16)


def _rows(start, size):
    if not isinstance(size, int):
        size = pl.multiple_of(size, RUN_ALIGN)
    return pl.ds(pl.multiple_of(start, RUN_ALIGN), size)


def _dispatch_kernel(cnt_ref, lst_ref, base_ref, rows_ref, tail_ref, post_ref, h2_ref, xs_ref,
                     buf, zbuf, sem, zsem, *, n_experts):
    j = pl.program_id(0)
    tm = h2_ref.shape[0]
    n_local = buf.shape[1]

    def start_runs(tile, slot):
        def run(e, carry):
            r = tile * n_experts + e
            n = cnt_ref[r]

            @pl.when(n > 0)
            def _():
                pltpu.make_async_copy(buf.at[slot, _rows(lst_ref[r], n), :],
                                      xs_ref.at[_rows(base_ref[r], n), :], sem.at[slot]).start()
            return carry
        lax.fori_loop(0, n_experts, run, 0)

    def wait_runs(tile, slot):
        n = rows_ref[tile]

        @pl.when(n > 0)
        def _():
            pltpu.make_async_copy(buf.at[slot, _rows(0, n), :], xs_ref.at[_rows(0, n), :],
                                  sem.at[slot]).wait()

    def zero_fill(op):
        def tail(e, carry):
            n = tail_ref[n_experts + e]

            @pl.when(n > 0)
            def _():
                getattr(pltpu.make_async_copy(zbuf.at[_rows(0, n), :],
                                              xs_ref.at[_rows(tail_ref[e], n), :], zsem), op)()
            return carry
        lax.fori_loop(0, n_experts, tail, 0)

        def spare(b, carry):
            getattr(pltpu.make_async_copy(zbuf, xs_ref.at[_rows(b * EXPERT_ROWS, EXPERT_ROWS), :],
                                          zsem), op)()
            return carry
        lax.fori_loop(tail_ref[2 * n_experts], xs_ref.shape[0] // EXPERT_ROWS, spare, 0)

    slot = j % 2
    @pl.when(j >= 2)
    def _():
        wait_runs(j - 2, slot)

    @pl.when(j == 0)
    def _():
        zbuf[...] = jnp.zeros_like(zbuf)
        zero_fill("start")
        zero_fill("wait")

    h = h2_ref[...]
    post = post_ref[0:TOP_K, :]
    chunk_of = lax.shift_right_logical(post, PERM_CHUNK.bit_length() - 1)
    offset = (post & (PERM_CHUNK - 1)).astype(F32)
    rows = lax.broadcasted_iota(I32, (PERM_CHUNK, tm), 0).astype(F32).astype(BF16)
    one = jnp.ones((PERM_CHUNK, tm), BF16)
    per = DISPATCH_ROWS // PERM_CHUNK
    for mc in range(n_local // DISPATCH_ROWS):
        parts = []
        for rc in range(mc * per, (mc + 1) * per):
            off = jnp.where(chunk_of == rc, offset, -1.0).astype(BF16)
            perm = jnp.zeros((PERM_CHUNK, tm), BF16)
            for k in range(TOP_K):
                perm = jnp.where(rows == off[k:k + 1, :], one, perm)
            parts.append(perm)
        rs = slice(mc * DISPATCH_ROWS, (mc + 1) * DISPATCH_ROWS)
        buf[slot, rs, :] = _pack_pairs(
            jnp.dot(jnp.concatenate(parts, axis=0), h, preferred_element_type=F32), True)

    start_runs(j, slot)

    @pl.when(j == pl.num_programs(0) - 1)
    def _():
        @pl.when(j >= 1)
        def _():
            wait_runs(j - 1, 1 - slot)
        wait_runs(j, slot)


def _local_rows(tm, n_experts):
    worst = TOP_K * tm + n_experts * (RUN_ALIGN - 1)
    return -(-worst // DISPATCH_ROWS) * DISPATCH_ROWS


def _dispatch(plan, post, h2, n_rows, n_experts):
    t, d = h2.shape
    tm = min(TOKEN_TILE, t)
    grid_spec = pltpu.PrefetchScalarGridSpec(
        num_scalar_prefetch=5,
        grid=(t // tm,),
        in_specs=[
            pl.BlockSpec((SUBLANES, tm), lambda i, *_: (i, 0)),
            pl.BlockSpec((tm, d), lambda i, *_: (i, 0)),
        ],
        out_specs=pl.BlockSpec(memory_space=pl.ANY),
        scratch_shapes=[pltpu.VMEM((2, _local_rows(tm, n_experts), d // 2), U32),
                        pltpu.VMEM((EXPERT_ROWS, d // 2), U32),
                        pltpu.SemaphoreType.DMA((2,)), pltpu.SemaphoreType.DMA(())],
    )
    return pl.pallas_call(
        functools.partial(_dispatch_kernel, n_experts=n_experts),
        grid_spec=grid_spec,
        out_shape=jax.ShapeDtypeStruct((n_rows, d // 2), U32),
        compiler_params=_params(1, VMEM_LIMIT),
        name="dispatch",
    )(plan["cnt"], plan["lst"], plan["base"], plan["rows"], plan["tail"], post, h2)


def _combine_kernel(cnt_ref, lst_ref, base_ref, rows_ref, post_ref, gatet_ref, x1_ref, ys_ref, o_ref,
                    buf, g_scr, y_scr, sem, *, n_experts):
    j = pl.program_id(0)
    tm, d = x1_ref.shape
    n_local = buf.shape[1]

    def start_runs(tile, slot):
        def run(e, carry):
            r = tile * n_experts + e
            n = cnt_ref[r]

            @pl.when(n > 0)
            def _():
                pltpu.make_async_copy(ys_ref.at[_rows(base_ref[r], n), :],
                                      buf.at[slot, _rows(lst_ref[r], n), :], sem.at[slot]).start()
            return carry
        lax.fori_loop(0, n_experts, run, 0)

    def wait_runs(tile, slot):
        n = rows_ref[tile]

        @pl.when(n > 0)
        def _():
            pltpu.make_async_copy(ys_ref.at[_rows(0, n), :], buf.at[slot, _rows(0, n), :],
                                  sem.at[slot]).wait()

    slot = j % 2
    @pl.when(j == 0)
    def _():
        buf[...] = jnp.zeros_like(buf)
        start_runs(j, slot)

    @pl.when(j + 1 < pl.num_programs(0))
    def _():
        start_runs(j + 1, 1 - slot)

    wait_runs(j, slot)

    post = post_ref[0:TOP_K, :]
    gate = gatet_ref[0:TOP_K, :].astype(BF16)
    chunk_of = lax.shift_right_logical(post, PERM_CHUNK.bit_length() - 1)
    offset = (post & (PERM_CHUNK - 1)).astype(F32)
    rows = lax.broadcasted_iota(I32, (PERM_CHUNK, tm), 0).astype(F32).astype(BF16)
    for rc in range(n_local // PERM_CHUNK):
        chunk = slice(rc * PERM_CHUNK, (rc + 1) * PERM_CHUNK)
        off = jnp.where(chunk_of == rc, offset, -1.0).astype(BF16)
        g = jnp.zeros((PERM_CHUNK, tm), BF16)
        for k in range(TOP_K):
            g = jnp.where(rows == off[k:k + 1, :], jnp.broadcast_to(gate[k:k + 1, :], g.shape), g)
        g_scr[chunk, :] = g
        y_scr[chunk, :d // 2], y_scr[chunk, d // 2:] = _unpack_pairs(buf[slot, chunk, :])
    o_ref[...] = x1_ref[...] + lax.dot_general(g_scr[...], y_scr[...], (((0,), (0,)), ((), ())),
                                               preferred_element_type=F32)


def _combine(plan, post, gatet, x1, ys, n_experts):
    t, d = x1.shape
    tm = min(TOKEN_TILE, t)
    grid_spec = pltpu.PrefetchScalarGridSpec(
        num_scalar_prefetch=4,
        grid=(t // tm,),
        in_specs=[
            pl.BlockSpec((SUBLANES, tm), lambda i, *_: (i, 0)),
            pl.BlockSpec((SUBLANES, tm), lambda i, *_: (i, 0)),
            pl.BlockSpec((tm, d), lambda i, *_: (i, 0)),
            pl.BlockSpec(memory_space=pl.ANY),
        ],
        out_specs=pl.BlockSpec((tm, d), lambda i, *_: (i, 0)),
        scratch_shapes=[pltpu.VMEM((2, _local_rows(tm, n_experts), d // 2), U32),
                        pltpu.VMEM((_local_rows(tm, n_experts), tm), BF16),
                        pltpu.VMEM((_local_rows(tm, n_experts), d), BF16),
                        pltpu.SemaphoreType.DMA((2,))],
    )
    return pl.pallas_call(
        functools.partial(_combine_kernel, n_experts=n_experts),
        grid_spec=grid_spec,
        out_shape=jax.ShapeDtypeStruct((t, d), F32),
        compiler_params=_params(1, VMEM_LIMIT),
        name="combine",
    )(plan["cnt"], plan["lst"], plan["base"], plan["rows"], post, gatet, x1, ys)


def _expert_kernel(be_ref, nxt_ref, par_ref, meta_ref, xs_ref, wgu_hbm, bg_ref, bu_ref, wd_hbm, bd_ref,
                   perm_ref, ys_ref, wgu_buf, wd_buf, wg_s, wu_s, wd_s, wsem):
    i = pl.program_id(0)
    n_used = meta_ref[0]
    active = i < n_used
    new_expert = (i == 0) | (be_ref[i] != be_ref[jnp.maximum(i - 1, 0)])

    def weight_copies(expert, slot):
        return (pltpu.make_async_copy(wgu_hbm.at[expert], wgu_buf.at[slot], wsem.at[slot]),
                pltpu.make_async_copy(wd_hbm.at[expert], wd_buf.at[slot], wsem.at[slot]))

    @pl.when(active & new_expert)
    def _():
        slot = par_ref[i]

        @pl.when(i == 0)
        def _():
            for cp in weight_copies(be_ref[i], slot):
                cp.start()

        for cp in weight_copies(be_ref[i], slot):
            cp.wait()
        nxt = nxt_ref[i]

        @pl.when(nxt >= 0)
        def _():
            for cp in weight_copies(nxt, 1 - slot):
                cp.start()

        width = perm_ref.shape[0]
        for c in range(wgu_buf.shape[2] // width):
            wc = wgu_buf[slot, :, c * width:(c + 1) * width].astype(BF16)
            r = jnp.dot(wc, perm_ref[...], preferred_element_type=F32)
            wg_s[:, c * LANES:(c + 1) * LANES] = r[:, :LANES].astype(BF16)
            wu_s[:, c * LANES:(c + 1) * LANES] = r[:, LANES:].astype(BF16)
        wd_s[...] = wd_buf[slot].astype(BF16)

    @pl.when(active)
    def _():
        xb = jnp.concatenate(_unpack_pairs(xs_ref[...]), axis=1)
        g = jnp.dot(xb, wg_s[...], preferred_element_type=F32) + bg_ref[0]
        up = jnp.dot(xb, wu_s[...], preferred_element_type=F32) + bu_ref[0]
        g = jnp.minimum(g, SWIGLU_LIMIT)
        up = jnp.clip(up, -SWIGLU_LIMIT, SWIGLU_LIMIT)
        act = g * (1.0 / (1.0 + jnp.exp(-SWIGLU_ALPHA * g))) * (up + 1.0)
        ys_ref[...] = _pack_pairs(
            jnp.dot(act.astype(BF16), wd_s[...], preferred_element_type=F32) + bd_ref[0])

    @pl.when(jnp.logical_not(active))
    def _():
        ys_ref[...] = jnp.zeros_like(ys_ref)


def _experts(blk, meta, xs, w_gate_up, b_gate_up, w_down, b_down):
    n_rows = xs.shape[0]
    n_experts, d, f2 = w_gate_up.shape
    f = f2 // 2
    bm = EXPERT_ROWS
    n_blocks = n_rows // bm
    width = 2 * LANES
    perm = np.zeros((width, width), np.float32)
    perm[2 * np.arange(LANES), np.arange(LANES)] = 1.0
    perm[2 * np.arange(LANES) + 1, LANES + np.arange(LANES)] = 1.0
    bg = b_gate_up[:, 0::2].reshape(n_experts, 1, f)
    bu = b_gate_up[:, 1::2].reshape(n_experts, 1, f)
    rows = lambda i, be, nxt, par, meta: (jnp.minimum(i, meta[0] - 1), 0)
    per_e = lambda i, be, nxt, par, meta: (be[i], 0, 0)
    grid_spec = pltpu.PrefetchScalarGridSpec(
        num_scalar_prefetch=4,
        grid=(n_blocks,),
        in_specs=[
            pl.BlockSpec((bm, d // 2), rows),
            pl.BlockSpec(memory_space=pl.ANY),
            pl.BlockSpec((1, 1, f), per_e),
            pl.BlockSpec((1, 1, f), per_e),
            pl.BlockSpec(memory_space=pl.ANY),
            pl.BlockSpec((1, 1, d), per_e),
            pl.BlockSpec((width, width), lambda i, *_: (0, 0)),
        ],
        out_specs=pl.BlockSpec((bm, d // 2), lambda i, *_: (i, 0)),
        scratch_shapes=[pltpu.VMEM((2, d, f2), F32), pltpu.VMEM((2, f, d), F32),
                        pltpu.VMEM((d, f), BF16), pltpu.VMEM((d, f), BF16),
                        pltpu.VMEM((f, d), BF16), pltpu.SemaphoreType.DMA((2,))],
    )
    return pl.pallas_call(
        _expert_kernel,
        grid_spec=grid_spec,
        out_shape=jax.ShapeDtypeStruct((n_rows, d // 2), U32),
        compiler_params=_params(1, VMEM_LIMIT),
        name="experts",
    )(blk["expert"], blk["next"], blk["slot"], meta, xs, w_gate_up, bg, bu, w_down,
      b_down.reshape(n_experts, 1, d), jnp.asarray(perm, BF16))


def _routing_plan(counts, n_experts, bm, n_blocks):
    cnt = counts[:, 0, :n_experts]
    run = (cnt + RUN_ALIGN - 1) // RUN_ALIGN * RUN_ALIGN
    per_expert = jnp.sum(run, axis=0)
    padded = (per_expert + bm - 1) // bm * bm
    pend = jnp.cumsum(padded)
    pstart = pend - padded
    base = pstart[None, :] + jnp.cumsum(run, axis=0) - run
    lst = jnp.cumsum(run, axis=1) - run
    n_used = pend[-1] // bm
    tail = jnp.concatenate([pstart + per_expert, padded - per_expert, n_used[None]])
    starts = jnp.arange(n_blocks, dtype=I32) * bm
    blk = jnp.sum((starts[:, None] >= pend[None, :]).astype(I32), axis=1)
    blk = jnp.minimum(blk, n_experts - 1)
    last = jnp.sum((((n_used - 1) * bm) >= pend).astype(I32))
    blk_e = jnp.where(jnp.arange(n_blocks) < n_used, blk, jnp.minimum(last, n_experts - 1))
    ids = jnp.arange(n_experts, dtype=I32)
    has_rows = padded > 0
    later = jnp.where(has_rows[None, :] & (ids[None, :] > ids[:, None]), ids[None, :], n_experts)
    next_e = jnp.min(later, axis=1)
    next_e = jnp.where(next_e < n_experts, next_e, -1)
    ordinal = jnp.cumsum(has_rows.astype(I32)) - 1
    onehot = blk_e[:, None] == ids[None, :]
    blocks = {"expert": blk_e.astype(I32),
              "next": jnp.sum(jnp.where(onehot, next_e[None, :], 0), axis=1).astype(I32),
              "slot": jnp.sum(jnp.where(onehot, (ordinal % 2)[None, :], 0), axis=1).astype(I32)}
    plan = {"cnt": run.reshape(-1).astype(I32), "lst": lst.reshape(-1).astype(I32),
            "base": base.reshape(-1).astype(I32), "rows": jnp.sum(run, axis=1).astype(I32),
            "tail": tail.astype(I32)}
    return plan, blocks, n_used.astype(I32).reshape(1)


def _layer(x2d, batch, seq, norm1, w_in, q_norm, k_norm, sinks, rel_bias, w_fourier, g_fourier_out,
           g_attn_out, w_out, norm2, w_router, b_router, w_gate_up, b_gate_up, w_down, b_down):
    t, d = x2d.shape
    n_experts = w_router.shape[1]
    u, q, kv = _inproj(x2d, norm1, w_in, q_norm, k_norm)
    yf = _fourier(u, w_fourier, g_fourier_out, batch, seq)
    ya = _attention(q, kv, sinks, rel_bias, g_attn_out, batch, seq)
    x1, h2, post, gatet, counts = _outproj(yf, ya, x2d, w_out, norm2, w_router, b_router)
    bm = EXPERT_ROWS
    n_tiles = t // min(TOKEN_TILE, t)
    worst_rows = t * TOP_K + n_tiles * n_experts * (RUN_ALIGN - 1) + n_experts * (bm - RUN_ALIGN)
    n_blocks = -(-worst_rows // bm)
    plan, blocks, meta = _routing_plan(counts, n_experts, bm, n_blocks)
    xs = _dispatch(plan, post, h2, n_blocks * bm, n_experts)
    ys = _experts(blocks, meta, xs, w_gate_up, b_gate_up, w_down, b_down)
    return _combine(plan, post, gatet, x1, ys, n_experts)


def kernel(x, norm1, w_in, q_norm, k_norm, sinks, rel_bias, w_fourier, g_fourier_out, g_attn_out,
           w_out, norm2, w_router, b_router, w_gate_up, b_gate_up, w_down, b_down):
    b, s, d = x.shape
    x2d = x.reshape(b * s, d)
    for l in range(norm1.shape[0]):
        x2d = _layer(x2d, b, s, norm1[l], w_in[l], q_norm[l], k_norm[l], sinks[l], rel_bias,
                     w_fourier[l], g_fourier_out[l], g_attn_out[l], w_out[l], norm2[l],
                     w_router[l], b_router[l], w_gate_up[l], b_gate_up[l], w_down[l], b_down[l])
    return x2d.reshape(b, s, d)
```

```python
import functools
import math

import jax
import jax.numpy as jnp
import numpy as np
from jax import lax
from jax.experimental import pallas as pl
from jax.experimental.pallas import tpu as pltpu

F32 = jnp.float32
BF16 = jnp.bfloat16
I32 = jnp.int32
U32 = jnp.uint32

NORM_EPS = 1e-5
QK_EPS = 1e-6
HEAD_DIM = 64
N_Q_HEADS = 8
N_KV_HEADS = 2
FOURIER_GROUPS = 4
FOURIER_CH = 128
FOURIER_WIDTH = FOURIER_GROUPS * FOURIER_CH
ATTN_WIDTH = N_Q_HEADS * HEAD_DIM
KV_WIDTH = N_KV_HEADS * HEAD_DIM
WINDOW = 128
Q_BLOCK = 128
N_BUCKETS = 32
MAX_DISTANCE = 128
TOP_K = 4
SWIGLU_ALPHA = 1.702
SWIGLU_LIMIT = 7.0
MASK_VALUE = -1e30
LOG2E = math.log2(math.e)

LANES = 128
SUBLANES = 8
TOKEN_TILE = 512
RUN_ALIGN = SUBLANES
ROW_GROUPS = 2
PERM_CHUNK = 256
DISPATCH_ROWS = 3 * PERM_CHUNK
EXPERT_ROWS = 512
FOURIER_ROWS = 512
BF16_BITS = 16
VMEM_LIMIT = 56 * 1024 * 1024


def _params(n_axes, vmem=None):
    return pltpu.CompilerParams(
        dimension_semantics=("arbitrary",) * n_axes, vmem_limit_bytes=vmem)


def _pair_head_norm(xc, gain, lo):
    x2 = xc * xc
    s_lo = jnp.sum(jnp.where(lo, x2, 0.0), axis=-1, keepdims=True)
    s_hi = jnp.sum(jnp.where(lo, 0.0, x2), axis=-1, keepdims=True)
    inv = jnp.where(lo, lax.rsqrt(s_lo * (1.0 / HEAD_DIM) + QK_EPS),
                    lax.rsqrt(s_hi * (1.0 / HEAD_DIM) + QK_EPS))
    return xc * inv * gain


def _inproj_kernel(x_ref, g1_ref, w_ref, qg_ref, kg_ref, u_ref, q_ref, kv_ref):
    rows = x_ref.shape[0] // ROW_GROUPS
    lo = lax.broadcasted_iota(I32, (rows, LANES), 1) < HEAD_DIM
    q0 = FOURIER_WIDTH
    k0 = q0 + ATTN_WIDTH
    for grp in range(ROW_GROUPS):
        rs = slice(grp * rows, (grp + 1) * rows)
        x = x_ref[rs, :]
        ms = jnp.mean(x * x, axis=-1, keepdims=True)
        h = (x * lax.rsqrt(ms + NORM_EPS) * g1_ref[...]).astype(BF16)
        z = jnp.dot(h, w_ref[...], preferred_element_type=F32)
        u_ref[rs, :] = z[:, :FOURIER_WIDTH].astype(BF16)
        for c in range(ATTN_WIDTH // LANES):
            qc = _pair_head_norm(z[:, q0 + c * LANES:q0 + (c + 1) * LANES], qg_ref[...], lo)
            q_ref[rs, c * LANES:(c + 1) * LANES] = (qc * (HEAD_DIM ** -0.5 * LOG2E)).astype(BF16)
        kc = _pair_head_norm(z[:, k0:k0 + KV_WIDTH], kg_ref[...], lo)
        vc = z[:, k0 + KV_WIDTH:k0 + 2 * KV_WIDTH]
        kv_ref[rs, 0:LANES] = kc.astype(BF16)
        kv_ref[rs, LANES:2 * LANES] = pltpu.roll(kc, HEAD_DIM, 1).astype(BF16)
        kv_ref[rs, 2 * LANES:3 * LANES] = vc.astype(BF16)
        kv_ref[rs, 3 * LANES:4 * LANES] = pltpu.roll(vc, HEAD_DIM, 1).astype(BF16)


def _inproj(x2d, norm1, w_in, q_norm, k_norm):
    t, d = x2d.shape
    tm = min(TOKEN_TILE, t)
    n_in = w_in.shape[1]
    qg = jnp.tile(q_norm, LANES // HEAD_DIM).reshape(1, LANES)
    kg = jnp.tile(k_norm, LANES // HEAD_DIM).reshape(1, LANES)
    full = lambda i: (0, 0)
    return pl.pallas_call(
        _inproj_kernel,
        grid=(t // tm,),
        in_specs=[
            pl.BlockSpec((tm, d), lambda i: (i, 0)),
            pl.BlockSpec((1, d), full),
            pl.BlockSpec((d, n_in), full),
            pl.BlockSpec((1, LANES), full),
            pl.BlockSpec((1, LANES), full),
        ],
        out_specs=[
            pl.BlockSpec((tm, FOURIER_WIDTH), lambda i: (i, 0)),
            pl.BlockSpec((tm, ATTN_WIDTH), lambda i: (i, 0)),
            pl.BlockSpec((tm, 4 * LANES), lambda i: (i, 0)),
        ],
        out_shape=[
            jax.ShapeDtypeStruct((t, FOURIER_WIDTH), BF16),
            jax.ShapeDtypeStruct((t, ATTN_WIDTH), BF16),
            jax.ShapeDtypeStruct((t, 4 * LANES), BF16),
        ],
        compiler_params=_params(1, VMEM_LIMIT),
        name="inproj",
    )(x2d, norm1.reshape(1, d), w_in.astype(BF16), qg, kg)


def _fourier_kernel(u_ref, cs_ref, ss_ref, rev_ref, cc_ref, sc_ref, wf_ref, g_ref, o_ref,
                    p_scr, q_scr, e_scr, *, scale, row_block):
    for g in range(FOURIER_GROUPS):
        sl = slice(g * FOURIER_CH, (g + 1) * FOURIER_CH)
        w = wf_ref[g].astype(BF16)
        a = (jnp.dot(cc_ref[...], w, preferred_element_type=F32) * scale).astype(BF16)
        b = (jnp.dot(sc_ref[...], w, preferred_element_type=F32) * scale).astype(BF16)
        ug = u_ref[:, sl]
        p_scr[:, sl] = jnp.dot(ug, a, preferred_element_type=F32).astype(BF16)
        q_scr[:, sl] = jnp.dot(ug, b, preferred_element_type=F32).astype(BF16)
    half = u_ref.shape[0] // 2
    gain = g_ref[...]

    def norm(y):
        ms = jnp.mean(y * y, axis=-1, keepdims=True)
        return y * lax.rsqrt(ms + NORM_EPS) * gain

    n_blk = half // row_block
    mid = None
    for r in range(n_blk):
        rs = slice(r * row_block, (r + 1) * row_block)
        extra = SUBLANES if r == n_blk - 1 else 0
        c = jnp.dot(cs_ref[r * row_block:(r + 1) * row_block + extra, :], p_scr[...],
                    preferred_element_type=F32)
        d = jnp.dot(ss_ref[rs, :], q_scr[...], preferred_element_type=F32)
        o_ref[rs, :] = norm(c[:row_block] + d).astype(BF16)
        e_scr[rs, :] = norm(c[:row_block] - d).astype(BF16)
        if extra:
            mid = norm(c[row_block:row_block + 1])
    for r in range(n_blk):
        z = jnp.dot(rev_ref[r * row_block:(r + 1) * row_block, :], e_scr[...],
                    preferred_element_type=F32)
        if r == 0:
            z = jnp.where(lax.broadcasted_iota(I32, z.shape, 0) == 0, mid, z)
        o_ref[half + r * row_block:half + (r + 1) * row_block, :] = z.astype(BF16)


def _dft_tables(n):
    k = np.arange(n, dtype=np.int64)
    ang = 2.0 * np.pi * ((k[:, None] * k[None, :]) % n).astype(np.float64) / n
    return np.cos(ang), np.sin(ang)


def _fourier(u, w_fourier, g_out, batch, seq):
    cs, ss = _dft_tables(seq)
    cc, sc = _dft_tables(FOURIER_CH)
    scale = 1.0 / math.sqrt(seq * FOURIER_CH)
    half = seq // 2
    row_block = min(FOURIER_ROWS, half)
    rev = np.zeros((half, half), np.float32)
    rev[np.arange(1, half), half - np.arange(1, half)] = 1.0
    full2 = lambda b: (0, 0)
    return pl.pallas_call(
        functools.partial(_fourier_kernel, scale=scale, row_block=row_block),
        grid=(batch,),
        in_specs=[
            pl.BlockSpec((seq, FOURIER_WIDTH), lambda b: (b, 0)),
            pl.BlockSpec((half + SUBLANES, seq), full2),
            pl.BlockSpec((half, seq), full2),
            pl.BlockSpec((half, half), full2),
            pl.BlockSpec((FOURIER_CH, FOURIER_CH), full2),
            pl.BlockSpec((FOURIER_CH, FOURIER_CH), full2),
            pl.BlockSpec((FOURIER_GROUPS, FOURIER_CH, FOURIER_CH), lambda b: (0, 0, 0)),
            pl.BlockSpec((1, FOURIER_WIDTH), full2),
        ],
        out_specs=pl.BlockSpec((seq, FOURIER_WIDTH), lambda b: (b, 0)),
        out_shape=jax.ShapeDtypeStruct((batch * seq, FOURIER_WIDTH), BF16),
        scratch_shapes=[pltpu.VMEM((seq, FOURIER_WIDTH), BF16),
                        pltpu.VMEM((seq, FOURIER_WIDTH), BF16),
                        pltpu.VMEM((half, FOURIER_WIDTH), BF16)],
        compiler_params=_params(1, VMEM_LIMIT),
        name="fourier",
    )(u, jnp.asarray(cs[:half + SUBLANES], BF16), jnp.asarray(ss[:half], BF16),
      jnp.asarray(rev, BF16), jnp.asarray(cc, BF16), jnp.asarray(-sc, BF16), w_fourier,
      g_out.reshape(1, FOURIER_WIDTH))


def _attn_kernel(sink_ref, q_ref, kvp_ref, kvo_ref, kvn_ref, bias_a_ref, bias_b_ref, g_ref, o_ref,
                 acc_ref):
    kv = jnp.concatenate([kvp_ref[...], kvo_ref[...], kvn_ref[...]], axis=0)
    nk = kv.shape[0]
    lo = lax.broadcasted_iota(I32, (nk, LANES), 1) < HEAD_DIM
    k_a, k_b = kv[:, 0:LANES], kv[:, LANES:2 * LANES]
    v_a, v_b = kv[:, 2 * LANES:3 * LANES], kv[:, 3 * LANES:4 * LANES]
    zero = jnp.zeros_like(k_a)
    k_lo = (jnp.where(lo, k_a, zero), jnp.where(lo, k_b, zero))
    k_hi = (jnp.where(lo, zero, k_b), jnp.where(lo, zero, k_a))
    v_lo = (jnp.where(lo, v_a, zero), jnp.where(lo, v_b, zero))
    v_hi = (jnp.where(lo, zero, v_b), jnp.where(lo, zero, v_a))
    lo_out = lax.broadcasted_iota(I32, (Q_BLOCK, LANES), 1) < HEAD_DIM
    bias_refs = (bias_a_ref, bias_b_ref)
    rows2 = q_ref.shape[0]
    nt = (((1,), (1,)), ((), ()))
    for h in range(N_KV_HEADS):
        qs = jnp.concatenate([q_ref[:, (2 * h) * LANES:(2 * h + 1) * LANES],
                              q_ref[:, (2 * h + 1) * LANES:(2 * h + 2) * LANES]], axis=0)
        s_par = (lax.dot_general(qs, k_lo[h], nt, preferred_element_type=F32),
                 lax.dot_general(qs, k_hi[h], nt, preferred_element_type=F32))
        for sb in range(2):
            keys = slice(sb * Q_BLOCK, sb * Q_BLOCK + 3 * Q_BLOCK)
            vcat = jnp.concatenate([v_lo[h][keys, :], v_hi[h][keys, :]], axis=0)
            for c in range(2):
                r0 = c * rows2 + sb * Q_BLOCK
                probs, invs = [], []
                for par in range(2):
                    hq = 4 * h + 2 * c + par
                    s = s_par[par][r0:r0 + Q_BLOCK, keys] + bias_refs[sb][hq]
                    sink = sink_ref[hq]
                    m = jnp.maximum(jnp.max(s, axis=-1, keepdims=True), sink)
                    p = jnp.exp2(s - m)
                    denom = jnp.sum(p, axis=-1, keepdims=True) + jnp.exp2(sink - m)
                    probs.append(p.astype(BF16))
                    invs.append(1.0 / denom)
                pcat = jnp.concatenate(probs, axis=1)
                chunk = 2 * h + c
                o = jnp.dot(pcat, vcat, preferred_element_type=F32)
                acc_ref[sb * Q_BLOCK:(sb + 1) * Q_BLOCK, chunk * LANES:(chunk + 1) * LANES] = (
                    o * jnp.where(lo_out, invs[0], invs[1]))
    y = acc_ref[...]
    ms = jnp.mean(y * y, axis=-1, keepdims=True)
    o_ref[...] = (y * lax.rsqrt(ms + NORM_EPS) * g_ref[...]).astype(BF16)


def _t5_bucket(rel):
    nb = N_BUCKETS // 2
    max_exact = nb // 2
    ret = (rel > 0).astype(jnp.int32) * nb
    n = jnp.abs(rel)
    nf = jnp.maximum(n, 1).astype(jnp.float32)
    large = max_exact + (jnp.log(nf / max_exact) / math.log(MAX_DISTANCE / max_exact)
                         * (nb - max_exact)).astype(jnp.int32)
    large = jnp.minimum(large, nb - 1)
    return ret + jnp.where(n < max_exact, n, large)


def _attention(q, kv, sinks, rel_bias, g_out, batch, seq):
    nb = seq // Q_BLOCK
    assert nb % 2 == 0
    nb2 = nb // 2
    qi = jnp.arange(Q_BLOCK, dtype=jnp.int32)[:, None]
    kj = jnp.arange(3 * Q_BLOCK, dtype=jnp.int32)[None, :]
    rel = kj - Q_BLOCK - qi
    period = 4 * Q_BLOCK
    p = jnp.arange(period, dtype=jnp.int32)
    off = jnp.where(p < 3 * Q_BLOCK, p, p - period) - Q_BLOCK
    hit = _t5_bucket(off)[None, :, None] == jnp.arange(N_BUCKETS, dtype=jnp.int32)
    by_off = jnp.sum(jnp.where(hit, rel_bias.astype(F32).T[:, None, :], 0.0), axis=-1)
    bias = jnp.tile(by_off, (1, Q_BLOCK))[:, :Q_BLOCK * (period - 1)]
    bias = bias.reshape(N_Q_HEADS, Q_BLOCK, period - 1)[:, :, :3 * Q_BLOCK]
    band = jnp.abs(rel) <= WINDOW
    first = band & (kj >= Q_BLOCK)
    last = band & (kj < 2 * Q_BLOCK)
    table = jnp.stack([jnp.where(msk[None], bias * LOG2E, MASK_VALUE) for msk in (first, band, last)])
    q_rows = 2 * Q_BLOCK
    grid_spec = pltpu.PrefetchScalarGridSpec(
        num_scalar_prefetch=1,
        grid=(batch, nb2),
        in_specs=[
            pl.BlockSpec((q_rows, ATTN_WIDTH), lambda b, i, s: (b * nb2 + i, 0)),
            pl.BlockSpec((Q_BLOCK, 4 * LANES),
                         lambda b, i, s: (b * nb + jnp.maximum(2 * i - 1, 0), 0)),
            pl.BlockSpec((q_rows, 4 * LANES), lambda b, i, s: (b * nb2 + i, 0)),
            pl.BlockSpec((Q_BLOCK, 4 * LANES),
                         lambda b, i, s: (b * nb + jnp.minimum(2 * i + 2, nb - 1), 0)),
            pl.BlockSpec((None, N_Q_HEADS, Q_BLOCK, 3 * Q_BLOCK),
                         lambda b, i, s: (jnp.where(i == 0, 0, 1), 0, 0, 0)),
            pl.BlockSpec((None, N_Q_HEADS, Q_BLOCK, 3 * Q_BLOCK),
                         lambda b, i, s: (jnp.where(i == nb2 - 1, 2, 1), 0, 0, 0)),
            pl.BlockSpec((1, ATTN_WIDTH), lambda b, i, s: (0, 0)),
        ],
        out_specs=pl.BlockSpec((q_rows, ATTN_WIDTH), lambda b, i, s: (b * nb2 + i, 0)),
        scratch_shapes=[pltpu.VMEM((q_rows, ATTN_WIDTH), F32)],
    )
    return pl.pallas_call(
        _attn_kernel,
        grid_spec=grid_spec,
        out_shape=jax.ShapeDtypeStruct((batch * seq, ATTN_WIDTH), BF16),
        compiler_params=_params(2, VMEM_LIMIT),
        name="attention",
    )(sinks.astype(F32) * LOG2E, q, kv, kv, kv, table, table, g_out.reshape(1, ATTN_WIDTH))


def _outproj_kernel(yf_ref, ya_ref, x_ref, wo_ref, g2_ref, wrt_ref, brt_ref, triu_ref, scan_ref,
                    x1_ref, h2_ref, post_ref, gatet_ref, cnt_ref, *, n_experts):
    half = yf_ref.shape[1]
    mix = (jnp.dot(yf_ref[...], wo_ref[:half, :], preferred_element_type=F32)
           + jnp.dot(ya_ref[...], wo_ref[half:, :], preferred_element_type=F32))
    x1 = x_ref[...] + mix
    x1_ref[...] = x1
    ms = jnp.mean(x1 * x1, axis=-1, keepdims=True)
    h2 = x1 * lax.rsqrt(ms + NORM_EPS) * g2_ref[...]
    h2_ref[...] = h2.astype(BF16)
    h_hi = h2.astype(BF16)
    h_lo = (h2 - h_hi.astype(F32)).astype(BF16)
    nt = (((1,), (1,)), ((), ()))
    t1 = lax.dot_general(wrt_ref[...], h_hi, nt, preferred_element_type=F32)
    t2 = lax.dot_general(wrt_ref[:n_experts, :], h_lo, nt, preferred_element_type=F32)
    logits = t1[:n_experts] + t1[n_experts:] + t2 + brt_ref[...]
    tm = logits.shape[1]
    sub_e = lax.broadcasted_iota(I32, (n_experts, tm), 0).astype(F32)
    work = logits
    vals, idxs = [], []
    for _ in range(TOP_K):
        m = jnp.max(work, axis=0, keepdims=True)
        ik = jnp.min(jnp.where(work == m, sub_e, float(n_experts)), axis=0, keepdims=True)
        work = jnp.where(sub_e == ik, -jnp.inf, work)
        vals.append(m)
        idxs.append(ik)
    exps = [jnp.exp(v - vals[0]) for v in vals]
    inv = 1.0 / (exps[0] + exps[1] + exps[2] + exps[3])
    gates = [e * inv for e in exps]

    sub = lax.broadcasted_iota(I32, (LANES, tm), 0).astype(F32)
    onehot = jnp.zeros((LANES, tm), F32)
    for k in range(TOP_K):
        onehot = onehot + jnp.where(sub == idxs[k] + float(k * n_experts), 1.0, 0.0)
    onehot_b = onehot.astype(BF16)
    before = jnp.dot(onehot_b, triu_ref[...], preferred_element_type=F32)
    counts = lax.dot_general(jnp.ones((SUBLANES, tm), BF16), onehot_b, nt,
                             preferred_element_type=F32)[0:1, :]
    total = counts
    for k in range(1, TOP_K):
        total = total + pltpu.roll(counts, k * n_experts, 1)
    cnt_ref[0] = total.astype(I32)
    colsum = jnp.sum(onehot, axis=1, keepdims=True)
    blocks = [colsum[k * n_experts:(k + 1) * n_experts] for k in range(TOP_K)]
    total_e = blocks[0] + blocks[1] + blocks[2] + blocks[3]
    run_e = jnp.floor((total_e + (RUN_ALIGN - 1)) * (1.0 / RUN_ALIGN)) * RUN_ALIGN
    run_start = jnp.dot(scan_ref[...], jnp.broadcast_to(run_e, (n_experts, LANES)).astype(BF16),
                        preferred_element_type=F32)[:, 0:1]
    adds, acc = [], run_start
    for k in range(TOP_K):
        adds.append(acc)
        acc = acc + blocks[k]
    placed = (before + jnp.concatenate(adds, axis=0)) * onehot
    pos = [jnp.sum(placed[k * n_experts:(k + 1) * n_experts], axis=0, keepdims=True)
           for k in range(TOP_K)]
    post_ref[...] = jnp.concatenate(pos + [jnp.zeros((SUBLANES - TOP_K, tm), F32)], axis=0).astype(I32)
    gatet_ref[...] = jnp.concatenate(gates + [jnp.zeros((SUBLANES - TOP_K, tm), F32)], axis=0)


def _outproj(yf, ya, x2d, w_out, norm2, w_router, b_router):
    t, d = x2d.shape
    tm = min(TOKEN_TILE, t)
    n_tiles = t // tm
    n_experts = w_router.shape[1]
    assert TOP_K * n_experts == LANES
    triu = np.triu(np.ones((tm, tm), np.float32), 1)
    scan = np.tril(np.ones((n_experts, n_experts), np.float32), -1)
    wr_hi = w_router.astype(BF16)
    wr_lo = (w_router - wr_hi.astype(F32)).astype(BF16)
    wrt = jnp.concatenate([wr_hi.T, wr_lo.T], axis=0)
    full = lambda i: (0, 0)
    row = lambda i: (i, 0)
    return pl.pallas_call(
        functools.partial(_outproj_kernel, n_experts=n_experts),
        grid=(n_tiles,),
        in_specs=[
            pl.BlockSpec((tm, yf.shape[1]), row),
            pl.BlockSpec((tm, ya.shape[1]), row),
            pl.BlockSpec((tm, d), row),
            pl.BlockSpec((w_out.shape[0], d), full),
            pl.BlockSpec((1, d), full),
            pl.BlockSpec((2 * n_experts, d), full),
            pl.BlockSpec((n_experts, 1), full),
            pl.BlockSpec((tm, tm), full),
            pl.BlockSpec((n_experts, n_experts), full),
        ],
        out_specs=[
            pl.BlockSpec((tm, d), row),
            pl.BlockSpec((tm, d), row),
            pl.BlockSpec((SUBLANES, tm), row),
            pl.BlockSpec((SUBLANES, tm), row),
            pl.BlockSpec((1, 1, LANES), lambda i: (i, 0, 0)),
        ],
        out_shape=[
            jax.ShapeDtypeStruct((t, d), F32),
            jax.ShapeDtypeStruct((t, d), BF16),
            jax.ShapeDtypeStruct((n_tiles * SUBLANES, tm), I32),
            jax.ShapeDtypeStruct((n_tiles * SUBLANES, tm), F32),
            jax.ShapeDtypeStruct((n_tiles, 1, LANES), I32),
        ],
        compiler_params=_params(1, VMEM_LIMIT),
        name="outproj_router",
    )(yf, ya, x2d, w_out.astype(BF16), norm2.reshape(1, d), wrt, b_router.reshape(n_experts, 1),
      jnp.asarray(triu, BF16), jnp.asarray(scan, BF16))


def _pack_pairs(x, is_bf16_exact=False):
    half = x.shape[1] // 2
    a, b = x[:, :half], x[:, half:]
    if not is_bf16_exact:
        a, b = a.astype(BF16).astype(F32), b.astype(BF16).astype(F32)
    return lax.bitcast_convert_type(a, U32) | (lax.bitcast_convert_type(b, U32) >> BF16_BITS)


def _unpack_pairs(w):
    hi = lax.bitcast_convert_type(w & U32(((1 << BF16_BITS) - 1) << BF16_BITS), F32)
    lo = lax.bitcast_convert_type(w << BF16_BITS, F32)
    return hi.astype(BF16), lo.astype(BF16)


def _rows(start, size):
    if not isinstance(size, int):
        size = pl.multiple_of(size, RUN_ALIGN)
    return pl.ds(pl.multiple_of(start, RUN_ALIGN), size)


def _dispatch_kernel(cnt_ref, lst_ref, base_ref, rows_ref, tail_ref, post_ref, h2_ref, xs_ref,
                     buf, zbuf, sem, zsem, *, n_experts):
    j = pl.program_id(0)
    tm = h2_ref.shape[0]
    n_local = buf.shape[1]

    def start_runs(tile, slot):
        def run(e, carry):
            r = tile * n_experts + e
            n = cnt_ref[r]

            @pl.when(n > 0)
            def _():
                pltpu.make_async_copy(buf.at[slot, _rows(lst_ref[r], n), :],
                                      xs_ref.at[_rows(base_ref[r], n), :], sem.at[slot]).start()
            return carry
        lax.fori_loop(0, n_experts, run, 0)

    def wait_runs(tile, slot):
        n = rows_ref[tile]

        @pl.when(n > 0)
        def _():
            pltpu.make_async_copy(buf.at[slot, _rows(0, n), :], xs_ref.at[_rows(0, n), :],
                                  sem.at[slot]).wait()

    def zero_fill(op):
        def tail(e, carry):
            n = tail_ref[n_experts + e]

            @pl.when(n > 0)
            def _():
                getattr(pltpu.make_async_copy(zbuf.at[_rows(0, n), :],
                                              xs_ref.at[_rows(tail_ref[e], n), :], zsem), op)()
            return carry
        lax.fori_loop(0, n_experts, tail, 0)

        def spare(b, carry):
            getattr(pltpu.make_async_copy(zbuf, xs_ref.at[_rows(b * EXPERT_ROWS, EXPERT_ROWS), :],
                                          zsem), op)()
            return carry
        lax.fori_loop(tail_ref[2 * n_experts], xs_ref.shape[0] // EXPERT_ROWS, spare, 0)

    slot = j % 2
    @pl.when(j >= 2)
    def _():
        wait_runs(j - 2, slot)

    @pl.when(j == 0)
    def _():
        zbuf[...] = jnp.zeros_like(zbuf)
        zero_fill("start")
        zero_fill("wait")

    h = h2_ref[...]
    post = post_ref[0:TOP_K, :]
    chunk_of = lax.shift_right_logical(post, PERM_CHUNK.bit_length() - 1)
    offset = (post & (PERM_CHUNK - 1)).astype(F32)
    rows = lax.broadcasted_iota(I32, (PERM_CHUNK, tm), 0).astype(F32).astype(BF16)
    one = jnp.ones((PERM_CHUNK, tm), BF16)
    per = DISPATCH_ROWS // PERM_CHUNK
    for mc in range(n_local // DISPATCH_ROWS):
        parts = []
        for rc in range(mc * per, (mc + 1) * per):
            off = jnp.where(chunk_of == rc, offset, -1.0).astype(BF16)
            perm = jnp.zeros((PERM_CHUNK, tm), BF16)
            for k in range(TOP_K):
                perm = jnp.where(rows == off[k:k + 1, :], one, perm)
            parts.append(perm)
        rs = slice(mc * DISPATCH_ROWS, (mc + 1) * DISPATCH_ROWS)
        buf[slot, rs, :] = _pack_pairs(
            jnp.dot(jnp.concatenate(parts, axis=0), h, preferred_element_type=F32), True)

    start_runs(j, slot)

    @pl.when(j == pl.num_programs(0) - 1)
    def _():
        @pl.when(j >= 1)
        def _():
            wait_runs(j - 1, 1 - slot)
        wait_runs(j, slot)


def _local_rows(tm, n_experts):
    worst = TOP_K * tm + n_experts * (RUN_ALIGN - 1)
    return -(-worst // DISPATCH_ROWS) * DISPATCH_ROWS


def _dispatch(plan, post, h2, n_rows, n_experts):
    t, d = h2.shape
    tm = min(TOKEN_TILE, t)
    grid_spec = pltpu.PrefetchScalarGridSpec(
        num_scalar_prefetch=5,
        grid=(t // tm,),
        in_specs=[
            pl.BlockSpec((SUBLANES, tm), lambda i, *_: (i, 0)),
            pl.BlockSpec((tm, d), lambda i, *_: (i, 0)),
        ],
        out_specs=pl.BlockSpec(memory_space=pl.ANY),
        scratch_shapes=[pltpu.VMEM((2, _local_rows(tm, n_experts), d // 2), U32),
                        pltpu.VMEM((EXPERT_ROWS, d // 2), U32),
                        pltpu.SemaphoreType.DMA((2,)), pltpu.SemaphoreType.DMA(())],
    )
    return pl.pallas_call(
        functools.partial(_dispatch_kernel, n_experts=n_experts),
        grid_spec=grid_spec,
        out_shape=jax.ShapeDtypeStruct((n_rows, d // 2), U32),
        compiler_params=_params(1, VMEM_LIMIT),
        name="dispatch",
    )(plan["cnt"], plan["lst"], plan["base"], plan["rows"], plan["tail"], post, h2)


def _combine_kernel(cnt_ref, lst_ref, base_ref, rows_ref, post_ref, gatet_ref, x1_ref, ys_ref, o_ref,
                    buf, g_scr, y_scr, sem, *, n_experts):
    j = pl.program_id(0)
    tm, d = x1_ref.shape
    n_local = buf.shape[1]

    def start_runs(tile, slot):
        def run(e, carry):
            r = tile * n_experts + e
            n = cnt_ref[r]

            @pl.when(n > 0)
            def _():
                pltpu.make_async_copy(ys_ref.at[_rows(base_ref[r], n), :],
                                      buf.at[slot, _rows(lst_ref[r], n), :], sem.at[slot]).start()
            return carry
        lax.fori_loop(0, n_experts, run, 0)

    def wait_runs(tile, slot):
        n = rows_ref[tile]

        @pl.when(n > 0)
        def _():
            pltpu.make_async_copy(ys_ref.at[_rows(0, n), :], buf.at[slot, _rows(0, n), :],
                                  sem.at[slot]).wait()

    slot = j % 2
    @pl.when(j == 0)
    def _():
        buf[...] = jnp.zeros_like(buf)
        start_runs(j, slot)

    @pl.when(j + 1 < pl.num_programs(0))
    def _():
        start_runs(j + 1, 1 - slot)

    wait_runs(j, slot)

    post = post_ref[0:TOP_K, :]
    gate = gatet_ref[0:TOP_K, :].astype(BF16)
    chunk_of = lax.shift_right_logical(post, PERM_CHUNK.bit_length() - 1)
    offset = (post & (PERM_CHUNK - 1)).astype(F32)
    rows = lax.broadcasted_iota(I32, (PERM_CHUNK, tm), 0).astype(F32).astype(BF16)
    for rc in range(n_local // PERM_CHUNK):
        chunk = slice(rc * PERM_CHUNK, (rc + 1) * PERM_CHUNK)
        off = jnp.where(chunk_of == rc, offset, -1.0).astype(BF16)
        g = jnp.zeros((PERM_CHUNK, tm), BF16)
        for k in range(TOP_K):
            g = jnp.where(rows == off[k:k + 1, :], jnp.broadcast_to(gate[k:k + 1, :], g.shape), g)
        g_scr[chunk, :] = g
        y_scr[chunk, :d // 2], y_scr[chunk, d // 2:] = _unpack_pairs(buf[slot, chunk, :])
    o_ref[...] = x1_ref[...] + lax.dot_general(g_scr[...], y_scr[...], (((0,), (0,)), ((), ())),
                                               preferred_element_type=F32)


def _combine(plan, post, gatet, x1, ys, n_experts):
    t, d = x1.shape
    tm = min(TOKEN_TILE, t)
    grid_spec = pltpu.PrefetchScalarGridSpec(
        num_scalar_prefetch=4,
        grid=(t // tm,),
        in_specs=[
            pl.BlockSpec((SUBLANES, tm), lambda i, *_: (i, 0)),
            pl.BlockSpec((SUBLANES, tm), lambda i, *_: (i, 0)),
            pl.BlockSpec((tm, d), lambda i, *_: (i, 0)),
            pl.BlockSpec(memory_space=pl.ANY),
        ],
        out_specs=pl.BlockSpec((tm, d), lambda i, *_: (i, 0)),
        scratch_shapes=[pltpu.VMEM((2, _local_rows(tm, n_experts), d // 2), U32),
                        pltpu.VMEM((_local_rows(tm, n_experts), tm), BF16),
                        pltpu.VMEM((_local_rows(tm, n_experts), d), BF16),
                        pltpu.SemaphoreType.DMA((2,))],
    )
    return pl.pallas_call(
        functools.partial(_combine_kernel, n_experts=n_experts),
        grid_spec=grid_spec,
        out_shape=jax.ShapeDtypeStruct((t, d), F32),
        compiler_params=_params(1, VMEM_LIMIT),
        name="combine",
    )(plan["cnt"], plan["lst"], plan["base"], plan["rows"], post, gatet, x1, ys)


def _expert_kernel(be_ref, nxt_ref, par_ref, meta_ref, xs_ref, wgu_hbm, bg_ref, bu_ref, wd_hbm, bd_ref,
                   perm_ref, ys_ref, wgu_buf, wd_buf, wg_s, wu_s, wd_s, wsem):
    i = pl.program_id(0)
    n_used = meta_ref[0]
    active = i < n_used
    new_expert = (i == 0) | (be_ref[i] != be_ref[jnp.maximum(i - 1, 0)])

    def weight_copies(expert, slot):
        return (pltpu.make_async_copy(wgu_hbm.at[expert], wgu_buf.at[slot], wsem.at[slot]),
                pltpu.make_async_copy(wd_hbm.at[expert], wd_buf.at[slot], wsem.at[slot]))

    @pl.when(active & new_expert)
    def _():
        slot = par_ref[i]

        @pl.when(i == 0)
        def _():
            for cp in weight_copies(be_ref[i], slot):
                cp.start()

        for cp in weight_copies(be_ref[i], slot):
            cp.wait()
        nxt = nxt_ref[i]

        @pl.when(nxt >= 0)
        def _():
            for cp in weight_copies(nxt, 1 - slot):
                cp.start()

        width = perm_ref.shape[0]
        for c in range(wgu_buf.shape[2] // width):
            wc = wgu_buf[slot, :, c * width:(c + 1) * width].astype(BF16)
            r = jnp.dot(wc, perm_ref[...], preferred_element_type=F32)
            wg_s[:, c * LANES:(c + 1) * LANES] = r[:, :LANES].astype(BF16)
            wu_s[:, c * LANES:(c + 1) * LANES] = r[:, LANES:].astype(BF16)
        wd_s[...] = wd_buf[slot].astype(BF16)

    @pl.when(active)
    def _():
        xb = jnp.concatenate(_unpack_pairs(xs_ref[...]), axis=1)
        g = jnp.dot(xb, wg_s[...], preferred_element_type=F32) + bg_ref[0]
        up = jnp.dot(xb, wu_s[...], preferred_element_type=F32) + bu_ref[0]
        g = jnp.minimum(g, SWIGLU_LIMIT)
        up = jnp.clip(up, -SWIGLU_LIMIT, SWIGLU_LIMIT)
        act = g * (1.0 / (1.0 + jnp.exp(-SWIGLU_ALPHA * g))) * (up + 1.0)
        ys_ref[...] = _pack_pairs(
            jnp.dot(act.astype(BF16), wd_s[...], preferred_element_type=F32) + bd_ref[0])

    @pl.when(jnp.logical_not(active))
    def _():
        ys_ref[...] = jnp.zeros_like(ys_ref)


def _experts(blk, meta, xs, w_gate_up, b_gate_up, w_down, b_down):
    n_rows = xs.shape[0]
    n_experts, d, f2 = w_gate_up.shape
    f = f2 // 2
    bm = EXPERT_ROWS
    n_blocks = n_rows // bm
    width = 2 * LANES
    perm = np.zeros((width, width), np.float32)
    perm[2 * np.arange(LANES), np.arange(LANES)] = 1.0
    perm[2 * np.arange(LANES) + 1, LANES + np.arange(LANES)] = 1.0
    bg = b_gate_up[:, 0::2].reshape(n_experts, 1, f)
    bu = b_gate_up[:, 1::2].reshape(n_experts, 1, f)
    rows = lambda i, be, nxt, par, meta: (jnp.minimum(i, meta[0] - 1), 0)
    per_e = lambda i, be, nxt, par, meta: (be[i], 0, 0)
    grid_spec = pltpu.PrefetchScalarGridSpec(
        num_scalar_prefetch=4,
        grid=(n_blocks,),
        in_specs=[
            pl.BlockSpec((bm, d // 2), rows),
            pl.BlockSpec(memory_space=pl.ANY),
            pl.BlockSpec((1, 1, f), per_e),
            pl.BlockSpec((1, 1, f), per_e),
            pl.BlockSpec(memory_space=pl.ANY),
            pl.BlockSpec((1, 1, d), per_e),
            pl.BlockSpec((width, width), lambda i, *_: (0, 0)),
        ],
        out_specs=pl.BlockSpec((bm, d // 2), lambda i, *_: (i, 0)),
        scratch_shapes=[pltpu.VMEM((2, d, f2), F32), pltpu.VMEM((2, f, d), F32),
                        pltpu.VMEM((d, f), BF16), pltpu.VMEM((d, f), BF16),
                        pltpu.VMEM((f, d), BF16), pltpu.SemaphoreType.DMA((2,))],
    )
    return pl.pallas_call(
        _expert_kernel,
        grid_spec=grid_spec,
        out_shape=jax.ShapeDtypeStruct((n_rows, d // 2), U32),
        compiler_params=_params(1, VMEM_LIMIT),
        name="experts",
    )(blk["expert"], blk["next"], blk["slot"], meta, xs, w_gate_up, bg, bu, w_down,
      b_down.reshape(n_experts, 1, d), jnp.asarray(perm, BF16))


def _routing_plan(counts, n_experts, bm, n_blocks):
    cnt = counts[:, 0, :n_experts]
    run = (cnt + RUN_ALIGN - 1) // RUN_ALIGN * RUN_ALIGN
    per_expert = jnp.sum(run, axis=0)
    padded = (per_expert + bm - 1) // bm * bm
    pend = jnp.cumsum(padded)
    pstart = pend - padded
    base = pstart[None, :] + jnp.cumsum(run, axis=0) - run
    lst = jnp.cumsum(run, axis=1) - run
    n_used = pend[-1] // bm
    tail = jnp.concatenate([pstart + per_expert, padded - per_expert, n_used[None]])
    starts = jnp.arange(n_blocks, dtype=I32) * bm
    blk = jnp.sum((starts[:, None] >= pend[None, :]).astype(I32), axis=1)
    blk = jnp.minimum(blk, n_experts - 1)
    last = jnp.sum((((n_used - 1) * bm) >= pend).astype(I32))
    blk_e = jnp.where(jnp.arange(n_blocks) < n_used, blk, jnp.minimum(last, n_experts - 1))
    ids = jnp.arange(n_experts, dtype=I32)
    has_rows = padded > 0
    later = jnp.where(has_rows[None, :] & (ids[None, :] > ids[:, None]), ids[None, :], n_experts)
    next_e = jnp.min(later, axis=1)
    next_e = jnp.where(next_e < n_experts, next_e, -1)
    ordinal = jnp.cumsum(has_rows.astype(I32)) - 1
    onehot = blk_e[:, None] == ids[None, :]
    blocks = {"expert": blk_e.astype(I32),
              "next": jnp.sum(jnp.where(onehot, next_e[None, :], 0), axis=1).astype(I32),
              "slot": jnp.sum(jnp.where(onehot, (ordinal % 2)[None, :], 0), axis=1).astype(I32)}
    plan = {"cnt": run.reshape(-1).astype(I32), "lst": lst.reshape(-1).astype(I32),
            "base": base.reshape(-1).astype(I32), "rows": jnp.sum(run, axis=1).astype(I32),
            "tail": tail.astype(I32)}
    return plan, blocks, n_used.astype(I32).reshape(1)


def _layer(x2d, batch, seq, norm1, w_in, q_norm, k_norm, sinks, rel_bias, w_fourier, g_fourier_out,
           g_attn_out, w_out, norm2, w_router, b_router, w_gate_up, b_gate_up, w_down, b_down):
    t, d = x2d.shape
    n_experts = w_router.shape[1]
    u, q, kv = _inproj(x2d, norm1, w_in, q_norm, k_norm)
    yf = _fourier(u, w_fourier, g_fourier_out, batch, seq)
    ya = _attention(q, kv, sinks, rel_bias, g_attn_out, batch, seq)
    x1, h2, post, gatet, counts = _outproj(yf, ya, x2d, w_out, norm2, w_router, b_router)
    bm = EXPERT_ROWS
    n_tiles = t // min(TOKEN_TILE, t)
    worst_rows = t * TOP_K + n_tiles * n_experts * (RUN_ALIGN - 1) + n_experts * (bm - RUN_ALIGN)
    n_blocks = -(-worst_rows // bm)
    plan, blocks, meta = _routing_plan(counts, n_experts, bm, n_blocks)
    xs = _dispatch(plan, post, h2, n_blocks * bm, n_experts)
    ys = _experts(blocks, meta, xs, w_gate_up, b_gate_up, w_down, b_down)
    return _combine(plan, post, gatet, x1, ys, n_experts)


def kernel(x, norm1, w_in, q_norm, k_norm, sinks, rel_bias, w_fourier, g_fourier_out, g_attn_out,
           w_out, norm2, w_router, b_router, w_gate_up, b_gate_up, w_down, b_down):
    b, s, d = x.shape
    x2d = x.reshape(b * s, d)
    for l in range(norm1.shape[0]):
        x2d = _layer(x2d, b, s, norm1[l], w_in[l], q_norm[l], k_norm[l], sinks[l], rel_bias,
                     w_fourier[l], g_fourier_out[l], g_attn_out[l], w_out[l], norm2[l],
                     w_router[l], b_router[l], w_gate_up[l], b_gate_up[l], w_down[l], b_down[l])
    return x2d.reshape(b, s, d)
```

```python
import functools
import math

import jax
import jax.numpy as jnp
import numpy as np
from jax import lax
from jax.experimental import pallas as pl
from jax.experimental.pallas import tpu as pltpu

F32 = jnp.float32
BF16 = jnp.bfloat16
I32 = jnp.int32
U32 = jnp.uint32

NORM_EPS = 1e-5
QK_EPS = 1e-6
HEAD_DIM = 64
N_Q_HEADS = 8
N_KV_HEADS = 2
FOURIER_GROUPS = 4
FOURIER_CH = 128
FOURIER_WIDTH = FOURIER_GROUPS * FOURIER_CH
ATTN_WIDTH = N_Q_HEADS * HEAD_DIM
KV_WIDTH = N_KV_HEADS * HEAD_DIM
WINDOW = 128
Q_BLOCK = 128
N_BUCKETS = 32
MAX_DISTANCE = 128
TOP_K = 4
SWIGLU_ALPHA = 1.702
SWIGLU_LIMIT = 7.0
MASK_VALUE = -1e30
LOG2E = math.log2(math.e)

LANES = 128
SUBLANES = 8
TOKEN_TILE = 512
RUN_ALIGN = SUBLANES
ROW_GROUPS = 2
PERM_CHUNK = 256
DISPATCH_ROWS = 3 * PERM_CHUNK
EXPERT_ROWS = 256
FOURIER_ROWS = 512
BF16_BITS = 16
VMEM_LIMIT = 56 * 1024 * 1024


def _params(n_axes, vmem=None):
    return pltpu.CompilerParams(
        dimension_semantics=("arbitrary",) * n_axes, vmem_limit_bytes=vmem)


def _pair_head_norm(xc, gain, lo):
    x2 = xc * xc
    s_lo = jnp.sum(jnp.where(lo, x2, 0.0), axis=-1, keepdims=True)
    s_hi = jnp.sum(jnp.where(lo, 0.0, x2), axis=-1, keepdims=True)
    inv = jnp.where(lo, lax.rsqrt(s_lo * (1.0 / HEAD_DIM) + QK_EPS),
                    lax.rsqrt(s_hi * (1.0 / HEAD_DIM) + QK_EPS))
    return xc * inv * gain


def _inproj_kernel(x_ref, g1_ref, w_ref, qg_ref, kg_ref, u_ref, q_ref, kv_ref):
    rows = x_ref.shape[0] // ROW_GROUPS
    lo = lax.broadcasted_iota(I32, (rows, LANES), 1) < HEAD_DIM
    q0 = FOURIER_WIDTH
    k0 = q0 + ATTN_WIDTH
    for grp in range(ROW_GROUPS):
        rs = slice(grp * rows, (grp + 1) * rows)
        x = x_ref[rs, :]
        ms = jnp.mean(x * x, axis=-1, keepdims=True)
        h = (x * lax.rsqrt(ms + NORM_EPS) * g1_ref[...]).astype(BF16)
        z = jnp.dot(h, w_ref[...], preferred_element_type=F32)
        u_ref[rs, :] = z[:, :FOURIER_WIDTH].astype(BF16)
        for c in range(ATTN_WIDTH // LANES):
            qc = _pair_head_norm(z[:, q0 + c * LANES:q0 + (c + 1) * LANES], qg_ref[...], lo)
            q_ref[rs, c * LANES:(c + 1) * LANES] = (qc * (HEAD_DIM ** -0.5 * LOG2E)).astype(BF16)
        kc = _pair_head_norm(z[:, k0:k0 + KV_WIDTH], kg_ref[...], lo)
        vc = z[:, k0 + KV_WIDTH:k0 + 2 * KV_WIDTH]
        kv_ref[rs, 0:LANES] = kc.astype(BF16)
        kv_ref[rs, LANES:2 * LANES] = pltpu.roll(kc, HEAD_DIM, 1).astype(BF16)
        kv_ref[rs, 2 * LANES:3 * LANES] = vc.astype(BF16)
        kv_ref[rs, 3 * LANES:4 * LANES] = pltpu.roll(vc, HEAD_DIM, 1).astype(BF16)


def _inproj(x2d, norm1, w_in, q_norm, k_norm):
    t, d = x2d.shape
    tm = min(TOKEN_TILE, t)
    n_in = w_in.shape[1]
    qg = jnp.tile(q_norm, LANES // HEAD_DIM).reshape(1, LANES)
    kg = jnp.tile(k_norm, LANES // HEAD_DIM).reshape(1, LANES)
    full = lambda i: (0, 0)
    return pl.pallas_call(
        _inproj_kernel,
        grid=(t // tm,),
        in_specs=[
            pl.BlockSpec((tm, d), lambda i: (i, 0)),
            pl.BlockSpec((1, d), full),
            pl.BlockSpec((d, n_in), full),
            pl.BlockSpec((1, LANES), full),
            pl.BlockSpec((1, LANES), full),
        ],
        out_specs=[
            pl.BlockSpec((tm, FOURIER_WIDTH), lambda i: (i, 0)),
            pl.BlockSpec((tm, ATTN_WIDTH), lambda i: (i, 0)),
            pl.BlockSpec((tm, 4 * LANES), lambda i: (i, 0)),
        ],
        out_shape=[
            jax.ShapeDtypeStruct((t, FOURIER_WIDTH), BF16),
            jax.ShapeDtypeStruct((t, ATTN_WIDTH), BF16),
            jax.ShapeDtypeStruct((t, 4 * LANES), BF16),
        ],
        compiler_params=_params(1, VMEM_LIMIT),
        name="inproj",
    )(x2d, norm1.reshape(1, d), w_in.astype(BF16), qg, kg)


def _fourier_kernel(u_ref, cs_ref, ss_ref, rev_ref, cc_ref, sc_ref, wf_ref, g_ref, o_ref,
                    p_scr, q_scr, e_scr, *, scale, row_block):
    for g in range(FOURIER_GROUPS):
        sl = slice(g * FOURIER_CH, (g + 1) * FOURIER_CH)
        w = wf_ref[g].astype(BF16)
        a = (jnp.dot(cc_ref[...], w, preferred_element_type=F32) * scale).astype(BF16)
        b = (jnp.dot(sc_ref[...], w, preferred_element_type=F32) * scale).astype(BF16)
        ug = u_ref[:, sl]
        p_scr[:, sl] = jnp.dot(ug, a, preferred_element_type=F32).astype(BF16)
        q_scr[:, sl] = jnp.dot(ug, b, preferred_element_type=F32).astype(BF16)
    half = u_ref.shape[0] // 2
    gain = g_ref[...]

    def norm(y):
        ms = jnp.mean(y * y, axis=-1, keepdims=True)
        return y * lax.rsqrt(ms + NORM_EPS) * gain

    n_blk = half // row_block
    mid = None
    for r in range(n_blk):
        rs = slice(r * row_block, (r + 1) * row_block)
        extra = SUBLANES if r == n_blk - 1 else 0
        c = jnp.dot(cs_ref[r * row_block:(r + 1) * row_block + extra, :], p_scr[...],
                    preferred_element_type=F32)
        d = jnp.dot(ss_ref[rs, :], q_scr[...], preferred_element_type=F32)
        o_ref[rs, :] = norm(c[:row_block] + d).astype(BF16)
        e_scr[rs, :] = norm(c[:row_block] - d).astype(BF16)
        if extra:
            mid = norm(c[row_block:row_block + 1])
    for r in range(n_blk):
        z = jnp.dot(rev_ref[r * row_block:(r + 1) * row_block, :], e_scr[...],
                    preferred_element_type=F32)
        if r == 0:
            z = jnp.where(lax.broadcasted_iota(I32, z.shape, 0) == 0, mid, z)
        o_ref[half + r * row_block:half + (r + 1) * row_block, :] = z.astype(BF16)


def _dft_tables(n):
    k = np.arange(n, dtype=np.int64)
    ang = 2.0 * np.pi * ((k[:, None] * k[None, :]) % n).astype(np.float64) / n
    return np.cos(ang), np.sin(ang)


def _fourier(u, w_fourier, g_out, batch, seq):
    cs, ss = _dft_tables(seq)
    cc, sc = _dft_tables(FOURIER_CH)
    scale = 1.0 / math.sqrt(seq * FOURIER_CH)
    half = seq // 2
    row_block = min(FOURIER_ROWS, half)
    rev = np.zeros((half, half), np.float32)
    rev[np.arange(1, half), half - np.arange(1, half)] = 1.0
    full2 = lambda b: (0, 0)
    return pl.pallas_call(
        functools.partial(_fourier_kernel, scale=scale, row_block=row_block),
        grid=(batch,),
        in_specs=[
            pl.BlockSpec((seq, FOURIER_WIDTH), lambda b: (b, 0)),
            pl.BlockSpec((half + SUBLANES, seq), full2),
            pl.BlockSpec((half, seq), full2),
            pl.BlockSpec((half, half), full2),
            pl.BlockSpec((FOURIER_CH, FOURIER_CH), full2),
            pl.BlockSpec((FOURIER_CH, FOURIER_CH), full2),
            pl.BlockSpec((FOURIER_GROUPS, FOURIER_CH, FOURIER_CH), lambda b: (0, 0, 0)),
            pl.BlockSpec((1, FOURIER_WIDTH), full2),
        ],
        out_specs=pl.BlockSpec((seq, FOURIER_WIDTH), lambda b: (b, 0)),
        out_shape=jax.ShapeDtypeStruct((batch * seq, FOURIER_WIDTH), BF16),
        scratch_shapes=[pltpu.VMEM((seq, FOURIER_WIDTH), BF16),
                        pltpu.VMEM((seq, FOURIER_WIDTH), BF16),
                        pltpu.VMEM((half, FOURIER_WIDTH), BF16)],
        compiler_params=_params(1, VMEM_LIMIT),
        name="fourier",
    )(u, jnp.asarray(cs[:half + SUBLANES], BF16), jnp.asarray(ss[:half], BF16),
      jnp.asarray(rev, BF16), jnp.asarray(cc, BF16), jnp.asarray(-sc, BF16), w_fourier,
      g_out.reshape(1, FOURIER_WIDTH))


def _attn_kernel(sink_ref, q_ref, kvp_ref, kvo_ref, kvn_ref, bias_a_ref, bias_b_ref, g_ref, o_ref,
                 acc_ref):
    kv = jnp.concatenate([kvp_ref[...], kvo_ref[...], kvn_ref[...]], axis=0)
    nk = kv.shape[0]
    lo = lax.broadcasted_iota(I32, (nk, LANES), 1) < HEAD_DIM
    k_a, k_b = kv[:, 0:LANES], kv[:, LANES:2 * LANES]
    v_a, v_b = kv[:, 2 * LANES:3 * LANES], kv[:, 3 * LANES:4 * LANES]
    zero = jnp.zeros_like(k_a)
    k_lo = (jnp.where(lo, k_a, zero), jnp.where(lo, k_b, zero))
    k_hi = (jnp.where(lo, zero, k_b), jnp.where(lo, zero, k_a))
    v_lo = (jnp.where(lo, v_a, zero), jnp.where(lo, v_b, zero))
    v_hi = (jnp.where(lo, zero, v_b), jnp.where(lo, zero, v_a))
    lo_out = lax.broadcasted_iota(I32, (Q_BLOCK, LANES), 1) < HEAD_DIM
    bias_refs = (bias_a_ref, bias_b_ref)
    rows2 = q_ref.shape[0]
    nt = (((1,), (1,)), ((), ()))
    for h in range(N_KV_HEADS):
        qs = jnp.concatenate([q_ref[:, (2 * h) * LANES:(2 * h + 1) * LANES],
                              q_ref[:, (2 * h + 1) * LANES:(2 * h + 2) * LANES]], axis=0)
        s_par = (lax.dot_general(qs, k_lo[h], nt, preferred_element_type=F32),
                 lax.dot_general(qs, k_hi[h], nt, preferred_element_type=F32))
        for sb in range(2):
            keys = slice(sb * Q_BLOCK, sb * Q_BLOCK + 3 * Q_BLOCK)
            vcat = jnp.concatenate([v_lo[h][keys, :], v_hi[h][keys, :]], axis=0)
            for c in range(2):
                r0 = c * rows2 + sb * Q_BLOCK
                probs, invs = [], []
                for par in range(2):
                    hq = 4 * h + 2 * c + par
                    s = s_par[par][r0:r0 + Q_BLOCK, keys] + bias_refs[sb][hq]
                    sink = sink_ref[hq]
                    m = jnp.maximum(jnp.max(s, axis=-1, keepdims=True), sink)
                    p = jnp.exp2(s - m)
                    denom = jnp.sum(p, axis=-1, keepdims=True) + jnp.exp2(sink - m)
                    probs.append(p.astype(BF16))
                    invs.append(1.0 / denom)
                pcat = jnp.concatenate(probs, axis=1)
                chunk = 2 * h + c
                o = jnp.dot(pcat, vcat, preferred_element_type=F32)
                acc_ref[sb * Q_BLOCK:(sb + 1) * Q_BLOCK, chunk * LANES:(chunk + 1) * LANES] = (
                    o * jnp.where(lo_out, invs[0], invs[1]))
    y = acc_ref[...]
    ms = jnp.mean(y * y, axis=-1, keepdims=True)
    o_ref[...] = (y * lax.rsqrt(ms + NORM_EPS) * g_ref[...]).astype(BF16)


def _t5_bucket(rel):
    nb = N_BUCKETS // 2
    max_exact = nb // 2
    ret = (rel > 0).astype(jnp.int32) * nb
    n = jnp.abs(rel)
    nf = jnp.maximum(n, 1).astype(jnp.float32)
    large = max_exact + (jnp.log(nf / max_exact) / math.log(MAX_DISTANCE / max_exact)
                         * (nb - max_exact)).astype(jnp.int32)
    large = jnp.minimum(large, nb - 1)
    return ret + jnp.where(n < max_exact, n, large)


def _attention(q, kv, sinks, rel_bias, g_out, batch, seq):
    nb = seq // Q_BLOCK
    assert nb % 2 == 0
    nb2 = nb // 2
    qi = jnp.arange(Q_BLOCK, dtype=jnp.int32)[:, None]
    kj = jnp.arange(3 * Q_BLOCK, dtype=jnp.int32)[None, :]
    rel = kj - Q_BLOCK - qi
    period = 4 * Q_BLOCK
    p = jnp.arange(period, dtype=jnp.int32)
    off = jnp.where(p < 3 * Q_BLOCK, p, p - period) - Q_BLOCK
    hit = _t5_bucket(off)[None, :, None] == jnp.arange(N_BUCKETS, dtype=jnp.int32)
    by_off = jnp.sum(jnp.where(hit, rel_bias.astype(F32).T[:, None, :], 0.0), axis=-1)
    bias = jnp.tile(by_off, (1, Q_BLOCK))[:, :Q_BLOCK * (period - 1)]
    bias = bias.reshape(N_Q_HEADS, Q_BLOCK, period - 1)[:, :, :3 * Q_BLOCK]
    band = jnp.abs(rel) <= WINDOW
    first = band & (kj >= Q_BLOCK)
    last = band & (kj < 2 * Q_BLOCK)
    table = jnp.stack([jnp.where(msk[None], bias * LOG2E, MASK_VALUE) for msk in (first, band, last)])
    q_rows = 2 * Q_BLOCK
    grid_spec = pltpu.PrefetchScalarGridSpec(
        num_scalar_prefetch=1,
        grid=(batch, nb2),
        in_specs=[
            pl.BlockSpec((q_rows, ATTN_WIDTH), lambda b, i, s: (b * nb2 + i, 0)),
            pl.BlockSpec((Q_BLOCK, 4 * LANES),
                         lambda b, i, s: (b * nb + jnp.maximum(2 * i - 1, 0), 0)),
            pl.BlockSpec((q_rows, 4 * LANES), lambda b, i, s: (b * nb2 + i, 0)),
            pl.BlockSpec((Q_BLOCK, 4 * LANES),
                         lambda b, i, s: (b * nb + jnp.minimum(2 * i + 2, nb - 1), 0)),
            pl.BlockSpec((None, N_Q_HEADS, Q_BLOCK, 3 * Q_BLOCK),
                         lambda b, i, s: (jnp.where(i == 0, 0, 1), 0, 0, 0)),
            pl.BlockSpec((None, N_Q_HEADS, Q_BLOCK, 3 * Q_BLOCK),
                         lambda b, i, s: (jnp.where(i == nb2 - 1, 2, 1), 0, 0, 0)),
            pl.BlockSpec((1, ATTN_WIDTH), lambda b, i, s: (0, 0)),
        ],
        out_specs=pl.BlockSpec((q_rows, ATTN_WIDTH), lambda b, i, s: (b * nb2 + i, 0)),
        scratch_shapes=[pltpu.VMEM((q_rows, ATTN_WIDTH), F32)],
    )
    return pl.pallas_call(
        _attn_kernel,
        grid_spec=grid_spec,
        out_shape=jax.ShapeDtypeStruct((batch * seq, ATTN_WIDTH), BF16),
        compiler_params=_params(2, VMEM_LIMIT),
        name="attention",
    )(sinks.astype(F32) * LOG2E, q, kv, kv, kv, table, table, g_out.reshape(1, ATTN_WIDTH))


def _outproj_kernel(yf_ref, ya_ref, x_ref, wo_ref, g2_ref, wrt_ref, brt_ref, triu_ref, scan_ref,
                    x1_ref, h2_ref, post_ref, gatet_ref, cnt_ref, *, n_experts):
    half = yf_ref.shape[1]
    mix = (jnp.dot(yf_ref[...], wo_ref[:half, :], preferred_element_type=F32)
           + jnp.dot(ya_ref[...], wo_ref[half:, :], preferred_element_type=F32))
    x1 = x_ref[...] + mix
    x1_ref[...] = x1
    ms = jnp.mean(x1 * x1, axis=-1, keepdims=True)
    h2 = x1 * lax.rsqrt(ms + NORM_EPS) * g2_ref[...]
    h2_ref[...] = h2.astype(BF16)
    h_hi = h2.astype(BF16)
    h_lo = (h2 - h_hi.astype(F32)).astype(BF16)
    nt = (((1,), (1,)), ((), ()))
    t1 = lax.dot_general(wrt_ref[...], h_hi, nt, preferred_element_type=F32)
    t2 = lax.dot_general(wrt_ref[:n_experts, :], h_lo, nt, preferred_element_type=F32)
    logits = t1[:n_experts] + t1[n_experts:] + t2 + brt_ref[...]
    tm = logits.shape[1]
    sub_e = lax.broadcasted_iota(I32, (n_experts, tm), 0).astype(F32)
    work = logits
    vals, idxs = [], []
    for _ in range(TOP_K):
        m = jnp.max(work, axis=0, keepdims=True)
        ik = jnp.min(jnp.where(work == m, sub_e, float(n_experts)), axis=0, keepdims=True)
        work = jnp.where(sub_e == ik, -jnp.inf, work)
        vals.append(m)
        idxs.append(ik)
    exps = [jnp.exp(v - vals[0]) for v in vals]
    inv = 1.0 / (exps[0] + exps[1] + exps[2] + exps[3])
    gates = [e * inv for e in exps]

    sub = lax.broadcasted_iota(I32, (LANES, tm), 0).astype(F32)
    onehot = jnp.zeros((LANES, tm), F32)
    for k in range(TOP_K):
        onehot = onehot + jnp.where(sub == idxs[k] + float(k * n_experts), 1.0, 0.0)
    onehot_b = onehot.astype(BF16)
    before = jnp.dot(onehot_b, triu_ref[...], preferred_element_type=F32)
    counts = lax.dot_general(jnp.ones((SUBLANES, tm), BF16), onehot_b, nt,
                             preferred_element_type=F32)[0:1, :]
    total = counts
    for k in range(1, TOP_K):
        total = total + pltpu.roll(counts, k * n_experts, 1)
    cnt_ref[0] = total.astype(I32)
    colsum = jnp.sum(onehot, axis=1, keepdims=True)
    blocks = [colsum[k * n_experts:(k + 1) * n_experts] for k in range(TOP_K)]
    total_e = blocks[0] + blocks[1] + blocks[2] + blocks[3]
    run_e = jnp.floor((total_e + (RUN_ALIGN - 1)) * (1.0 / RUN_ALIGN)) * RUN_ALIGN
    run_start = jnp.dot(scan_ref[...], jnp.broadcast_to(run_e, (n_experts, LANES)).astype(BF16),
                        preferred_element_type=F32)[:, 0:1]
    adds, acc = [], run_start
    for k in range(TOP_K):
        adds.append(acc)
        acc = acc + blocks[k]
    placed = (before + jnp.concatenate(adds, axis=0)) * onehot
    pos = [jnp.sum(placed[k * n_experts:(k + 1) * n_experts], axis=0, keepdims=True)
           for k in range(TOP_K)]
    post_ref[...] = jnp.concatenate(pos + [jnp.zeros((SUBLANES - TOP_K, tm), F32)], axis=0).astype(I32)
    gatet_ref[...] = jnp.concatenate(gates + [jnp.zeros((SUBLANES - TOP_K, tm), F32)], axis=0)


def _outproj(yf, ya, x2d, w_out, norm2, w_router, b_router):
    t, d = x2d.shape
    tm = min(TOKEN_TILE, t)
    n_tiles = t // tm
    n_experts = w_router.shape[1]
    assert TOP_K * n_experts == LANES
    triu = np.triu(np.ones((tm, tm), np.float32), 1)
    scan = np.tril(np.ones((n_experts, n_experts), np.float32), -1)
    wr_hi = w_router.astype(BF16)
    wr_lo = (w_router - wr_hi.astype(F32)).astype(BF16)
    wrt = jnp.concatenate([wr_hi.T, wr_lo.T], axis=0)
    full = lambda i: (0, 0)
    row = lambda i: (i, 0)
    return pl.pallas_call(
        functools.partial(_outproj_kernel, n_experts=n_experts),
        grid=(n_tiles,),
        in_specs=[
            pl.BlockSpec((tm, yf.shape[1]), row),
            pl.BlockSpec((tm, ya.shape[1]), row),
            pl.BlockSpec((tm, d), row),
            pl.BlockSpec((w_out.shape[0], d), full),
            pl.BlockSpec((1, d), full),
            pl.BlockSpec((2 * n_experts, d), full),
            pl.BlockSpec((n_experts, 1), full),
            pl.BlockSpec((tm, tm), full),
            pl.BlockSpec((n_experts, n_experts), full),
        ],
        out_specs=[
            pl.BlockSpec((tm, d), row),
            pl.BlockSpec((tm, d), row),
            pl.BlockSpec((SUBLANES, tm), row),
            pl.BlockSpec((SUBLANES, tm), row),
            pl.BlockSpec((1, 1, LANES), lambda i: (i, 0, 0)),
        ],
        out_shape=[
            jax.ShapeDtypeStruct((t, d), F32),
            jax.ShapeDtypeStruct((t, d), BF16),
            jax.ShapeDtypeStruct((n_tiles * SUBLANES, tm), I32),
            jax.ShapeDtypeStruct((n_tiles * SUBLANES, tm), F32),
            jax.ShapeDtypeStruct((n_tiles, 1, LANES), I32),
        ],
        compiler_params=_params(1, VMEM_LIMIT),
        name="outproj_router",
    )(yf, ya, x2d, w_out.astype(BF16), norm2.reshape(1, d), wrt, b_router.reshape(n_experts, 1),
      jnp.asarray(triu, BF16), jnp.asarray(scan, BF16))


def _pack_pairs(x, is_bf16_exact=False):
    half = x.shape[1] // 2
    a, b = x[:, :half], x[:, half:]
    if not is_bf16_exact:
        a, b = a.astype(BF16).astype(F32), b.astype(BF16).astype(F32)
    return lax.bitcast_convert_type(a, U32) | (lax.bitcast_convert_type(b, U32) >> BF16_BITS)


def _unpack_pairs(w):
    hi = lax.bitcast_convert_type(w & U32(((1 << BF16_BITS) - 1) << BF16_BITS), F32)
    lo = lax.bitcast_convert_type(w << BF16_BITS, F32)
    return hi.astype(BF16), lo.astype(BF16)


def _rows(start, size):
    if not isinstance(size, int):
        size = pl.multiple_of(size, RUN_ALIGN)
    return pl.ds(pl.multiple_of(start, RUN_ALIGN), size)


def _dispatch_kernel(cnt_ref, lst_ref, base_ref, rows_ref, tail_ref, post_ref, h2_ref, xs_ref,
                     buf, zbuf, sem, zsem, *, n_experts):
    j = pl.program_id(0)
    tm = h2_ref.shape[0]
    n_local = buf.shape[1]

    def start_runs(tile, slot):
        def run(e, carry):
            r = tile * n_experts + e
            n = cnt_ref[r]

            @pl.when(n > 0)
            def _():
                pltpu.make_async_copy(buf.at[slot, _rows(lst_ref[r], n), :],
                                      xs_ref.at[_rows(base_ref[r], n), :], sem.at[slot]).start()
            return carry
        lax.fori_loop(0, n_experts, run, 0)

    def wait_runs(tile, slot):
        n = rows_ref[tile]

        @pl.when(n > 0)
        def _():
            pltpu.make_async_copy(buf.at[slot, _rows(0, n), :], xs_ref.at[_rows(0, n), :],
                                  sem.at[slot]).wait()

    def zero_fill(op):
        def tail(e, carry):
            n = tail_ref[n_experts + e]

            @pl.when(n > 0)
            def _():
                getattr(pltpu.make_async_copy(zbuf.at[_rows(0, n), :],
                                              xs_ref.at[_rows(tail_ref[e], n), :], zsem), op)()
            return carry
        lax.fori_loop(0, n_experts, tail, 0)

        def spare(b, carry):
            getattr(pltpu.make_async_copy(zbuf, xs_ref.at[_rows(b * EXPERT_ROWS, EXPERT_ROWS), :],
                                          zsem), op)()
            return carry
        lax.fori_loop(tail_ref[2 * n_experts], xs_ref.shape[0] // EXPERT_ROWS, spare, 0)

    slot = j % 2
    @pl.when(j >= 2)
    def _():
        wait_runs(j - 2, slot)

    @pl.when(j == 0)
    def _():
        zbuf[...] = jnp.zeros_like(zbuf)
        zero_fill("start")
        zero_fill("wait")

    h = h2_ref[...]
    post = post_ref[0:TOP_K, :]
    chunk_of = lax.shift_right_logical(post, PERM_CHUNK.bit_length() - 1)
    offset = (post & (PERM_CHUNK - 1)).astype(F32)
    rows = lax.broadcasted_iota(I32, (PERM_CHUNK, tm), 0).astype(F32).astype(BF16)
    one = jnp.ones((PERM_CHUNK, tm), BF16)
    per = DISPATCH_ROWS // PERM_CHUNK
    for mc in range(n_local // DISPATCH_ROWS):
        parts = []
        for rc in range(mc * per, (mc + 1) * per):
            off = jnp.where(chunk_of == rc, offset, -1.0).astype(BF16)
            perm = jnp.zeros((PERM_CHUNK, tm), BF16)
            for k in range(TOP_K):
                perm = jnp.where(rows == off[k:k + 1, :], one, perm)
            parts.append(perm)
        rs = slice(mc * DISPATCH_ROWS, (mc + 1) * DISPATCH_ROWS)
        buf[slot, rs, :] = _pack_pairs(
            jnp.dot(jnp.concatenate(parts, axis=0), h, preferred_element_type=F32), True)

    start_runs(j, slot)

    @pl.when(j == pl.num_programs(0) - 1)
    def _():
        @pl.when(j >= 1)
        def _():
            wait_runs(j - 1, 1 - slot)
        wait_runs(j, slot)


def _local_rows(tm, n_experts):
    worst = TOP_K * tm + n_experts * (RUN_ALIGN - 1)
    return -(-worst // DISPATCH_ROWS) * DISPATCH_ROWS


def _dispatch(plan, post, h2, n_rows, n_experts):
    t, d = h2.shape
    tm = min(TOKEN_TILE, t)
    grid_spec = pltpu.PrefetchScalarGridSpec(
        num_scalar_prefetch=5,
        grid=(t // tm,),
        in_specs=[
            pl.BlockSpec((SUBLANES, tm), lambda i, *_: (i, 0)),
            pl.BlockSpec((tm, d), lambda i, *_: (i, 0)),
        ],
        out_specs=pl.BlockSpec(memory_space=pl.ANY),
        scratch_shapes=[pltpu.VMEM((2, _local_rows(tm, n_experts), d // 2), U32),
                        pltpu.VMEM((EXPERT_ROWS, d // 2), U32),
                        pltpu.SemaphoreType.DMA((2,)), pltpu.SemaphoreType.DMA(())],
    )
    return pl.pallas_call(
        functools.partial(_dispatch_kernel, n_experts=n_experts),
        grid_spec=grid_spec,
        out_shape=jax.ShapeDtypeStruct((n_rows, d // 2), U32),
        compiler_params=_params(1, VMEM_LIMIT),
        name="dispatch",
    )(plan["cnt"], plan["lst"], plan["base"], plan["rows"], plan["tail"], post, h2)


def _combine_kernel(cnt_ref, lst_ref, base_ref, rows_ref, post_ref, gatet_ref, x1_ref, ys_ref, o_ref,
                    buf, g_scr, y_scr, sem, *, n_experts):
    j = pl.program_id(0)
    tm, d = x1_ref.shape
    n_local = buf.shape[1]

    def start_runs(tile, slot):
        def run(e, carry):
            r = tile * n_experts + e
            n = cnt_ref[r]

            @pl.when(n > 0)
            def _():
                pltpu.make_async_copy(ys_ref.at[_rows(base_ref[r], n), :],
                                      buf.at[slot, _rows(lst_ref[r], n), :], sem.at[slot]).start()
            return carry
        lax.fori_loop(0, n_experts, run, 0)

    def wait_runs(tile, slot):
        n = rows_ref[tile]

        @pl.when(n > 0)
        def _():
            pltpu.make_async_copy(ys_ref.at[_rows(0, n), :], buf.at[slot, _rows(0, n), :],
                                  sem.at[slot]).wait()

    slot = j % 2
    @pl.when(j == 0)
    def _():
        buf[...] = jnp.zeros_like(buf)
        start_runs(j, slot)

    @pl.when(j + 1 < pl.num_programs(0))
    def _():
        start_runs(j + 1, 1 - slot)

    wait_runs(j, slot)

    post = post_ref[0:TOP_K, :]
    gate = gatet_ref[0:TOP_K, :].astype(BF16)
    chunk_of = lax.shift_right_logical(post, PERM_CHUNK.bit_length() - 1)
    offset = (post & (PERM_CHUNK - 1)).astype(F32)
    rows = lax.broadcasted_iota(I32, (PERM_CHUNK, tm), 0).astype(F32).astype(BF16)
    for rc in range(n_local // PERM_CHUNK):
        chunk = slice(rc * PERM_CHUNK, (rc + 1) * PERM_CHUNK)
        off = jnp.where(chunk_of == rc, offset, -1.0).astype(BF16)
        g = jnp.zeros((PERM_CHUNK, tm), BF16)
        for k in range(TOP_K):
            g = jnp.where(rows == off[k:k + 1, :], jnp.broadcast_to(gate[k:k + 1, :], g.shape), g)
        g_scr[chunk, :] = g
        y_scr[chunk, :d // 2], y_scr[chunk, d // 2:] = _unpack_pairs(buf[slot, chunk, :])
    o_ref[...] = x1_ref[...] + lax.dot_general(g_scr[...], y_scr[...], (((0,), (0,)), ((), ())),
                                               preferred_element_type=F32)


def _combine(plan, post, gatet, x1, ys, n_experts):
    t, d = x1.shape
    tm = min(TOKEN_TILE, t)
    grid_spec = pltpu.PrefetchScalarGridSpec(
        num_scalar_prefetch=4,
        grid=(t // tm,),
        in_specs=[
            pl.BlockSpec((SUBLANES, tm), lambda i, *_: (i, 0)),
            pl.BlockSpec((SUBLANES, tm), lambda i, *_: (i, 0)),
            pl.BlockSpec((tm, d), lambda i, *_: (i, 0)),
            pl.BlockSpec(memory_space=pl.ANY),
        ],
        out_specs=pl.BlockSpec((tm, d), lambda i, *_: (i, 0)),
        scratch_shapes=[pltpu.VMEM((2, _local_rows(tm, n_experts), d // 2), U32),
                        pltpu.VMEM((_local_rows(tm, n_experts), tm), BF16),
                        pltpu.VMEM((_local_rows(tm, n_experts), d), BF16),
                        pltpu.SemaphoreType.DMA((2,))],
    )
    return pl.pallas_call(
        functools.partial(_combine_kernel, n_experts=n_experts),
        grid_spec=grid_spec,
        out_shape=jax.ShapeDtypeStruct((t, d), F32),
        compiler_params=_params(1, VMEM_LIMIT),
        name="combine",
    )(plan["cnt"], plan["lst"], plan["base"], plan["rows"], post, gatet, x1, ys)


def _expert_kernel(be_ref, nxt_ref, par_ref, meta_ref, xs_ref, wgu_hbm, bg_ref, bu_ref, wd_hbm, bd_ref,
                   perm_ref, ys_ref, wgu_buf, wd_buf, wg_s, wu_s, wd_s, wsem):
    i = pl.program_id(0)
    n_used = meta_ref[0]
    active = i < n_used
    new_expert = (i == 0) | (be_ref[i] != be_ref[jnp.maximum(i - 1, 0)])

    def weight_copies(expert, slot):
        return (pltpu.make_async_copy(wgu_hbm.at[expert], wgu_buf.at[slot], wsem.at[slot]),
                pltpu.make_async_copy(wd_hbm.at[expert], wd_buf.at[slot], wsem.at[slot]))

    @pl.when(active & new_expert)
    def _():
        slot = par_ref[i]

        @pl.when(i == 0)
        def _():
            for cp in weight_copies(be_ref[i], slot):
                cp.start()

        for cp in weight_copies(be_ref[i], slot):
            cp.wait()
        nxt = nxt_ref[i]

        @pl.when(nxt >= 0)
        def _():
            for cp in weight_copies(nxt, 1 - slot):
                cp.start()

        width = perm_ref.shape[0]
        for c in range(wgu_buf.shape[2] // width):
            wc = wgu_buf[slot, :, c * width:(c + 1) * width].astype(BF16)
            r = jnp.dot(wc, perm_ref[...], preferred_element_type=F32)
            wg_s[:, c * LANES:(c + 1) * LANES] = r[:, :LANES].astype(BF16)
            wu_s[:, c * LANES:(c + 1) * LANES] = r[:, LANES:].astype(BF16)
        wd_s[...] = wd_buf[slot].astype(BF16)

    @pl.when(active)
    def _():
        xb = jnp.concatenate(_unpack_pairs(xs_ref[...]), axis=1)
        g = jnp.dot(xb, wg_s[...], preferred_element_type=F32) + bg_ref[0]
        up = jnp.dot(xb, wu_s[...], preferred_element_type=F32) + bu_ref[0]
        g = jnp.minimum(g, SWIGLU_LIMIT)
        up = jnp.clip(up, -SWIGLU_LIMIT, SWIGLU_LIMIT)
        act = g * (1.0 / (1.0 + jnp.exp(-SWIGLU_ALPHA * g))) * (up + 1.0)
        ys_ref[...] = _pack_pairs(
            jnp.dot(act.astype(BF16), wd_s[...], preferred_element_type=F32) + bd_ref[0])

    @pl.when(jnp.logical_not(active))
    def _():
        ys_ref[...] = jnp.zeros_like(ys_ref)


def _experts(blk, meta, xs, w_gate_up, b_gate_up, w_down, b_down):
    n_rows = xs.shape[0]
    n_experts, d, f2 = w_gate_up.shape
    f = f2 // 2
    bm = EXPERT_ROWS
    n_blocks = n_rows // bm
    width = 2 * LANES
    perm = np.zeros((width, width), np.float32)
    perm[2 * np.arange(LANES), np.arange(LANES)] = 1.0
    perm[2 * np.arange(LANES) + 1, LANES + np.arange(LANES)] = 1.0
    bg = b_gate_up[:, 0::2].reshape(n_experts, 1, f)
    bu = b_gate_up[:, 1::2].reshape(n_experts, 1, f)
    rows = lambda i, be, nxt, par, meta: (jnp.minimum(i, meta[0] - 1), 0)
    per_e = lambda i, be, nxt, par, meta: (be[i], 0, 0)
    grid_spec = pltpu.PrefetchScalarGridSpec(
        num_scalar_prefetch=4,
        grid=(n_blocks,),
        in_specs=[
            pl.BlockSpec((bm, d // 2), rows),
            pl.BlockSpec(memory_space=pl.ANY),
            pl.BlockSpec((1, 1, f), per_e),
            pl.BlockSpec((1, 1, f), per_e),
            pl.BlockSpec(memory_space=pl.ANY),
            pl.BlockSpec((1, 1, d), per_e),
            pl.BlockSpec((width, width), lambda i, *_: (0, 0)),
        ],
        out_specs=pl.BlockSpec((bm, d // 2), lambda i, *_: (i, 0)),
        scratch_shapes=[pltpu.VMEM((2, d, f2), F32), pltpu.VMEM((2, f, d), F32),
                        pltpu.VMEM((d, f), BF16), pltpu.VMEM((d, f), BF16),
                        pltpu.VMEM((f, d), BF16), pltpu.SemaphoreType.DMA((2,))],
    )
    return pl.pallas_call(
        _expert_kernel,
        grid_spec=grid_spec,
        out_shape=jax.ShapeDtypeStruct((n_rows, d // 2), U32),
        compiler_params=_params(1, VMEM_LIMIT),
        name="experts",
    )(blk["expert"], blk["next"], blk["slot"], meta, xs, w_gate_up, bg, bu, w_down,
      b_down.reshape(n_experts, 1, d), jnp.asarray(perm, BF16))


def _routing_plan(counts, n_experts, bm, n_blocks):
    cnt = counts[:, 0, :n_experts]
    run = (cnt + RUN_ALIGN - 1) // RUN_ALIGN * RUN_ALIGN
    per_expert = jnp.sum(run, axis=0)
    padded = (per_expert + bm - 1) // bm * bm
    pend = jnp.cumsum(padded)
    pstart = pend - padded
    base = pstart[None, :] + jnp.cumsum(run, axis=0) - run
    lst = jnp.cumsum(run, axis=1) - run
    n_used = pend[-1] // bm
    tail = jnp.concatenate([pstart + per_expert, padded - per_expert, n_used[None]])
    starts = jnp.arange(n_blocks, dtype=I32) * bm
    blk = jnp.sum((starts[:, None] >= pend[None, :]).astype(I32), axis=1)
    blk = jnp.minimum(blk, n_experts - 1)
    last = jnp.sum((((n_used - 1) * bm) >= pend).astype(I32))
    blk_e = jnp.where(jnp.arange(n_blocks) < n_used, blk, jnp.minimum(last, n_experts - 1))
    ids = jnp.arange(n_experts, dtype=I32)
    has_rows = padded > 0
    later = jnp.where(has_rows[None, :] & (ids[None, :] > ids[:, None]), ids[None, :], n_experts)
    next_e = jnp.min(later, axis=1)
    next_e = jnp.where(next_e < n_experts, next_e, -1)
    ordinal = jnp.cumsum(has_rows.astype(I32)) - 1
    onehot = blk_e[:, None] == ids[None, :]
    blocks = {"expert": blk_e.astype(I32),
              "next": jnp.sum(jnp.where(onehot, next_e[None, :], 0), axis=1).astype(I32),
              "slot": jnp.sum(jnp.where(onehot, (ordinal % 2)[None, :], 0), axis=1).astype(I32)}
    plan = {"cnt": run.reshape(-1).astype(I32), "lst": lst.reshape(-1).astype(I32),
            "base": base.reshape(-1).astype(I32), "rows": jnp.sum(run, axis=1).astype(I32),
            "tail": tail.astype(I32)}
    return plan, blocks, n_used.astype(I32).reshape(1)


def _layer(x2d, batch, seq, norm1, w_in, q_norm, k_norm, sinks, rel_bias, w_fourier, g_fourier_out,
           g_attn_out, w_out, norm2, w_router, b_router, w_gate_up, b_gate_up, w_down, b_down):
    t, d = x2d.shape
    n_experts = w_router.shape[1]
    u, q, kv = _inproj(x2d, norm1, w_in, q_norm, k_norm)
    yf = _fourier(u, w_fourier, g_fourier_out, batch, seq)
    ya = _attention(q, kv, sinks, rel_bias, g_attn_out, batch, seq)
    x1, h2, post, gatet, counts = _outproj(yf, ya, x2d, w_out, norm2, w_router, b_router)
    bm = EXPERT_ROWS
    n_tiles = t // min(TOKEN_TILE, t)
    worst_rows = t * TOP_K + n_tiles * n_experts * (RUN_ALIGN - 1) + n_experts * (bm - RUN_ALIGN)
    n_blocks = -(-worst_rows // bm)
    plan, blocks, meta = _routing_plan(counts, n_experts, bm, n_blocks)
    xs = _dispatch(plan, post, h2, n_blocks * bm, n_experts)
    ys = _experts(blocks, meta, xs, w_gate_up, b_gate_up, w_down, b_down)
    return _combine(plan, post, gatet, x1, ys, n_experts)


def kernel(x, norm1, w_in, q_norm, k_norm, sinks, rel_bias, w_fourier, g_fourier_out, g_attn_out,
           w_out, norm2, w_router, b_router, w_gate_up, b_gate_up, w_down, b_down):
    b, s, d = x.shape
    x2d = x.reshape(b * s, d)
    for l in range(norm1.shape[0]):
        x2d = _layer(x2d, b, s, norm1[l], w_in[l], q_norm[l], k_norm[l], sinks[l], rel_bias,
                     w_fourier[l], g_fourier_out[l], g_attn_out[l], w_out[l], norm2[l],
                     w_router[l], b_router[l], w_gate_up[l], b_gate_up[l], w_down[l], b_down[l])
    return x2d.reshape(b, s, d)
```

```python
import functools
import math

import jax
import jax.numpy as jnp
import numpy as np
from jax import lax
from jax.experimental import pallas as pl
from jax.experimental.pallas import tpu as pltpu

F32 = jnp.float32
BF16 = jnp.bfloat16
I32 = jnp.int32
U32 = jnp.uint32

NORM_EPS = 1e-5
QK_EPS = 1e-6
HEAD_DIM = 64
N_Q_HEADS = 8
N_KV_HEADS = 2
FOURIER_GROUPS = 4
FOURIER_CH = 128
FOURIER_WIDTH = FOURIER_GROUPS * FOURIER_CH
ATTN_WIDTH = N_Q_HEADS * HEAD_DIM
KV_WIDTH = N_KV_HEADS * HEAD_DIM
WINDOW = 128
Q_BLOCK = 128
N_BUCKETS = 32
MAX_DISTANCE = 128
TOP_K = 4
SWIGLU_ALPHA = 1.702
SWIGLU_LIMIT = 7.0
MASK_VALUE = -1e30
LOG2E = math.log2(math.e)

LANES = 128
SUBLANES = 8
TOKEN_TILE = 512
RUN_ALIGN = SUBLANES
ROW_GROUPS = 2
PERM_CHUNK = 256
DISPATCH_ROWS = 3 * PERM_CHUNK
EXPERT_ROWS = 512
FOURIER_ROWS = 512
BF16_BITS = 16
VMEM_LIMIT = 56 * 1024 * 1024


def _params(n_axes, vmem=None):
    return pltpu.CompilerParams(
        dimension_semantics=("arbitrary",) * n_axes, vmem_limit_bytes=vmem)


def _pair_head_norm(xc, gain, lo):
    x2 = xc * xc
    s_lo = jnp.sum(jnp.where(lo, x2, 0.0), axis=-1, keepdims=True)
    s_hi = jnp.sum(jnp.where(lo, 0.0, x2), axis=-1, keepdims=True)
    inv = jnp.where(lo, lax.rsqrt(s_lo * (1.0 / HEAD_DIM) + QK_EPS),
                    lax.rsqrt(s_hi * (1.0 / HEAD_DIM) + QK_EPS))
    return xc * inv * gain


def _inproj_kernel(x_ref, g1_ref, w_ref, qg_ref, kg_ref, u_ref, q_ref, kv_ref):
    rows = x_ref.shape[0] // ROW_GROUPS
    lo = lax.broadcasted_iota(I32, (rows, LANES), 1) < HEAD_DIM
    q0 = FOURIER_WIDTH
    k0 = q0 + ATTN_WIDTH
    for grp in range(ROW_GROUPS):
        rs = slice(grp * rows, (grp + 1) * rows)
        x = x_ref[rs, :]
        ms = jnp.mean(x * x, axis=-1, keepdims=True)
        h = (x * lax.rsqrt(ms + NORM_EPS) * g1_ref[...]).astype(BF16)
        z = jnp.dot(h, w_ref[...], preferred_element_type=F32)
        u_ref[rs, :] = z[:, :FOURIER_WIDTH].astype(BF16)
        for c in range(ATTN_WIDTH // LANES):
            qc = _pair_head_norm(z[:, q0 + c * LANES:q0 + (c + 1) * LANES], qg_ref[...], lo)
            q_ref[rs, c * LANES:(c + 1) * LANES] = (qc * (HEAD_DIM ** -0.5 * LOG2E)).astype(BF16)
        kc = _pair_head_norm(z[:, k0:k0 + KV_WIDTH], kg_ref[...], lo)
        vc = z[:, k0 + KV_WIDTH:k0 + 2 * KV_WIDTH]
        kv_ref[rs, 0:LANES] = kc.astype(BF16)
        kv_ref[rs, LANES:2 * LANES] = pltpu.roll(kc, HEAD_DIM, 1).astype(BF16)
        kv_ref[rs, 2 * LANES:3 * LANES] = vc.astype(BF16)
        kv_ref[rs, 3 * LANES:4 * LANES] = pltpu.roll(vc, HEAD_DIM, 1).astype(BF16)


def _inproj(x2d, norm1, w_in, q_norm, k_norm):
    t, d = x2d.shape
    tm = min(TOKEN_TILE, t)
    n_in = w_in.shape[1]
    qg = jnp.tile(q_norm, LANES // HEAD_DIM).reshape(1, LANES)
    kg = jnp.tile(k_norm, LANES // HEAD_DIM).reshape(1, LANES)
    full = lambda i: (0, 0)
    return pl.pallas_call(
        _inproj_kernel,
        grid=(t // tm,),
        in_specs=[
            pl.BlockSpec((tm, d), lambda i: (i, 0)),
            pl.BlockSpec((1, d), full),
            pl.BlockSpec((d, n_in), full),
            pl.BlockSpec((1, LANES), full),
            pl.BlockSpec((1, LANES), full),
        ],
        out_specs=[
            pl.BlockSpec((tm, FOURIER_WIDTH), lambda i: (i, 0)),
            pl.BlockSpec((tm, ATTN_WIDTH), lambda i: (i, 0)),
            pl.BlockSpec((tm, 4 * LANES), lambda i: (i, 0)),
        ],
        out_shape=[
            jax.ShapeDtypeStruct((t, FOURIER_WIDTH), BF16),
            jax.ShapeDtypeStruct((t, ATTN_WIDTH), BF16),
            jax.ShapeDtypeStruct((t, 4 * LANES), BF16),
        ],
        compiler_params=_params(1, VMEM_LIMIT),
        name="inproj",
    )(x2d, norm1.reshape(1, d), w_in.astype(BF16), qg, kg)


def _fourier_kernel(u_ref, cs_ref, ss_ref, rev_ref, cc_ref, sc_ref, wf_ref, g_ref, o_ref,
                    p_scr, q_scr, e_scr, *, scale, row_block):
    for g in range(FOURIER_GROUPS):
        sl = slice(g * FOURIER_CH, (g + 1) * FOURIER_CH)
        w = wf_ref[g].astype(BF16)
        a = (jnp.dot(cc_ref[...], w, preferred_element_type=F32) * scale).astype(BF16)
        b = (jnp.dot(sc_ref[...], w, preferred_element_type=F32) * scale).astype(BF16)
        ug = u_ref[:, sl]
        p_scr[:, sl] = jnp.dot(ug, a, preferred_element_type=F32).astype(BF16)
        q_scr[:, sl] = jnp.dot(ug, b, preferred_element_type=F32).astype(BF16)
    half = u_ref.shape[0] // 2
    gain = g_ref[...]

    def norm(y):
        ms = jnp.mean(y * y, axis=-1, keepdims=True)
        return y * lax.rsqrt(ms + NORM_EPS) * gain

    n_blk = half // row_block
    mid = None
    for r in range(n_blk):
        rs = slice(r * row_block, (r + 1) * row_block)
        extra = SUBLANES if r == n_blk - 1 else 0
        c = jnp.dot(cs_ref[r * row_block:(r + 1) * row_block + extra, :], p_scr[...],
                    preferred_element_type=F32)
        d = jnp.dot(ss_ref[rs, :], q_scr[...], preferred_element_type=F32)
        o_ref[rs, :] = norm(c[:row_block] + d).astype(BF16)
        e_scr[rs, :] = norm(c[:row_block] - d).astype(BF16)
        if extra:
            mid = norm(c[row_block:row_block + 1])
    for r in range(n_blk):
        z = jnp.dot(rev_ref[r * row_block:(r + 1) * row_block, :], e_scr[...],
                    preferred_element_type=F32)
        if r == 0:
            z = jnp.where(lax.broadcasted_iota(I32, z.shape, 0) == 0, mid, z)
        o_ref[half + r * row_block:half + (r + 1) * row_block, :] = z.astype(BF16)


def _dft_tables(n):
    k = np.arange(n, dtype=np.int64)
    ang = 2.0 * np.pi * ((k[:, None] * k[None, :]) % n).astype(np.float64) / n
    return np.cos(ang), np.sin(ang)


def _fourier(u, w_fourier, g_out, batch, seq):
    cs, ss = _dft_tables(seq)
    cc, sc = _dft_tables(FOURIER_CH)
    scale = 1.0 / math.sqrt(seq * FOURIER_CH)
    half = seq // 2
    row_block = min(FOURIER_ROWS, half)
    rev = np.zeros((half, half), np.float32)
    rev[np.arange(1, half), half - np.arange(1, half)] = 1.0
    full2 = lambda b: (0, 0)
    return pl.pallas_call(
        functools.partial(_fourier_kernel, scale=scale, row_block=row_block),
        grid=(batch,),
        in_specs=[
            pl.BlockSpec((seq, FOURIER_WIDTH), lambda b: (b, 0)),
            pl.BlockSpec((half + SUBLANES, seq), full2),
            pl.BlockSpec((half, seq), full2),
            pl.BlockSpec((half, half), full2),
            pl.BlockSpec((FOURIER_CH, FOURIER_CH), full2),
            pl.BlockSpec((FOURIER_CH, FOURIER_CH), full2),
            pl.BlockSpec((FOURIER_GROUPS, FOURIER_CH, FOURIER_CH), lambda b: (0, 0, 0)),
            pl.BlockSpec((1, FOURIER_WIDTH), full2),
        ],
        out_specs=pl.BlockSpec((seq, FOURIER_WIDTH), lambda b: (b, 0)),
        out_shape=jax.ShapeDtypeStruct((batch * seq, FOURIER_WIDTH), BF16),
        scratch_shapes=[pltpu.VMEM((seq, FOURIER_WIDTH), BF16),
                        pltpu.VMEM((seq, FOURIER_WIDTH), BF16),
                        pltpu.VMEM((half, FOURIER_WIDTH), BF16)],
        compiler_params=_params(1, VMEM_LIMIT),
        name="fourier",
    )(u, jnp.asarray(cs[:half + SUBLANES], BF16), jnp.asarray(ss[:half], BF16),
      jnp.asarray(rev, BF16), jnp.asarray(cc, BF16), jnp.asarray(-sc, BF16), w_fourier,
      g_out.reshape(1, FOURIER_WIDTH))


def _attn_kernel(sink_ref, q_ref, kvp_ref, kvo_ref, kvn_ref, bias_a_ref, bias_b_ref, g_ref, o_ref,
                 acc_ref):
    kv = jnp.concatenate([kvp_ref[...], kvo_ref[...], kvn_ref[...]], axis=0)
    nk = kv.shape[0]
    lo = lax.broadcasted_iota(I32, (nk, LANES), 1) < HEAD_DIM
    k_a, k_b = kv[:, 0:LANES], kv[:, LANES:2 * LANES]
    v_a, v_b = kv[:, 2 * LANES:3 * LANES], kv[:, 3 * LANES:4 * LANES]
    zero = jnp.zeros_like(k_a)
    k_lo = (jnp.where(lo, k_a, zero), jnp.where(lo, k_b, zero))
    k_hi = (jnp.where(lo, zero, k_b), jnp.where(lo, zero, k_a))
    v_lo = (jnp.where(lo, v_a, zero), jnp.where(lo, v_b, zero))
    v_hi = (jnp.where(lo, zero, v_b), jnp.where(lo, zero, v_a))
    lo_out = lax.broadcasted_iota(I32, (Q_BLOCK, LANES), 1) < HEAD_DIM
    bias_refs = (bias_a_ref, bias_b_ref)
    rows2 = q_ref.shape[0]
    nt = (((1,), (1,)), ((), ()))
    for h in range(N_KV_HEADS):
        qs = jnp.concatenate([q_ref[:, (2 * h) * LANES:(2 * h + 1) * LANES],
                              q_ref[:, (2 * h + 1) * LANES:(2 * h + 2) * LANES]], axis=0)
        s_par = (lax.dot_general(qs, k_lo[h], nt, preferred_element_type=F32),
                 lax.dot_general(qs, k_hi[h], nt, preferred_element_type=F32))
        for sb in range(2):
            keys = slice(sb * Q_BLOCK, sb * Q_BLOCK + 3 * Q_BLOCK)
            vcat = jnp.concatenate([v_lo[h][keys, :], v_hi[h][keys, :]], axis=0)
            for c in range(2):
                r0 = c * rows2 + sb * Q_BLOCK
                probs, invs = [], []
                for par in range(2):
                    hq = 4 * h + 2 * c + par
                    s = s_par[par][r0:r0 + Q_BLOCK, keys] + bias_refs[sb][hq]
                    sink = sink_ref[hq]
                    m = jnp.maximum(jnp.max(s, axis=-1, keepdims=True), sink)
                    p = jnp.exp2(s - m)
                    denom = jnp.sum(p, axis=-1, keepdims=True) + jnp.exp2(sink - m)
                    probs.append(p.astype(BF16))
                    invs.append(1.0 / denom)
                pcat = jnp.concatenate(probs, axis=1)
                chunk = 2 * h + c
                o = jnp.dot(pcat, vcat, preferred_element_type=F32)
                acc_ref[sb * Q_BLOCK:(sb + 1) * Q_BLOCK, chunk * LANES:(chunk + 1) * LANES] = (
                    o * jnp.where(lo_out, invs[0], invs[1]))
    y = acc_ref[...]
    ms = jnp.mean(y * y, axis=-1, keepdims=True)
    o_ref[...] = (y * lax.rsqrt(ms + NORM_EPS) * g_ref[...]).astype(BF16)


def _t5_bucket(rel):
    nb = N_BUCKETS // 2
    max_exact = nb // 2
    ret = (rel > 0).astype(jnp.int32) * nb
    n = jnp.abs(rel)
    nf = jnp.maximum(n, 1).astype(jnp.float32)
    large = max_exact + (jnp.log(nf / max_exact) / math.log(MAX_DISTANCE / max_exact)
                         * (nb - max_exact)).astype(jnp.int32)
    large = jnp.minimum(large, nb - 1)
    return ret + jnp.where(n < max_exact, n, large)


def _attention(q, kv, sinks, rel_bias, g_out, batch, seq):
    nb = seq // Q_BLOCK
    assert nb % 2 == 0
    nb2 = nb // 2
    qi = jnp.arange(Q_BLOCK, dtype=jnp.int32)[:, None]
    kj = jnp.arange(3 * Q_BLOCK, dtype=jnp.int32)[None, :]
    rel = kj - Q_BLOCK - qi
    period = 4 * Q_BLOCK
    p = jnp.arange(period, dtype=jnp.int32)
    off = jnp.where(p < 3 * Q_BLOCK, p, p - period) - Q_BLOCK
    hit = _t5_bucket(off)[None, :, None] == jnp.arange(N_BUCKETS, dtype=jnp.int32)
    by_off = jnp.sum(jnp.where(hit, rel_bias.astype(F32).T[:, None, :], 0.0), axis=-1)
    bias = jnp.tile(by_off, (1, Q_BLOCK))[:, :Q_BLOCK * (period - 1)]
    bias = bias.reshape(N_Q_HEADS, Q_BLOCK, period - 1)[:, :, :3 * Q_BLOCK]
    band = jnp.abs(rel) <= WINDOW
    first = band & (kj >= Q_BLOCK)
    last = band & (kj < 2 * Q_BLOCK)
    table = jnp.stack([jnp.where(msk[None], bias * LOG2E, MASK_VALUE) for msk in (first, band, last)])
    q_rows = 2 * Q_BLOCK
    grid_spec = pltpu.PrefetchScalarGridSpec(
        num_scalar_prefetch=1,
        grid=(batch, nb2),
        in_specs=[
            pl.BlockSpec((q_rows, ATTN_WIDTH), lambda b, i, s: (b * nb2 + i, 0)),
            pl.BlockSpec((Q_BLOCK, 4 * LANES),
                         lambda b, i, s: (b * nb + jnp.maximum(2 * i - 1, 0), 0)),
            pl.BlockSpec((q_rows, 4 * LANES), lambda b, i, s: (b * nb2 + i, 0)),
            pl.BlockSpec((Q_BLOCK, 4 * LANES),
                         lambda b, i, s: (b * nb + jnp.minimum(2 * i + 2, nb - 1), 0)),
            pl.BlockSpec((None, N_Q_HEADS, Q_BLOCK, 3 * Q_BLOCK),
                         lambda b, i, s: (jnp.where(i == 0, 0, 1), 0, 0, 0)),
            pl.BlockSpec((None, N_Q_HEADS, Q_BLOCK, 3 * Q_BLOCK),
                         lambda b, i, s: (jnp.where(i == nb2 - 1, 2, 1), 0, 0, 0)),
            pl.BlockSpec((1, ATTN_WIDTH), lambda b, i, s: (0, 0)),
        ],
        out_specs=pl.BlockSpec((q_rows, ATTN_WIDTH), lambda b, i, s: (b * nb2 + i, 0)),
        scratch_shapes=[pltpu.VMEM((q_rows, ATTN_WIDTH), F32)],
    )
    return pl.pallas_call(
        _attn_kernel,
        grid_spec=grid_spec,
        out_shape=jax.ShapeDtypeStruct((batch * seq, ATTN_WIDTH), BF16),
        compiler_params=_params(2, VMEM_LIMIT),
        name="attention",
    )(sinks.astype(F32) * LOG2E, q, kv, kv, kv, table, table, g_out.reshape(1, ATTN_WIDTH))


def _outproj_kernel(yf_ref, ya_ref, x_ref, wo_ref, g2_ref, wrt_ref, brt_ref, triu_ref, scan_ref,
                    x1_ref, h2_ref, post_ref, gatet_ref, cnt_ref, *, n_experts):
    half = yf_ref.shape[1]
    mix = (jnp.dot(yf_ref[...], wo_ref[:half, :], preferred_element_type=F32)
           + jnp.dot(ya_ref[...], wo_ref[half:, :], preferred_element_type=F32))
    x1 = x_ref[...] + mix
    x1_ref[...] = x1
    ms = jnp.mean(x1 * x1, axis=-1, keepdims=True)
    h2 = x1 * lax.rsqrt(ms + NORM_EPS) * g2_ref[...]
    h2_ref[...] = h2.astype(BF16)
    h_hi = h2.astype(BF16)
    h_lo = (h2 - h_hi.astype(F32)).astype(BF16)
    nt = (((1,), (1,)), ((), ()))
    t1 = lax.dot_general(wrt_ref[...], h_hi, nt, preferred_element_type=F32)
    t2 = lax.dot_general(wrt_ref[:n_experts, :], h_lo, nt, preferred_element_type=F32)
    logits = t1[:n_experts] + t1[n_experts:] + t2 + brt_ref[...]
    tm = logits.shape[1]
    sub_e = lax.broadcasted_iota(I32, (n_experts, tm), 0).astype(F32)
    work = logits
    vals, idxs = [], []
    for _ in range(TOP_K):
        m = jnp.max(work, axis=0, keepdims=True)
        ik = jnp.min(jnp.where(work == m, sub_e, float(n_experts)), axis=0, keepdims=True)
        work = jnp.where(sub_e == ik, -jnp.inf, work)
        vals.append(m)
        idxs.append(ik)
    exps = [jnp.exp(v - vals[0]) for v in vals]
    inv = 1.0 / (exps[0] + exps[1] + exps[2] + exps[3])
    gates = [e * inv for e in exps]

    sub = lax.broadcasted_iota(I32, (LANES, tm), 0).astype(F32)
    onehot = jnp.zeros((LANES, tm), F32)
    for k in range(TOP_K):
        onehot = onehot + jnp.where(sub == idxs[k] + float(k * n_experts), 1.0, 0.0)
    onehot_b = onehot.astype(BF16)
    before = jnp.dot(onehot_b, triu_ref[...], preferred_element_type=F32)
    counts = lax.dot_general(jnp.ones((SUBLANES, tm), BF16), onehot_b, nt,
                             preferred_element_type=F32)[0:1, :]
    total = counts
    for k in range(1, TOP_K):
        total = total + pltpu.roll(counts, k * n_experts, 1)
    cnt_ref[0] = total.astype(I32)
    colsum = jnp.sum(onehot, axis=1, keepdims=True)
    blocks = [colsum[k * n_experts:(k + 1) * n_experts] for k in range(TOP_K)]
    total_e = blocks[0] + blocks[1] + blocks[2] + blocks[3]
    run_e = jnp.floor((total_e + (RUN_ALIGN - 1)) * (1.0 / RUN_ALIGN)) * RUN_ALIGN
    run_start = jnp.dot(scan_ref[...], jnp.broadcast_to(run_e, (n_experts, LANES)).astype(BF16),
                        preferred_element_type=F32)[:, 0:1]
    adds, acc = [], run_start
    for k in range(TOP_K):
        adds.append(acc)
        acc = acc + blocks[k]
    placed = (before + jnp.concatenate(adds, axis=0)) * onehot
    pos = [jnp.sum(placed[k * n_experts:(k + 1) * n_experts], axis=0, keepdims=True)
           for k in range(TOP_K)]
    post_ref[...] = jnp.concatenate(pos + [jnp.zeros((SUBLANES - TOP_K, tm), F32)], axis=0).astype(I32)
    gatet_ref[...] = jnp.concatenate(gates + [jnp.zeros((SUBLANES - TOP_K, tm), F32)], axis=0)


def _outproj(yf, ya, x2d, w_out, norm2, w_router, b_router):
    t, d = x2d.shape
    tm = min(TOKEN_TILE, t)
    n_tiles = t // tm
    n_experts = w_router.shape[1]
    assert TOP_K * n_experts == LANES
    triu = np.triu(np.ones((tm, tm), np.float32), 1)
    scan = np.tril(np.ones((n_experts, n_experts), np.float32), -1)
    wr_hi = w_router.astype(BF16)
    wr_lo = (w_router - wr_hi.astype(F32)).astype(BF16)
    wrt = jnp.concatenate([wr_hi.T, wr_lo.T], axis=0)
    full = lambda i: (0, 0)
    row = lambda i: (i, 0)
    return pl.pallas_call(
        functools.partial(_outproj_kernel, n_experts=n_experts),
        grid=(n_tiles,),
        in_specs=[
            pl.BlockSpec((tm, yf.shape[1]), row),
            pl.BlockSpec((tm, ya.shape[1]), row),
            pl.BlockSpec((tm, d), row),
            pl.BlockSpec((w_out.shape[0], d), full),
            pl.BlockSpec((1, d), full),
            pl.BlockSpec((2 * n_experts, d), full),
            pl.BlockSpec((n_experts, 1), full),
            pl.BlockSpec((tm, tm), full),
            pl.BlockSpec((n_experts, n_experts), full),
        ],
        out_specs=[
            pl.BlockSpec((tm, d), row),
            pl.BlockSpec((tm, d), row),
            pl.BlockSpec((SUBLANES, tm), row),
            pl.BlockSpec((SUBLANES, tm), row),
            pl.BlockSpec((1, 1, LANES), lambda i: (i, 0, 0)),
        ],
        out_shape=[
            jax.ShapeDtypeStruct((t, d), F32),
            jax.ShapeDtypeStruct((t, d), BF16),
            jax.ShapeDtypeStruct((n_tiles * SUBLANES, tm), I32),
            jax.ShapeDtypeStruct((n_tiles * SUBLANES, tm), F32),
            jax.ShapeDtypeStruct((n_tiles, 1, LANES), I32),
        ],
        compiler_params=_params(1, VMEM_LIMIT),
        name="outproj_router",
    )(yf, ya, x2d, w_out.astype(BF16), norm2.reshape(1, d), wrt, b_router.reshape(n_experts, 1),
      jnp.asarray(triu, BF16), jnp.asarray(scan, BF16))


def _pack_pairs(x, is_bf16_exact=False):
    half = x.shape[1] // 2
    a, b = x[:, :half], x[:, half:]
    if not is_bf16_exact:
        a, b = a.astype(BF16).astype(F32), b.astype(BF16).astype(F32)
    return lax.bitcast_convert_type(a, U32) | (lax.bitcast_convert_type(b, U32) >> BF16_BITS)


def _unpack_pairs(w):
    hi = lax.bitcast_convert_type(w & U32(((1 << BF16_BITS) - 1) << BF16_BITS), F32)
    lo = lax.bitcast_convert_type(w << BF16_BITS, F32)
    return hi.astype(BF16), lo.astype(BF16)


def _rows(start, size):
    if not isinstance(size, int):
        size = pl.multiple_of(size, RUN_ALIGN)
    return pl.ds(pl.multiple_of(start, RUN_ALIGN), size)


def _dispatch_kernel(cnt_ref, lst_ref, base_ref, rows_ref, tail_ref, post_ref, h2_ref, xs_ref,
                     buf, zbuf, sem, zsem, *, n_experts):
    j = pl.program_id(0)
    tm = h2_ref.shape[0]
    n_local = buf.shape[1]

    def start_runs(tile, slot):
        def run(e, carry):
            r = tile * n_experts + e
            n = cnt_ref[r]

            @pl.when(n > 0)
            def _():
                pltpu.make_async_copy(buf.at[slot, _rows(lst_ref[r], n), :],
                                      xs_ref.at[_rows(base_ref[r], n), :], sem.at[slot]).start()
            return carry
        lax.fori_loop(0, n_experts, run, 0)

    def wait_runs(tile, slot):
        n = rows_ref[tile]

        @pl.when(n > 0)
        def _():
            pltpu.make_async_copy(buf.at[slot, _rows(0, n), :], xs_ref.at[_rows(0, n), :],
                                  sem.at[slot]).wait()

    def zero_fill(op):
        def tail(e, carry):
            n = tail_ref[n_experts + e]

            @pl.when(n > 0)
            def _():
                getattr(pltpu.make_async_copy(zbuf.at[_rows(0, n), :],
                                              xs_ref.at[_rows(tail_ref[e], n), :], zsem), op)()
            return carry
        lax.fori_loop(0, n_experts, tail, 0)

        def spare(b, carry):
            getattr(pltpu.make_async_copy(zbuf, xs_ref.at[_rows(b * EXPERT_ROWS, EXPERT_ROWS), :],
                                          zsem), op)()
            return carry
        lax.fori_loop(tail_ref[2 * n_experts], xs_ref.shape[0] // EXPERT_ROWS, spare, 0)

    slot = j % 2
    @pl.when(j >= 2)
    def _():
        wait_runs(j - 2, slot)

    @pl.when(j == 0)
    def _():
        zbuf[...] = jnp.zeros_like(zbuf)
        zero_fill("start")
        zero_fill("wait")

    h = h2_ref[...]
    post = post_ref[0:TOP_K, :]
    chunk_of = lax.shift_right_logical(post, PERM_CHUNK.bit_length() - 1)
    offset = (post & (PERM_CHUNK - 1)).astype(F32)
    rows = lax.broadcasted_iota(I32, (PERM_CHUNK, tm), 0).astype(F32).astype(BF16)
    one = jnp.ones((PERM_CHUNK, tm), BF16)
    per = DISPATCH_ROWS // PERM_CHUNK
    for mc in range(n_local // DISPATCH_ROWS):
        parts = []
        for rc in range(mc * per, (mc + 1) * per):
            off = jnp.where(chunk_of == rc, offset, -1.0).astype(BF16)
            perm = jnp.zeros((PERM_CHUNK, tm), BF16)
            for k in range(TOP_K):
                perm = jnp.where(rows == off[k:k + 1, :], one, perm)
            parts.append(perm)
        rs = slice(mc * DISPATCH_ROWS, (mc + 1) * DISPATCH_ROWS)
        buf[slot, rs, :] = _pack_pairs(
            jnp.dot(jnp.concatenate(parts, axis=0), h, preferred_element_type=F32), True)

    start_runs(j, slot)

    @pl.when(j == pl.num_programs(0) - 1)
    def _():
        @pl.when(j >= 1)
        def _():
            wait_runs(j - 1, 1 - slot)
        wait_runs(j, slot)


def _local_rows(tm, n_experts):
    worst = TOP_K * tm + n_experts * (RUN_ALIGN - 1)
    return -(-worst // DISPATCH_ROWS) * DISPATCH_ROWS


def _dispatch(plan, post, h2, n_rows, n_experts):
    t, d = h2.shape
    tm = min(TOKEN_TILE, t)
    grid_spec = pltpu.PrefetchScalarGridSpec(
        num_scalar_prefetch=5,
        grid=(t // tm,),
        in_specs=[
            pl.BlockSpec((SUBLANES, tm), lambda i, *_: (i, 0)),
            pl.BlockSpec((tm, d), lambda i, *_: (i, 0)),
        ],
        out_specs=pl.BlockSpec(memory_space=pl.ANY),
        scratch_shapes=[pltpu.VMEM((2, _local_rows(tm, n_experts), d // 2), U32),
                        pltpu.VMEM((EXPERT_ROWS, d // 2), U32),
                        pltpu.SemaphoreType.DMA((2,)), pltpu.SemaphoreType.DMA(())],
    )
    return pl.pallas_call(
        functools.partial(_dispatch_kernel, n_experts=n_experts),
        grid_spec=grid_spec,
        out_shape=jax.ShapeDtypeStruct((n_rows, d // 2), U32),
        compiler_params=_params(1, VMEM_LIMIT),
        name="dispatch",
    )(plan["cnt"], plan["lst"], plan["base"], plan["rows"], plan["tail"], post, h2)


def _combine_kernel(cnt_ref, lst_ref, base_ref, rows_ref, post_ref, gatet_ref, x1_ref, ys_ref, o_ref,
                    buf, g_scr, y_scr, sem, *, n_experts):
    j = pl.program_id(0)
    tm, d = x1_ref.shape
    n_local = buf.shape[1]

    def start_runs(tile, slot):
        def run(e, carry):
            r = tile * n_experts + e
            n = cnt_ref[r]

            @pl.when(n > 0)
            def _():
                pltpu.make_async_copy(ys_ref.at[_rows(base_ref[r], n), :],
                                      buf.at[slot, _rows(lst_ref[r], n), :], sem.at[slot]).start()
            return carry
        lax.fori_loop(0, n_experts, run, 0)

    def wait_runs(tile, slot):
        n = rows_ref[tile]

        @pl.when(n > 0)
        def _():
            pltpu.make_async_copy(ys_ref.at[_rows(0, n), :], buf.at[slot, _rows(0, n), :],
                                  sem.at[slot]).wait()

    slot = j % 2
    @pl.when(j == 0)
    def _():
        buf[...] = jnp.zeros_like(buf)
        start_runs(j, slot)

    @pl.when(j + 1 < pl.num_programs(0))
    def _():
        start_runs(j + 1, 1 - slot)

    wait_runs(j, slot)

    post = post_ref[0:TOP_K, :]
    gate = gatet_ref[0:TOP_K, :].astype(BF16)
    chunk_of = lax.shift_right_logical(post, PERM_CHUNK.bit_length() - 1)
    offset = (post & (PERM_CHUNK - 1)).astype(F32)
    rows = lax.broadcasted_iota(I32, (PERM_CHUNK, tm), 0).astype(F32).astype(BF16)
    for rc in range(n_local // PERM_CHUNK):
        chunk = slice(rc * PERM_CHUNK, (rc + 1) * PERM_CHUNK)
        off = jnp.where(chunk_of == rc, offset, -1.0).astype(BF16)
        g = jnp.zeros((PERM_CHUNK, tm), BF16)
        for k in range(TOP_K):
            g = jnp.where(rows == off[k:k + 1, :], jnp.broadcast_to(gate[k:k + 1, :], g.shape), g)
        g_scr[chunk, :] = g
        y_scr[chunk, :d // 2], y_scr[chunk, d // 2:] = _unpack_pairs(buf[slot, chunk, :])
    o_ref[...] = x1_ref[...] + lax.dot_general(g_scr[...], y_scr[...], (((0,), (0,)), ((), ())),
                                               preferred_element_type=F32)


def _combine(plan, post, gatet, x1, ys, n_experts):
    t, d = x1.shape
    tm = min(TOKEN_TILE, t)
    grid_spec = pltpu.PrefetchScalarGridSpec(
        num_scalar_prefetch=4,
        grid=(t // tm,),
        in_specs=[
            pl.BlockSpec((SUBLANES, tm), lambda i, *_: (i, 0)),
            pl.BlockSpec((SUBLANES, tm), lambda i, *_: (i, 0)),
            pl.BlockSpec((tm, d), lambda i, *_: (i, 0)),
            pl.BlockSpec(memory_space=pl.ANY),
        ],
        out_specs=pl.BlockSpec((tm, d), lambda i, *_: (i, 0)),
        scratch_shapes=[pltpu.VMEM((2, _local_rows(tm, n_experts), d // 2), U32),
                        pltpu.VMEM((_local_rows(tm, n_experts), tm), BF16),
                        pltpu.VMEM((_local_rows(tm, n_experts), d), BF16),
                        pltpu.SemaphoreType.DMA((2,))],
    )
    return pl.pallas_call(
        functools.partial(_combine_kernel, n_experts=n_experts),
        grid_spec=grid_spec,
        out_shape=jax.ShapeDtypeStruct((t, d), F32),
        compiler_params=_params(1, VMEM_LIMIT),
        name="combine",
    )(plan["cnt"], plan["lst"], plan["base"], plan["rows"], post, gatet, x1, ys)


def _expert_kernel(be_ref, nxt_ref, par_ref, meta_ref, xs_ref, wgu_hbm, wd_hbm, bias_ref, ys_ref,
                   wgu_buf, wd_buf, wg_s, wu_s, wd_s, wsem):
    i = pl.program_id(0)
    n_used = meta_ref[0]
    active = i < n_used
    new_expert = (i == 0) | (be_ref[i] != be_ref[jnp.maximum(i - 1, 0)])

    def weight_copies(expert, slot):
        return (pltpu.make_async_copy(wgu_hbm.at[expert], wgu_buf.at[slot], wsem.at[slot]),
                pltpu.make_async_copy(wd_hbm.at[expert], wd_buf.at[slot], wsem.at[slot]))

    @pl.when(active & new_expert)
    def _():
        slot = par_ref[i]

        @pl.when(i == 0)
        def _():
            for cp in weight_copies(be_ref[i], slot):
                cp.start()

        for cp in weight_copies(be_ref[i], slot):
            cp.wait()
        nxt = nxt_ref[i]

        @pl.when(nxt >= 0)
        def _():
            for cp in weight_copies(nxt, 1 - slot):
                cp.start()

        width = 2 * LANES
        src = lax.broadcasted_iota(I32, (width, width), 0)
        dst = lax.broadcasted_iota(I32, (width, width), 1)
        perm = jnp.where(src == jnp.where(dst < LANES, 2 * dst, 2 * (dst - LANES) + 1), 1.0, 0.0
                         ).astype(BF16)
        for c in range(wgu_buf.shape[2] // width):
            wc = wgu_buf[slot, :, c * width:(c + 1) * width].astype(BF16)
            r = jnp.dot(wc, perm, preferred_element_type=F32)
            wg_s[:, c * LANES:(c + 1) * LANES] = r[:, :LANES].astype(BF16)
            wu_s[:, c * LANES:(c + 1) * LANES] = r[:, LANES:].astype(BF16)
        wd_s[...] = wd_buf[slot].astype(BF16)

    @pl.when(active)
    def _():
        f = wg_s.shape[1]
        bias = bias_ref[be_ref[i]]
        xb = jnp.concatenate(_unpack_pairs(xs_ref[...]), axis=1)
        g = jnp.dot(xb, wg_s[...], preferred_element_type=F32) + bias[:, :f]
        up = jnp.dot(xb, wu_s[...], preferred_element_type=F32) + bias[:, f:2 * f]
        g = jnp.minimum(g, SWIGLU_LIMIT)
        up = jnp.clip(up, -SWIGLU_LIMIT, SWIGLU_LIMIT)
        act = g * (1.0 / (1.0 + jnp.exp(-SWIGLU_ALPHA * g))) * (up + 1.0)
        ys_ref[...] = _pack_pairs(
            jnp.dot(act.astype(BF16), wd_s[...], preferred_element_type=F32) + bias[:, 2 * f:])

    @pl.when(jnp.logical_not(active))
    def _():
        ys_ref[...] = jnp.zeros_like(ys_ref)


def _experts(blk, meta, xs, w_gate_up, b_gate_up, w_down, b_down):
    n_rows = xs.shape[0]
    n_experts, d, f2 = w_gate_up.shape
    f = f2 // 2
    bm = EXPERT_ROWS
    n_blocks = n_rows // bm
    bias = jnp.concatenate([b_gate_up[:, 0::2], b_gate_up[:, 1::2], b_down], axis=1)
    bias = bias.reshape(n_experts, 1, f2 + d)
    rows = lambda i, be, nxt, par, meta: (jnp.minimum(i, meta[0] - 1), 0)
    grid_spec = pltpu.PrefetchScalarGridSpec(
        num_scalar_prefetch=4,
        grid=(n_blocks,),
        in_specs=[
            pl.BlockSpec((bm, d // 2), rows),
            pl.BlockSpec(memory_space=pl.ANY),
            pl.BlockSpec(memory_space=pl.ANY),
            pl.BlockSpec((n_experts, 1, f2 + d), lambda i, *_: (0, 0, 0)),
        ],
        out_specs=pl.BlockSpec((bm, d // 2), lambda i, *_: (i, 0)),
        scratch_shapes=[pltpu.VMEM((2, d, f2), F32), pltpu.VMEM((2, f, d), F32),
                        pltpu.VMEM((d, f), BF16), pltpu.VMEM((d, f), BF16),
                        pltpu.VMEM((f, d), BF16), pltpu.SemaphoreType.DMA((2,))],
    )
    return pl.pallas_call(
        _expert_kernel,
        grid_spec=grid_spec,
        out_shape=jax.ShapeDtypeStruct((n_rows, d // 2), U32),
        compiler_params=_params(1, VMEM_LIMIT),
        name="experts",
    )(blk["expert"], blk["next"], blk["slot"], meta, xs, w_gate_up, w_down, bias)


def _routing_plan(counts, n_experts, bm, n_blocks):
    cnt = counts[:, 0, :n_experts]
    run = (cnt + RUN_ALIGN - 1) // RUN_ALIGN * RUN_ALIGN
    per_expert = jnp.sum(run, axis=0)
    padded = (per_expert + bm - 1) // bm * bm
    pend = jnp.cumsum(padded)
    pstart = pend - padded
    base = pstart[None, :] + jnp.cumsum(run, axis=0) - run
    lst = jnp.cumsum(run, axis=1) - run
    n_used = pend[-1] // bm
    tail = jnp.concatenate([pstart + per_expert, padded - per_expert, n_used[None]])
    starts = jnp.arange(n_blocks, dtype=I32) * bm
    blk = jnp.sum((starts[:, None] >= pend[None, :]).astype(I32), axis=1)
    blk = jnp.minimum(blk, n_experts - 1)
    last = jnp.sum((((n_used - 1) * bm) >= pend).astype(I32))
    blk_e = jnp.where(jnp.arange(n_blocks) < n_used, blk, jnp.minimum(last, n_experts - 1))
    ids = jnp.arange(n_experts, dtype=I32)
    has_rows = padded > 0
    later = jnp.where(has_rows[None, :] & (ids[None, :] > ids[:, None]), ids[None, :], n_experts)
    next_e = jnp.min(later, axis=1)
    next_e = jnp.where(next_e < n_experts, next_e, -1)
    ordinal = jnp.cumsum(has_rows.astype(I32)) - 1
    onehot = blk_e[:, None] == ids[None, :]
    blocks = {"expert": blk_e.astype(I32),
              "next": jnp.sum(jnp.where(onehot, next_e[None, :], 0), axis=1).astype(I32),
              "slot": jnp.sum(jnp.where(onehot, (ordinal % 2)[None, :], 0), axis=1).astype(I32)}
    plan = {"cnt": run.reshape(-1).astype(I32), "lst": lst.reshape(-1).astype(I32),
            "base": base.reshape(-1).astype(I32), "rows": jnp.sum(run, axis=1).astype(I32),
            "tail": tail.astype(I32)}
    return plan, blocks, n_used.astype(I32).reshape(1)


def _layer(x2d, batch, seq, norm1, w_in, q_norm, k_norm, sinks, rel_bias, w_fourier, g_fourier_out,
           g_attn_out, w_out, norm2, w_router, b_router, w_gate_up, b_gate_up, w_down, b_down):
    t, d = x2d.shape
    n_experts = w_router.shape[1]
    u, q, kv = _inproj(x2d, norm1, w_in, q_norm, k_norm)
    yf = _fourier(u, w_fourier, g_fourier_out, batch, seq)
    ya = _attention(q, kv, sinks, rel_bias, g_attn_out, batch, seq)
    x1, h2, post, gatet, counts = _outproj(yf, ya, x2d, w_out, norm2, w_router, b_router)
    bm = EXPERT_ROWS
    n_tiles = t // min(TOKEN_TILE, t)
    worst_rows = t * TOP_K + n_tiles * n_experts * (RUN_ALIGN - 1) + n_experts * (bm - RUN_ALIGN)
    n_blocks = -(-worst_rows // bm)
    plan, blocks, meta = _routing_plan(counts, n_experts, bm, n_blocks)
    xs = _dispatch(plan, post, h2, n_blocks * bm, n_experts)
    ys = _experts(blocks, meta, xs, w_gate_up, b_gate_up, w_down, b_down)
    return _combine(plan, post, gatet, x1, ys, n_experts)


def kernel(x, norm1, w_in, q_norm, k_norm, sinks, rel_bias, w_fourier, g_fourier_out, g_attn_out,
           w_out, norm2, w_router, b_router, w_gate_up, b_gate_up, w_down, b_down):
    b, s, d = x.shape
    x2d = x.reshape(b * s, d)
    for l in range(norm1.shape[0]):
        x2d = _layer(x2d, b, s, norm1[l], w_in[l], q_norm[l], k_norm[l], sinks[l], rel_bias,
                     w_fourier[l], g_fourier_out[l], g_attn_out[l], w_out[l], norm2[l],
                     w_router[l], b_router[l], w_gate_up[l], b_gate_up[l], w_down[l], b_down[l])
    return x2d.reshape(b, s, d)
```

```python
import functools
import math

import jax
import jax.numpy as jnp
import numpy as np
from jax import lax
from jax.experimental import pallas as pl
from jax.experimental.pallas import tpu as pltpu

F32 = jnp.float32
BF16 = jnp.bfloat16
I32 = jnp.int32
U32 = jnp.uint32

NORM_EPS = 1e-5
QK_EPS = 1e-6
HEAD_DIM = 64
N_Q_HEADS = 8
N_KV_HEADS = 2
FOURIER_GROUPS = 4
FOURIER_CH = 128
FOURIER_WIDTH = FOURIER_GROUPS * FOURIER_CH
ATTN_WIDTH = N_Q_HEADS * HEAD_DIM
KV_WIDTH = N_KV_HEADS * HEAD_DIM
WINDOW = 128
Q_BLOCK = 128
N_BUCKETS = 32
MAX_DISTANCE = 128
TOP_K = 4
SWIGLU_ALPHA = 1.702
SWIGLU_LIMIT = 7.0
MASK_VALUE = -1e30
LOG2E = math.log2(math.e)

LANES = 128
SUBLANES = 8
TOKEN_TILE = 512
RUN_ALIGN = SUBLANES
ROW_GROUPS = 2
PERM_CHUNK = 256
DISPATCH_ROWS = 3 * PERM_CHUNK
EXPERT_ROWS = 512
FOURIER_ROWS = 512
BF16_BITS = 16
VMEM_LIMIT = 56 * 1024 * 1024


def _params(n_axes, vmem=None):
    return pltpu.CompilerParams(
        dimension_semantics=("arbitrary",) * n_axes, vmem_limit_bytes=vmem)


def _pair_head_norm(xc, gain, lo):
    x2 = xc * xc
    s_lo = jnp.sum(jnp.where(lo, x2, 0.0), axis=-1, keepdims=True)
    s_hi = jnp.sum(jnp.where(lo, 0.0, x2), axis=-1, keepdims=True)
    inv = jnp.where(lo, lax.rsqrt(s_lo * (1.0 / HEAD_DIM) + QK_EPS),
                    lax.rsqrt(s_hi * (1.0 / HEAD_DIM) + QK_EPS))
    return xc * inv * gain


def _inproj_kernel(x_ref, g1_ref, w_ref, qg_ref, kg_ref, u_ref, q_ref, kv_ref):
    rows = x_ref.shape[0] // ROW_GROUPS
    lo = lax.broadcasted_iota(I32, (rows, LANES), 1) < HEAD_DIM
    q0 = FOURIER_WIDTH
    k0 = q0 + ATTN_WIDTH
    for grp in range(ROW_GROUPS):
        rs = slice(grp * rows, (grp + 1) * rows)
        x = x_ref[rs, :]
        ms = jnp.mean(x * x, axis=-1, keepdims=True)
        h = (x * lax.rsqrt(ms + NORM_EPS) * g1_ref[...]).astype(BF16)
        z = jnp.dot(h, w_ref[...], preferred_element_type=F32)
        u_ref[rs, :] = z[:, :FOURIER_WIDTH].astype(BF16)
        for c in range(ATTN_WIDTH // LANES):
            qc = _pair_head_norm(z[:, q0 + c * LANES:q0 + (c + 1) * LANES], qg_ref[...], lo)
            q_ref[rs, c * LANES:(c + 1) * LANES] = (qc * (HEAD_DIM ** -0.5 * LOG2E)).astype(BF16)
        kc = _pair_head_norm(z[:, k0:k0 + KV_WIDTH], kg_ref[...], lo)
        vc = z[:, k0 + KV_WIDTH:k0 + 2 * KV_WIDTH]
        kv_ref[rs, 0:LANES] = kc.astype(BF16)
        kv_ref[rs, LANES:2 * LANES] = pltpu.roll(kc, HEAD_DIM, 1).astype(BF16)
        kv_ref[rs, 2 * LANES:3 * LANES] = vc.astype(BF16)
        kv_ref[rs, 3 * LANES:4 * LANES] = pltpu.roll(vc, HEAD_DIM, 1).astype(BF16)


def _inproj(x2d, norm1, w_in, q_norm, k_norm):
    t, d = x2d.shape
    tm = min(TOKEN_TILE, t)
    n_in = w_in.shape[1]
    qg = jnp.tile(q_norm, LANES // HEAD_DIM).reshape(1, LANES)
    kg = jnp.tile(k_norm, LANES // HEAD_DIM).reshape(1, LANES)
    full = lambda i: (0, 0)
    return pl.pallas_call(
        _inproj_kernel,
        grid=(t // tm,),
        in_specs=[
            pl.BlockSpec((tm, d), lambda i: (i, 0)),
            pl.BlockSpec((1, d), full),
            pl.BlockSpec((d, n_in), full),
            pl.BlockSpec((1, LANES), full),
            pl.BlockSpec((1, LANES), full),
        ],
        out_specs=[
            pl.BlockSpec((tm, FOURIER_WIDTH), lambda i: (i, 0)),
            pl.BlockSpec((tm, ATTN_WIDTH), lambda i: (i, 0)),
            pl.BlockSpec((tm, 4 * LANES), lambda i: (i, 0)),
        ],
        out_shape=[
            jax.ShapeDtypeStruct((t, FOURIER_WIDTH), BF16),
            jax.ShapeDtypeStruct((t, ATTN_WIDTH), BF16),
            jax.ShapeDtypeStruct((t, 4 * LANES), BF16),
        ],
        compiler_params=_params(1, VMEM_LIMIT),
        name="inproj",
    )(x2d, norm1.reshape(1, d), w_in.astype(BF16), qg, kg)


def _fourier_kernel(u_ref, cs_ref, ss_ref, rev_ref, cc_ref, sc_ref, wf_ref, g_ref, o_ref,
                    p_scr, q_scr, e_scr, *, scale, row_block):
    for g in range(FOURIER_GROUPS):
        sl = slice(g * FOURIER_CH, (g + 1) * FOURIER_CH)
        w = wf_ref[g].astype(BF16)
        a = (jnp.dot(cc_ref[...], w, preferred_element_type=F32) * scale).astype(BF16)
        b = (jnp.dot(sc_ref[...], w, preferred_element_type=F32) * scale).astype(BF16)
        ug = u_ref[:, sl]
        p_scr[:, sl] = jnp.dot(ug, a, preferred_element_type=F32).astype(BF16)
        q_scr[:, sl] = jnp.dot(ug, b, preferred_element_type=F32).astype(BF16)
    half = u_ref.shape[0] // 2
    gain = g_ref[...]

    def norm(y):
        ms = jnp.mean(y * y, axis=-1, keepdims=True)
        return y * lax.rsqrt(ms + NORM_EPS) * gain

    n_blk = half // row_block
    mid = None
    for r in range(n_blk):
        rs = slice(r * row_block, (r + 1) * row_block)
        extra = SUBLANES if r == n_blk - 1 else 0
        c = jnp.dot(cs_ref[r * row_block:(r + 1) * row_block + extra, :], p_scr[...],
                    preferred_element_type=F32)
        d = jnp.dot(ss_ref[rs, :], q_scr[...], preferred_element_type=F32)
        o_ref[rs, :] = norm(c[:row_block] + d).astype(BF16)
        e_scr[rs, :] = norm(c[:row_block] - d).astype(BF16)
        if extra:
            mid = norm(c[row_block:row_block + 1])
    for r in range(n_blk):
        z = jnp.dot(rev_ref[r * row_block:(r + 1) * row_block, :], e_scr[...],
                    preferred_element_type=F32)
        if r == 0:
            z = jnp.where(lax.broadcasted_iota(I32, z.shape, 0) == 0, mid, z)
        o_ref[half + r * row_block:half + (r + 1) * row_block, :] = z.astype(BF16)


def _dft_tables(n):
    k = np.arange(n, dtype=np.int64)
    ang = 2.0 * np.pi * ((k[:, None] * k[None, :]) % n).astype(np.float64) / n
    return np.cos(ang), np.sin(ang)


def _fourier(u, w_fourier, g_out, batch, seq):
    cs, ss = _dft_tables(seq)
    cc, sc = _dft_tables(FOURIER_CH)
    scale = 1.0 / math.sqrt(seq * FOURIER_CH)
    half = seq // 2
    row_block = min(FOURIER_ROWS, half)
    rev = np.zeros((half, half), np.float32)
    rev[np.arange(1, half), half - np.arange(1, half)] = 1.0
    full2 = lambda b: (0, 0)
    return pl.pallas_call(
        functools.partial(_fourier_kernel, scale=scale, row_block=row_block),
        grid=(batch,),
        in_specs=[
            pl.BlockSpec((seq, FOURIER_WIDTH), lambda b: (b, 0)),
            pl.BlockSpec((half + SUBLANES, seq), full2),
            pl.BlockSpec((half, seq), full2),
            pl.BlockSpec((half, half), full2),
            pl.BlockSpec((FOURIER_CH, FOURIER_CH), full2),
            pl.BlockSpec((FOURIER_CH, FOURIER_CH), full2),
            pl.BlockSpec((FOURIER_GROUPS, FOURIER_CH, FOURIER_CH), lambda b: (0, 0, 0)),
            pl.BlockSpec((1, FOURIER_WIDTH), full2),
        ],
        out_specs=pl.BlockSpec((seq, FOURIER_WIDTH), lambda b: (b, 0)),
        out_shape=jax.ShapeDtypeStruct((batch * seq, FOURIER_WIDTH), BF16),
        scratch_shapes=[pltpu.VMEM((seq, FOURIER_WIDTH), BF16),
                        pltpu.VMEM((seq, FOURIER_WIDTH), BF16),
                        pltpu.VMEM((half, FOURIER_WIDTH), BF16)],
        compiler_params=_params(1, VMEM_LIMIT),
        name="fourier",
    )(u, jnp.asarray(cs[:half + SUBLANES], BF16), jnp.asarray(ss[:half], BF16),
      jnp.asarray(rev, BF16), jnp.asarray(cc, BF16), jnp.asarray(-sc, BF16), w_fourier,
      g_out.reshape(1, FOURIER_WIDTH))


def _attn_kernel(sink_ref, q_ref, kvp_ref, kvo_ref, kvn_ref, bias_a_ref, bias_b_ref, g_ref, o_ref,
                 acc_ref):
    kv = jnp.concatenate([kvp_ref[...], kvo_ref[...], kvn_ref[...]], axis=0)
    nk = kv.shape[0]
    lo = lax.broadcasted_iota(I32, (nk, LANES), 1) < HEAD_DIM
    k_a, k_b = kv[:, 0:LANES], kv[:, LANES:2 * LANES]
    v_a, v_b = kv[:, 2 * LANES:3 * LANES], kv[:, 3 * LANES:4 * LANES]
    zero = jnp.zeros_like(k_a)
    k_lo = (jnp.where(lo, k_a, zero), jnp.where(lo, k_b, zero))
    k_hi = (jnp.where(lo, zero, k_b), jnp.where(lo, zero, k_a))
    v_lo = (jnp.where(lo, v_a, zero), jnp.where(lo, v_b, zero))
    v_hi = (jnp.where(lo, zero, v_b), jnp.where(lo, zero, v_a))
    lo_out = lax.broadcasted_iota(I32, (Q_BLOCK, LANES), 1) < HEAD_DIM
    bias_refs = (bias_a_ref, bias_b_ref)
    rows2 = q_ref.shape[0]
    nt = (((1,), (1,)), ((), ()))
    for h in range(N_KV_HEADS):
        qs = jnp.concatenate([q_ref[:, (2 * h) * LANES:(2 * h + 1) * LANES],
                              q_ref[:, (2 * h + 1) * LANES:(2 * h + 2) * LANES]], axis=0)
        s_par = (lax.dot_general(qs, k_lo[h], nt, preferred_element_type=F32),
                 lax.dot_general(qs, k_hi[h], nt, preferred_element_type=F32))
        for sb in range(2):
            keys = slice(sb * Q_BLOCK, sb * Q_BLOCK + 3 * Q_BLOCK)
            vcat = jnp.concatenate([v_lo[h][keys, :], v_hi[h][keys, :]], axis=0)
            for c in range(2):
                r0 = c * rows2 + sb * Q_BLOCK
                probs, invs = [], []
                for par in range(2):
                    hq = 4 * h + 2 * c + par
                    s = s_par[par][r0:r0 + Q_BLOCK, keys] + bias_refs[sb][hq]
                    sink = sink_ref[hq]
                    m = jnp.maximum(jnp.max(s, axis=-1, keepdims=True), sink)
                    p = jnp.exp2(s - m)
                    denom = jnp.sum(p, axis=-1, keepdims=True) + jnp.exp2(sink - m)
                    probs.append(p.astype(BF16))
                    invs.append(1.0 / denom)
                pcat = jnp.concatenate(probs, axis=1)
                chunk = 2 * h + c
                o = jnp.dot(pcat, vcat, preferred_element_type=F32)
                acc_ref[sb * Q_BLOCK:(sb + 1) * Q_BLOCK, chunk * LANES:(chunk + 1) * LANES] = (
                    o * jnp.where(lo_out, invs[0], invs[1]))
    y = acc_ref[...]
    ms = jnp.mean(y * y, axis=-1, keepdims=True)
    o_ref[...] = (y * lax.rsqrt(ms + NORM_EPS) * g_ref[...]).astype(BF16)


def _t5_bucket(rel):
    nb = N_BUCKETS // 2
    max_exact = nb // 2
    ret = (rel > 0).astype(jnp.int32) * nb
    n = jnp.abs(rel)
    nf = jnp.maximum(n, 1).astype(jnp.float32)
    large = max_exact + (jnp.log(nf / max_exact) / math.log(MAX_DISTANCE / max_exact)
                         * (nb - max_exact)).astype(jnp.int32)
    large = jnp.minimum(large, nb - 1)
    return ret + jnp.where(n < max_exact, n, large)


def _attention(q, kv, sinks, rel_bias, g_out, batch, seq):
    nb = seq // Q_BLOCK
    assert nb % 2 == 0
    nb2 = nb // 2
    qi = jnp.arange(Q_BLOCK, dtype=jnp.int32)[:, None]
    kj = jnp.arange(3 * Q_BLOCK, dtype=jnp.int32)[None, :]
    rel = kj - Q_BLOCK - qi
    period = 4 * Q_BLOCK
    p = jnp.arange(period, dtype=jnp.int32)
    off = jnp.where(p < 3 * Q_BLOCK, p, p - period) - Q_BLOCK
    hit = _t5_bucket(off)[None, :, None] == jnp.arange(N_BUCKETS, dtype=jnp.int32)
    by_off = jnp.sum(jnp.where(hit, rel_bias.astype(F32).T[:, None, :], 0.0), axis=-1)
    bias = jnp.tile(by_off, (1, Q_BLOCK))[:, :Q_BLOCK * (period - 1)]
    bias = bias.reshape(N_Q_HEADS, Q_BLOCK, period - 1)[:, :, :3 * Q_BLOCK]
    band = jnp.abs(rel) <= WINDOW
    first = band & (kj >= Q_BLOCK)
    last = band & (kj < 2 * Q_BLOCK)
    table = jnp.stack([jnp.where(msk[None], bias * LOG2E, MASK_VALUE) for msk in (first, band, last)])
    q_rows = 2 * Q_BLOCK
    grid_spec = pltpu.PrefetchScalarGridSpec(
        num_scalar_prefetch=1,
        grid=(batch, nb2),
        in_specs=[
            pl.BlockSpec((q_rows, ATTN_WIDTH), lambda b, i, s: (b * nb2 + i, 0)),
            pl.BlockSpec((Q_BLOCK, 4 * LANES),
                         lambda b, i, s: (b * nb + jnp.maximum(2 * i - 1, 0), 0)),
            pl.BlockSpec((q_rows, 4 * LANES), lambda b, i, s: (b * nb2 + i, 0)),
            pl.BlockSpec((Q_BLOCK, 4 * LANES),
                         lambda b, i, s: (b * nb + jnp.minimum(2 * i + 2, nb - 1), 0)),
            pl.BlockSpec((None, N_Q_HEADS, Q_BLOCK, 3 * Q_BLOCK),
                         lambda b, i, s: (jnp.where(i == 0, 0, 1), 0, 0, 0)),
            pl.BlockSpec((None, N_Q_HEADS, Q_BLOCK, 3 * Q_BLOCK),
                         lambda b, i, s: (jnp.where(i == nb2 - 1, 2, 1), 0, 0, 0)),
            pl.BlockSpec((1, ATTN_WIDTH), lambda b, i, s: (0, 0)),
        ],
        out_specs=pl.BlockSpec((q_rows, ATTN_WIDTH), lambda b, i, s: (b * nb2 + i, 0)),
        scratch_shapes=[pltpu.VMEM((q_rows, ATTN_WIDTH), F32)],
    )
    return pl.pallas_call(
        _attn_kernel,
        grid_spec=grid_spec,
        out_shape=jax.ShapeDtypeStruct((batch * seq, ATTN_WIDTH), BF16),
        compiler_params=_params(2, VMEM_LIMIT),
        name="attention",
    )(sinks.astype(F32) * LOG2E, q, kv, kv, kv, table, table, g_out.reshape(1, ATTN_WIDTH))


def _outproj_kernel(yf_ref, ya_ref, x_ref, wo_ref, g2_ref, wrt_ref, brt_ref, triu_ref, scan_ref,
                    x1_ref, h2_ref, post_ref, gatet_ref, cnt_ref, *, n_experts):
    half = yf_ref.shape[1]
    mix = (jnp.dot(yf_ref[...], wo_ref[:half, :], preferred_element_type=F32)
           + jnp.dot(ya_ref[...], wo_ref[half:, :], preferred_element_type=F32))
    x1 = x_ref[...] + mix
    x1_ref[...] = x1
    ms = jnp.mean(x1 * x1, axis=-1, keepdims=True)
    h2 = x1 * lax.rsqrt(ms + NORM_EPS) * g2_ref[...]
    h2_ref[...] = h2.astype(BF16)
    h_hi = h2.astype(BF16)
    h_lo = (h2 - h_hi.astype(F32)).astype(BF16)
    nt = (((1,), (1,)), ((), ()))
    t1 = lax.dot_general(wrt_ref[...], h_hi, nt, preferred_element_type=F32)
    t2 = lax.dot_general(wrt_ref[:n_experts, :], h_lo, nt, preferred_element_type=F32)
    logits = t1[:n_experts] + t1[n_experts:] + t2 + brt_ref[...]
    tm = logits.shape[1]
    sub_e = lax.broadcasted_iota(I32, (n_experts, tm), 0).astype(F32)
    work = logits
    vals, idxs = [], []
    for _ in range(TOP_K):
        m = jnp.max(work, axis=0, keepdims=True)
        ik = jnp.min(jnp.where(work == m, sub_e, float(n_experts)), axis=0, keepdims=True)
        work = jnp.where(sub_e == ik, -jnp.inf, work)
        vals.append(m)
        idxs.append(ik)
    exps = [jnp.exp(v - vals[0]) for v in vals]
    inv = 1.0 / (exps[0] + exps[1] + exps[2] + exps[3])
    gates = [e * inv for e in exps]

    sub = lax.broadcasted_iota(I32, (LANES, tm), 0).astype(F32)
    onehot = jnp.zeros((LANES, tm), F32)
    for k in range(TOP_K):
        onehot = onehot + jnp.where(sub == idxs[k] + float(k * n_experts), 1.0, 0.0)
    onehot_b = onehot.astype(BF16)
    before = jnp.dot(onehot_b, triu_ref[...], preferred_element_type=F32)
    counts = lax.dot_general(jnp.ones((SUBLANES, tm), BF16), onehot_b, nt,
                             preferred_element_type=F32)[0:1, :]
    total = counts
    for k in range(1, TOP_K):
        total = total + pltpu.roll(counts, k * n_experts, 1)
    cnt_ref[0] = total.astype(I32)
    colsum = jnp.sum(onehot, axis=1, keepdims=True)
    blocks = [colsum[k * n_experts:(k + 1) * n_experts] for k in range(TOP_K)]
    total_e = blocks[0] + blocks[1] + blocks[2] + blocks[3]
    run_e = jnp.floor((total_e + (RUN_ALIGN - 1)) * (1.0 / RUN_ALIGN)) * RUN_ALIGN
    run_start = jnp.dot(scan_ref[...], jnp.broadcast_to(run_e, (n_experts, LANES)).astype(BF16),
                        preferred_element_type=F32)[:, 0:1]
    adds, acc = [], run_start
    for k in range(TOP_K):
        adds.append(acc)
        acc = acc + blocks[k]
    placed = (before + jnp.concatenate(adds, axis=0)) * onehot
    pos = [jnp.sum(placed[k * n_experts:(k + 1) * n_experts], axis=0, keepdims=True)
           for k in range(TOP_K)]
    post_ref[...] = jnp.concatenate(pos + [jnp.zeros((SUBLANES - TOP_K, tm), F32)], axis=0).astype(I32)
    gatet_ref[...] = jnp.concatenate(gates + [jnp.zeros((SUBLANES - TOP_K, tm), F32)], axis=0)


def _outproj(yf, ya, x2d, w_out, norm2, w_router, b_router):
    t, d = x2d.shape
    tm = min(TOKEN_TILE, t)
    n_tiles = t // tm
    n_experts = w_router.shape[1]
    assert TOP_K * n_experts == LANES
    triu = np.triu(np.ones((tm, tm), np.float32), 1)
    scan = np.tril(np.ones((n_experts, n_experts), np.float32), -1)
    wr_hi = w_router.astype(BF16)
    wr_lo = (w_router - wr_hi.astype(F32)).astype(BF16)
    wrt = jnp.concatenate([wr_hi.T, wr_lo.T], axis=0)
    full = lambda i: (0, 0)
    row = lambda i: (i, 0)
    return pl.pallas_call(
        functools.partial(_outproj_kernel, n_experts=n_experts),
        grid=(n_tiles,),
        in_specs=[
            pl.BlockSpec((tm, yf.shape[1]), row),
            pl.BlockSpec((tm, ya.shape[1]), row),
            pl.BlockSpec((tm, d), row),
            pl.BlockSpec((w_out.shape[0], d), full),
            pl.BlockSpec((1, d), full),
            pl.BlockSpec((2 * n_experts, d), full),
            pl.BlockSpec((n_experts, 1), full),
            pl.BlockSpec((tm, tm), full),
            pl.BlockSpec((n_experts, n_experts), full),
        ],
        out_specs=[
            pl.BlockSpec((tm, d), row),
            pl.BlockSpec((tm, d), row),
            pl.BlockSpec((SUBLANES, tm), row),
            pl.BlockSpec((SUBLANES, tm), row),
            pl.BlockSpec((1, 1, LANES), lambda i: (i, 0, 0)),
        ],
        out_shape=[
            jax.ShapeDtypeStruct((t, d), F32),
            jax.ShapeDtypeStruct((t, d), BF16),
            jax.ShapeDtypeStruct((n_tiles * SUBLANES, tm), I32),
            jax.ShapeDtypeStruct((n_tiles * SUBLANES, tm), F32),
            jax.ShapeDtypeStruct((n_tiles, 1, LANES), I32),
        ],
        compiler_params=_params(1, VMEM_LIMIT),
        name="outproj_router",
    )(yf, ya, x2d, w_out.astype(BF16), norm2.reshape(1, d), wrt, b_router.reshape(n_experts, 1),
      jnp.asarray(triu, BF16), jnp.asarray(scan, BF16))


def _pack_pairs(x, is_bf16_exact=False):
    half = x.shape[1] // 2
    a, b = x[:, :half], x[:, half:]
    if not is_bf16_exact:
        a, b = a.astype(BF16).astype(F32), b.astype(BF16).astype(F32)
    return lax.bitcast_convert_type(a, U32) | (lax.bitcast_convert_type(b, U32) >> BF16_BITS)


def _unpack_pairs(w):
    hi = lax.bitcast_convert_type(w & U32(((1 << BF16_BITS) - 1) << BF16_BITS), F32)
    lo = lax.bitcast_convert_type(w << BF16_BITS, F32)
    return hi.astype(BF16), lo.astype(BF16)


def _rows(start, size):
    if not isinstance(size, int):
        size = pl.multiple_of(size, RUN_ALIGN)
    return pl.ds(pl.multiple_of(start, RUN_ALIGN), size)


def _dispatch_kernel(cnt_ref, lst_ref, base_ref, rows_ref, tail_ref, post_ref, h2_ref, xs_ref,
                     buf, zbuf, sem, zsem, *, n_experts):
    j = pl.program_id(0)
    tm = h2_ref.shape[0]
    n_local = buf.shape[1]

    def start_runs(tile, slot):
        def run(e, carry):
            r = tile * n_experts + e
            n = cnt_ref[r]

            @pl.when(n > 0)
            def _():
                pltpu.make_async_copy(buf.at[slot, _rows(lst_ref[r], n), :],
                                      xs_ref.at[_rows(base_ref[r], n), :], sem.at[slot]).start()
            return carry
        lax.fori_loop(0, n_experts, run, 0)

    def wait_runs(tile, slot):
        n = rows_ref[tile]

        @pl.when(n > 0)
        def _():
            pltpu.make_async_copy(buf.at[slot, _rows(0, n), :], xs_ref.at[_rows(0, n), :],
                                  sem.at[slot]).wait()

    def zero_fill(op):
        def tail(e, carry):
            n = tail_ref[n_experts + e]

            @pl.when(n > 0)
            def _():
                getattr(pltpu.make_async_copy(zbuf.at[_rows(0, n), :],
                                              xs_ref.at[_rows(tail_ref[e], n), :], zsem), op)()
            return carry
        lax.fori_loop(0, n_experts, tail, 0)

        def spare(b, carry):
            getattr(pltpu.make_async_copy(zbuf, xs_ref.at[_rows(b * EXPERT_ROWS, EXPERT_ROWS), :],
                                          zsem), op)()
            return carry
        lax.fori_loop(tail_ref[2 * n_experts], xs_ref.shape[0] // EXPERT_ROWS, spare, 0)

    slot = j % 2
    @pl.when(j >= 2)
    def _():
        wait_runs(j - 2, slot)

    @pl.when(j == 0)
    def _():
        zbuf[...] = jnp.zeros_like(zbuf)
        zero_fill("start")
        zero_fill("wait")

    h = h2_ref[...]
    post = post_ref[0:TOP_K, :]
    chunk_of = lax.shift_right_logical(post, PERM_CHUNK.bit_length() - 1)
    offset = (post & (PERM_CHUNK - 1)).astype(F32)
    rows = lax.broadcasted_iota(I32, (PERM_CHUNK, tm), 0).astype(F32).astype(BF16)
    one = jnp.ones((PERM_CHUNK, tm), BF16)
    per = DISPATCH_ROWS // PERM_CHUNK
    for mc in range(n_local // DISPATCH_ROWS):
        parts = []
        for rc in range(mc * per, (mc + 1) * per):
            off = jnp.where(chunk_of == rc, offset, -1.0).astype(BF16)
            perm = jnp.zeros((PERM_CHUNK, tm), BF16)
            for k in range(TOP_K):
                perm = jnp.where(rows == off[k:k + 1, :], one, perm)
            parts.append(perm)
        rs = slice(mc * DISPATCH_ROWS, (mc + 1) * DISPATCH_ROWS)
        buf[slot, rs, :] = _pack_pairs(
            jnp.dot(jnp.concatenate(parts, axis=0), h, preferred_element_type=F32), True)

    start_runs(j, slot)

    @pl.when(j == pl.num_programs(0) - 1)
    def _():
        @pl.when(j >= 1)
        def _():
            wait_runs(j - 1, 1 - slot)
        wait_runs(j, slot)


def _local_rows(tm, n_experts):
    worst = TOP_K * tm + n_experts * (RUN_ALIGN - 1)
    return -(-worst // DISPATCH_ROWS) * DISPATCH_ROWS


def _dispatch(plan, post, h2, n_rows, n_experts):
    t, d = h2.shape
    tm = min(TOKEN_TILE, t)
    grid_spec = pltpu.PrefetchScalarGridSpec(
        num_scalar_prefetch=5,
        grid=(t // tm,),
        in_specs=[
            pl.BlockSpec((SUBLANES, tm), lambda i, *_: (i, 0)),
            pl.BlockSpec((tm, d), lambda i, *_: (i, 0)),
        ],
        out_specs=pl.BlockSpec(memory_space=pl.ANY),
        scratch_shapes=[pltpu.VMEM((2, _local_rows(tm, n_experts), d // 2), U32),
                        pltpu.VMEM((EXPERT_ROWS, d // 2), U32),
                        pltpu.SemaphoreType.DMA((2,)), pltpu.SemaphoreType.DMA(())],
    )
    return pl.pallas_call(
        functools.partial(_dispatch_kernel, n_experts=n_experts),
        grid_spec=grid_spec,
        out_shape=jax.ShapeDtypeStruct((n_rows, d // 2), U32),
        compiler_params=_params(1, VMEM_LIMIT),
        name="dispatch",
    )(plan["cnt"], plan["lst"], plan["base"], plan["rows"], plan["tail"], post, h2)


def _combine_kernel(cnt_ref, lst_ref, base_ref, rows_ref, post_ref, gatet_ref, x1_ref, ys_ref, o_ref,
                    buf, g_scr, y_scr, sem, *, n_experts):
    j = pl.program_id(0)
    tm, d = x1_ref.shape
    n_local = buf.shape[1]

    def start_runs(tile, slot):
        def run(e, carry):
            r = tile * n_experts + e
            n = cnt_ref[r]

            @pl.when(n > 0)
            def _():
                pltpu.make_async_copy(ys_ref.at[_rows(base_ref[r], n), :],
                                      buf.at[slot, _rows(lst_ref[r], n), :], sem.at[slot]).start()
            return carry
        lax.fori_loop(0, n_experts, run, 0)

    def wait_runs(tile, slot):
        n = rows_ref[tile]

        @pl.when(n > 0)
        def _():
            pltpu.make_async_copy(ys_ref.at[_rows(0, n), :], buf.at[slot, _rows(0, n), :],
                                  sem.at[slot]).wait()

    slot = j % 2
    @pl.when(j == 0)
    def _():
        buf[...] = jnp.zeros_like(buf)
        start_runs(j, slot)

    @pl.when(j + 1 < pl.num_programs(0))
    def _():
        start_runs(j + 1, 1 - slot)

    wait_runs(j, slot)

    post = post_ref[0:TOP_K, :]
    gate = gatet_ref[0:TOP_K, :].astype(BF16)
    chunk_of = lax.shift_right_logical(post, PERM_CHUNK.bit_length() - 1)
    offset = (post & (PERM_CHUNK - 1)).astype(F32)
    rows = lax.broadcasted_iota(I32, (PERM_CHUNK, tm), 0).astype(F32).astype(BF16)
    for rc in range(n_local // PERM_CHUNK):
        chunk = slice(rc * PERM_CHUNK, (rc + 1) * PERM_CHUNK)
        off = jnp.where(chunk_of == rc, offset, -1.0).astype(BF16)
        g = jnp.zeros((PERM_CHUNK, tm), BF16)
        for k in range(TOP_K):
            g = jnp.where(rows == off[k:k + 1, :], jnp.broadcast_to(gate[k:k + 1, :], g.shape), g)
        g_scr[chunk, :] = g
        y_scr[chunk, :d // 2], y_scr[chunk, d // 2:] = _unpack_pairs(buf[slot, chunk, :])
    o_ref[...] = x1_ref[...] + lax.dot_general(g_scr[...], y_scr[...], (((0,), (0,)), ((), ())),
                                               preferred_element_type=F32)


def _combine(plan, post, gatet, x1, ys, n_experts):
    t, d = x1.shape
    tm = min(TOKEN_TILE, t)
    grid_spec = pltpu.PrefetchScalarGridSpec(
        num_scalar_prefetch=4,
        grid=(t // tm,),
        in_specs=[
            pl.BlockSpec((SUBLANES, tm), lambda i, *_: (i, 0)),
            pl.BlockSpec((SUBLANES, tm), lambda i, *_: (i, 0)),
            pl.BlockSpec((tm, d), lambda i, *_: (i, 0)),
            pl.BlockSpec(memory_space=pl.ANY),
        ],
        out_specs=pl.BlockSpec((tm, d), lambda i, *_: (i, 0)),
        scratch_shapes=[pltpu.VMEM((2, _local_rows(tm, n_experts), d // 2), U32),
                        pltpu.VMEM((_local_rows(tm, n_experts), tm), BF16),
                        pltpu.VMEM((_local_rows(tm, n_experts), d), BF16),
                        pltpu.SemaphoreType.DMA((2,))],
    )
    return pl.pallas_call(
        functools.partial(_combine_kernel, n_experts=n_experts),
        grid_spec=grid_spec,
        out_shape=jax.ShapeDtypeStruct((t, d), F32),
        compiler_params=_params(1, VMEM_LIMIT),
        name="combine",
    )(plan["cnt"], plan["lst"], plan["base"], plan["rows"], post, gatet, x1, ys)


def _expert_kernel(be_ref, nxt_ref, par_ref, meta_ref, xs_ref, wgu_hbm, wd_hbm, bias_ref, ys_ref,
                   wgu_buf, wd_buf, wg_s, wu_s, wd_s, wsem):
    i = pl.program_id(0)
    n_used = meta_ref[0]
    active = i < n_used
    new_expert = (i == 0) | (be_ref[i] != be_ref[jnp.maximum(i - 1, 0)])

    def weight_copies(expert, slot):
        return (pltpu.make_async_copy(wgu_hbm.at[expert], wgu_buf.at[slot], wsem.at[slot]),
                pltpu.make_async_copy(wd_hbm.at[expert], wd_buf.at[slot], wsem.at[slot]))

    @pl.when(active & new_expert)
    def _():
        slot = par_ref[i]

        @pl.when(i == 0)
        def _():
            for cp in weight_copies(be_ref[i], slot):
                cp.start()

        for cp in weight_copies(be_ref[i], slot):
            cp.wait()
        nxt = nxt_ref[i]

        @pl.when(nxt >= 0)
        def _():
            for cp in weight_copies(nxt, 1 - slot):
                cp.start(priority=1)

        width = 2 * LANES
        src = lax.broadcasted_iota(I32, (width, width), 0)
        dst = lax.broadcasted_iota(I32, (width, width), 1)
        perm = jnp.where(src == jnp.where(dst < LANES, 2 * dst, 2 * (dst - LANES) + 1), 1.0, 0.0
                         ).astype(BF16)
        for c in range(wgu_buf.shape[2] // width):
            wc = wgu_buf[slot, :, c * width:(c + 1) * width].astype(BF16)
            r = jnp.dot(wc, perm, preferred_element_type=F32)
            wg_s[:, c * LANES:(c + 1) * LANES] = r[:, :LANES].astype(BF16)
            wu_s[:, c * LANES:(c + 1) * LANES] = r[:, LANES:].astype(BF16)
        wd_s[...] = wd_buf[slot].astype(BF16)

    @pl.when(active)
    def _():
        f = wg_s.shape[1]
        bias = bias_ref[be_ref[i]]
        xb = jnp.concatenate(_unpack_pairs(xs_ref[...]), axis=1)
        g = jnp.dot(xb, wg_s[...], preferred_element_type=F32) + bias[:, :f]
        up = jnp.dot(xb, wu_s[...], preferred_element_type=F32) + bias[:, f:2 * f]
        g = jnp.minimum(g, SWIGLU_LIMIT)
        up = jnp.clip(up, -SWIGLU_LIMIT, SWIGLU_LIMIT)
        act = g * (1.0 / (1.0 + jnp.exp(-SWIGLU_ALPHA * g))) * (up + 1.0)
        ys_ref[...] = _pack_pairs(
            jnp.dot(act.astype(BF16), wd_s[...], preferred_element_type=F32) + bias[:, 2 * f:])

    @pl.when(jnp.logical_not(active))
    def _():
        ys_ref[...] = jnp.zeros_like(ys_ref)


def _experts(blk, meta, xs, w_gate_up, b_gate_up, w_down, b_down):
    n_rows = xs.shape[0]
    n_experts, d, f2 = w_gate_up.shape
    f = f2 // 2
    bm = EXPERT_ROWS
    n_blocks = n_rows // bm
    bias = jnp.concatenate([b_gate_up[:, 0::2], b_gate_up[:, 1::2], b_down], axis=1)
    bias = bias.reshape(n_experts, 1, f2 + d)
    rows = lambda i, be, nxt, par, meta: (jnp.minimum(i, meta[0] - 1), 0)
    grid_spec = pltpu.PrefetchScalarGridSpec(
        num_scalar_prefetch=4,
        grid=(n_blocks,),
        in_specs=[
            pl.BlockSpec((bm, d // 2), rows),
            pl.BlockSpec(memory_space=pl.ANY),
            pl.BlockSpec(memory_space=pl.ANY),
            pl.BlockSpec((n_experts, 1, f2 + d), lambda i, *_: (0, 0, 0)),
        ],
        out_specs=pl.BlockSpec((bm, d // 2), lambda i, *_: (i, 0)),
        scratch_shapes=[pltpu.VMEM((2, d, f2), F32), pltpu.VMEM((2, f, d), F32),
                        pltpu.VMEM((d, f), BF16), pltpu.VMEM((d, f), BF16),
                        pltpu.VMEM((f, d), BF16), pltpu.SemaphoreType.DMA((2,))],
    )
    return pl.pallas_call(
        _expert_kernel,
        grid_spec=grid_spec,
        out_shape=jax.ShapeDtypeStruct((n_rows, d // 2), U32),
        compiler_params=_params(1, VMEM_LIMIT),
        name="experts",
    )(blk["expert"], blk["next"], blk["slot"], meta, xs, w_gate_up, w_down, bias)


def _routing_plan(counts, n_experts, bm, n_blocks):
    cnt = counts[:, 0, :n_experts]
    run = (cnt + RUN_ALIGN - 1) // RUN_ALIGN * RUN_ALIGN
    per_expert = jnp.sum(run, axis=0)
    padded = (per_expert + bm - 1) // bm * bm
    pend = jnp.cumsum(padded)
    pstart = pend - padded
    base = pstart[None, :] + jnp.cumsum(run, axis=0) - run
    lst = jnp.cumsum(run, axis=1) - run
    n_used = pend[-1] // bm
    tail = jnp.concatenate([pstart + per_expert, padded - per_expert, n_used[None]])
    starts = jnp.arange(n_blocks, dtype=I32) * bm
    blk = jnp.sum((starts[:, None] >= pend[None, :]).astype(I32), axis=1)
    blk = jnp.minimum(blk, n_experts - 1)
    last = jnp.sum((((n_used - 1) * bm) >= pend).astype(I32))
    blk_e = jnp.where(jnp.arange(n_blocks) < n_used, blk, jnp.minimum(last, n_experts - 1))
    ids = jnp.arange(n_experts, dtype=I32)
    has_rows = padded > 0
    later = jnp.where(has_rows[None, :] & (ids[None, :] > ids[:, None]), ids[None, :], n_experts)
    next_e = jnp.min(later, axis=1)
    next_e = jnp.where(next_e < n_experts, next_e, -1)
    ordinal = jnp.cumsum(has_rows.astype(I32)) - 1
    onehot = blk_e[:, None] == ids[None, :]
    blocks = {"expert": blk_e.astype(I32),
              "next": jnp.sum(jnp.where(onehot, next_e[None, :], 0), axis=1).astype(I32),
              "slot": jnp.sum(jnp.where(onehot, (ordinal % 2)[None, :], 0), axis=1).astype(I32)}
    plan = {"cnt": run.reshape(-1).astype(I32), "lst": lst.reshape(-1).astype(I32),
            "base": base.reshape(-1).astype(I32), "rows": jnp.sum(run, axis=1).astype(I32),
            "tail": tail.astype(I32)}
    return plan, blocks, n_used.astype(I32).reshape(1)


def _layer(x2d, batch, seq, norm1, w_in, q_norm, k_norm, sinks, rel_bias, w_fourier, g_fourier_out,
           g_attn_out, w_out, norm2, w_router, b_router, w_gate_up, b_gate_up, w_down, b_down):
    t, d = x2d.shape
    n_experts = w_router.shape[1]
    u, q, kv = _inproj(x2d, norm1, w_in, q_norm, k_norm)
    yf = _fourier(u, w_fourier, g_fourier_out, batch, seq)
    ya = _attention(q, kv, sinks, rel_bias, g_attn_out, batch, seq)
    x1, h2, post, gatet, counts = _outproj(yf, ya, x2d, w_out, norm2, w_router, b_router)
    bm = EXPERT_ROWS
    n_tiles = t // min(TOKEN_TILE, t)
    worst_rows = t * TOP_K + n_tiles * n_experts * (RUN_ALIGN - 1) + n_experts * (bm - RUN_ALIGN)
    n_blocks = -(-worst_rows // bm)
    plan, blocks, meta = _routing_plan(counts, n_experts, bm, n_blocks)
    xs = _dispatch(plan, post, h2, n_blocks * bm, n_experts)
    ys = _experts(blocks, meta, xs, w_gate_up, b_gate_up, w_down, b_down)
    return _combine(plan, post, gatet, x1, ys, n_experts)


def kernel(x, norm1, w_in, q_norm, k_norm, sinks, rel_bias, w_fourier, g_fourier_out, g_attn_out,
           w_out, norm2, w_router, b_router, w_gate_up, b_gate_up, w_down, b_down):
    b, s, d = x.shape
    x2d = x.reshape(b * s, d)
    for l in range(norm1.shape[0]):
        x2d = _layer(x2d, b, s, norm1[l], w_in[l], q_norm[l], k_norm[l], sinks[l], rel_bias,
                     w_fourier[l], g_fourier_out[l], g_attn_out[l], w_out[l], norm2[l],
                     w_router[l], b_router[l], w_gate_up[l], b_gate_up[l], w_down[l], b_down[l])
    return x2d.reshape(b, s, d)
```

```python
import functools
import math

import jax
import jax.numpy as jnp
import numpy as np
from jax import lax
from jax.experimental import pallas as pl
from jax.experimental.pallas import tpu as pltpu

F32 = jnp.float32
BF16 = jnp.bfloat16
I32 = jnp.int32
U32 = jnp.uint32

NORM_EPS = 1e-5
QK_EPS = 1e-6
HEAD_DIM = 64
N_Q_HEADS = 8
N_KV_HEADS = 2
FOURIER_GROUPS = 4
FOURIER_CH = 128
FOURIER_WIDTH = FOURIER_GROUPS * FOURIER_CH
ATTN_WIDTH = N_Q_HEADS * HEAD_DIM
KV_WIDTH = N_KV_HEADS * HEAD_DIM
WINDOW = 128
Q_BLOCK = 128
N_BUCKETS = 32
MAX_DISTANCE = 128
TOP_K = 4
SWIGLU_ALPHA = 1.702
SWIGLU_LIMIT = 7.0
MASK_VALUE = -1e30
LOG2E = math.log2(math.e)

LANES = 128
SUBLANES = 8
TOKEN_TILE = 512
RUN_ALIGN = SUBLANES
ROW_GROUPS = 2
PERM_CHUNK = 256
DISPATCH_ROWS = 3 * PERM_CHUNK
EXPERT_ROWS = 512
FOURIER_ROWS = 512
BF16_BITS = 16
VMEM_LIMIT = 56 * 1024 * 1024


def _params(n_axes, vmem=None):
    return pltpu.CompilerParams(
        dimension_semantics=("arbitrary",) * n_axes, vmem_limit_bytes=vmem)


def _pair_head_norm(xc, gain, lo):
    x2 = xc * xc
    s_lo = jnp.sum(jnp.where(lo, x2, 0.0), axis=-1, keepdims=True)
    s_hi = jnp.sum(jnp.where(lo, 0.0, x2), axis=-1, keepdims=True)
    inv = jnp.where(lo, lax.rsqrt(s_lo * (1.0 / HEAD_DIM) + QK_EPS),
                    lax.rsqrt(s_hi * (1.0 / HEAD_DIM) + QK_EPS))
    return xc * inv * gain


def _inproj_kernel(x_ref, g1_ref, w_ref, qg_ref, kg_ref, u_ref, q_ref, kv_ref):
    rows = x_ref.shape[0] // ROW_GROUPS
    lo = lax.broadcasted_iota(I32, (rows, LANES), 1) < HEAD_DIM
    q0 = FOURIER_WIDTH
    k0 = q0 + ATTN_WIDTH
    for grp in range(ROW_GROUPS):
        rs = slice(grp * rows, (grp + 1) * rows)
        x = x_ref[rs, :]
        ms = jnp.mean(x * x, axis=-1, keepdims=True)
        h = (x * lax.rsqrt(ms + NORM_EPS) * g1_ref[...]).astype(BF16)
        z = jnp.dot(h, w_ref[...], preferred_element_type=F32)
        u_ref[rs, :] = z[:, :FOURIER_WIDTH].astype(BF16)
        for c in range(ATTN_WIDTH // LANES):
            qc = _pair_head_norm(z[:, q0 + c * LANES:q0 + (c + 1) * LANES], qg_ref[...], lo)
            q_ref[rs, c * LANES:(c + 1) * LANES] = (qc * (HEAD_DIM ** -0.5 * LOG2E)).astype(BF16)
        kc = _pair_head_norm(z[:, k0:k0 + KV_WIDTH], kg_ref[...], lo)
        vc = z[:, k0 + KV_WIDTH:k0 + 2 * KV_WIDTH]
        kv_ref[rs, 0:LANES] = kc.astype(BF16)
        kv_ref[rs, LANES:2 * LANES] = pltpu.roll(kc, HEAD_DIM, 1).astype(BF16)
        kv_ref[rs, 2 * LANES:3 * LANES] = vc.astype(BF16)
        kv_ref[rs, 3 * LANES:4 * LANES] = pltpu.roll(vc, HEAD_DIM, 1).astype(BF16)


def _inproj(x2d, norm1, w_in, q_norm, k_norm):
    t, d = x2d.shape
    tm = min(TOKEN_TILE, t)
    n_in = w_in.shape[1]
    qg = jnp.tile(q_norm, LANES // HEAD_DIM).reshape(1, LANES)
    kg = jnp.tile(k_norm, LANES // HEAD_DIM).reshape(1, LANES)
    full = lambda i: (0, 0)
    return pl.pallas_call(
        _inproj_kernel,
        grid=(t // tm,),
        in_specs=[
            pl.BlockSpec((tm, d), lambda i: (i, 0)),
            pl.BlockSpec((1, d), full),
            pl.BlockSpec((d, n_in), full),
            pl.BlockSpec((1, LANES), full),
            pl.BlockSpec((1, LANES), full),
        ],
        out_specs=[
            pl.BlockSpec((tm, FOURIER_WIDTH), lambda i: (i, 0)),
            pl.BlockSpec((tm, ATTN_WIDTH), lambda i: (i, 0)),
            pl.BlockSpec((tm, 4 * LANES), lambda i: (i, 0)),
        ],
        out_shape=[
            jax.ShapeDtypeStruct((t, FOURIER_WIDTH), BF16),
            jax.ShapeDtypeStruct((t, ATTN_WIDTH), BF16),
            jax.ShapeDtypeStruct((t, 4 * LANES), BF16),
        ],
        compiler_params=_params(1, VMEM_LIMIT),
        name="inproj",
    )(x2d, norm1.reshape(1, d), w_in.astype(BF16), qg, kg)


def _fourier_kernel(u_ref, cs_ref, ss_ref, rev_ref, cc_ref, sc_ref, wf_ref, g_ref, o_ref,
                    p_scr, q_scr, e_scr, *, scale, row_block):
    for g in range(FOURIER_GROUPS):
        sl = slice(g * FOURIER_CH, (g + 1) * FOURIER_CH)
        w = wf_ref[g].astype(BF16)
        a = (jnp.dot(cc_ref[...], w, preferred_element_type=F32) * scale).astype(BF16)
        b = (jnp.dot(sc_ref[...], w, preferred_element_type=F32) * scale).astype(BF16)
        ug = u_ref[:, sl]
        p_scr[:, sl] = jnp.dot(ug, a, preferred_element_type=F32).astype(BF16)
        q_scr[:, sl] = jnp.dot(ug, b, preferred_element_type=F32).astype(BF16)
    half = u_ref.shape[0] // 2
    gain = g_ref[...]

    def norm(y):
        ms = jnp.mean(y * y, axis=-1, keepdims=True)
        return y * lax.rsqrt(ms + NORM_EPS) * gain

    n_blk = half // row_block
    mid = None
    for r in range(n_blk):
        rs = slice(r * row_block, (r + 1) * row_block)
        extra = SUBLANES if r == n_blk - 1 else 0
        c = jnp.dot(cs_ref[r * row_block:(r + 1) * row_block + extra, :], p_scr[...],
                    preferred_element_type=F32)
        d = jnp.dot(ss_ref[rs, :], q_scr[...], preferred_element_type=F32)
        o_ref[rs, :] = norm(c[:row_block] + d).astype(BF16)
        e_scr[rs, :] = norm(c[:row_block] - d).astype(BF16)
        if extra:
            mid = norm(c[row_block:row_block + 1])
    for r in range(n_blk):
        z = jnp.dot(rev_ref[r * row_block:(r + 1) * row_block, :], e_scr[...],
                    preferred_element_type=F32)
        if r == 0:
            z = jnp.where(lax.broadcasted_iota(I32, z.shape, 0) == 0, mid, z)
        o_ref[half + r * row_block:half + (r + 1) * row_block, :] = z.astype(BF16)


def _dft_tables(n):
    k = np.arange(n, dtype=np.int64)
    ang = 2.0 * np.pi * ((k[:, None] * k[None, :]) % n).astype(np.float64) / n
    return np.cos(ang), np.sin(ang)


def _fourier(u, w_fourier, g_out, batch, seq):
    cs, ss = _dft_tables(seq)
    cc, sc = _dft_tables(FOURIER_CH)
    scale = 1.0 / math.sqrt(seq * FOURIER_CH)
    half = seq // 2
    row_block = min(FOURIER_ROWS, half)
    rev = np.zeros((half, half), np.float32)
    rev[np.arange(1, half), half - np.arange(1, half)] = 1.0
    full2 = lambda b: (0, 0)
    return pl.pallas_call(
        functools.partial(_fourier_kernel, scale=scale, row_block=row_block),
        grid=(batch,),
        in_specs=[
            pl.BlockSpec((seq, FOURIER_WIDTH), lambda b: (b, 0)),
            pl.BlockSpec((half + SUBLANES, seq), full2),
            pl.BlockSpec((half, seq), full2),
            pl.BlockSpec((half, half), full2),
            pl.BlockSpec((FOURIER_CH, FOURIER_CH), full2),
            pl.BlockSpec((FOURIER_CH, FOURIER_CH), full2),
            pl.BlockSpec((FOURIER_GROUPS, FOURIER_CH, FOURIER_CH), lambda b: (0, 0, 0)),
            pl.BlockSpec((1, FOURIER_WIDTH), full2),
        ],
        out_specs=pl.BlockSpec((seq, FOURIER_WIDTH), lambda b: (b, 0)),
        out_shape=jax.ShapeDtypeStruct((batch * seq, FOURIER_WIDTH), BF16),
        scratch_shapes=[pltpu.VMEM((seq, FOURIER_WIDTH), BF16),
                        pltpu.VMEM((seq, FOURIER_WIDTH), BF16),
                        pltpu.VMEM((half, FOURIER_WIDTH), BF16)],
        compiler_params=_params(1, VMEM_LIMIT),
        name="fourier",
    )(u, jnp.asarray(cs[:half + SUBLANES], BF16), jnp.asarray(ss[:half], BF16),
      jnp.asarray(rev, BF16), jnp.asarray(cc, BF16), jnp.asarray(-sc, BF16), w_fourier,
      g_out.reshape(1, FOURIER_WIDTH))


def _attn_kernel(sink_ref, q_ref, kvp_ref, kvo_ref, kvn_ref, bias_a_ref, bias_b_ref, g_ref, o_ref,
                 acc_ref):
    kv = jnp.concatenate([kvp_ref[...], kvo_ref[...], kvn_ref[...]], axis=0)
    nk = kv.shape[0]
    lo = lax.broadcasted_iota(I32, (nk, LANES), 1) < HEAD_DIM
    k_a, k_b = kv[:, 0:LANES], kv[:, LANES:2 * LANES]
    v_a, v_b = kv[:, 2 * LANES:3 * LANES], kv[:, 3 * LANES:4 * LANES]
    zero = jnp.zeros_like(k_a)
    k_lo = (jnp.where(lo, k_a, zero), jnp.where(lo, k_b, zero))
    k_hi = (jnp.where(lo, zero, k_b), jnp.where(lo, zero, k_a))
    v_lo = (jnp.where(lo, v_a, zero), jnp.where(lo, v_b, zero))
    v_hi = (jnp.where(lo, zero, v_b), jnp.where(lo, zero, v_a))
    lo_out = lax.broadcasted_iota(I32, (Q_BLOCK, LANES), 1) < HEAD_DIM
    bias_refs = (bias_a_ref, bias_b_ref)
    rows2 = q_ref.shape[0]
    nt = (((1,), (1,)), ((), ()))
    for h in range(N_KV_HEADS):
        qs = jnp.concatenate([q_ref[:, (2 * h) * LANES:(2 * h + 1) * LANES],
                              q_ref[:, (2 * h + 1) * LANES:(2 * h + 2) * LANES]], axis=0)
        s_par = (lax.dot_general(qs, k_lo[h], nt, preferred_element_type=F32),
                 lax.dot_general(qs, k_hi[h], nt, preferred_element_type=F32))
        for sb in range(2):
            keys = slice(sb * Q_BLOCK, sb * Q_BLOCK + 3 * Q_BLOCK)
            vcat = jnp.concatenate([v_lo[h][keys, :], v_hi[h][keys, :]], axis=0)
            for c in range(2):
                r0 = c * rows2 + sb * Q_BLOCK
                probs, invs = [], []
                for par in range(2):
                    hq = 4 * h + 2 * c + par
                    s = s_par[par][r0:r0 + Q_BLOCK, keys] + bias_refs[sb][hq]
                    sink = sink_ref[hq]
                    m = jnp.maximum(jnp.max(s, axis=-1, keepdims=True), sink)
                    p = jnp.exp2(s - m)
                    denom = jnp.sum(p, axis=-1, keepdims=True) + jnp.exp2(sink - m)
                    probs.append(p.astype(BF16))
                    invs.append(1.0 / denom)
                pcat = jnp.concatenate(probs, axis=1)
                chunk = 2 * h + c
                o = jnp.dot(pcat, vcat, preferred_element_type=F32)
                acc_ref[sb * Q_BLOCK:(sb + 1) * Q_BLOCK, chunk * LANES:(chunk + 1) * LANES] = (
                    o * jnp.where(lo_out, invs[0], invs[1]))
    y = acc_ref[...]
    ms = jnp.mean(y * y, axis=-1, keepdims=True)
    o_ref[...] = (y * lax.rsqrt(ms + NORM_EPS) * g_ref[...]).astype(BF16)


def _t5_bucket(rel):
    nb = N_BUCKETS // 2
    max_exact = nb // 2
    ret = (rel > 0).astype(jnp.int32) * nb
    n = jnp.abs(rel)
    nf = jnp.maximum(n, 1).astype(jnp.float32)
    large = max_exact + (jnp.log(nf / max_exact) / math.log(MAX_DISTANCE / max_exact)
                         * (nb - max_exact)).astype(jnp.int32)
    large = jnp.minimum(large, nb - 1)
    return ret + jnp.where(n < max_exact, n, large)


def _attention(q, kv, sinks, rel_bias, g_out, batch, seq):
    nb = seq // Q_BLOCK
    assert nb % 2 == 0
    nb2 = nb // 2
    qi = jnp.arange(Q_BLOCK, dtype=jnp.int32)[:, None]
    kj = jnp.arange(3 * Q_BLOCK, dtype=jnp.int32)[None, :]
    rel = kj - Q_BLOCK - qi
    period = 4 * Q_BLOCK
    p = jnp.arange(period, dtype=jnp.int32)
    off = jnp.where(p < 3 * Q_BLOCK, p, p - period) - Q_BLOCK
    hit = _t5_bucket(off)[None, :, None] == jnp.arange(N_BUCKETS, dtype=jnp.int32)
    by_off = jnp.sum(jnp.where(hit, rel_bias.astype(F32).T[:, None, :], 0.0), axis=-1)
    bias = jnp.tile(by_off, (1, Q_BLOCK))[:, :Q_BLOCK * (period - 1)]
    bias = bias.reshape(N_Q_HEADS, Q_BLOCK, period - 1)[:, :, :3 * Q_BLOCK]
    band = jnp.abs(rel) <= WINDOW
    first = band & (kj >= Q_BLOCK)
    last = band & (kj < 2 * Q_BLOCK)
    table = jnp.stack([jnp.where(msk[None], bias * LOG2E, MASK_VALUE) for msk in (first, band, last)])
    q_rows = 2 * Q_BLOCK
    grid_spec = pltpu.PrefetchScalarGridSpec(
        num_scalar_prefetch=1,
        grid=(batch, nb2),
        in_specs=[
            pl.BlockSpec((q_rows, ATTN_WIDTH), lambda b, i, s: (b * nb2 + i, 0)),
            pl.BlockSpec((Q_BLOCK, 4 * LANES),
                         lambda b, i, s: (b * nb + jnp.maximum(2 * i - 1, 0), 0)),
            pl.BlockSpec((q_rows, 4 * LANES), lambda b, i, s: (b * nb2 + i, 0)),
            pl.BlockSpec((Q_BLOCK, 4 * LANES),
                         lambda b, i, s: (b * nb + jnp.minimum(2 * i + 2, nb - 1), 0)),
            pl.BlockSpec((None, N_Q_HEADS, Q_BLOCK, 3 * Q_BLOCK),
                         lambda b, i, s: (jnp.where(i == 0, 0, 1), 0, 0, 0)),
            pl.BlockSpec((None, N_Q_HEADS, Q_BLOCK, 3 * Q_BLOCK),
                         lambda b, i, s: (jnp.where(i == nb2 - 1, 2, 1), 0, 0, 0)),
            pl.BlockSpec((1, ATTN_WIDTH), lambda b, i, s: (0, 0)),
        ],
        out_specs=pl.BlockSpec((q_rows, ATTN_WIDTH), lambda b, i, s: (b * nb2 + i, 0)),
        scratch_shapes=[pltpu.VMEM((q_rows, ATTN_WIDTH), F32)],
    )
    return pl.pallas_call(
        _attn_kernel,
        grid_spec=grid_spec,
        out_shape=jax.ShapeDtypeStruct((batch * seq, ATTN_WIDTH), BF16),
        compiler_params=_params(2, VMEM_LIMIT),
        name="attention",
    )(sinks.astype(F32) * LOG2E, q, kv, kv, kv, table, table, g_out.reshape(1, ATTN_WIDTH))


def _outproj_kernel(yf_ref, ya_ref, x_ref, wo_ref, g2_ref, wrt_ref, brt_ref, triu_ref, scan_ref,
                    x1_ref, h2_ref, post_ref, gatet_ref, cnt_ref, *, n_experts):
    half = yf_ref.shape[1]
    mix = (jnp.dot(yf_ref[...], wo_ref[:half, :], preferred_element_type=F32)
           + jnp.dot(ya_ref[...], wo_ref[half:, :], preferred_element_type=F32))
    x1 = x_ref[...] + mix
    x1_ref[...] = x1
    ms = jnp.mean(x1 * x1, axis=-1, keepdims=True)
    h2 = x1 * lax.rsqrt(ms + NORM_EPS) * g2_ref[...]
    h2_ref[...] = h2.astype(BF16)
    h_hi = h2.astype(BF16)
    h_lo = (h2 - h_hi.astype(F32)).astype(BF16)
    nt = (((1,), (1,)), ((), ()))
    t1 = lax.dot_general(wrt_ref[...], h_hi, nt, preferred_element_type=F32)
    t2 = lax.dot_general(wrt_ref[:n_experts, :], h_lo, nt, preferred_element_type=F32)
    logits = t1[:n_experts] + t1[n_experts:] + t2 + brt_ref[...]
    tm = logits.shape[1]
    sub_e = lax.broadcasted_iota(I32, (n_experts, tm), 0).astype(F32)
    work = logits
    vals, idxs = [], []
    for _ in range(TOP_K):
        m = jnp.max(work, axis=0, keepdims=True)
        ik = jnp.min(jnp.where(work == m, sub_e, float(n_experts)), axis=0, keepdims=True)
        work = jnp.where(sub_e == ik, -jnp.inf, work)
        vals.append(m)
        idxs.append(ik)
    exps = [jnp.exp(v - vals[0]) for v in vals]
    inv = 1.0 / (exps[0] + exps[1] + exps[2] + exps[3])
    gates = [e * inv for e in exps]

    sub = lax.broadcasted_iota(I32, (LANES, tm), 0).astype(F32)
    onehot = jnp.zeros((LANES, tm), F32)
    for k in range(TOP_K):
        onehot = onehot + jnp.where(sub == idxs[k] + float(k * n_experts), 1.0, 0.0)
    onehot_b = onehot.astype(BF16)
    before = jnp.dot(onehot_b, triu_ref[...], preferred_element_type=F32)
    counts = lax.dot_general(jnp.ones((SUBLANES, tm), BF16), onehot_b, nt,
                             preferred_element_type=F32)[0:1, :]
    total = counts
    for k in range(1, TOP_K):
        total = total + pltpu.roll(counts, k * n_experts, 1)
    cnt_ref[0] = total.astype(I32)
    colsum = jnp.sum(onehot, axis=1, keepdims=True)
    blocks = [colsum[k * n_experts:(k + 1) * n_experts] for k in range(TOP_K)]
    total_e = blocks[0] + blocks[1] + blocks[2] + blocks[3]
    run_e = jnp.floor((total_e + (RUN_ALIGN - 1)) * (1.0 / RUN_ALIGN)) * RUN_ALIGN
    run_start = jnp.dot(scan_ref[...], jnp.broadcast_to(run_e, (n_experts, LANES)).astype(BF16),
                        preferred_element_type=F32)[:, 0:1]
    adds, acc = [], run_start
    for k in range(TOP_K):
        adds.append(acc)
        acc = acc + blocks[k]
    placed = (before + jnp.concatenate(adds, axis=0)) * onehot
    pos = [jnp.sum(placed[k * n_experts:(k + 1) * n_experts], axis=0, keepdims=True)
           for k in range(TOP_K)]
    post_ref[...] = jnp.concatenate(pos + [jnp.zeros((SUBLANES - TOP_K, tm), F32)], axis=0).astype(I32)
    gatet_ref[...] = jnp.concatenate(gates + [jnp.zeros((SUBLANES - TOP_K, tm), F32)], axis=0)


def _outproj(yf, ya, x2d, w_out, norm2, w_router, b_router):
    t, d = x2d.shape
    tm = min(TOKEN_TILE, t)
    n_tiles = t // tm
    n_experts = w_router.shape[1]
    assert TOP_K * n_experts == LANES
    triu = np.triu(np.ones((tm, tm), np.float32), 1)
    scan = np.tril(np.ones((n_experts, n_experts), np.float32), -1)
    wr_hi = w_router.astype(BF16)
    wr_lo = (w_router - wr_hi.astype(F32)).astype(BF16)
    wrt = jnp.concatenate([wr_hi.T, wr_lo.T], axis=0)
    full = lambda i: (0, 0)
    row = lambda i: (i, 0)
    return pl.pallas_call(
        functools.partial(_outproj_kernel, n_experts=n_experts),
        grid=(n_tiles,),
        in_specs=[
            pl.BlockSpec((tm, yf.shape[1]), row),
            pl.BlockSpec((tm, ya.shape[1]), row),
            pl.BlockSpec((tm, d), row),
            pl.BlockSpec((w_out.shape[0], d), full),
            pl.BlockSpec((1, d), full),
            pl.BlockSpec((2 * n_experts, d), full),
            pl.BlockSpec((n_experts, 1), full),
            pl.BlockSpec((tm, tm), full),
            pl.BlockSpec((n_experts, n_experts), full),
        ],
        out_specs=[
            pl.BlockSpec((tm, d), row),
            pl.BlockSpec((tm, d), row),
            pl.BlockSpec((SUBLANES, tm), row),
            pl.BlockSpec((SUBLANES, tm), row),
            pl.BlockSpec((1, 1, LANES), lambda i: (i, 0, 0)),
        ],
        out_shape=[
            jax.ShapeDtypeStruct((t, d), F32),
            jax.ShapeDtypeStruct((t, d), BF16),
            jax.ShapeDtypeStruct((n_tiles * SUBLANES, tm), I32),
            jax.ShapeDtypeStruct((n_tiles * SUBLANES, tm), F32),
            jax.ShapeDtypeStruct((n_tiles, 1, LANES), I32),
        ],
        compiler_params=_params(1, VMEM_LIMIT),
        name="outproj_router",
    )(yf, ya, x2d, w_out.astype(BF16), norm2.reshape(1, d), wrt, b_router.reshape(n_experts, 1),
      jnp.asarray(triu, BF16), jnp.asarray(scan, BF16))


def _pack_pairs(x, is_bf16_exact=False):
    half = x.shape[1] // 2
    a, b = x[:, :half], x[:, half:]
    if not is_bf16_exact:
        a, b = a.astype(BF16).astype(F32), b.astype(BF16).astype(F32)
    return lax.bitcast_convert_type(a, U32) | (lax.bitcast_convert_type(b, U32) >> BF16_BITS)


def _unpack_pairs(w):
    hi = lax.bitcast_convert_type(w & U32(((1 << BF16_BITS) - 1) << BF16_BITS), F32)
    lo = lax.bitcast_convert_type(w << BF16_BITS, F32)
    return hi.astype(BF16), lo.astype(BF16)


def _rows(start, size):
    if not isinstance(size, int):
        size = pl.multiple_of(size, RUN_ALIGN)
    return pl.ds(pl.multiple_of(start, RUN_ALIGN), size)


def _dispatch_kernel(cnt_ref, lst_ref, base_ref, rows_ref, tail_ref, post_ref, h2_ref, xs_ref,
                     buf, zbuf, sem, zsem, *, n_experts):
    j = pl.program_id(0)
    tm = h2_ref.shape[0]
    n_local = buf.shape[1]

    def start_runs(tile, slot):
        def run(e, carry):
            r = tile * n_experts + e
            n = cnt_ref[r]

            @pl.when(n > 0)
            def _():
                pltpu.make_async_copy(buf.at[slot, _rows(lst_ref[r], n), :],
                                      xs_ref.at[_rows(base_ref[r], n), :], sem.at[slot]).start(
                                          priority=1)
            return carry
        lax.fori_loop(0, n_experts, run, 0)

    def wait_runs(tile, slot):
        n = rows_ref[tile]

        @pl.when(n > 0)
        def _():
            pltpu.make_async_copy(buf.at[slot, _rows(0, n), :], xs_ref.at[_rows(0, n), :],
                                  sem.at[slot]).wait()

    def zero_fill(op):
        def tail(e, carry):
            n = tail_ref[n_experts + e]

            @pl.when(n > 0)
            def _():
                getattr(pltpu.make_async_copy(zbuf.at[_rows(0, n), :],
                                              xs_ref.at[_rows(tail_ref[e], n), :], zsem), op)()
            return carry
        lax.fori_loop(0, n_experts, tail, 0)

        def spare(b, carry):
            getattr(pltpu.make_async_copy(zbuf, xs_ref.at[_rows(b * EXPERT_ROWS, EXPERT_ROWS), :],
                                          zsem), op)()
            return carry
        lax.fori_loop(tail_ref[2 * n_experts], xs_ref.shape[0] // EXPERT_ROWS, spare, 0)

    slot = j % 2
    @pl.when(j >= 2)
    def _():
        wait_runs(j - 2, slot)

    @pl.when(j == 0)
    def _():
        zbuf[...] = jnp.zeros_like(zbuf)
        zero_fill("start")
        zero_fill("wait")

    h = h2_ref[...]
    post = post_ref[0:TOP_K, :]
    chunk_of = lax.shift_right_logical(post, PERM_CHUNK.bit_length() - 1)
    offset = (post & (PERM_CHUNK - 1)).astype(F32)
    rows = lax.broadcasted_iota(I32, (PERM_CHUNK, tm), 0).astype(F32).astype(BF16)
    one = jnp.ones((PERM_CHUNK, tm), BF16)
    per = DISPATCH_ROWS // PERM_CHUNK
    for mc in range(n_local // DISPATCH_ROWS):
        parts = []
        for rc in range(mc * per, (mc + 1) * per):
            off = jnp.where(chunk_of == rc, offset, -1.0).astype(BF16)
            perm = jnp.zeros((PERM_CHUNK, tm), BF16)
            for k in range(TOP_K):
                perm = jnp.where(rows == off[k:k + 1, :], one, perm)
            parts.append(perm)
        rs = slice(mc * DISPATCH_ROWS, (mc + 1) * DISPATCH_ROWS)
        buf[slot, rs, :] = _pack_pairs(
            jnp.dot(jnp.concatenate(parts, axis=0), h, preferred_element_type=F32), True)

    start_runs(j, slot)

    @pl.when(j == pl.num_programs(0) - 1)
    def _():
        @pl.when(j >= 1)
        def _():
            wait_runs(j - 1, 1 - slot)
        wait_runs(j, slot)


def _local_rows(tm, n_experts):
    worst = TOP_K * tm + n_experts * (RUN_ALIGN - 1)
    return -(-worst // DISPATCH_ROWS) * DISPATCH_ROWS


def _dispatch(plan, post, h2, n_rows, n_experts):
    t, d = h2.shape
    tm = min(TOKEN_TILE, t)
    grid_spec = pltpu.PrefetchScalarGridSpec(
        num_scalar_prefetch=5,
        grid=(t // tm,),
        in_specs=[
            pl.BlockSpec((SUBLANES, tm), lambda i, *_: (i, 0)),
            pl.BlockSpec((tm, d), lambda i, *_: (i, 0)),
        ],
        out_specs=pl.BlockSpec(memory_space=pl.ANY),
        scratch_shapes=[pltpu.VMEM((2, _local_rows(tm, n_experts), d // 2), U32),
                        pltpu.VMEM((EXPERT_ROWS, d // 2), U32),
                        pltpu.SemaphoreType.DMA((2,)), pltpu.SemaphoreType.DMA(())],
    )
    return pl.pallas_call(
        functools.partial(_dispatch_kernel, n_experts=n_experts),
        grid_spec=grid_spec,
        out_shape=jax.ShapeDtypeStruct((n_rows, d // 2), U32),
        compiler_params=_params(1, VMEM_LIMIT),
        name="dispatch",
    )(plan["cnt"], plan["lst"], plan["base"], plan["rows"], plan["tail"], post, h2)


def _combine_kernel(cnt_ref, lst_ref, base_ref, rows_ref, post_ref, gatet_ref, x1_ref, ys_ref, o_ref,
                    buf, g_scr, y_scr, sem, *, n_experts):
    j = pl.program_id(0)
    tm, d = x1_ref.shape
    n_local = buf.shape[1]

    def start_runs(tile, slot):
        def run(e, carry):
            r = tile * n_experts + e
            n = cnt_ref[r]

            @pl.when(n > 0)
            def _():
                pltpu.make_async_copy(ys_ref.at[_rows(base_ref[r], n), :],
                                      buf.at[slot, _rows(lst_ref[r], n), :], sem.at[slot]).start(
                                          priority=1)
            return carry
        lax.fori_loop(0, n_experts, run, 0)

    def wait_runs(tile, slot):
        n = rows_ref[tile]

        @pl.when(n > 0)
        def _():
            pltpu.make_async_copy(ys_ref.at[_rows(0, n), :], buf.at[slot, _rows(0, n), :],
                                  sem.at[slot]).wait()

    slot = j % 2
    @pl.when(j == 0)
    def _():
        buf[...] = jnp.zeros_like(buf)
        start_runs(j, slot)

    @pl.when(j + 1 < pl.num_programs(0))
    def _():
        start_runs(j + 1, 1 - slot)

    wait_runs(j, slot)

    post = post_ref[0:TOP_K, :]
    gate = gatet_ref[0:TOP_K, :].astype(BF16)
    chunk_of = lax.shift_right_logical(post, PERM_CHUNK.bit_length() - 1)
    offset = (post & (PERM_CHUNK - 1)).astype(F32)
    rows = lax.broadcasted_iota(I32, (PERM_CHUNK, tm), 0).astype(F32).astype(BF16)
    for rc in range(n_local // PERM_CHUNK):
        chunk = slice(rc * PERM_CHUNK, (rc + 1) * PERM_CHUNK)
        off = jnp.where(chunk_of == rc, offset, -1.0).astype(BF16)
        g = jnp.zeros((PERM_CHUNK, tm), BF16)
        for k in range(TOP_K):
            g = jnp.where(rows == off[k:k + 1, :], jnp.broadcast_to(gate[k:k + 1, :], g.shape), g)
        g_scr[chunk, :] = g
        y_scr[chunk, :d // 2], y_scr[chunk, d // 2:] = _unpack_pairs(buf[slot, chunk, :])
    o_ref[...] = x1_ref[...] + lax.dot_general(g_scr[...], y_scr[...], (((0,), (0,)), ((), ())),
                                               preferred_element_type=F32)


def _combine(plan, post, gatet, x1, ys, n_experts):
    t, d = x1.shape
    tm = min(TOKEN_TILE, t)
    grid_spec = pltpu.PrefetchScalarGridSpec(
        num_scalar_prefetch=4,
        grid=(t // tm,),
        in_specs=[
            pl.BlockSpec((SUBLANES, tm), lambda i, *_: (i, 0)),
            pl.BlockSpec((SUBLANES, tm), lambda i, *_: (i, 0)),
            pl.BlockSpec((tm, d), lambda i, *_: (i, 0)),
            pl.BlockSpec(memory_space=pl.ANY),
        ],
        out_specs=pl.BlockSpec((tm, d), lambda i, *_: (i, 0)),
        scratch_shapes=[pltpu.VMEM((2, _local_rows(tm, n_experts), d // 2), U32),
                        pltpu.VMEM((_local_rows(tm, n_experts), tm), BF16),
                        pltpu.VMEM((_local_rows(tm, n_experts), d), BF16),
                        pltpu.SemaphoreType.DMA((2,))],
    )
    return pl.pallas_call(
        functools.partial(_combine_kernel, n_experts=n_experts),
        grid_spec=grid_spec,
        out_shape=jax.ShapeDtypeStruct((t, d), F32),
        compiler_params=_params(1, VMEM_LIMIT),
        name="combine",
    )(plan["cnt"], plan["lst"], plan["base"], plan["rows"], post, gatet, x1, ys)


def _expert_kernel(be_ref, nxt_ref, par_ref, meta_ref, xs_ref, wgu_hbm, wd_hbm, bias_ref, ys_ref,
                   wgu_buf, wd_buf, wg_s, wu_s, wd_s, wsem):
    i = pl.program_id(0)
    n_used = meta_ref[0]
    active = i < n_used
    new_expert = (i == 0) | (be_ref[i] != be_ref[jnp.maximum(i - 1, 0)])

    def weight_copies(expert, slot):
        return (pltpu.make_async_copy(wgu_hbm.at[expert], wgu_buf.at[slot], wsem.at[slot]),
                pltpu.make_async_copy(wd_hbm.at[expert], wd_buf.at[slot], wsem.at[slot]))

    @pl.when(active & new_expert)
    def _():
        slot = par_ref[i]

        @pl.when(i == 0)
        def _():
            for cp in weight_copies(be_ref[i], slot):
                cp.start()

        for cp in weight_copies(be_ref[i], slot):
            cp.wait()
        nxt = nxt_ref[i]

        @pl.when(nxt >= 0)
        def _():
            for cp in weight_copies(nxt, 1 - slot):
                cp.start(priority=1)

        width = 2 * LANES
        src = lax.broadcasted_iota(I32, (width, width), 0)
        dst = lax.broadcasted_iota(I32, (width, width), 1)
        perm = jnp.where(src == jnp.where(dst < LANES, 2 * dst, 2 * (dst - LANES) + 1), 1.0, 0.0
                         ).astype(BF16)
        for c in range(wgu_buf.shape[2] // width):
            wc = wgu_buf[slot, :, c * width:(c + 1) * width].astype(BF16)
            r = jnp.dot(wc, perm, preferred_element_type=F32)
            wg_s[:, c * LANES:(c + 1) * LANES] = r[:, :LANES].astype(BF16)
            wu_s[:, c * LANES:(c + 1) * LANES] = r[:, LANES:].astype(BF16)
        wd_s[...] = wd_buf[slot].astype(BF16)

    @pl.when(active)
    def _():
        f = wg_s.shape[1]
        bias = bias_ref[be_ref[i]]
        xb = jnp.concatenate(_unpack_pairs(xs_ref[...]), axis=1)
        g = jnp.dot(xb, wg_s[...], preferred_element_type=F32) + bias[:, :f]
        up = jnp.dot(xb, wu_s[...], preferred_element_type=F32) + bias[:, f:2 * f]
        g = jnp.minimum(g, SWIGLU_LIMIT)
        up = jnp.clip(up, -SWIGLU_LIMIT, SWIGLU_LIMIT)
        act = g * (1.0 / (1.0 + jnp.exp(-SWIGLU_ALPHA * g))) * (up + 1.0)
        ys_ref[...] = _pack_pairs(
            jnp.dot(act.astype(BF16), wd_s[...], preferred_element_type=F32) + bias[:, 2 * f:])

    @pl.when(jnp.logical_not(active))
    def _():
        ys_ref[...] = jnp.zeros_like(ys_ref)


def _experts(blk, meta, xs, w_gate_up, b_gate_up, w_down, b_down):
    n_rows = xs.shape[0]
    n_experts, d, f2 = w_gate_up.shape
    f = f2 // 2
    bm = EXPERT_ROWS
    n_blocks = n_rows // bm
    bias = jnp.concatenate([b_gate_up[:, 0::2], b_gate_up[:, 1::2], b_down], axis=1)
    bias = bias.reshape(n_experts, 1, f2 + d)
    rows = lambda i, be, nxt, par, meta: (jnp.minimum(i, meta[0] - 1), 0)
    grid_spec = pltpu.PrefetchScalarGridSpec(
        num_scalar_prefetch=4,
        grid=(n_blocks,),
        in_specs=[
            pl.BlockSpec((bm, d // 2), rows),
            pl.BlockSpec(memory_space=pl.ANY),
            pl.BlockSpec(memory_space=pl.ANY),
            pl.BlockSpec((n_experts, 1, f2 + d), lambda i, *_: (0, 0, 0)),
        ],
        out_specs=pl.BlockSpec((bm, d // 2), lambda i, *_: (i, 0)),
        scratch_shapes=[pltpu.VMEM((2, d, f2), F32), pltpu.VMEM((2, f, d), F32),
                        pltpu.VMEM((d, f), BF16), pltpu.VMEM((d, f), BF16),
                        pltpu.VMEM((f, d), BF16), pltpu.SemaphoreType.DMA((2,))],
    )
    return pl.pallas_call(
        _expert_kernel,
        grid_spec=grid_spec,
        out_shape=jax.ShapeDtypeStruct((n_rows, d // 2), U32),
        compiler_params=_params(1, VMEM_LIMIT),
        name="experts",
    )(blk["expert"], blk["next"], blk["slot"], meta, xs, w_gate_up, w_down, bias)


def _routing_plan(counts, n_experts, bm, n_blocks):
    cnt = counts[:, 0, :n_experts]
    run = (cnt + RUN_ALIGN - 1) // RUN_ALIGN * RUN_ALIGN
    per_expert = jnp.sum(run, axis=0)
    padded = (per_expert + bm - 1) // bm * bm
    pend = jnp.cumsum(padded)
    pstart = pend - padded
    base = pstart[None, :] + jnp.cumsum(run, axis=0) - run
    lst = jnp.cumsum(run, axis=1) - run
    n_used = pend[-1] // bm
    tail = jnp.concatenate([pstart + per_expert, padded - per_expert, n_used[None]])
    starts = jnp.arange(n_blocks, dtype=I32) * bm
    blk = jnp.sum((starts[:, None] >= pend[None, :]).astype(I32), axis=1)
    blk = jnp.minimum(blk, n_experts - 1)
    last = jnp.sum((((n_used - 1) * bm) >= pend).astype(I32))
    blk_e = jnp.where(jnp.arange(n_blocks) < n_used, blk, jnp.minimum(last, n_experts - 1))
    ids = jnp.arange(n_experts, dtype=I32)
    has_rows = padded > 0
    later = jnp.where(has_rows[None, :] & (ids[None, :] > ids[:, None]), ids[None, :], n_experts)
    next_e = jnp.min(later, axis=1)
    next_e = jnp.where(next_e < n_experts, next_e, -1)
    ordinal = jnp.cumsum(has_rows.astype(I32)) - 1
    onehot = blk_e[:, None] == ids[None, :]
    blocks = {"expert": blk_e.astype(I32),
              "next": jnp.sum(jnp.where(onehot, next_e[None, :], 0), axis=1).astype(I32),
              "slot": jnp.sum(jnp.where(onehot, (ordinal % 2)[None, :], 0), axis=1).astype(I32)}
    plan = {"cnt": run.reshape(-1).astype(I32), "lst": lst.reshape(-1).astype(I32),
            "base": base.reshape(-1).astype(I32), "rows": jnp.sum(run, axis=1).astype(I32),
            "tail": tail.astype(I32)}
    return plan, blocks, n_used.astype(I32).reshape(1)


def _layer(x2d, batch, seq, norm1, w_in, q_norm, k_norm, sinks, rel_bias, w_fourier, g_fourier_out,
           g_attn_out, w_out, norm2, w_router, b_router, w_gate_up, b_gate_up, w_down, b_down):
    t, d = x2d.shape
    n_experts = w_router.shape[1]
    u, q, kv = _inproj(x2d, norm1, w_in, q_norm, k_norm)
    yf = _fourier(u, w_fourier, g_fourier_out, batch, seq)
    ya = _attention(q, kv, sinks, rel_bias, g_attn_out, batch, seq)
    x1, h2, post, gatet, counts = _outproj(yf, ya, x2d, w_out, norm2, w_router, b_router)
    bm = EXPERT_ROWS
    n_tiles = t // min(TOKEN_TILE, t)
    worst_rows = t * TOP_K + n_tiles * n_experts * (RUN_ALIGN - 1) + n_experts * (bm - RUN_ALIGN)
    n_blocks = -(-worst_rows // bm)
    plan, blocks, meta = _routing_plan(counts, n_experts, bm, n_blocks)
    xs = _dispatch(plan, post, h2, n_blocks * bm, n_experts)
    ys = _experts(blocks, meta, xs, w_gate_up, b_gate_up, w_down, b_down)
    return _combine(plan, post, gatet, x1, ys, n_experts)


def kernel(x, norm1, w_in, q_norm, k_norm, sinks, rel_bias, w_fourier, g_fourier_out, g_attn_out,
           w_out, norm2, w_router, b_router, w_gate_up, b_gate_up, w_down, b_down):
    b, s, d = x.shape
    x2d = x.reshape(b * s, d)
    for l in range(norm1.shape[0]):
        x2d = _layer(x2d, b, s, norm1[l], w_in[l], q_norm[l], k_norm[l], sinks[l], rel_bias,
                     w_fourier[l], g_fourier_out[l], g_attn_out[l], w_out[l], norm2[l],
                     w_router[l], b_router[l], w_gate_up[l], b_gate_up[l], w_down[l], b_down[l])
    return x2d.reshape(b, s, d)
```

```python
import functools
import math

import jax
import jax.numpy as jnp
import numpy as np
from jax import lax
from jax.experimental import pallas as pl
from jax.experimental.pallas import tpu as pltpu

F32 = jnp.float32
BF16 = jnp.bfloat16
I32 = jnp.int32
U32 = jnp.uint32

NORM_EPS = 1e-5
QK_EPS = 1e-6
HEAD_DIM = 64
N_Q_HEADS = 8
N_KV_HEADS = 2
FOURIER_GROUPS = 4
FOURIER_CH = 128
FOURIER_WIDTH = FOURIER_GROUPS * FOURIER_CH
ATTN_WIDTH = N_Q_HEADS * HEAD_DIM
KV_WIDTH = N_KV_HEADS * HEAD_DIM
WINDOW = 128
Q_BLOCK = 128
N_BUCKETS = 32
MAX_DISTANCE = 128
TOP_K = 4
SWIGLU_ALPHA = 1.702
SWIGLU_LIMIT = 7.0
MASK_VALUE = -1e30
LOG2E = math.log2(math.e)

LANES = 128
SUBLANES = 8
TOKEN_TILE = 1024
ROUTE_TILE = 512
RUN_ALIGN = SUBLANES
ROW_GROUPS = 4
PERM_CHUNK = 256
DISPATCH_ROWS = 3 * PERM_CHUNK
EXPERT_ROWS = 512
FOURIER_ROWS = 512
BF16_BITS = 16
VMEM_LIMIT = 56 * 1024 * 1024


def _params(n_axes, vmem=None):
    return pltpu.CompilerParams(
        dimension_semantics=("arbitrary",) * n_axes, vmem_limit_bytes=vmem)


def _pair_head_norm(xc, gain, lo):
    x2 = xc * xc
    s_lo = jnp.sum(jnp.where(lo, x2, 0.0), axis=-1, keepdims=True)
    s_hi = jnp.sum(jnp.where(lo, 0.0, x2), axis=-1, keepdims=True)
    inv = jnp.where(lo, lax.rsqrt(s_lo * (1.0 / HEAD_DIM) + QK_EPS),
                    lax.rsqrt(s_hi * (1.0 / HEAD_DIM) + QK_EPS))
    return xc * inv * gain


def _inproj_kernel(x_ref, g1_ref, w_ref, qg_ref, kg_ref, u_ref, q_ref, kv_ref):
    rows = x_ref.shape[0] // ROW_GROUPS
    lo = lax.broadcasted_iota(I32, (rows, LANES), 1) < HEAD_DIM
    q0 = FOURIER_WIDTH
    k0 = q0 + ATTN_WIDTH
    for grp in range(ROW_GROUPS):
        rs = slice(grp * rows, (grp + 1) * rows)
        x = x_ref[rs, :]
        ms = jnp.mean(x * x, axis=-1, keepdims=True)
        h = (x * lax.rsqrt(ms + NORM_EPS) * g1_ref[...]).astype(BF16)
        z = jnp.dot(h, w_ref[...], preferred_element_type=F32)
        u_ref[rs, :] = z[:, :FOURIER_WIDTH].astype(BF16)
        for c in range(ATTN_WIDTH // LANES):
            qc = _pair_head_norm(z[:, q0 + c * LANES:q0 + (c + 1) * LANES], qg_ref[...], lo)
            q_ref[rs, c * LANES:(c + 1) * LANES] = (qc * (HEAD_DIM ** -0.5 * LOG2E)).astype(BF16)
        kc = _pair_head_norm(z[:, k0:k0 + KV_WIDTH], kg_ref[...], lo)
        vc = z[:, k0 + KV_WIDTH:k0 + 2 * KV_WIDTH]
        kv_ref[rs, 0:LANES] = kc.astype(BF16)
        kv_ref[rs, LANES:2 * LANES] = pltpu.roll(kc, HEAD_DIM, 1).astype(BF16)
        kv_ref[rs, 2 * LANES:3 * LANES] = vc.astype(BF16)
        kv_ref[rs, 3 * LANES:4 * LANES] = pltpu.roll(vc, HEAD_DIM, 1).astype(BF16)


def _inproj(x2d, norm1, w_in, q_norm, k_norm):
    t, d = x2d.shape
    tm = min(TOKEN_TILE, t)
    n_in = w_in.shape[1]
    qg = jnp.tile(q_norm, LANES // HEAD_DIM).reshape(1, LANES)
    kg = jnp.tile(k_norm, LANES // HEAD_DIM).reshape(1, LANES)
    full = lambda i: (0, 0)
    return pl.pallas_call(
        _inproj_kernel,
        grid=(t // tm,),
        in_specs=[
            pl.BlockSpec((tm, d), lambda i: (i, 0)),
            pl.BlockSpec((1, d), full),
            pl.BlockSpec((d, n_in), full),
            pl.BlockSpec((1, LANES), full),
            pl.BlockSpec((1, LANES), full),
        ],
        out_specs=[
            pl.BlockSpec((tm, FOURIER_WIDTH), lambda i: (i, 0)),
            pl.BlockSpec((tm, ATTN_WIDTH), lambda i: (i, 0)),
            pl.BlockSpec((tm, 4 * LANES), lambda i: (i, 0)),
        ],
        out_shape=[
            jax.ShapeDtypeStruct((t, FOURIER_WIDTH), BF16),
            jax.ShapeDtypeStruct((t, ATTN_WIDTH), BF16),
            jax.ShapeDtypeStruct((t, 4 * LANES), BF16),
        ],
        compiler_params=_params(1, VMEM_LIMIT),
        name="inproj",
    )(x2d, norm1.reshape(1, d), w_in.astype(BF16), qg, kg)


def _fourier_kernel(u_ref, cs_ref, ss_ref, rev_ref, cc_ref, sc_ref, wf_ref, g_ref, o_ref,
                    p_scr, q_scr, e_scr, *, scale, row_block):
    for g in range(FOURIER_GROUPS):
        sl = slice(g * FOURIER_CH, (g + 1) * FOURIER_CH)
        w = wf_ref[g].astype(BF16)
        a = (jnp.dot(cc_ref[...], w, preferred_element_type=F32) * scale).astype(BF16)
        b = (jnp.dot(sc_ref[...], w, preferred_element_type=F32) * scale).astype(BF16)
        ug = u_ref[:, sl]
        p_scr[:, sl] = jnp.dot(ug, a, preferred_element_type=F32).astype(BF16)
        q_scr[:, sl] = jnp.dot(ug, b, preferred_element_type=F32).astype(BF16)
    half = u_ref.shape[0] // 2
    gain = g_ref[...]

    def norm(y):
        ms = jnp.mean(y * y, axis=-1, keepdims=True)
        return y * lax.rsqrt(ms + NORM_EPS) * gain

    n_blk = half // row_block
    mid = None
    for r in range(n_blk):
        rs = slice(r * row_block, (r + 1) * row_block)
        extra = SUBLANES if r == n_blk - 1 else 0
        c = jnp.dot(cs_ref[r * row_block:(r + 1) * row_block + extra, :], p_scr[...],
                    preferred_element_type=F32)
        d = jnp.dot(ss_ref[rs, :], q_scr[...], preferred_element_type=F32)
        o_ref[rs, :] = norm(c[:row_block] + d).astype(BF16)
        e_scr[rs, :] = norm(c[:row_block] - d).astype(BF16)
        if extra:
            mid = norm(c[row_block:row_block + 1])
    for r in range(n_blk):
        z = jnp.dot(rev_ref[r * row_block:(r + 1) * row_block, :], e_scr[...],
                    preferred_element_type=F32)
        if r == 0:
            z = jnp.where(lax.broadcasted_iota(I32, z.shape, 0) == 0, mid, z)
        o_ref[half + r * row_block:half + (r + 1) * row_block, :] = z.astype(BF16)


def _dft_tables(n):
    k = np.arange(n, dtype=np.int64)
    ang = 2.0 * np.pi * ((k[:, None] * k[None, :]) % n).astype(np.float64) / n
    return np.cos(ang), np.sin(ang)


def _fourier(u, w_fourier, g_out, batch, seq):
    cs, ss = _dft_tables(seq)
    cc, sc = _dft_tables(FOURIER_CH)
    scale = 1.0 / math.sqrt(seq * FOURIER_CH)
    half = seq // 2
    row_block = min(FOURIER_ROWS, half)
    rev = np.zeros((half, half), np.float32)
    rev[np.arange(1, half), half - np.arange(1, half)] = 1.0
    full2 = lambda b: (0, 0)
    return pl.pallas_call(
        functools.partial(_fourier_kernel, scale=scale, row_block=row_block),
        grid=(batch,),
        in_specs=[
            pl.BlockSpec((seq, FOURIER_WIDTH), lambda b: (b, 0)),
            pl.BlockSpec((half + SUBLANES, seq), full2),
            pl.BlockSpec((half, seq), full2),
            pl.BlockSpec((half, half), full2),
            pl.BlockSpec((FOURIER_CH, FOURIER_CH), full2),
            pl.BlockSpec((FOURIER_CH, FOURIER_CH), full2),
            pl.BlockSpec((FOURIER_GROUPS, FOURIER_CH, FOURIER_CH), lambda b: (0, 0, 0)),
            pl.BlockSpec((1, FOURIER_WIDTH), full2),
        ],
        out_specs=pl.BlockSpec((seq, FOURIER_WIDTH), lambda b: (b, 0)),
        out_shape=jax.ShapeDtypeStruct((batch * seq, FOURIER_WIDTH), BF16),
        scratch_shapes=[pltpu.VMEM((seq, FOURIER_WIDTH), BF16),
                        pltpu.VMEM((seq, FOURIER_WIDTH), BF16),
                        pltpu.VMEM((half, FOURIER_WIDTH), BF16)],
        compiler_params=_params(1, VMEM_LIMIT),
        name="fourier",
    )(u, jnp.asarray(cs[:half + SUBLANES], BF16), jnp.asarray(ss[:half], BF16),
      jnp.asarray(rev, BF16), jnp.asarray(cc, BF16), jnp.asarray(-sc, BF16), w_fourier,
      g_out.reshape(1, FOURIER_WIDTH))


def _attn_kernel(sink_ref, q_ref, kvp_ref, kvo_ref, kvn_ref, bias_a_ref, bias_b_ref, g_ref, o_ref,
                 acc_ref):
    kv = jnp.concatenate([kvp_ref[...], kvo_ref[...], kvn_ref[...]], axis=0)
    nk = kv.shape[0]
    lo = lax.broadcasted_iota(I32, (nk, LANES), 1) < HEAD_DIM
    k_a, k_b = kv[:, 0:LANES], kv[:, LANES:2 * LANES]
    v_a, v_b = kv[:, 2 * LANES:3 * LANES], kv[:, 3 * LANES:4 * LANES]
    zero = jnp.zeros_like(k_a)
    k_lo = (jnp.where(lo, k_a, zero), jnp.where(lo, k_b, zero))
    k_hi = (jnp.where(lo, zero, k_b), jnp.where(lo, zero, k_a))
    v_lo = (jnp.where(lo, v_a, zero), jnp.where(lo, v_b, zero))
    v_hi = (jnp.where(lo, zero, v_b), jnp.where(lo, zero, v_a))
    lo_out = lax.broadcasted_iota(I32, (Q_BLOCK, LANES), 1) < HEAD_DIM
    bias_refs = (bias_a_ref, bias_b_ref)
    rows2 = q_ref.shape[0]
    nt = (((1,), (1,)), ((), ()))
    for h in range(N_KV_HEADS):
        qs = jnp.concatenate([q_ref[:, (2 * h) * LANES:(2 * h + 1) * LANES],
                              q_ref[:, (2 * h + 1) * LANES:(2 * h + 2) * LANES]], axis=0)
        s_par = (lax.dot_general(qs, k_lo[h], nt, preferred_element_type=F32),
                 lax.dot_general(qs, k_hi[h], nt, preferred_element_type=F32))
        for sb in range(2):
            keys = slice(sb * Q_BLOCK, sb * Q_BLOCK + 3 * Q_BLOCK)
            vcat = jnp.concatenate([v_lo[h][keys, :], v_hi[h][keys, :]], axis=0)
            for c in range(2):
                r0 = c * rows2 + sb * Q_BLOCK
                probs, invs = [], []
                for par in range(2):
                    hq = 4 * h + 2 * c + par
                    s = s_par[par][r0:r0 + Q_BLOCK, keys] + bias_refs[sb][hq]
                    sink = sink_ref[hq]
                    m = jnp.maximum(jnp.max(s, axis=-1, keepdims=True), sink)
                    p = jnp.exp2(s - m)
                    denom = jnp.sum(p, axis=-1, keepdims=True) + jnp.exp2(sink - m)
                    probs.append(p.astype(BF16))
                    invs.append(1.0 / denom)
                pcat = jnp.concatenate(probs, axis=1)
                chunk = 2 * h + c
                o = jnp.dot(pcat, vcat, preferred_element_type=F32)
                acc_ref[sb * Q_BLOCK:(sb + 1) * Q_BLOCK, chunk * LANES:(chunk + 1) * LANES] = (
                    o * jnp.where(lo_out, invs[0], invs[1]))
    y = acc_ref[...]
    ms = jnp.mean(y * y, axis=-1, keepdims=True)
    o_ref[...] = (y * lax.rsqrt(ms + NORM_EPS) * g_ref[...]).astype(BF16)


def _t5_bucket(rel):
    nb = N_BUCKETS // 2
    max_exact = nb // 2
    ret = (rel > 0).astype(jnp.int32) * nb
    n = jnp.abs(rel)
    nf = jnp.maximum(n, 1).astype(jnp.float32)
    large = max_exact + (jnp.log(nf / max_exact) / math.log(MAX_DISTANCE / max_exact)
                         * (nb - max_exact)).astype(jnp.int32)
    large = jnp.minimum(large, nb - 1)
    return ret + jnp.where(n < max_exact, n, large)


def _attention(q, kv, sinks, rel_bias, g_out, batch, seq):
    nb = seq // Q_BLOCK
    assert nb % 2 == 0
    nb2 = nb // 2
    qi = jnp.arange(Q_BLOCK, dtype=jnp.int32)[:, None]
    kj = jnp.arange(3 * Q_BLOCK, dtype=jnp.int32)[None, :]
    rel = kj - Q_BLOCK - qi
    period = 4 * Q_BLOCK
    p = jnp.arange(period, dtype=jnp.int32)
    off = jnp.where(p < 3 * Q_BLOCK, p, p - period) - Q_BLOCK
    hit = _t5_bucket(off)[None, :, None] == jnp.arange(N_BUCKETS, dtype=jnp.int32)
    by_off = jnp.sum(jnp.where(hit, rel_bias.astype(F32).T[:, None, :], 0.0), axis=-1)
    bias = jnp.tile(by_off, (1, Q_BLOCK))[:, :Q_BLOCK * (period - 1)]
    bias = bias.reshape(N_Q_HEADS, Q_BLOCK, period - 1)[:, :, :3 * Q_BLOCK]
    band = jnp.abs(rel) <= WINDOW
    first = band & (kj >= Q_BLOCK)
    last = band & (kj < 2 * Q_BLOCK)
    table = jnp.stack([jnp.where(msk[None], bias * LOG2E, MASK_VALUE) for msk in (first, band, last)])
    q_rows = 2 * Q_BLOCK
    grid_spec = pltpu.PrefetchScalarGridSpec(
        num_scalar_prefetch=1,
        grid=(batch, nb2),
        in_specs=[
            pl.BlockSpec((q_rows, ATTN_WIDTH), lambda b, i, s: (b * nb2 + i, 0)),
            pl.BlockSpec((Q_BLOCK, 4 * LANES),
                         lambda b, i, s: (b * nb + jnp.maximum(2 * i - 1, 0), 0)),
            pl.BlockSpec((q_rows, 4 * LANES), lambda b, i, s: (b * nb2 + i, 0)),
            pl.BlockSpec((Q_BLOCK, 4 * LANES),
                         lambda b, i, s: (b * nb + jnp.minimum(2 * i + 2, nb - 1), 0)),
            pl.BlockSpec((None, N_Q_HEADS, Q_BLOCK, 3 * Q_BLOCK),
                         lambda b, i, s: (jnp.where(i == 0, 0, 1), 0, 0, 0)),
            pl.BlockSpec((None, N_Q_HEADS, Q_BLOCK, 3 * Q_BLOCK),
                         lambda b, i, s: (jnp.where(i == nb2 - 1, 2, 1), 0, 0, 0)),
            pl.BlockSpec((1, ATTN_WIDTH), lambda b, i, s: (0, 0)),
        ],
        out_specs=pl.BlockSpec((q_rows, ATTN_WIDTH), lambda b, i, s: (b * nb2 + i, 0)),
        scratch_shapes=[pltpu.VMEM((q_rows, ATTN_WIDTH), F32)],
    )
    return pl.pallas_call(
        _attn_kernel,
        grid_spec=grid_spec,
        out_shape=jax.ShapeDtypeStruct((batch * seq, ATTN_WIDTH), BF16),
        compiler_params=_params(2, VMEM_LIMIT),
        name="attention",
    )(sinks.astype(F32) * LOG2E, q, kv, kv, kv, table, table, g_out.reshape(1, ATTN_WIDTH))


def _outproj_kernel(yf_ref, ya_ref, x_ref, wo_ref, g2_ref, wrt_ref, brt_ref, triu_ref, sel_ref, scan_ref,
                    x1_ref, h2_ref, post_ref, gatet_ref, cnt_ref, *, n_experts):
    half = yf_ref.shape[1]
    mix = (jnp.dot(yf_ref[...], wo_ref[:half, :], preferred_element_type=F32)
           + jnp.dot(ya_ref[...], wo_ref[half:, :], preferred_element_type=F32))
    x1 = x_ref[...] + mix
    x1_ref[...] = x1
    ms = jnp.mean(x1 * x1, axis=-1, keepdims=True)
    h2 = x1 * lax.rsqrt(ms + NORM_EPS) * g2_ref[...]
    h2_ref[...] = h2.astype(BF16)
    h_hi = h2.astype(BF16)
    h_lo = (h2 - h_hi.astype(F32)).astype(BF16)
    nt = (((1,), (1,)), ((), ()))
    t1 = lax.dot_general(wrt_ref[...], h_hi, nt, preferred_element_type=F32)
    t2 = lax.dot_general(wrt_ref[:n_experts, :], h_lo, nt, preferred_element_type=F32)
    logits = t1[:n_experts] + t1[n_experts:] + t2 + brt_ref[...]
    tm = logits.shape[1]
    sub_e = lax.broadcasted_iota(I32, (n_experts, tm), 0).astype(F32)
    work = logits
    vals, idxs = [], []
    for _ in range(TOP_K):
        m = jnp.max(work, axis=0, keepdims=True)
        ik = jnp.min(jnp.where(work == m, sub_e, float(n_experts)), axis=0, keepdims=True)
        work = jnp.where(sub_e == ik, -jnp.inf, work)
        vals.append(m)
        idxs.append(ik)
    exps = [jnp.exp(v - vals[0]) for v in vals]
    inv = 1.0 / (exps[0] + exps[1] + exps[2] + exps[3])
    gates = [e * inv for e in exps]

    sub = lax.broadcasted_iota(I32, (LANES, tm), 0).astype(F32)
    onehot = jnp.zeros((LANES, tm), F32)
    for k in range(TOP_K):
        onehot = onehot + jnp.where(sub == idxs[k] + float(k * n_experts), 1.0, 0.0)
    onehot_b = onehot.astype(BF16)
    before = jnp.dot(onehot_b, triu_ref[...], preferred_element_type=F32)
    counts = lax.dot_general(sel_ref[...], onehot_b, nt, preferred_element_type=F32)
    total = counts
    for k in range(1, TOP_K):
        total = total + pltpu.roll(counts, k * n_experts, 1)
    cnt_ref[0] = total.astype(I32)
    rt = min(ROUTE_TILE, tm)
    adds_all = []
    for grp in range(tm // rt):
        colsum = jnp.sum(onehot[:, grp * rt:(grp + 1) * rt], axis=1, keepdims=True)
        blocks = [colsum[k * n_experts:(k + 1) * n_experts] for k in range(TOP_K)]
        total_e = blocks[0] + blocks[1] + blocks[2] + blocks[3]
        run_e = jnp.floor((total_e + (RUN_ALIGN - 1)) * (1.0 / RUN_ALIGN)) * RUN_ALIGN
        run_start = jnp.dot(scan_ref[...],
                            jnp.broadcast_to(run_e, (n_experts, LANES)).astype(BF16),
                            preferred_element_type=F32)[:, 0:1]
        adds, acc = [], run_start
        for k in range(TOP_K):
            adds.append(acc)
            acc = acc + blocks[k]
        adds_all.append(jnp.broadcast_to(jnp.concatenate(adds, axis=0), (LANES, rt)))
    placed = (before + jnp.concatenate(adds_all, axis=1)) * onehot
    pos = [jnp.sum(placed[k * n_experts:(k + 1) * n_experts], axis=0, keepdims=True)
           for k in range(TOP_K)]
    post_ref[...] = jnp.concatenate(pos + [jnp.zeros((SUBLANES - TOP_K, tm), F32)], axis=0).astype(I32)
    gatet_ref[...] = jnp.concatenate(gates + [jnp.zeros((SUBLANES - TOP_K, tm), F32)], axis=0)


def _outproj(yf, ya, x2d, w_out, norm2, w_router, b_router):
    t, d = x2d.shape
    tm = min(TOKEN_TILE, t)
    n_tiles = t // tm
    n_experts = w_router.shape[1]
    assert TOP_K * n_experts == LANES
    group = np.arange(tm) // min(ROUTE_TILE, tm)
    same = group[:, None] == group[None, :]
    triu = np.triu(np.ones((tm, tm), np.float32), 1) * same
    sel = (np.arange(SUBLANES)[:, None] == group[None, :]).astype(np.float32)
    scan = np.tril(np.ones((n_experts, n_experts), np.float32), -1)
    wr_hi = w_router.astype(BF16)
    wr_lo = (w_router - wr_hi.astype(F32)).astype(BF16)
    wrt = jnp.concatenate([wr_hi.T, wr_lo.T], axis=0)
    full = lambda i: (0, 0)
    row = lambda i: (i, 0)
    return pl.pallas_call(
        functools.partial(_outproj_kernel, n_experts=n_experts),
        grid=(n_tiles,),
        in_specs=[
            pl.BlockSpec((tm, yf.shape[1]), row),
            pl.BlockSpec((tm, ya.shape[1]), row),
            pl.BlockSpec((tm, d), row),
            pl.BlockSpec((w_out.shape[0], d), full),
            pl.BlockSpec((1, d), full),
            pl.BlockSpec((2 * n_experts, d), full),
            pl.BlockSpec((n_experts, 1), full),
            pl.BlockSpec((tm, tm), full),
            pl.BlockSpec((SUBLANES, tm), full),
            pl.BlockSpec((n_experts, n_experts), full),
        ],
        out_specs=[
            pl.BlockSpec((tm, d), row),
            pl.BlockSpec((tm, d), row),
            pl.BlockSpec((SUBLANES, tm), row),
            pl.BlockSpec((SUBLANES, tm), row),
            pl.BlockSpec((1, SUBLANES, LANES), lambda i: (i, 0, 0)),
        ],
        out_shape=[
            jax.ShapeDtypeStruct((t, d), F32),
            jax.ShapeDtypeStruct((t, d), BF16),
            jax.ShapeDtypeStruct((n_tiles * SUBLANES, tm), I32),
            jax.ShapeDtypeStruct((n_tiles * SUBLANES, tm), F32),
            jax.ShapeDtypeStruct((n_tiles, SUBLANES, LANES), I32),
        ],
        compiler_params=_params(1, VMEM_LIMIT),
        name="outproj_router",
    )(yf, ya, x2d, w_out.astype(BF16), norm2.reshape(1, d), wrt, b_router.reshape(n_experts, 1),
      jnp.asarray(triu, BF16), jnp.asarray(sel, BF16), jnp.asarray(scan, BF16))


def _pack_pairs(x, is_bf16_exact=False):
    half = x.shape[1] // 2
    a, b = x[:, :half], x[:, half:]
    if not is_bf16_exact:
        a, b = a.astype(BF16).astype(F32), b.astype(BF16).astype(F32)
    return lax.bitcast_convert_type(a, U32) | (lax.bitcast_convert_type(b, U32) >> BF16_BITS)


def _unpack_pairs(w):
    hi = lax.bitcast_convert_type(w & U32(((1 << BF16_BITS) - 1) << BF16_BITS), F32)
    lo = lax.bitcast_convert_type(w << BF16_BITS, F32)
    return hi.astype(BF16), lo.astype(BF16)


def _rows(start, size):
    if not isinstance(size, int):
        size = pl.multiple_of(size, RUN_ALIGN)
    return pl.ds(pl.multiple_of(start, RUN_ALIGN), size)


def _dispatch_kernel(cnt_ref, lst_ref, base_ref, rows_ref, tail_ref, post_ref, h2_ref, xs_ref,
                     buf, zbuf, sem, zsem, *, n_experts):
    j = pl.program_id(0)
    tm = h2_ref.shape[0]
    n_local = buf.shape[1]

    def start_runs(tile, slot):
        def run(e, carry):
            r = tile * n_experts + e
            n = cnt_ref[r]

            @pl.when(n > 0)
            def _():
                pltpu.make_async_copy(buf.at[slot, _rows(lst_ref[r], n), :],
                                      xs_ref.at[_rows(base_ref[r], n), :], sem.at[slot]).start(
                                          priority=1)
            return carry
        lax.fori_loop(0, n_experts, run, 0)

    def wait_runs(tile, slot):
        n = rows_ref[tile]

        @pl.when(n > 0)
        def _():
            pltpu.make_async_copy(buf.at[slot, _rows(0, n), :], xs_ref.at[_rows(0, n), :],
                                  sem.at[slot]).wait()

    def zero_fill(op):
        def tail(e, carry):
            n = tail_ref[n_experts + e]

            @pl.when(n > 0)
            def _():
                getattr(pltpu.make_async_copy(zbuf.at[_rows(0, n), :],
                                              xs_ref.at[_rows(tail_ref[e], n), :], zsem), op)()
            return carry
        lax.fori_loop(0, n_experts, tail, 0)

        def spare(b, carry):
            getattr(pltpu.make_async_copy(zbuf, xs_ref.at[_rows(b * EXPERT_ROWS, EXPERT_ROWS), :],
                                          zsem), op)()
            return carry
        lax.fori_loop(tail_ref[2 * n_experts], xs_ref.shape[0] // EXPERT_ROWS, spare, 0)

    slot = j % 2
    @pl.when(j >= 2)
    def _():
        wait_runs(j - 2, slot)

    @pl.when(j == 0)
    def _():
        zbuf[...] = jnp.zeros_like(zbuf)
        zero_fill("start")
        zero_fill("wait")

    h = h2_ref[...]
    post = post_ref[0:TOP_K, :]
    chunk_of = lax.shift_right_logical(post, PERM_CHUNK.bit_length() - 1)
    offset = (post & (PERM_CHUNK - 1)).astype(F32)
    rows = lax.broadcasted_iota(I32, (PERM_CHUNK, tm), 0).astype(F32).astype(BF16)
    one = jnp.ones((PERM_CHUNK, tm), BF16)
    per = DISPATCH_ROWS // PERM_CHUNK
    for mc in range(n_local // DISPATCH_ROWS):
        parts = []
        for rc in range(mc * per, (mc + 1) * per):
            off = jnp.where(chunk_of == rc, offset, -1.0).astype(BF16)
            perm = jnp.zeros((PERM_CHUNK, tm), BF16)
            for k in range(TOP_K):
                perm = jnp.where(rows == off[k:k + 1, :], one, perm)
            parts.append(perm)
        rs = slice(mc * DISPATCH_ROWS, (mc + 1) * DISPATCH_ROWS)
        buf[slot, rs, :] = _pack_pairs(
            jnp.dot(jnp.concatenate(parts, axis=0), h, preferred_element_type=F32), True)

    start_runs(j, slot)

    @pl.when(j == pl.num_programs(0) - 1)
    def _():
        @pl.when(j >= 1)
        def _():
            wait_runs(j - 1, 1 - slot)
        wait_runs(j, slot)


def _local_rows(tm, n_experts):
    worst = TOP_K * tm + n_experts * (RUN_ALIGN - 1)
    return -(-worst // DISPATCH_ROWS) * DISPATCH_ROWS


def _dispatch(plan, post, h2, n_rows, n_experts):
    t, d = h2.shape
    tm = min(ROUTE_TILE, t)
    per = min(TOKEN_TILE, t) // tm
    grid_spec = pltpu.PrefetchScalarGridSpec(
        num_scalar_prefetch=5,
        grid=(t // tm,),
        in_specs=[
            pl.BlockSpec((SUBLANES, tm), lambda i, *_: (i // per, i % per)),
            pl.BlockSpec((tm, d), lambda i, *_: (i, 0)),
        ],
        out_specs=pl.BlockSpec(memory_space=pl.ANY),
        scratch_shapes=[pltpu.VMEM((2, _local_rows(tm, n_experts), d // 2), U32),
                        pltpu.VMEM((EXPERT_ROWS, d // 2), U32),
                        pltpu.SemaphoreType.DMA((2,)), pltpu.SemaphoreType.DMA(())],
    )
    return pl.pallas_call(
        functools.partial(_dispatch_kernel, n_experts=n_experts),
        grid_spec=grid_spec,
        out_shape=jax.ShapeDtypeStruct((n_rows, d // 2), U32),
        compiler_params=_params(1, VMEM_LIMIT),
        name="dispatch",
    )(plan["cnt"], plan["lst"], plan["base"], plan["rows"], plan["tail"], post, h2)


def _combine_kernel(cnt_ref, lst_ref, base_ref, rows_ref, post_ref, gatet_ref, x1_ref, ys_ref, o_ref,
                    buf, g_scr, y_scr, sem, *, n_experts):
    j = pl.program_id(0)
    tm, d = x1_ref.shape
    n_local = buf.shape[1]

    def start_runs(tile, slot):
        def run(e, carry):
            r = tile * n_experts + e
            n = cnt_ref[r]

            @pl.when(n > 0)
            def _():
                pltpu.make_async_copy(ys_ref.at[_rows(base_ref[r], n), :],
                                      buf.at[slot, _rows(lst_ref[r], n), :], sem.at[slot]).start(
                                          priority=1)
            return carry
        lax.fori_loop(0, n_experts, run, 0)

    def wait_runs(tile, slot):
        n = rows_ref[tile]

        @pl.when(n > 0)
        def _():
            pltpu.make_async_copy(ys_ref.at[_rows(0, n), :], buf.at[slot, _rows(0, n), :],
                                  sem.at[slot]).wait()

    slot = j % 2
    @pl.when(j == 0)
    def _():
        buf[...] = jnp.zeros_like(buf)
        start_runs(j, slot)

    @pl.when(j + 1 < pl.num_programs(0))
    def _():
        start_runs(j + 1, 1 - slot)

    wait_runs(j, slot)

    post = post_ref[0:TOP_K, :]
    gate = gatet_ref[0:TOP_K, :].astype(BF16)
    chunk_of = lax.shift_right_logical(post, PERM_CHUNK.bit_length() - 1)
    offset = (post & (PERM_CHUNK - 1)).astype(F32)
    rows = lax.broadcasted_iota(I32, (PERM_CHUNK, tm), 0).astype(F32).astype(BF16)
    for rc in range(n_local // PERM_CHUNK):
        chunk = slice(rc * PERM_CHUNK, (rc + 1) * PERM_CHUNK)
        off = jnp.where(chunk_of == rc, offset, -1.0).astype(BF16)
        g = jnp.zeros((PERM_CHUNK, tm), BF16)
        for k in range(TOP_K):
            g = jnp.where(rows == off[k:k + 1, :], jnp.broadcast_to(gate[k:k + 1, :], g.shape), g)
        g_scr[chunk, :] = g
        y_scr[chunk, :d // 2], y_scr[chunk, d // 2:] = _unpack_pairs(buf[slot, chunk, :])
    o_ref[...] = x1_ref[...] + lax.dot_general(g_scr[...], y_scr[...], (((0,), (0,)), ((), ())),
                                               preferred_element_type=F32)


def _combine(plan, post, gatet, x1, ys, n_experts):
    t, d = x1.shape
    tm = min(ROUTE_TILE, t)
    per = min(TOKEN_TILE, t) // tm
    grid_spec = pltpu.PrefetchScalarGridSpec(
        num_scalar_prefetch=4,
        grid=(t // tm,),
        in_specs=[
            pl.BlockSpec((SUBLANES, tm), lambda i, *_: (i // per, i % per)),
            pl.BlockSpec((SUBLANES, tm), lambda i, *_: (i // per, i % per)),
            pl.BlockSpec((tm, d), lambda i, *_: (i, 0)),
            pl.BlockSpec(memory_space=pl.ANY),
        ],
        out_specs=pl.BlockSpec((tm, d), lambda i, *_: (i, 0)),
        scratch_shapes=[pltpu.VMEM((2, _local_rows(tm, n_experts), d // 2), U32),
                        pltpu.VMEM((_local_rows(tm, n_experts), tm), BF16),
                        pltpu.VMEM((_local_rows(tm, n_experts), d), BF16),
                        pltpu.SemaphoreType.DMA((2,))],
    )
    return pl.pallas_call(
        functools.partial(_combine_kernel, n_experts=n_experts),
        grid_spec=grid_spec,
        out_shape=jax.ShapeDtypeStruct((t, d), F32),
        compiler_params=_params(1, VMEM_LIMIT),
        name="combine",
    )(plan["cnt"], plan["lst"], plan["base"], plan["rows"], post, gatet, x1, ys)


def _expert_kernel(be_ref, nxt_ref, par_ref, meta_ref, xs_ref, wgu_hbm, wd_hbm, bias_ref, ys_ref,
                   wgu_buf, wd_buf, wg_s, wu_s, wd_s, wsem):
    i = pl.program_id(0)
    n_used = meta_ref[0]
    active = i < n_used
    new_expert = (i == 0) | (be_ref[i] != be_ref[jnp.maximum(i - 1, 0)])

    def weight_copies(expert, slot):
        return (pltpu.make_async_copy(wgu_hbm.at[expert], wgu_buf.at[slot], wsem.at[slot]),
                pltpu.make_async_copy(wd_hbm.at[expert], wd_buf.at[slot], wsem.at[slot]))

    @pl.when(active & new_expert)
    def _():
        slot = par_ref[i]

        @pl.when(i == 0)
        def _():
            for cp in weight_copies(be_ref[i], slot):
                cp.start()

        for cp in weight_copies(be_ref[i], slot):
            cp.wait()
        nxt = nxt_ref[i]

        @pl.when(nxt >= 0)
        def _():
            for cp in weight_copies(nxt, 1 - slot):
                cp.start(priority=1)

        width = 2 * LANES
        src = lax.broadcasted_iota(I32, (width, width), 0)
        dst = lax.broadcasted_iota(I32, (width, width), 1)
        perm = jnp.where(src == jnp.where(dst < LANES, 2 * dst, 2 * (dst - LANES) + 1), 1.0, 0.0
                         ).astype(BF16)
        for c in range(wgu_buf.shape[2] // width):
            wc = wgu_buf[slot, :, c * width:(c + 1) * width].astype(BF16)
            r = jnp.dot(wc, perm, preferred_element_type=F32)
            wg_s[:, c * LANES:(c + 1) * LANES] = r[:, :LANES].astype(BF16)
            wu_s[:, c * LANES:(c + 1) * LANES] = r[:, LANES:].astype(BF16)
        wd_s[...] = wd_buf[slot].astype(BF16)

    @pl.when(active)
    def _():
        f = wg_s.shape[1]
        bias = bias_ref[be_ref[i]]
        xb = jnp.concatenate(_unpack_pairs(xs_ref[...]), axis=1)
        g = jnp.dot(xb, wg_s[...], preferred_element_type=F32) + bias[:, :f]
        up = jnp.dot(xb, wu_s[...], preferred_element_type=F32) + bias[:, f:2 * f]
        g = jnp.minimum(g, SWIGLU_LIMIT)
        up = jnp.clip(up, -SWIGLU_LIMIT, SWIGLU_LIMIT)
        act = g * (1.0 / (1.0 + jnp.exp(-SWIGLU_ALPHA * g))) * (up + 1.0)
        ys_ref[...] = _pack_pairs(
            jnp.dot(act.astype(BF16), wd_s[...], preferred_element_type=F32) + bias[:, 2 * f:])

    @pl.when(jnp.logical_not(active))
    def _():
        ys_ref[...] = jnp.zeros_like(ys_ref)


def _experts(blk, meta, xs, w_gate_up, b_gate_up, w_down, b_down):
    n_rows = xs.shape[0]
    n_experts, d, f2 = w_gate_up.shape
    f = f2 // 2
    bm = EXPERT_ROWS
    n_blocks = n_rows // bm
    bias = jnp.concatenate([b_gate_up[:, 0::2], b_gate_up[:, 1::2], b_down], axis=1)
    bias = bias.reshape(n_experts, 1, f2 + d)
    rows = lambda i, be, nxt, par, meta: (jnp.minimum(i, meta[0] - 1), 0)
    grid_spec = pltpu.PrefetchScalarGridSpec(
        num_scalar_prefetch=4,
        grid=(n_blocks,),
        in_specs=[
            pl.BlockSpec((bm, d // 2), rows),
            pl.BlockSpec(memory_space=pl.ANY),
            pl.BlockSpec(memory_space=pl.ANY),
            pl.BlockSpec((n_experts, 1, f2 + d), lambda i, *_: (0, 0, 0)),
        ],
        out_specs=pl.BlockSpec((bm, d // 2), lambda i, *_: (i, 0)),
        scratch_shapes=[pltpu.VMEM((2, d, f2), F32), pltpu.VMEM((2, f, d), F32),
                        pltpu.VMEM((d, f), BF16), pltpu.VMEM((d, f), BF16),
                        pltpu.VMEM((f, d), BF16), pltpu.SemaphoreType.DMA((2,))],
    )
    return pl.pallas_call(
        _expert_kernel,
        grid_spec=grid_spec,
        out_shape=jax.ShapeDtypeStruct((n_rows, d // 2), U32),
        compiler_params=_params(1, VMEM_LIMIT),
        name="experts",
    )(blk["expert"], blk["next"], blk["slot"], meta, xs, w_gate_up, w_down, bias)


def _routing_plan(counts, n_experts, bm, n_blocks, per_step):
    cnt = counts[:, :per_step, :n_experts].reshape(-1, n_experts)
    run = (cnt + RUN_ALIGN - 1) // RUN_ALIGN * RUN_ALIGN
    per_expert = jnp.sum(run, axis=0)
    padded = (per_expert + bm - 1) // bm * bm
    pend = jnp.cumsum(padded)
    pstart = pend - padded
    base = pstart[None, :] + jnp.cumsum(run, axis=0) - run
    lst = jnp.cumsum(run, axis=1) - run
    n_used = pend[-1] // bm
    tail = jnp.concatenate([pstart + per_expert, padded - per_expert, n_used[None]])
    starts = jnp.arange(n_blocks, dtype=I32) * bm
    blk = jnp.sum((starts[:, None] >= pend[None, :]).astype(I32), axis=1)
    blk = jnp.minimum(blk, n_experts - 1)
    last = jnp.sum((((n_used - 1) * bm) >= pend).astype(I32))
    blk_e = jnp.where(jnp.arange(n_blocks) < n_used, blk, jnp.minimum(last, n_experts - 1))
    ids = jnp.arange(n_experts, dtype=I32)
    has_rows = padded > 0
    later = jnp.where(has_rows[None, :] & (ids[None, :] > ids[:, None]), ids[None, :], n_experts)
    next_e = jnp.min(later, axis=1)
    next_e = jnp.where(next_e < n_experts, next_e, -1)
    ordinal = jnp.cumsum(has_rows.astype(I32)) - 1
    onehot = blk_e[:, None] == ids[None, :]
    blocks = {"expert": blk_e.astype(I32),
              "next": jnp.sum(jnp.where(onehot, next_e[None, :], 0), axis=1).astype(I32),
              "slot": jnp.sum(jnp.where(onehot, (ordinal % 2)[None, :], 0), axis=1).astype(I32)}
    plan = {"cnt": run.reshape(-1).astype(I32), "lst": lst.reshape(-1).astype(I32),
            "base": base.reshape(-1).astype(I32), "rows": jnp.sum(run, axis=1).astype(I32),
            "tail": tail.astype(I32)}
    return plan, blocks, n_used.astype(I32).reshape(1)


def _layer(x2d, batch, seq, norm1, w_in, q_norm, k_norm, sinks, rel_bias, w_fourier, g_fourier_out,
           g_attn_out, w_out, norm2, w_router, b_router, w_gate_up, b_gate_up, w_down, b_down):
    t, d = x2d.shape
    n_experts = w_router.shape[1]
    u, q, kv = _inproj(x2d, norm1, w_in, q_norm, k_norm)
    yf = _fourier(u, w_fourier, g_fourier_out, batch, seq)
    ya = _attention(q, kv, sinks, rel_bias, g_attn_out, batch, seq)
    x1, h2, post, gatet, counts = _outproj(yf, ya, x2d, w_out, norm2, w_router, b_router)
    bm = EXPERT_ROWS
    route = min(ROUTE_TILE, t)
    worst_rows = t * TOP_K + (t // route) * n_experts * (RUN_ALIGN - 1) + n_experts * (bm - RUN_ALIGN)
    n_blocks = -(-worst_rows // bm)
    plan, blocks, meta = _routing_plan(counts, n_experts, bm, n_blocks, min(TOKEN_TILE, t) // route)
    xs = _dispatch(plan, post, h2, n_blocks * bm, n_experts)
    ys = _experts(blocks, meta, xs, w_gate_up, b_gate_up, w_down, b_down)
    return _combine(plan, post, gatet, x1, ys, n_experts)


def kernel(x, norm1, w_in, q_norm, k_norm, sinks, rel_bias, w_fourier, g_fourier_out, g_attn_out,
           w_out, norm2, w_router, b_router, w_gate_up, b_gate_up, w_down, b_down):
    b, s, d = x.shape
    x2d = x.reshape(b * s, d)
    for l in range(norm1.shape[0]):
        x2d = _layer(x2d, b, s, norm1[l], w_in[l], q_norm[l], k_norm[l], sinks[l], rel_bias,
                     w_fourier[l], g_fourier_out[l], g_attn_out[l], w_out[l], norm2[l],
                     w_router[l], b_router[l], w_gate_up[l], b_gate_up[l], w_down[l], b_down[l])
    return x2d.reshape(b, s, d)
```

```python
import functools
import math

import jax
import jax.numpy as jnp
import numpy as np
from jax import lax
from jax.experimental import pallas as pl
from jax.experimental.pallas import tpu as pltpu

F32 = jnp.float32
BF16 = jnp.bfloat16
I32 = jnp.int32
U32 = jnp.uint32

NORM_EPS = 1e-5
QK_EPS = 1e-6
HEAD_DIM = 64
N_Q_HEADS = 8
N_KV_HEADS = 2
FOURIER_GROUPS = 4
FOURIER_CH = 128
FOURIER_WIDTH = FOURIER_GROUPS * FOURIER_CH
ATTN_WIDTH = N_Q_HEADS * HEAD_DIM
KV_WIDTH = N_KV_HEADS * HEAD_DIM
WINDOW = 128
Q_BLOCK = 128
N_BUCKETS = 32
MAX_DISTANCE = 128
TOP_K = 4
SWIGLU_ALPHA = 1.702
SWIGLU_LIMIT = 7.0
MASK_VALUE = -1e30
LOG2E = math.log2(math.e)

LANES = 128
SUBLANES = 8
TOKEN_TILE = 1024
ROUTE_TILE = 512
RUN_ALIGN = SUBLANES
ROW_GROUPS = 4
PERM_CHUNK = 256
DISPATCH_ROWS = 3 * PERM_CHUNK
EXPERT_ROWS = 512
STEP_BLOCKS = 2
FOURIER_ROWS = 512
BF16_BITS = 16
VMEM_LIMIT = 56 * 1024 * 1024


def _params(n_axes, vmem=None):
    return pltpu.CompilerParams(
        dimension_semantics=("arbitrary",) * n_axes, vmem_limit_bytes=vmem)


def _pair_head_norm(xc, gain, lo):
    x2 = xc * xc
    s_lo = jnp.sum(jnp.where(lo, x2, 0.0), axis=-1, keepdims=True)
    s_hi = jnp.sum(jnp.where(lo, 0.0, x2), axis=-1, keepdims=True)
    inv = jnp.where(lo, lax.rsqrt(s_lo * (1.0 / HEAD_DIM) + QK_EPS),
                    lax.rsqrt(s_hi * (1.0 / HEAD_DIM) + QK_EPS))
    return xc * inv * gain


def _inproj_kernel(x_ref, g1_ref, w_ref, qg_ref, kg_ref, u_ref, q_ref, kv_ref):
    rows = x_ref.shape[0] // ROW_GROUPS
    lo = lax.broadcasted_iota(I32, (rows, LANES), 1) < HEAD_DIM
    q0 = FOURIER_WIDTH
    k0 = q0 + ATTN_WIDTH
    for grp in range(ROW_GROUPS):
        rs = slice(grp * rows, (grp + 1) * rows)
        x = x_ref[rs, :]
        ms = jnp.mean(x * x, axis=-1, keepdims=True)
        h = (x * lax.rsqrt(ms + NORM_EPS) * g1_ref[...]).astype(BF16)
        z = jnp.dot(h, w_ref[...], preferred_element_type=F32)
        u_ref[rs, :] = z[:, :FOURIER_WIDTH].astype(BF16)
        for c in range(ATTN_WIDTH // LANES):
            qc = _pair_head_norm(z[:, q0 + c * LANES:q0 + (c + 1) * LANES], qg_ref[...], lo)
            q_ref[rs, c * LANES:(c + 1) * LANES] = (qc * (HEAD_DIM ** -0.5 * LOG2E)).astype(BF16)
        kc = _pair_head_norm(z[:, k0:k0 + KV_WIDTH], kg_ref[...], lo)
        vc = z[:, k0 + KV_WIDTH:k0 + 2 * KV_WIDTH]
        kv_ref[rs, 0:LANES] = kc.astype(BF16)
        kv_ref[rs, LANES:2 * LANES] = pltpu.roll(kc, HEAD_DIM, 1).astype(BF16)
        kv_ref[rs, 2 * LANES:3 * LANES] = vc.astype(BF16)
        kv_ref[rs, 3 * LANES:4 * LANES] = pltpu.roll(vc, HEAD_DIM, 1).astype(BF16)


def _inproj(x2d, norm1, w_in, q_norm, k_norm):
    t, d = x2d.shape
    tm = min(TOKEN_TILE, t)
    n_in = w_in.shape[1]
    qg = jnp.tile(q_norm, LANES // HEAD_DIM).reshape(1, LANES)
    kg = jnp.tile(k_norm, LANES // HEAD_DIM).reshape(1, LANES)
    full = lambda i: (0, 0)
    return pl.pallas_call(
        _inproj_kernel,
        grid=(t // tm,),
        in_specs=[
            pl.BlockSpec((tm, d), lambda i: (i, 0)),
            pl.BlockSpec((1, d), full),
            pl.BlockSpec((d, n_in), full),
            pl.BlockSpec((1, LANES), full),
            pl.BlockSpec((1, LANES), full),
        ],
        out_specs=[
            pl.BlockSpec((tm, FOURIER_WIDTH), lambda i: (i, 0)),
            pl.BlockSpec((tm, ATTN_WIDTH), lambda i: (i, 0)),
            pl.BlockSpec((tm, 4 * LANES), lambda i: (i, 0)),
        ],
        out_shape=[
            jax.ShapeDtypeStruct((t, FOURIER_WIDTH), BF16),
            jax.ShapeDtypeStruct((t, ATTN_WIDTH), BF16),
            jax.ShapeDtypeStruct((t, 4 * LANES), BF16),
        ],
        compiler_params=_params(1, VMEM_LIMIT),
        name="inproj",
    )(x2d, norm1.reshape(1, d), w_in.astype(BF16), qg, kg)


def _fourier_kernel(u_ref, cs_ref, ss_ref, rev_ref, cc_ref, sc_ref, wf_ref, g_ref, o_ref,
                    p_scr, q_scr, e_scr, *, scale, row_block):
    for g in range(FOURIER_GROUPS):
        sl = slice(g * FOURIER_CH, (g + 1) * FOURIER_CH)
        w = wf_ref[g].astype(BF16)
        a = (jnp.dot(cc_ref[...], w, preferred_element_type=F32) * scale).astype(BF16)
        b = (jnp.dot(sc_ref[...], w, preferred_element_type=F32) * scale).astype(BF16)
        ug = u_ref[:, sl]
        p_scr[:, sl] = jnp.dot(ug, a, preferred_element_type=F32).astype(BF16)
        q_scr[:, sl] = jnp.dot(ug, b, preferred_element_type=F32).astype(BF16)
    half = u_ref.shape[0] // 2
    gain = g_ref[...]

    def norm(y):
        ms = jnp.mean(y * y, axis=-1, keepdims=True)
        return y * lax.rsqrt(ms + NORM_EPS) * gain

    n_blk = half // row_block
    mid = None
    for r in range(n_blk):
        rs = slice(r * row_block, (r + 1) * row_block)
        extra = SUBLANES if r == n_blk - 1 else 0
        c = jnp.dot(cs_ref[r * row_block:(r + 1) * row_block + extra, :], p_scr[...],
                    preferred_element_type=F32)
        d = jnp.dot(ss_ref[rs, :], q_scr[...], preferred_element_type=F32)
        o_ref[rs, :] = norm(c[:row_block] + d).astype(BF16)
        e_scr[rs, :] = norm(c[:row_block] - d).astype(BF16)
        if extra:
            mid = norm(c[row_block:row_block + 1])
    for r in range(n_blk):
        z = jnp.dot(rev_ref[r * row_block:(r + 1) * row_block, :], e_scr[...],
                    preferred_element_type=F32)
        if r == 0:
            z = jnp.where(lax.broadcasted_iota(I32, z.shape, 0) == 0, mid, z)
        o_ref[half + r * row_block:half + (r + 1) * row_block, :] = z.astype(BF16)


def _dft_tables(n):
    k = np.arange(n, dtype=np.int64)
    ang = 2.0 * np.pi * ((k[:, None] * k[None, :]) % n).astype(np.float64) / n
    return np.cos(ang), np.sin(ang)


def _fourier(u, w_fourier, g_out, batch, seq):
    cs, ss = _dft_tables(seq)
    cc, sc = _dft_tables(FOURIER_CH)
    scale = 1.0 / math.sqrt(seq * FOURIER_CH)
    half = seq // 2
    row_block = min(FOURIER_ROWS, half)
    rev = np.zeros((half, half), np.float32)
    rev[np.arange(1, half), half - np.arange(1, half)] = 1.0
    full2 = lambda b: (0, 0)
    return pl.pallas_call(
        functools.partial(_fourier_kernel, scale=scale, row_block=row_block),
        grid=(batch,),
        in_specs=[
            pl.BlockSpec((seq, FOURIER_WIDTH), lambda b: (b, 0)),
            pl.BlockSpec((half + SUBLANES, seq), full2),
            pl.BlockSpec((half, seq), full2),
            pl.BlockSpec((half, half), full2),
            pl.BlockSpec((FOURIER_CH, FOURIER_CH), full2),
            pl.BlockSpec((FOURIER_CH, FOURIER_CH), full2),
            pl.BlockSpec((FOURIER_GROUPS, FOURIER_CH, FOURIER_CH), lambda b: (0, 0, 0)),
            pl.BlockSpec((1, FOURIER_WIDTH), full2),
        ],
        out_specs=pl.BlockSpec((seq, FOURIER_WIDTH), lambda b: (b, 0)),
        out_shape=jax.ShapeDtypeStruct((batch * seq, FOURIER_WIDTH), BF16),
        scratch_shapes=[pltpu.VMEM((seq, FOURIER_WIDTH), BF16),
                        pltpu.VMEM((seq, FOURIER_WIDTH), BF16),
                        pltpu.VMEM((half, FOURIER_WIDTH), BF16)],
        compiler_params=_params(1, VMEM_LIMIT),
        name="fourier",
    )(u, jnp.asarray(cs[:half + SUBLANES], BF16), jnp.asarray(ss[:half], BF16),
      jnp.asarray(rev, BF16), jnp.asarray(cc, BF16), jnp.asarray(-sc, BF16), w_fourier,
      g_out.reshape(1, FOURIER_WIDTH))


def _attn_kernel(sink_ref, q_ref, kvp_ref, kvo_ref, kvn_ref, bias_a_ref, bias_b_ref, g_ref, o_ref,
                 acc_ref):
    kv = jnp.concatenate([kvp_ref[...], kvo_ref[...], kvn_ref[...]], axis=0)
    nk = kv.shape[0]
    lo = lax.broadcasted_iota(I32, (nk, LANES), 1) < HEAD_DIM
    k_a, k_b = kv[:, 0:LANES], kv[:, LANES:2 * LANES]
    v_a, v_b = kv[:, 2 * LANES:3 * LANES], kv[:, 3 * LANES:4 * LANES]
    zero = jnp.zeros_like(k_a)
    k_lo = (jnp.where(lo, k_a, zero), jnp.where(lo, k_b, zero))
    k_hi = (jnp.where(lo, zero, k_b), jnp.where(lo, zero, k_a))
    v_lo = (jnp.where(lo, v_a, zero), jnp.where(lo, v_b, zero))
    v_hi = (jnp.where(lo, zero, v_b), jnp.where(lo, zero, v_a))
    lo_out = lax.broadcasted_iota(I32, (Q_BLOCK, LANES), 1) < HEAD_DIM
    bias_refs = (bias_a_ref, bias_b_ref)
    rows2 = q_ref.shape[0]
    nt = (((1,), (1,)), ((), ()))
    for h in range(N_KV_HEADS):
        qs = jnp.concatenate([q_ref[:, (2 * h) * LANES:(2 * h + 1) * LANES],
                              q_ref[:, (2 * h + 1) * LANES:(2 * h + 2) * LANES]], axis=0)
        s_par = (lax.dot_general(qs, k_lo[h], nt, preferred_element_type=F32),
                 lax.dot_general(qs, k_hi[h], nt, preferred_element_type=F32))
        for sb in range(2):
            keys = slice(sb * Q_BLOCK, sb * Q_BLOCK + 3 * Q_BLOCK)
            vcat = jnp.concatenate([v_lo[h][keys, :], v_hi[h][keys, :]], axis=0)
            for c in range(2):
                r0 = c * rows2 + sb * Q_BLOCK
                probs, invs = [], []
                for par in range(2):
                    hq = 4 * h + 2 * c + par
                    s = s_par[par][r0:r0 + Q_BLOCK, keys] + bias_refs[sb][hq]
                    sink = sink_ref[hq]
                    m = jnp.maximum(jnp.max(s, axis=-1, keepdims=True), sink)
                    p = jnp.exp2(s - m)
                    denom = jnp.sum(p, axis=-1, keepdims=True) + jnp.exp2(sink - m)
                    probs.append(p.astype(BF16))
                    invs.append(1.0 / denom)
                pcat = jnp.concatenate(probs, axis=1)
                chunk = 2 * h + c
                o = jnp.dot(pcat, vcat, preferred_element_type=F32)
                acc_ref[sb * Q_BLOCK:(sb + 1) * Q_BLOCK, chunk * LANES:(chunk + 1) * LANES] = (
                    o * jnp.where(lo_out, invs[0], invs[1]))
    y = acc_ref[...]
    ms = jnp.mean(y * y, axis=-1, keepdims=True)
    o_ref[...] = (y * lax.rsqrt(ms + NORM_EPS) * g_ref[...]).astype(BF16)


def _t5_bucket(rel):
    nb = N_BUCKETS // 2
    max_exact = nb // 2
    ret = (rel > 0).astype(jnp.int32) * nb
    n = jnp.abs(rel)
    nf = jnp.maximum(n, 1).astype(jnp.float32)
    large = max_exact + (jnp.log(nf / max_exact) / math.log(MAX_DISTANCE / max_exact)
                         * (nb - max_exact)).astype(jnp.int32)
    large = jnp.minimum(large, nb - 1)
    return ret + jnp.where(n < max_exact, n, large)


def _attention(q, kv, sinks, rel_bias, g_out, batch, seq):
    nb = seq // Q_BLOCK
    assert nb % 2 == 0
    nb2 = nb // 2
    qi = jnp.arange(Q_BLOCK, dtype=jnp.int32)[:, None]
    kj = jnp.arange(3 * Q_BLOCK, dtype=jnp.int32)[None, :]
    rel = kj - Q_BLOCK - qi
    period = 4 * Q_BLOCK
    p = jnp.arange(period, dtype=jnp.int32)
    off = jnp.where(p < 3 * Q_BLOCK, p, p - period) - Q_BLOCK
    hit = _t5_bucket(off)[None, :, None] == jnp.arange(N_BUCKETS, dtype=jnp.int32)
    by_off = jnp.sum(jnp.where(hit, rel_bias.astype(F32).T[:, None, :], 0.0), axis=-1)
    bias = jnp.tile(by_off, (1, Q_BLOCK))[:, :Q_BLOCK * (period - 1)]
    bias = bias.reshape(N_Q_HEADS, Q_BLOCK, period - 1)[:, :, :3 * Q_BLOCK]
    band = jnp.abs(rel) <= WINDOW
    first = band & (kj >= Q_BLOCK)
    last = band & (kj < 2 * Q_BLOCK)
    table = jnp.stack([jnp.where(msk[None], bias * LOG2E, MASK_VALUE) for msk in (first, band, last)])
    q_rows = 2 * Q_BLOCK
    grid_spec = pltpu.PrefetchScalarGridSpec(
        num_scalar_prefetch=1,
        grid=(batch, nb2),
        in_specs=[
            pl.BlockSpec((q_rows, ATTN_WIDTH), lambda b, i, s: (b * nb2 + i, 0)),
            pl.BlockSpec((Q_BLOCK, 4 * LANES),
                         lambda b, i, s: (b * nb + jnp.maximum(2 * i - 1, 0), 0)),
            pl.BlockSpec((q_rows, 4 * LANES), lambda b, i, s: (b * nb2 + i, 0)),
            pl.BlockSpec((Q_BLOCK, 4 * LANES),
                         lambda b, i, s: (b * nb + jnp.minimum(2 * i + 2, nb - 1), 0)),
            pl.BlockSpec((None, N_Q_HEADS, Q_BLOCK, 3 * Q_BLOCK),
                         lambda b, i, s: (jnp.where(i == 0, 0, 1), 0, 0, 0)),
            pl.BlockSpec((None, N_Q_HEADS, Q_BLOCK, 3 * Q_BLOCK),
                         lambda b, i, s: (jnp.where(i == nb2 - 1, 2, 1), 0, 0, 0)),
            pl.BlockSpec((1, ATTN_WIDTH), lambda b, i, s: (0, 0)),
        ],
        out_specs=pl.BlockSpec((q_rows, ATTN_WIDTH), lambda b, i, s: (b * nb2 + i, 0)),
        scratch_shapes=[pltpu.VMEM((q_rows, ATTN_WIDTH), F32)],
    )
    return pl.pallas_call(
        _attn_kernel,
        grid_spec=grid_spec,
        out_shape=jax.ShapeDtypeStruct((batch * seq, ATTN_WIDTH), BF16),
        compiler_params=_params(2, VMEM_LIMIT),
        name="attention",
    )(sinks.astype(F32) * LOG2E, q, kv, kv, kv, table, table, g_out.reshape(1, ATTN_WIDTH))


def _outproj_kernel(yf_ref, ya_ref, x_ref, wo_ref, g2_ref, wrt_ref, brt_ref, triu_ref, sel_ref, scan_ref,
                    x1_ref, h2_ref, post_ref, gatet_ref, cnt_ref, *, n_experts):
    half = yf_ref.shape[1]
    mix = (jnp.dot(yf_ref[...], wo_ref[:half, :], preferred_element_type=F32)
           + jnp.dot(ya_ref[...], wo_ref[half:, :], preferred_element_type=F32))
    x1 = x_ref[...] + mix
    x1_ref[...] = x1
    ms = jnp.mean(x1 * x1, axis=-1, keepdims=True)
    h2 = x1 * lax.rsqrt(ms + NORM_EPS) * g2_ref[...]
    h2_ref[...] = h2.astype(BF16)
    h_hi = h2.astype(BF16)
    h_lo = (h2 - h_hi.astype(F32)).astype(BF16)
    nt = (((1,), (1,)), ((), ()))
    t1 = lax.dot_general(wrt_ref[...], h_hi, nt, preferred_element_type=F32)
    t2 = lax.dot_general(wrt_ref[:n_experts, :], h_lo, nt, preferred_element_type=F32)
    logits = t1[:n_experts] + t1[n_experts:] + t2 + brt_ref[...]
    tm = logits.shape[1]
    sub_e = lax.broadcasted_iota(I32, (n_experts, tm), 0).astype(F32)
    work = logits
    vals, idxs = [], []
    for _ in range(TOP_K):
        m = jnp.max(work, axis=0, keepdims=True)
        ik = jnp.min(jnp.where(work == m, sub_e, float(n_experts)), axis=0, keepdims=True)
        work = jnp.where(sub_e == ik, -jnp.inf, work)
        vals.append(m)
        idxs.append(ik)
    exps = [jnp.exp(v - vals[0]) for v in vals]
    inv = 1.0 / (exps[0] + exps[1] + exps[2] + exps[3])
    gates = [e * inv for e in exps]

    sub = lax.broadcasted_iota(I32, (LANES, tm), 0).astype(F32)
    onehot = jnp.zeros((LANES, tm), F32)
    for k in range(TOP_K):
        onehot = onehot + jnp.where(sub == idxs[k] + float(k * n_experts), 1.0, 0.0)
    onehot_b = onehot.astype(BF16)
    before = jnp.dot(onehot_b, triu_ref[...], preferred_element_type=F32)
    counts = lax.dot_general(sel_ref[...], onehot_b, nt, preferred_element_type=F32)
    total = counts
    for k in range(1, TOP_K):
        total = total + pltpu.roll(counts, k * n_experts, 1)
    cnt_ref[0] = total.astype(I32)
    rt = min(ROUTE_TILE, tm)
    adds_all = []
    for grp in range(tm // rt):
        colsum = jnp.sum(onehot[:, grp * rt:(grp + 1) * rt], axis=1, keepdims=True)
        blocks = [colsum[k * n_experts:(k + 1) * n_experts] for k in range(TOP_K)]
        total_e = blocks[0] + blocks[1] + blocks[2] + blocks[3]
        run_e = jnp.floor((total_e + (RUN_ALIGN - 1)) * (1.0 / RUN_ALIGN)) * RUN_ALIGN
        run_start = jnp.dot(scan_ref[...],
                            jnp.broadcast_to(run_e, (n_experts, LANES)).astype(BF16),
                            preferred_element_type=F32)[:, 0:1]
        adds, acc = [], run_start
        for k in range(TOP_K):
            adds.append(acc)
            acc = acc + blocks[k]
        adds_all.append(jnp.broadcast_to(jnp.concatenate(adds, axis=0), (LANES, rt)))
    placed = (before + jnp.concatenate(adds_all, axis=1)) * onehot
    pos = [jnp.sum(placed[k * n_experts:(k + 1) * n_experts], axis=0, keepdims=True)
           for k in range(TOP_K)]
    post_ref[...] = jnp.concatenate(pos + [jnp.zeros((SUBLANES - TOP_K, tm), F32)], axis=0).astype(I32)
    gatet_ref[...] = jnp.concatenate(gates + [jnp.zeros((SUBLANES - TOP_K, tm), F32)], axis=0)


def _outproj(yf, ya, x2d, w_out, norm2, w_router, b_router):
    t, d = x2d.shape
    tm = min(TOKEN_TILE, t)
    n_tiles = t // tm
    n_experts = w_router.shape[1]
    assert TOP_K * n_experts == LANES
    group = np.arange(tm) // min(ROUTE_TILE, tm)
    same = group[:, None] == group[None, :]
    triu = np.triu(np.ones((tm, tm), np.float32), 1) * same
    sel = (np.arange(SUBLANES)[:, None] == group[None, :]).astype(np.float32)
    scan = np.tril(np.ones((n_experts, n_experts), np.float32), -1)
    wr_hi = w_router.astype(BF16)
    wr_lo = (w_router - wr_hi.astype(F32)).astype(BF16)
    wrt = jnp.concatenate([wr_hi.T, wr_lo.T], axis=0)
    full = lambda i: (0, 0)
    row = lambda i: (i, 0)
    return pl.pallas_call(
        functools.partial(_outproj_kernel, n_experts=n_experts),
        grid=(n_tiles,),
        in_specs=[
            pl.BlockSpec((tm, yf.shape[1]), row),
            pl.BlockSpec((tm, ya.shape[1]), row),
            pl.BlockSpec((tm, d), row),
            pl.BlockSpec((w_out.shape[0], d), full),
            pl.BlockSpec((1, d), full),
            pl.BlockSpec((2 * n_experts, d), full),
            pl.BlockSpec((n_experts, 1), full),
            pl.BlockSpec((tm, tm), full),
            pl.BlockSpec((SUBLANES, tm), full),
            pl.BlockSpec((n_experts, n_experts), full),
        ],
        out_specs=[
            pl.BlockSpec((tm, d), row),
            pl.BlockSpec((tm, d), row),
            pl.BlockSpec((SUBLANES, tm), row),
            pl.BlockSpec((SUBLANES, tm), row),
            pl.BlockSpec((1, SUBLANES, LANES), lambda i: (i, 0, 0)),
        ],
        out_shape=[
            jax.ShapeDtypeStruct((t, d), F32),
            jax.ShapeDtypeStruct((t, d), BF16),
            jax.ShapeDtypeStruct((n_tiles * SUBLANES, tm), I32),
            jax.ShapeDtypeStruct((n_tiles * SUBLANES, tm), F32),
            jax.ShapeDtypeStruct((n_tiles, SUBLANES, LANES), I32),
        ],
        compiler_params=_params(1, VMEM_LIMIT),
        name="outproj_router",
    )(yf, ya, x2d, w_out.astype(BF16), norm2.reshape(1, d), wrt, b_router.reshape(n_experts, 1),
      jnp.asarray(triu, BF16), jnp.asarray(sel, BF16), jnp.asarray(scan, BF16))


def _pack_pairs(x, is_bf16_exact=False):
    half = x.shape[1] // 2
    a, b = x[:, :half], x[:, half:]
    if not is_bf16_exact:
        a, b = a.astype(BF16).astype(F32), b.astype(BF16).astype(F32)
    return lax.bitcast_convert_type(a, U32) | (lax.bitcast_convert_type(b, U32) >> BF16_BITS)


def _unpack_pairs(w):
    hi = lax.bitcast_convert_type(w & U32(((1 << BF16_BITS) - 1) << BF16_BITS), F32)
    lo = lax.bitcast_convert_type(w << BF16_BITS, F32)
    return hi.astype(BF16), lo.astype(BF16)


def _rows(start, size):
    if not isinstance(size, int):
        size = pl.multiple_of(size, RUN_ALIGN)
    return pl.ds(pl.multiple_of(start, RUN_ALIGN), size)


def _dispatch_kernel(cnt_ref, lst_ref, base_ref, rows_ref, tail_ref, post_ref, h2_ref, xs_ref,
                     buf, zbuf, sem, zsem, *, n_experts):
    j = pl.program_id(0)
    tm = h2_ref.shape[0]
    n_local = buf.shape[1]

    def start_runs(tile, slot):
        def run(e, carry):
            r = tile * n_experts + e
            n = cnt_ref[r]

            @pl.when(n > 0)
            def _():
                pltpu.make_async_copy(buf.at[slot, _rows(lst_ref[r], n), :],
                                      xs_ref.at[_rows(base_ref[r], n), :], sem.at[slot]).start(
                                          priority=1)
            return carry
        lax.fori_loop(0, n_experts, run, 0)

    def wait_runs(tile, slot):
        n = rows_ref[tile]

        @pl.when(n > 0)
        def _():
            pltpu.make_async_copy(buf.at[slot, _rows(0, n), :], xs_ref.at[_rows(0, n), :],
                                  sem.at[slot]).wait()

    def zero_fill(op):
        def tail(e, carry):
            n = tail_ref[n_experts + e]

            @pl.when(n > 0)
            def _():
                getattr(pltpu.make_async_copy(zbuf.at[_rows(0, n), :],
                                              xs_ref.at[_rows(tail_ref[e], n), :], zsem), op)()
            return carry
        lax.fori_loop(0, n_experts, tail, 0)

        def spare(b, carry):
            getattr(pltpu.make_async_copy(zbuf, xs_ref.at[_rows(b * EXPERT_ROWS, EXPERT_ROWS), :],
                                          zsem), op)()
            return carry
        lax.fori_loop(tail_ref[2 * n_experts], xs_ref.shape[0] // EXPERT_ROWS, spare, 0)

    slot = j % 2
    @pl.when(j >= 2)
    def _():
        wait_runs(j - 2, slot)

    @pl.when(j == 0)
    def _():
        zbuf[...] = jnp.zeros_like(zbuf)
        zero_fill("start")
        zero_fill("wait")

    h = h2_ref[...]
    post = post_ref[0:TOP_K, :]
    chunk_of = lax.shift_right_logical(post, PERM_CHUNK.bit_length() - 1)
    offset = (post & (PERM_CHUNK - 1)).astype(F32)
    rows = lax.broadcasted_iota(I32, (PERM_CHUNK, tm), 0).astype(F32).astype(BF16)
    one = jnp.ones((PERM_CHUNK, tm), BF16)
    per = DISPATCH_ROWS // PERM_CHUNK
    for mc in range(n_local // DISPATCH_ROWS):
        parts = []
        for rc in range(mc * per, (mc + 1) * per):
            off = jnp.where(chunk_of == rc, offset, -1.0).astype(BF16)
            perm = jnp.zeros((PERM_CHUNK, tm), BF16)
            for k in range(TOP_K):
                perm = jnp.where(rows == off[k:k + 1, :], one, perm)
            parts.append(perm)
        rs = slice(mc * DISPATCH_ROWS, (mc + 1) * DISPATCH_ROWS)
        buf[slot, rs, :] = _pack_pairs(
            jnp.dot(jnp.concatenate(parts, axis=0), h, preferred_element_type=F32), True)

    start_runs(j, slot)

    @pl.when(j == pl.num_programs(0) - 1)
    def _():
        @pl.when(j >= 1)
        def _():
            wait_runs(j - 1, 1 - slot)
        wait_runs(j, slot)


def _local_rows(tm, n_experts):
    worst = TOP_K * tm + n_experts * (RUN_ALIGN - 1)
    return -(-worst // DISPATCH_ROWS) * DISPATCH_ROWS


def _dispatch(plan, post, h2, n_rows, n_experts):
    t, d = h2.shape
    tm = min(ROUTE_TILE, t)
    per = min(TOKEN_TILE, t) // tm
    grid_spec = pltpu.PrefetchScalarGridSpec(
        num_scalar_prefetch=5,
        grid=(t // tm,),
        in_specs=[
            pl.BlockSpec((SUBLANES, tm), lambda i, *_: (i // per, i % per)),
            pl.BlockSpec((tm, d), lambda i, *_: (i, 0)),
        ],
        out_specs=pl.BlockSpec(memory_space=pl.ANY),
        scratch_shapes=[pltpu.VMEM((2, _local_rows(tm, n_experts), d // 2), U32),
                        pltpu.VMEM((EXPERT_ROWS, d // 2), U32),
                        pltpu.SemaphoreType.DMA((2,)), pltpu.SemaphoreType.DMA(())],
    )
    return pl.pallas_call(
        functools.partial(_dispatch_kernel, n_experts=n_experts),
        grid_spec=grid_spec,
        out_shape=jax.ShapeDtypeStruct((n_rows, d // 2), U32),
        compiler_params=_params(1, VMEM_LIMIT),
        name="dispatch",
    )(plan["cnt"], plan["lst"], plan["base"], plan["rows"], plan["tail"], post, h2)


def _combine_kernel(cnt_ref, lst_ref, base_ref, rows_ref, post_ref, gatet_ref, x1_ref, ys_ref, o_ref,
                    buf, g_scr, y_scr, sem, *, n_experts):
    j = pl.program_id(0)
    tm, d = x1_ref.shape
    n_local = buf.shape[1]

    def start_runs(tile, slot):
        def run(e, carry):
            r = tile * n_experts + e
            n = cnt_ref[r]

            @pl.when(n > 0)
            def _():
                pltpu.make_async_copy(ys_ref.at[_rows(base_ref[r], n), :],
                                      buf.at[slot, _rows(lst_ref[r], n), :], sem.at[slot]).start(
                                          priority=1)
            return carry
        lax.fori_loop(0, n_experts, run, 0)

    def wait_runs(tile, slot):
        n = rows_ref[tile]

        @pl.when(n > 0)
        def _():
            pltpu.make_async_copy(ys_ref.at[_rows(0, n), :], buf.at[slot, _rows(0, n), :],
                                  sem.at[slot]).wait()

    slot = j % 2
    @pl.when(j == 0)
    def _():
        buf[...] = jnp.zeros_like(buf)
        start_runs(j, slot)

    @pl.when(j + 1 < pl.num_programs(0))
    def _():
        start_runs(j + 1, 1 - slot)

    wait_runs(j, slot)

    post = post_ref[0:TOP_K, :]
    gate = gatet_ref[0:TOP_K, :].astype(BF16)
    chunk_of = lax.shift_right_logical(post, PERM_CHUNK.bit_length() - 1)
    offset = (post & (PERM_CHUNK - 1)).astype(F32)
    rows = lax.broadcasted_iota(I32, (PERM_CHUNK, tm), 0).astype(F32).astype(BF16)
    for rc in range(n_local // PERM_CHUNK):
        chunk = slice(rc * PERM_CHUNK, (rc + 1) * PERM_CHUNK)
        off = jnp.where(chunk_of == rc, offset, -1.0).astype(BF16)
        g = jnp.zeros((PERM_CHUNK, tm), BF16)
        for k in range(TOP_K):
            g = jnp.where(rows == off[k:k + 1, :], jnp.broadcast_to(gate[k:k + 1, :], g.shape), g)
        g_scr[chunk, :] = g
        y_scr[chunk, :d // 2], y_scr[chunk, d // 2:] = _unpack_pairs(buf[slot, chunk, :])
    o_ref[...] = x1_ref[...] + lax.dot_general(g_scr[...], y_scr[...], (((0,), (0,)), ((), ())),
                                               preferred_element_type=F32)


def _combine(plan, post, gatet, x1, ys, n_experts):
    t, d = x1.shape
    tm = min(ROUTE_TILE, t)
    per = min(TOKEN_TILE, t) // tm
    grid_spec = pltpu.PrefetchScalarGridSpec(
        num_scalar_prefetch=4,
        grid=(t // tm,),
        in_specs=[
            pl.BlockSpec((SUBLANES, tm), lambda i, *_: (i // per, i % per)),
            pl.BlockSpec((SUBLANES, tm), lambda i, *_: (i // per, i % per)),
            pl.BlockSpec((tm, d), lambda i, *_: (i, 0)),
            pl.BlockSpec(memory_space=pl.ANY),
        ],
        out_specs=pl.BlockSpec((tm, d), lambda i, *_: (i, 0)),
        scratch_shapes=[pltpu.VMEM((2, _local_rows(tm, n_experts), d // 2), U32),
                        pltpu.VMEM((_local_rows(tm, n_experts), tm), BF16),
                        pltpu.VMEM((_local_rows(tm, n_experts), d), BF16),
                        pltpu.SemaphoreType.DMA((2,))],
    )
    return pl.pallas_call(
        functools.partial(_combine_kernel, n_experts=n_experts),
        grid_spec=grid_spec,
        out_shape=jax.ShapeDtypeStruct((t, d), F32),
        compiler_params=_params(1, VMEM_LIMIT),
        name="combine",
    )(plan["cnt"], plan["lst"], plan["base"], plan["rows"], post, gatet, x1, ys)


def _expert_kernel(be_ref, nxt_ref, par_ref, meta_ref, xs_ref, wgu_hbm, wd_hbm, bias_ref, ys_ref,
                   wgu_buf, wd_buf, wg_s, wu_s, wd_s, wsem):
    n_used = meta_ref[0]
    bm = xs_ref.shape[0] // STEP_BLOCKS

    def weight_copies(expert, slot):
        return (pltpu.make_async_copy(wgu_hbm.at[expert], wgu_buf.at[slot], wsem.at[slot]),
                pltpu.make_async_copy(wd_hbm.at[expert], wd_buf.at[slot], wsem.at[slot]))

    def one_block(i, rows):
        active = i < n_used
        new_expert = (i == 0) | (be_ref[i] != be_ref[jnp.maximum(i - 1, 0)])

        @pl.when(active & new_expert)
        def _():
            slot = par_ref[i]

            @pl.when(i == 0)
            def _():
                for cp in weight_copies(be_ref[i], slot):
                    cp.start()

            for cp in weight_copies(be_ref[i], slot):
                cp.wait()
            nxt = nxt_ref[i]

            @pl.when(nxt >= 0)
            def _():
                for cp in weight_copies(nxt, 1 - slot):
                    cp.start(priority=1)

            width = 2 * LANES
            src = lax.broadcasted_iota(I32, (width, width), 0)
            dst = lax.broadcasted_iota(I32, (width, width), 1)
            perm = jnp.where(src == jnp.where(dst < LANES, 2 * dst, 2 * (dst - LANES) + 1), 1.0, 0.0
                             ).astype(BF16)
            for c in range(wgu_buf.shape[2] // width):
                wc = wgu_buf[slot, :, c * width:(c + 1) * width].astype(BF16)
                r = jnp.dot(wc, perm, preferred_element_type=F32)
                wg_s[:, c * LANES:(c + 1) * LANES] = r[:, :LANES].astype(BF16)
                wu_s[:, c * LANES:(c + 1) * LANES] = r[:, LANES:].astype(BF16)
            wd_s[...] = wd_buf[slot].astype(BF16)

        @pl.when(active)
        def _():
            f = wg_s.shape[1]
            bias = bias_ref[be_ref[i]]
            xb = jnp.concatenate(_unpack_pairs(xs_ref[rows, :]), axis=1)
            g = jnp.dot(xb, wg_s[...], preferred_element_type=F32) + bias[:, :f]
            up = jnp.dot(xb, wu_s[...], preferred_element_type=F32) + bias[:, f:2 * f]
            g = jnp.minimum(g, SWIGLU_LIMIT)
            up = jnp.clip(up, -SWIGLU_LIMIT, SWIGLU_LIMIT)
            act = g * (1.0 / (1.0 + jnp.exp(-SWIGLU_ALPHA * g))) * (up + 1.0)
            ys_ref[rows, :] = _pack_pairs(
                jnp.dot(act.astype(BF16), wd_s[...], preferred_element_type=F32) + bias[:, 2 * f:])

        @pl.when(jnp.logical_not(active))
        def _():
            ys_ref[rows, :] = jnp.zeros((bm, ys_ref.shape[1]), ys_ref.dtype)

    for sub in range(STEP_BLOCKS):
        one_block(STEP_BLOCKS * pl.program_id(0) + sub, slice(sub * bm, (sub + 1) * bm))


def _experts(blk, meta, xs, w_gate_up, b_gate_up, w_down, b_down):
    n_rows = xs.shape[0]
    n_experts, d, f2 = w_gate_up.shape
    f = f2 // 2
    bm = EXPERT_ROWS
    n_blocks = n_rows // bm
    bias = jnp.concatenate([b_gate_up[:, 0::2], b_gate_up[:, 1::2], b_down], axis=1)
    bias = bias.reshape(n_experts, 1, f2 + d)
    sb = STEP_BLOCKS
    rows = lambda i, be, nxt, par, meta: (jnp.minimum(i, (meta[0] - 1) // sb), 0)
    grid_spec = pltpu.PrefetchScalarGridSpec(
        num_scalar_prefetch=4,
        grid=(n_blocks // sb,),
        in_specs=[
            pl.BlockSpec((sb * bm, d // 2), rows),
            pl.BlockSpec(memory_space=pl.ANY),
            pl.BlockSpec(memory_space=pl.ANY),
            pl.BlockSpec((n_experts, 1, f2 + d), lambda i, *_: (0, 0, 0)),
        ],
        out_specs=pl.BlockSpec((sb * bm, d // 2), lambda i, *_: (i, 0)),
        scratch_shapes=[pltpu.VMEM((2, d, f2), F32), pltpu.VMEM((2, f, d), F32),
                        pltpu.VMEM((d, f), BF16), pltpu.VMEM((d, f), BF16),
                        pltpu.VMEM((f, d), BF16), pltpu.SemaphoreType.DMA((2,))],
    )
    return pl.pallas_call(
        _expert_kernel,
        grid_spec=grid_spec,
        out_shape=jax.ShapeDtypeStruct((n_rows, d // 2), U32),
        compiler_params=_params(1, VMEM_LIMIT),
        name="experts",
    )(blk["expert"], blk["next"], blk["slot"], meta, xs, w_gate_up, w_down, bias)


def _routing_plan(counts, n_experts, bm, n_blocks, per_step):
    cnt = counts[:, :per_step, :n_experts].reshape(-1, n_experts)
    run = (cnt + RUN_ALIGN - 1) // RUN_ALIGN * RUN_ALIGN
    per_expert = jnp.sum(run, axis=0)
    padded = (per_expert + bm - 1) // bm * bm
    pend = jnp.cumsum(padded)
    pstart = pend - padded
    base = pstart[None, :] + jnp.cumsum(run, axis=0) - run
    lst = jnp.cumsum(run, axis=1) - run
    n_used = pend[-1] // bm
    tail = jnp.concatenate([pstart + per_expert, padded - per_expert, n_used[None]])
    starts = jnp.arange(n_blocks, dtype=I32) * bm
    blk = jnp.sum((starts[:, None] >= pend[None, :]).astype(I32), axis=1)
    blk = jnp.minimum(blk, n_experts - 1)
    last = jnp.sum((((n_used - 1) * bm) >= pend).astype(I32))
    blk_e = jnp.where(jnp.arange(n_blocks) < n_used, blk, jnp.minimum(last, n_experts - 1))
    ids = jnp.arange(n_experts, dtype=I32)
    has_rows = padded > 0
    later = jnp.where(has_rows[None, :] & (ids[None, :] > ids[:, None]), ids[None, :], n_experts)
    next_e = jnp.min(later, axis=1)
    next_e = jnp.where(next_e < n_experts, next_e, -1)
    ordinal = jnp.cumsum(has_rows.astype(I32)) - 1
    onehot = blk_e[:, None] == ids[None, :]
    blocks = {"expert": blk_e.astype(I32),
              "next": jnp.sum(jnp.where(onehot, next_e[None, :], 0), axis=1).astype(I32),
              "slot": jnp.sum(jnp.where(onehot, (ordinal % 2)[None, :], 0), axis=1).astype(I32)}
    plan = {"cnt": run.reshape(-1).astype(I32), "lst": lst.reshape(-1).astype(I32),
            "base": base.reshape(-1).astype(I32), "rows": jnp.sum(run, axis=1).astype(I32),
            "tail": tail.astype(I32)}
    return plan, blocks, n_used.astype(I32).reshape(1)


def _layer(x2d, batch, seq, norm1, w_in, q_norm, k_norm, sinks, rel_bias, w_fourier, g_fourier_out,
           g_attn_out, w_out, norm2, w_router, b_router, w_gate_up, b_gate_up, w_down, b_down):
    t, d = x2d.shape
    n_experts = w_router.shape[1]
    u, q, kv = _inproj(x2d, norm1, w_in, q_norm, k_norm)
    yf = _fourier(u, w_fourier, g_fourier_out, batch, seq)
    ya = _attention(q, kv, sinks, rel_bias, g_attn_out, batch, seq)
    x1, h2, post, gatet, counts = _outproj(yf, ya, x2d, w_out, norm2, w_router, b_router)
    bm = EXPERT_ROWS
    route = min(ROUTE_TILE, t)
    worst_rows = t * TOP_K + (t // route) * n_experts * (RUN_ALIGN - 1) + n_experts * (bm - RUN_ALIGN)
    n_blocks = -(-worst_rows // (bm * STEP_BLOCKS)) * STEP_BLOCKS
    plan, blocks, meta = _routing_plan(counts, n_experts, bm, n_blocks, min(TOKEN_TILE, t) // route)
    xs = _dispatch(plan, post, h2, n_blocks * bm, n_experts)
    ys = _experts(blocks, meta, xs, w_gate_up, b_gate_up, w_down, b_down)
    return _combine(plan, post, gatet, x1, ys, n_experts)


def kernel(x, norm1, w_in, q_norm, k_norm, sinks, rel_bias, w_fourier, g_fourier_out, g_attn_out,
           w_out, norm2, w_router, b_router, w_gate_up, b_gate_up, w_down, b_down):
    b, s, d = x.shape
    x2d = x.reshape(b * s, d)
    for l in range(norm1.shape[0]):
        x2d = _layer(x2d, b, s, norm1[l], w_in[l], q_norm[l], k_norm[l], sinks[l], rel_bias,
                     w_fourier[l], g_fourier_out[l], g_attn_out[l], w_out[l], norm2[l],
                     w_router[l], b_router[l], w_gate_up[l], b_gate_up[l], w_down[l], b_down[l])
    return x2d.reshape(b, s, d)
```
